```python
import math
import jax, jax.numpy as jnp
from jax import lax
import numpy as np

D_MODEL = 1024
BATCH = 1
SEQ = 16384
DEPTH = 2

CHUNK = 64
Q_BLOCK = 128
A_HEADS = 8
A_HEAD_DIM = 64
A_WIDTH = A_HEADS * A_HEAD_DIM
IDX_HEADS = 8
IDX_DIM = 64
TOPK_MAX = 256
B_WIDTH = 512
B_GROUPS = 8
SHORT_CONV = 3
C_WIDTH = 512
POOL_WINDOWS = (2, 4, 8, 16)
C_GROUPS = len(POOL_WINDOWS)
C_GROUP_DIM = C_WIDTH // C_GROUPS
D_HEADS = 8
D_NOPE = 64
D_ROPE = 32
D_V = 64
Q_LORA = 384
KV_LORA = 256
ROPE_BASE = 10000.0
D_FF = 2816
FFN_CONV = 3
LN_EPS = 1e-5
RMS_EPS = 1e-6
DN_ALPHA = (2 * DEPTH) ** 0.25
DN_BETA = (8 * DEPTH) ** -0.25
NEG = -1e30

EVEN_SIZES = (A_WIDTH, A_WIDTH, A_WIDTH, IDX_HEADS * IDX_DIM, IDX_DIM, IDX_HEADS,
              B_WIDTH, B_WIDTH, B_WIDTH)
EVEN_IN = sum(EVEN_SIZES)
EVEN_OUT = A_WIDTH + B_WIDTH
ODD_SIZES = (C_WIDTH, Q_LORA, KV_LORA, D_ROPE)
ODD_IN = sum(ODD_SIZES)
ODD_OUT = C_WIDTH + D_HEADS * D_V
N_EVEN = (DEPTH + 1) // 2
N_ODD = DEPTH // 2

kernel_name = "hybrid_streaming_dsa_shortconv_pool_mla"


def _split(z, sizes):
    offs = []
    acc = 0
    for s in sizes[:-1]:
        acc += s
        offs.append(acc)
    return jnp.split(z, offs, axis=-1)


def layer_norm(x, g, b):
    x32 = x.astype(jnp.float32)
    mu = jnp.mean(x32, axis=-1, keepdims=True)
    var = jnp.mean(jnp.square(x32 - mu), axis=-1, keepdims=True)
    y = (x32 - mu) * lax.rsqrt(var + LN_EPS)
    return (y * g.astype(jnp.float32) + b.astype(jnp.float32)).astype(x.dtype)


def rms_norm(x, g):
    x32 = x.astype(jnp.float32)
    y = x32 * lax.rsqrt(jnp.mean(jnp.square(x32), axis=-1, keepdims=True) + RMS_EPS)
    return (y * g.astype(jnp.float32)).astype(x.dtype)


def causal_dwconv(x, w):
    k_w = w.shape[0]
    seq = x.shape[1]
    xp = jnp.pad(x, ((0, 0), (k_w - 1, 0), (0, 0)))
    y = xp[:, 0:seq] * w[0]
    for k in range(1, k_w):
        y = y + xp[:, k:k + seq] * w[k]
    return y


def chunk_limit(pos):
    return (pos // CHUNK + 1) * CHUNK


def alibi_slopes(n):
    return jnp.asarray([2.0 ** (-8.0 * (i + 1) / n) for i in range(n)], dtype=jnp.float32)


def rope_tables(seq, dim):
    inv = ROPE_BASE ** (-jnp.arange(0, dim, 2, dtype=jnp.float32) / dim)
    ang = jnp.arange(seq, dtype=jnp.float32)[:, None] * inv[None, :]
    return jnp.cos(ang), jnp.sin(ang)


def apply_rope(x, cos, sin):
    half = x.shape[-1] // 2
    x1, x2 = x[..., :half], x[..., half:]
    c = cos[None, :, None, :].astype(x.dtype)
    s = sin[None, :, None, :].astype(x.dtype)
    return jnp.concatenate([x1 * c - x2 * s, x2 * c + x1 * s], axis=-1)


def dsa_attention(q, k, v, q_idx, k_idx, w_idx):
    bsz, seq = q.shape[0], q.shape[1]
    topk = min(TOPK_MAX, seq // 4)
    nblk = seq // Q_BLOCK
    key_pos = jnp.arange(seq)
    slopes = alibi_slopes(A_HEADS)
    w_scaled = w_idx * (IDX_HEADS ** -0.5 * IDX_DIM ** -0.5)
    scale = A_HEAD_DIM ** -0.5

    def block(i):
        start = i * Q_BLOCK
        qb = lax.dynamic_slice_in_dim(q, start, Q_BLOCK, axis=1)
        qib = lax.dynamic_slice_in_dim(q_idx, start, Q_BLOCK, axis=1)
        wb = lax.dynamic_slice_in_dim(w_scaled, start, Q_BLOCK, axis=1)
        qpos = start + jnp.arange(Q_BLOCK)
        limit = chunk_limit(qpos)
        rel = jax.nn.relu(jnp.einsum('bthd,bsd->bths', qib, k_idx))
        score = jnp.einsum('bths,bth->bts', rel, wb)
        admissible = key_pos[None, :] < limit[:, None]
        score = jnp.where(admissible[None], score, NEG)
        _, idx = lax.top_k(score, topk)
        valid = idx < limit[None, :, None]
        k_sel = jax.vmap(lambda kk, ii: kk[ii])(k, idx)
        v_sel = jax.vmap(lambda vv, ii: vv[ii])(v, idx)
        logits = jnp.einsum('bthd,btjhd->bthj', qb, k_sel).astype(jnp.float32) * scale
        dist = jnp.abs(qpos[None, :, None] - idx).astype(jnp.float32)
        logits = logits - slopes[None, None, :, None] * dist[:, :, None, :]
        logits = jnp.where(valid[:, :, None, :], logits, NEG)
        p = jax.nn.softmax(logits, axis=-1).astype(v.dtype)
        return jnp.einsum('bthj,btjhd->bthd', p, v_sel)

    out = lax.map(block, jnp.arange(nblk))
    return jnp.moveaxis(out, 0, 1).reshape(bsz, seq, A_WIDTH)


def multiscale_pool(u, pool_w, pool_scale):
    bsz, seq = u.shape[0], u.shape[1]
    ug = u.reshape(bsz, seq, C_GROUPS, C_GROUP_DIM)
    cs = jnp.cumsum(ug.astype(jnp.float32), axis=1)
    cs = jnp.pad(cs, ((0, 0), (1, 0), (0, 0), (0, 0)))
    pos = jnp.arange(seq)
    means = []
    for g, win in enumerate(POOL_WINDOWS):
        hi = cs[:, 1:, g]
        lo = cs[:, jnp.maximum(pos + 1 - win, 0), g]
        cnt = jnp.minimum(pos + 1, win).astype(jnp.float32)
        means.append((hi - lo) / cnt[None, :, None])
    pooled = jnp.stack(means, axis=2).astype(u.dtype) - ug
    mixed = jnp.einsum('bsgc,gcd->bsgd', pooled, pool_w)
    return mixed.reshape(bsz, seq, C_WIDTH) * pool_scale


def mla_attention(q_lat, kv_lat, k_rope_in, q_norm_g, w_uq, kv_norm_g, w_ukv):
    bsz, seq = q_lat.shape[0], q_lat.shape[1]
    q = (rms_norm(q_lat, q_norm_g) @ w_uq).reshape(bsz, seq, D_HEADS, D_NOPE + D_ROPE)
    q_nope, q_rope = q[..., :D_NOPE], q[..., D_NOPE:]
    kv = (rms_norm(kv_lat, kv_norm_g) @ w_ukv).reshape(bsz, seq, D_HEADS, D_NOPE + D_V)
    k_nope, v = kv[..., :D_NOPE], kv[..., D_NOPE:]
    cos, sin = rope_tables(seq, D_ROPE)
    q_rope = apply_rope(q_rope, cos, sin)
    k_rope = apply_rope(k_rope_in[:, :, None, :], cos, sin)
    qh = jnp.concatenate([q_nope, q_rope], axis=-1)
    kh = jnp.concatenate([k_nope, jnp.broadcast_to(k_rope, (bsz, seq, D_HEADS, D_ROPE))], axis=-1)
    scale = (D_NOPE + D_ROPE) ** -0.5
    key_pos = jnp.arange(seq)
    nblk = seq // Q_BLOCK

    def block(i):
        start = i * Q_BLOCK
        qb = lax.dynamic_slice_in_dim(qh, start, Q_BLOCK, axis=1)
        limit = chunk_limit(start + jnp.arange(Q_BLOCK))
        logits = jnp.einsum('bthd,bshd->bhts', qb, kh).astype(jnp.float32) * scale
        mask = key_pos[None, :] < limit[:, None]
        logits = jnp.where(mask[None, None], logits, NEG)
        p = jax.nn.softmax(logits, axis=-1).astype(v.dtype)
        return jnp.einsum('bhts,bshd->bthd', p, v)

    out = lax.map(block, jnp.arange(nblk))
    return jnp.moveaxis(out, 0, 1).reshape(bsz, seq, D_HEADS * D_V)


def even_mixer(h, w_in, conv_w, w_out):
    bsz, seq = h.shape[0], h.shape[1]
    z = h @ w_in
    q, k, v, qi, ki, wi, bg, cg, xb = _split(z, EVEN_SIZES)
    ya = dsa_attention(q.reshape(bsz, seq, A_HEADS, A_HEAD_DIM),
                       k.reshape(bsz, seq, A_HEADS, A_HEAD_DIM),
                       v.reshape(bsz, seq, A_HEADS, A_HEAD_DIM),
                       qi.reshape(bsz, seq, IDX_HEADS, IDX_DIM), ki, wi)
    yb = bg * causal_dwconv(cg * xb, conv_w)
    return jnp.concatenate([ya, yb], axis=-1) @ w_out


def odd_mixer(h, w_in, pool_w, pool_scale, q_norm_g, w_uq, kv_norm_g, w_ukv, w_out):
    z = h @ w_in
    u, q_lat, kv_lat, k_rope = _split(z, ODD_SIZES)
    yc = multiscale_pool(u, pool_w, pool_scale)
    yd = mla_attention(q_lat, kv_lat, k_rope, q_norm_g, w_uq, kv_norm_g, w_ukv)
    return jnp.concatenate([yc, yd], axis=-1) @ w_out


def conv_ffn(h, w_up, conv_w, w_down):
    u = causal_dwconv(h @ w_up, conv_w)
    val, gate = u[..., :D_FF], u[..., D_FF:]
    return (jax.nn.silu(gate) * val) @ w_down


def setup_inputs(seed: int = 0) -> dict:
    key = jax.random.key(seed)
    ks = iter(jax.random.split(key, 40))
    f32 = jnp.float32

    def nrm(shape, scale):
        return jax.random.normal(next(ks), shape, f32) * scale

    d = D_MODEL
    x = nrm((BATCH, SEQ, d), 1.0)
    c = nrm((BATCH, d), 1.0)
    ada_w = nrm((DEPTH, d, 6 * d), 0.1 * d ** -0.5)
    ada_b = nrm((DEPTH, 6 * d), 0.01)
    ln_mix_g = 1.0 + nrm((DEPTH, d), 0.01)
    ln_mix_b = nrm((DEPTH, d), 0.01)
    ln_ffn_g = 1.0 + nrm((DEPTH, d), 0.01)
    ln_ffn_b = nrm((DEPTH, d), 0.01)

    even_col_scale = jnp.concatenate([
        jnp.ones((2 * A_WIDTH,), f32), jnp.full((A_WIDTH,), DN_BETA, f32),
        jnp.ones((EVEN_IN - 3 * A_WIDTH,), f32)])
    ev_w_in = nrm((N_EVEN, d, EVEN_IN), d ** -0.5) * even_col_scale
    ev_conv_w = nrm((N_EVEN, SHORT_CONV, B_WIDTH), SHORT_CONV ** -0.5)
    ev_w_out = nrm((N_EVEN, EVEN_OUT, d), DN_BETA * EVEN_OUT ** -0.5)

    od_w_in = nrm((N_ODD, d, ODD_IN), d ** -0.5)
    pool_w = nrm((N_ODD, C_GROUPS, C_GROUP_DIM, C_GROUP_DIM), C_GROUP_DIM ** -0.5)
    pool_scale = 1.0 + nrm((N_ODD, C_WIDTH), 0.1)
    q_norm_g = 1.0 + nrm((N_ODD, Q_LORA), 0.01)
    w_uq = nrm((N_ODD, Q_LORA, D_HEADS * (D_NOPE + D_ROPE)), Q_LORA ** -0.5)
    kv_norm_g = 1.0 + nrm((N_ODD, KV_LORA), 0.01)
    ukv_col_scale = jnp.tile(jnp.concatenate([jnp.ones((D_NOPE,), f32),
                                              jnp.full((D_V,), DN_BETA, f32)]), D_HEADS)
    w_ukv = nrm((N_ODD, KV_LORA, D_HEADS * (D_NOPE + D_V)), KV_LORA ** -0.5) * ukv_col_scale
    od_w_out = nrm((N_ODD, ODD_OUT, d), DN_BETA * ODD_OUT ** -0.5)

    ffn_w_up = nrm((DEPTH, d, 2 * D_FF), d ** -0.5)
    ffn_conv_w = nrm((DEPTH, FFN_CONV, 2 * D_FF), FFN_CONV ** -0.5)
    ffn_w_down = nrm((DEPTH, D_FF, d), DN_BETA * D_FF ** -0.5)

    return {"x": x, "c": c, "ada_w": ada_w, "ada_b": ada_b,
            "ln_mix_g": ln_mix_g, "ln_mix_b": ln_mix_b, "ln_ffn_g": ln_ffn_g, "ln_ffn_b": ln_ffn_b,
            "ev_w_in": ev_w_in, "ev_conv_w": ev_conv_w, "ev_w_out": ev_w_out,
            "od_w_in": od_w_in, "pool_w": pool_w, "pool_scale": pool_scale,
            "q_norm_g": q_norm_g, "w_uq": w_uq, "kv_norm_g": kv_norm_g, "w_ukv": w_ukv,
            "od_w_out": od_w_out,
            "ffn_w_up": ffn_w_up, "ffn_conv_w": ffn_conv_w, "ffn_w_down": ffn_w_down}


def reference(x, c, ada_w, ada_b, ln_mix_g, ln_mix_b, ln_ffn_g, ln_ffn_b,
              ev_w_in, ev_conv_w, ev_w_out,
              od_w_in, pool_w, pool_scale, q_norm_g, w_uq, kv_norm_g, w_ukv, od_w_out,
              ffn_w_up, ffn_conv_w, ffn_w_down):
    cond = jax.nn.silu(c)
    for l in range(DEPTH):
        mod = (cond @ ada_w[l] + ada_b[l])[:, None, :]
        sh_m, sc_m, g_m, sh_f, sc_f, g_f = jnp.split(mod, 6, axis=-1)
        h = x * (1.0 + sc_m) + sh_m
        if l % 2 == 0:
            e = l // 2
            y = even_mixer(h, ev_w_in[e], ev_conv_w[e], ev_w_out[e])
        else:
            o = l // 2
            y = odd_mixer(h, od_w_in[o], pool_w[o], pool_scale[o], q_norm_g[o], w_uq[o],
                          kv_norm_g[o], w_ukv[o], od_w_out[o])
        x = layer_norm(DN_ALPHA * x + (1.0 + g_m) * y, ln_mix_g[l], ln_mix_b[l])
        h = x * (1.0 + sc_f) + sh_f
        y = conv_ffn(h, ffn_w_up[l], ffn_conv_w[l], ffn_w_down[l])
        x = layer_norm(DN_ALPHA * x + (1.0 + g_f) * y, ln_ffn_g[l], ln_ffn_b[l])
    return x
```

```python
import functools

import numpy as np
import jax
import jax.numpy as jnp
from jax import lax
from jax.experimental import pallas as pl
from jax.experimental.pallas import tpu as pltpu

D_MODEL = 1024
DEPTH = 2
CHUNK = 64
N_HEADS = 8
A_HEAD_DIM = 64
A_WIDTH = N_HEADS * A_HEAD_DIM
IDX_DIM = 64
TOPK = 256
B_WIDTH = 512
SHORT_CONV = 3
C_WIDTH = 512
POOL_WINDOWS = (2, 4, 8, 16)
C_GROUP_DIM = C_WIDTH // len(POOL_WINDOWS)
D_NOPE = 64
D_ROPE = 32
D_V = 64
Q_LORA = 384
KV_LORA = 256
ROPE_BASE = 10000.0
D_FF = 2816
FFN_CONV = 3
LN_EPS = 1e-5
RMS_EPS = 1e-6
DN_ALPHA = (2 * DEPTH) ** 0.25
NEG = -1e30
IDX_W_SCALE = N_HEADS ** -0.5 * IDX_DIM ** -0.5
MLA_SCALE = (D_NOPE + D_ROPE) ** -0.5
ALIBI_SLOPES = tuple(2.0 ** (-8.0 * (i + 1) / N_HEADS) for i in range(N_HEADS))

LANE = 128
SUBLANE = 8
VMEM_LIMIT_BYTES = 56 * 1024 * 1024

TM = 512
FFN_CHUNK = 256
TQ_SEL = 128
TK_SEL = 512
TQ_ATT = 256
TK_ATT = 512
POOL_HALO = 16

INT_MIN = -2 ** 31
F32 = jnp.float32
BF16 = jnp.bfloat16


def _params():
    return pltpu.CompilerParams(dimension_semantics=("arbitrary",),
                                vmem_limit_bytes=VMEM_LIMIT_BYTES)


def _dot(a, b):
    return jnp.dot(a, b, preferred_element_type=F32)


def _dot_nt(a, b):
    return lax.dot_general(a, b, (((1,), (1,)), ((), ())), preferred_element_type=F32)


def _layer_norm(z, g, b):
    mu = jnp.mean(z, axis=-1, keepdims=True)
    zc = z - mu
    var = jnp.mean(zc * zc, axis=-1, keepdims=True)
    return zc * lax.rsqrt(var + LN_EPS) * g + b


def _rms_norm(z, g):
    return z * lax.rsqrt(jnp.mean(z * z, axis=-1, keepdims=True) + RMS_EPS) * g


def _causal_conv3(u, prev, w):
    w0, w1, w2 = w[0:1], w[1:2], w[2:3]
    y = pltpu.roll(u, 2, 0) * w0 + pltpu.roll(u, 1, 0) * w1 + u * w2
    head = u[0:SUBLANE]
    r = lax.broadcasted_iota(jnp.int32, (SUBLANE, 1), 0)
    h1 = jnp.where(r == 0, prev[7:8], pltpu.roll(head, 1, 0))
    h2 = jnp.where(r == 0, prev[6:7], jnp.where(r == 1, prev[7:8], pltpu.roll(head, 2, 0)))
    yh = h2 * w0 + h1 * w1 + head * w2
    return jnp.concatenate([yh, y[SUBLANE:]], axis=0)


def _adaln_kernel(c_ref, w_ref, b_ref, o_ref):
    c = c_ref[...]
    cond = c * jax.nn.sigmoid(c)
    o_ref[0] = jnp.sum(cond * w_ref[0], axis=0, keepdims=True) + b_ref[0]


def _adaln(c, ada_w, ada_b):
    depth, d, n = ada_w.shape
    tn = 1536
    return pl.pallas_call(
        _adaln_kernel,
        grid=(depth, n // tn),
        in_specs=[pl.BlockSpec((d, 1), lambda l, j: (0, 0)),
                  pl.BlockSpec((1, d, tn), lambda l, j: (l, 0, j)),
                  pl.BlockSpec((1, 1, tn), lambda l, j: (l, 0, j))],
        out_specs=pl.BlockSpec((1, 1, tn), lambda l, j: (l, 0, j)),
        out_shape=jax.ShapeDtypeStruct((depth, 1, n), F32),
        compiler_params=pltpu.CompilerParams(dimension_semantics=("arbitrary", "arbitrary"),
                                             vmem_limit_bytes=VMEM_LIMIT_BYTES),
        name="adaln",
    )(c.reshape(d, 1), ada_w, ada_b.reshape(depth, 1, n))


_E_Q = (0, 1024)
_E_K = (1024, 1536)
_E_V = (1536, 2048)
_E_QI = (2048, 3072)
_E_KI = (3072, 3200)
_E_WI = (3200, 3328)
_E_BG = (3328, 3840)
_E_CG = (3840, 4352)
_E_XB = (4352, 4864)
_E_COLS = 4864


def _pad_heads_alternating(w):
    d = w.shape[0]
    w4 = w.reshape(d, N_HEADS // 2, 2, 1, 64)
    eye = jnp.eye(2, dtype=w.dtype).reshape(1, 1, 2, 2, 1)
    return (w4 * eye).reshape(d, N_HEADS * LANE)


def _even_weights(w_in):
    d = w_in.shape[0]
    a = A_WIDTH
    q = _pad_heads_alternating(w_in[:, 0:a] * (A_HEAD_DIM ** -0.5))
    k = w_in[:, a:2 * a]
    v = w_in[:, 2 * a:3 * a]
    qi = _pad_heads_alternating(w_in[:, 3 * a:4 * a])
    o = 4 * a
    ki = w_in[:, o:o + IDX_DIM]
    wi = w_in[:, o + IDX_DIM:o + IDX_DIM + N_HEADS]
    o = o + IDX_DIM + N_HEADS
    wi_pad = jnp.concatenate([wi, jnp.zeros((d, LANE - N_HEADS), w_in.dtype)], axis=1)
    rest = w_in[:, o:o + 3 * B_WIDTH]
    w = jnp.concatenate([q, k, v, qi, ki, ki, wi_pad, rest], axis=1)
    assert w.shape[1] == _E_COLS
    return w.astype(BF16)


def _even_in_kernel(x_ref, mod_ref, w_ref, cw_ref,
                    q_ref, k_ref, v_ref, qi_ref, ki_ref, wi_ref, yb_ref, carry_ref):
    d = D_MODEL

    @pl.when(pl.program_id(0) == 0)
    def _():
        carry_ref[...] = jnp.zeros_like(carry_ref)

    h = (x_ref[...] * (1.0 + mod_ref[:, d:2 * d]) + mod_ref[:, 0:d]).astype(BF16)

    def proj(cols):
        return _dot(h, w_ref[:, cols[0]:cols[1]])

    q_ref[...] = proj(_E_Q).astype(BF16)
    k_ref[...] = proj(_E_K).astype(BF16)
    v_ref[...] = proj(_E_V).astype(BF16)
    qi_ref[...] = proj(_E_QI).astype(BF16)
    ki_ref[...] = proj(_E_KI).astype(BF16)
    wi_ref[...] = proj(_E_WI) * IDX_W_SCALE
    g = proj(_E_CG) * proj(_E_XB)
    y = _causal_conv3(g, carry_ref[...], cw_ref[...])
    yb_ref[...] = (proj(_E_BG) * y).astype(BF16)
    carry_ref[...] = g[TM - SUBLANE:TM]


def _even_in(x, mod, w, conv_w):
    s, d = x.shape
    row = lambda n: pl.BlockSpec((TM, n), lambda i: (i, 0))
    full = lambda a: pl.BlockSpec(a.shape, lambda i: (0,) * a.ndim)
    outs = [(N_HEADS * LANE, BF16), (A_WIDTH, BF16), (A_WIDTH, BF16), (N_HEADS * LANE, BF16),
            (LANE, BF16), (LANE, F32), (B_WIDTH, BF16)]
    return pl.pallas_call(
        _even_in_kernel,
        grid=(s // TM,),
        in_specs=[row(d), full(mod), full(w), full(conv_w)],
        out_specs=[row(n) for n, _ in outs],
        out_shape=[jax.ShapeDtypeStruct((s, n), t) for n, t in outs],
        scratch_shapes=[pltpu.VMEM((SUBLANE, B_WIDTH), F32)],
        compiler_params=_params(),
        name="even_in",
    )(x, mod, w, conv_w)


def _select_kernel(qi_ref, ki_ref, w_ref, bias_ref, keys_ref, *, seq):
    i = pl.program_id(0)
    start = i * TQ_SEL
    n_tiles = (start + TQ_SEL + TK_SEL - 1) // TK_SEL
    tpos = start + lax.broadcasted_iota(jnp.int32, (TQ_SEL, 1), 0)
    limit = (tpos // CHUNK + 1) * CHUNK
    int_min = jnp.int32(INT_MIN)

    def score_tile(j, carry):
        off = pl.multiple_of(j * TK_SEL, TK_SEL)
        kt = ki_ref[pl.ds(off, TK_SEL), :]
        score = jnp.zeros((TQ_SEL, TK_SEL), F32)
        for h in range(N_HEADS):
            rel = _dot_nt(qi_ref[:, h * LANE:(h + 1) * LANE], kt)
            score = score + w_ref[:, h:h + 1] * jnp.maximum(rel, 0.0)
        bits = pltpu.bitcast(score, jnp.int32)
        key = bits ^ ((bits >> 31) & jnp.int32(0x7FFFFFFF))
        key = jnp.where(bits == int_min, 0, key)
        spos = off + lax.broadcasted_iota(jnp.int32, (1, TK_SEL), 1)
        keys_ref[:, pl.ds(off, TK_SEL)] = jnp.where(spos < limit, key, int_min)
        return carry

    lax.fori_loop(0, n_tiles, score_tile, 0)

    def count(pred):
        def body(j, acc):
            off = pl.multiple_of(j * TK_SEL, TK_SEL)
            kk = keys_ref[:, pl.ds(off, TK_SEL)]
            spos = off + lax.broadcasted_iota(jnp.int32, (1, TK_SEL), 1)
            c = jnp.where(pred(kk, spos), 1.0, 0.0)
            for g in range(TK_SEL // LANE):
                acc = acc + c[:, g * LANE:(g + 1) * LANE]
            return acc
        acc = lax.fori_loop(0, n_tiles, body, jnp.zeros((TQ_SEL, LANE), F32))
        return jnp.sum(acc, axis=1, keepdims=True)

    def bit_step(b, thr):
        cand = thr + jnp.left_shift(jnp.int32(1), 31 - b)
        cnt = count(lambda kk, spos: kk >= cand)
        return jnp.where(cnt >= float(TOPK), cand, thr)

    thr = lax.fori_loop(0, 32, bit_step, jnp.full((TQ_SEL, 1), INT_MIN, jnp.int32))

    cnt_ge = count(lambda kk, spos: kk >= thr)
    cnt_gt = count(lambda kk, spos: kk > thr)
    room = float(TOPK) - cnt_gt
    sentinel = thr == int_min
    has_ties = jnp.logical_and(cnt_ge > float(TOPK), jnp.logical_not(sentinel))

    def tie_cutoff():
        def cut_step(b, cut):
            cand = cut + jnp.left_shift(jnp.int32(1), 14 - b)
            cnt = count(lambda kk, spos: jnp.logical_and(kk == thr, spos < cand))
            return jnp.where(cnt <= room, cand, cut)
        return lax.fori_loop(0, 15, cut_step, jnp.zeros((TQ_SEL, 1), jnp.int32))

    any_ties = jnp.max(jnp.where(has_ties, 1.0, 0.0)) > 0.0
    cut = lax.cond(any_ties, tie_cutoff, lambda: jnp.full((TQ_SEL, 1), 32767, jnp.int32))
    cut = jnp.where(sentinel, 0, cut)

    def write_tile(j, carry):
        off = pl.multiple_of(j * TK_SEL, TK_SEL)
        kk = keys_ref[:, pl.ds(off, TK_SEL)]
        spos = off + lax.broadcasted_iota(jnp.int32, (1, TK_SEL), 1)
        sel = jnp.logical_or(kk > thr, jnp.logical_and(kk == thr, spos < cut))
        bias_ref[:, pl.ds(off, TK_SEL)] = jnp.where(sel, 0.0, NEG).astype(BF16)
        return carry

    lax.fori_loop(0, n_tiles, write_tile, 0)

    def fill_tile(j, carry):
        off = pl.multiple_of(j * TK_SEL, TK_SEL)
        bias_ref[:, pl.ds(off, TK_SEL)] = jnp.full((TQ_SEL, TK_SEL), NEG, BF16)
        return carry

    lax.fori_loop(n_tiles, seq // TK_SEL, fill_tile, 0)


def _select(qi, ki, wi):
    s = qi.shape[0]
    assert s % TK_SEL == 0 and s <= 32768 and s >= 2 * TOPK
    return pl.pallas_call(
        functools.partial(_select_kernel, seq=s),
        grid=(s // TQ_SEL,),
        in_specs=[pl.BlockSpec((TQ_SEL, N_HEADS * LANE), lambda i: (i, 0)),
                  pl.BlockSpec((s, LANE), lambda i: (0, 0)),
                  pl.BlockSpec((TQ_SEL, LANE), lambda i: (i, 0))],
        out_specs=pl.BlockSpec((TQ_SEL, s), lambda i: (i, 0)),
        out_shape=jax.ShapeDtypeStruct((s, s), BF16),
        scratch_shapes=[pltpu.VMEM((TQ_SEL, s), jnp.int32)],
        compiler_params=_params(),
        name="dsa_select",
    )(qi, ki, wi)


def _flash_kernel(qi_of, kj_of, *refs, k_per_head, alibi, use_bias):
    if use_bias:
        q_ref, k_ref, v_ref, bias_ref, o_ref, m_ref, l_ref, acc_ref = refs
    else:
        q_ref, k_ref, v_ref, o_ref, m_ref, l_ref, acc_ref = refs
    p = pl.program_id(0)
    qi = qi_of[p]
    kj = kj_of[p]
    last = ((qi + 1) * TQ_ATT - 1) // TK_ATT

    @pl.when(kj == 0)
    def _():
        m_ref[...] = jnp.full_like(m_ref, NEG)
        l_ref[...] = jnp.zeros_like(l_ref)
        acc_ref[...] = jnp.zeros_like(acc_ref)

    tpos = qi * TQ_ATT + lax.broadcasted_iota(jnp.int32, (TQ_ATT, 1), 0)
    spos = kj * TK_ATT + lax.broadcasted_iota(jnp.int32, (1, TK_ATT), 1)
    if use_bias:
        base = bias_ref[...].astype(F32)
    else:
        limit = (tpos // CHUNK + 1) * CHUNK
        base = jnp.where(spos < limit, 0.0, NEG)
    if alibi:
        dist = jnp.abs(tpos - spos).astype(F32)

    for h in range(N_HEADS):
        g = h if k_per_head else h // 2
        s = _dot_nt(q_ref[:, h * LANE:(h + 1) * LANE], k_ref[:, g * LANE:(g + 1) * LANE])
        if alibi:
            s = s + (base - ALIBI_SLOPES[h] * dist)
        else:
            s = s + base
        m_prev = m_ref[h]
        m_new = jnp.maximum(m_prev, jnp.max(s, axis=1, keepdims=True))
        alpha = jnp.exp(m_prev - m_new)
        pexp = jnp.exp(s - m_new[:, 0:1])
        l_ref[h] = alpha * l_ref[h] + jnp.sum(pexp, axis=1, keepdims=True)
        vg = v_ref[:, (h // 2) * LANE:(h // 2 + 1) * LANE]
        acc_ref[h] = alpha * acc_ref[h] + _dot(pexp.astype(BF16), vg)
        m_ref[h] = m_new

    @pl.when(kj == last)
    def _():
        lane = lax.broadcasted_iota(jnp.int32, (1, LANE), 1)
        for g in range(N_HEADS // 2):
            lo = acc_ref[2 * g] / l_ref[2 * g]
            hi = acc_ref[2 * g + 1] / l_ref[2 * g + 1]
            o_ref[:, g * LANE:(g + 1) * LANE] = jnp.where(lane < D_V, lo, hi).astype(BF16)


def _flash(q, k, v, bias, *, k_per_head, alibi):
    s = q.shape[0]
    nq = s // TQ_ATT
    pairs = [(i, j) for i in range(nq) for j in range(((i + 1) * TQ_ATT - 1) // TK_ATT + 1)]
    qi_of = jnp.asarray(np.array([a for a, _ in pairs], np.int32))
    kj_of = jnp.asarray(np.array([b for _, b in pairs], np.int32))
    use_bias = bias is not None
    kw = k.shape[1]
    in_specs = [pl.BlockSpec((TQ_ATT, N_HEADS * LANE), lambda p, qi, kj: (qi[p], 0)),
                pl.BlockSpec((TK_ATT, kw), lambda p, qi, kj: (kj[p], 0)),
                pl.BlockSpec((TK_ATT, v.shape[1]), lambda p, qi, kj: (kj[p], 0))]
    args = [q, k, v]
    if use_bias:
        in_specs.append(pl.BlockSpec((TQ_ATT, TK_ATT), lambda p, qi, kj: (qi[p], kj[p])))
        args.append(bias)
    kern = functools.partial(_flash_kernel, k_per_head=k_per_head, alibi=alibi, use_bias=use_bias)
    return pl.pallas_call(
        kern,
        grid_spec=pltpu.PrefetchScalarGridSpec(
            num_scalar_prefetch=2,
            grid=(len(pairs),),
            in_specs=in_specs,
            out_specs=pl.BlockSpec((TQ_ATT, N_HEADS * D_V), lambda p, qi, kj: (qi[p], 0)),
            scratch_shapes=[pltpu.VMEM((N_HEADS, TQ_ATT, LANE), F32),
                            pltpu.VMEM((N_HEADS, TQ_ATT, LANE), F32),
                            pltpu.VMEM((N_HEADS, TQ_ATT, LANE), F32)]),
        out_shape=jax.ShapeDtypeStruct((s, N_HEADS * D_V), BF16),
        compiler_params=_params(),
        name="flash_dsa" if use_bias else "flash_mla",
    )(qi_of, kj_of, *args)


def _out_ln_kernel(ya_ref, yb_ref, x_ref, mod_ref, w_ref, g_ref, b_ref, o_ref):
    d = D_MODEL
    na = ya_ref.shape[1]
    y = _dot(ya_ref[...], w_ref[0:na, :]) + _dot(yb_ref[...], w_ref[na:, :])
    z = DN_ALPHA * x_ref[...] + (1.0 + mod_ref[:, 2 * d:3 * d]) * y
    o_ref[...] = _layer_norm(z, g_ref[...], b_ref[...])


def _out_ln(ya, yb, x, mod, w, g, b):
    s, d = x.shape
    row = lambda n: pl.BlockSpec((TM, n), lambda i: (i, 0))
    full = lambda a: pl.BlockSpec(a.shape, lambda i: (0,) * a.ndim)
    g = g.reshape(1, d)
    b = b.reshape(1, d)
    return pl.pallas_call(
        _out_ln_kernel,
        grid=(s // TM,),
        in_specs=[row(ya.shape[1]), row(yb.shape[1]), row(d), full(mod), full(w), full(g), full(b)],
        out_specs=row(d),
        out_shape=jax.ShapeDtypeStruct((s, d), F32),
        compiler_params=_params(),
        name="out_ln",
    )(ya, yb, x, mod, w, g, b)


def _ffn_kernel(x_ref, mod_ref, wup_ref, cw_ref, wdn_ref, g_ref, b_ref, o_ref, carry_ref, act_ref):
    d = D_MODEL

    @pl.when(pl.program_id(0) == 0)
    def _():
        carry_ref[...] = jnp.zeros_like(carry_ref)

    x = x_ref[...]
    h = (x * (1.0 + mod_ref[:, 4 * d:5 * d]) + mod_ref[:, 3 * d:4 * d]).astype(BF16)
    for c in range(D_FF // FFN_CHUNK):
        va = c * FFN_CHUNK
        ga = D_FF + va
        uv = _dot(h, wup_ref[:, va:va + FFN_CHUNK])
        ug = _dot(h, wup_ref[:, ga:ga + FFN_CHUNK])
        val = _causal_conv3(uv, carry_ref[:, va:va + FFN_CHUNK], cw_ref[:, va:va + FFN_CHUNK])
        gate = _causal_conv3(ug, carry_ref[:, ga:ga + FFN_CHUNK], cw_ref[:, ga:ga + FFN_CHUNK])
        carry_ref[:, va:va + FFN_CHUNK] = uv[TM - SUBLANE:TM]
        carry_ref[:, ga:ga + FFN_CHUNK] = ug[TM - SUBLANE:TM]
        act_ref[:, va:va + FFN_CHUNK] = (gate * jax.nn.sigmoid(gate) * val).astype(BF16)
    y = _dot(act_ref[...], wdn_ref[...])
    z = DN_ALPHA * x + (1.0 + mod_ref[:, 5 * d:6 * d]) * y
    o_ref[...] = _layer_norm(z, g_ref[...], b_ref[...])


def _ffn(x, mod, w_up, conv_w, w_down, g, b):
    s, d = x.shape
    assert D_FF % FFN_CHUNK == 0
    row = lambda n: pl.BlockSpec((TM, n), lambda i: (i, 0))
    full = lambda a: pl.BlockSpec(a.shape, lambda i: (0,) * a.ndim)
    once = lambda a: pl.BlockSpec(a.shape, lambda i: (0,) * a.ndim, pipeline_mode=pl.Buffered(1))
    g = g.reshape(1, d)
    b = b.reshape(1, d)
    return pl.pallas_call(
        _ffn_kernel,
        grid=(s // TM,),
        in_specs=[row(d), full(mod), once(w_up), full(conv_w), once(w_down), full(g), full(b)],
        out_specs=row(d),
        out_shape=jax.ShapeDtypeStruct((s, d), F32),
        scratch_shapes=[pltpu.VMEM((SUBLANE, 2 * D_FF), F32), pltpu.VMEM((TM, D_FF), BF16)],
        compiler_params=_params(),
        name="conv_ffn",
    )(x, mod, w_up, conv_w, w_down, g, b)


_O_U = (0, 512)
_O_QLAT = (512, 896)
_O_KVLAT = (896, 1152)
_O_KR = (1152, 1280)
_O_KR_ROT = (1280, 1408)
_O_COLS = 1408


def _rotate_half_cols(w):
    half = w.shape[-1] // 2
    return jnp.concatenate([-w[..., half:], w[..., :half]], axis=-1)


def _odd_weights(w_in, w_uq, w_ukv):
    d = w_in.shape[0]
    o = C_WIDTH + Q_LORA + KV_LORA
    kr = w_in[:, o:o + D_ROPE]
    zl = jnp.zeros((d, D_NOPE), w_in.dtype)
    zr = jnp.zeros((d, LANE - D_NOPE - D_ROPE), w_in.dtype)
    w1 = jnp.concatenate([w_in[:, 0:o], zl, kr, zr, zl, _rotate_half_cols(kr), zr], axis=1)
    assert w1.shape[1] == _O_COLS
    dq = D_NOPE + D_ROPE
    uq = w_uq.reshape(Q_LORA, N_HEADS, dq)
    padq = jnp.zeros((Q_LORA, N_HEADS, LANE - dq), w_uq.dtype)
    wq1 = jnp.concatenate([uq, padq], axis=2).reshape(Q_LORA, N_HEADS * LANE)
    wq2 = jnp.concatenate([jnp.zeros((Q_LORA, N_HEADS, D_NOPE), w_uq.dtype),
                           _rotate_half_cols(uq[:, :, D_NOPE:]), padq], axis=2
                          ).reshape(Q_LORA, N_HEADS * LANE)
    ukv = w_ukv.reshape(KV_LORA, N_HEADS, D_NOPE + D_V)
    wk = jnp.concatenate([ukv[:, :, :D_NOPE], jnp.zeros((KV_LORA, N_HEADS, LANE - D_NOPE), w_ukv.dtype)],
                         axis=2).reshape(KV_LORA, N_HEADS * LANE)
    wv = ukv[:, :, D_NOPE:].reshape(KV_LORA, N_HEADS * D_V)
    return w1.astype(BF16), wq1.astype(BF16), wq2.astype(BF16), wk.astype(BF16), wv.astype(BF16)


def _rope_lane_tables(seq):
    inv = ROPE_BASE ** (-jnp.arange(0, D_ROPE, 2, dtype=F32) / D_ROPE)
    ang = jnp.arange(seq, dtype=F32)[:, None] * inv[None, :]
    cos, sin = jnp.cos(ang), jnp.sin(ang)
    pad = jnp.zeros((seq, LANE - D_NOPE - D_ROPE), F32)
    cos_t = jnp.concatenate([jnp.ones((seq, D_NOPE), F32), cos, cos, pad], axis=1)
    sin_t = jnp.concatenate([jnp.zeros((seq, D_NOPE), F32), sin, sin, pad], axis=1)
    return cos_t, sin_t


def _odd_in_kernel(x_ref, mod_ref, w1_ref, wq1_ref, wq2_ref, wk_ref, wv_ref, pw_ref, ps_ref,
                   qg_ref, kvg_ref, cos_ref, sin_ref,
                   yc_ref, q_ref, k_ref, v_ref, carry_ref):
    d = D_MODEL
    i = pl.program_id(0)

    @pl.when(i == 0)
    def _():
        carry_ref[...] = jnp.zeros_like(carry_ref)

    h = (x_ref[...] * (1.0 + mod_ref[:, d:2 * d]) + mod_ref[:, 0:d]).astype(BF16)

    def proj(cols):
        return _dot(h, w1_ref[:, cols[0]:cols[1]])

    cos = cos_ref[...]
    sin = sin_ref[...]

    u = proj(_O_U)
    ext = jnp.concatenate([carry_ref[...], u], axis=0)
    pos = i * TM + lax.broadcasted_iota(jnp.int32, (TM, 1), 0)
    for g, win in enumerate(POOL_WINDOWS):
        sl = slice(g * C_GROUP_DIM, (g + 1) * C_GROUP_DIM)
        acc = ext[:, sl]
        shift = 1
        while shift < win:
            acc = acc + pltpu.roll(acc, shift, 0)
            shift *= 2
        cnt = jnp.minimum(pos + 1, win).astype(F32)
        pooled = acc[POOL_HALO:] / cnt - u[:, sl]
        mixed = _dot(pooled.astype(BF16), pw_ref[g])
        yc_ref[:, sl] = (mixed * ps_ref[:, sl]).astype(BF16)
    carry_ref[...] = u[TM - POOL_HALO:TM]

    r = _rms_norm(proj(_O_QLAT), qg_ref[...]).astype(BF16)
    qa = _dot(r, wq1_ref[...])
    qb = _dot(r, wq2_ref[...])
    kr = proj(_O_KR) * cos + proj(_O_KR_ROT) * sin
    rk = _rms_norm(proj(_O_KVLAT), kvg_ref[...]).astype(BF16)
    kn = _dot(rk, wk_ref[...])
    for hd in range(N_HEADS):
        sl = slice(hd * LANE, (hd + 1) * LANE)
        q_ref[:, sl] = ((qa[:, sl] * cos + qb[:, sl] * sin) * MLA_SCALE).astype(BF16)
        k_ref[:, sl] = (kn[:, sl] + kr).astype(BF16)
    v_ref[...] = _dot(rk, wv_ref[...]).astype(BF16)


def _odd_in(x, mod, w1, wq1, wq2, wk, wv, pool_w, pool_scale, q_norm_g, kv_norm_g, cos_t, sin_t):
    s, d = x.shape
    row = lambda n: pl.BlockSpec((TM, n), lambda i: (i, 0))
    full = lambda a: pl.BlockSpec(a.shape, lambda i: (0,) * a.ndim)
    consts = [w1, wq1, wq2, wk, wv, pool_w.astype(BF16), pool_scale.reshape(1, C_WIDTH),
              q_norm_g.reshape(1, Q_LORA), kv_norm_g.reshape(1, KV_LORA)]
    outs = [(C_WIDTH, BF16), (N_HEADS * LANE, BF16), (N_HEADS * LANE, BF16), (N_HEADS * D_V, BF16)]
    return pl.pallas_call(
        _odd_in_kernel,
        grid=(s // TM,),
        in_specs=[row(d), full(mod)] + [full(a) for a in consts] + [row(LANE), row(LANE)],
        out_specs=[row(n) for n, _ in outs],
        out_shape=[jax.ShapeDtypeStruct((s, n), t) for n, t in outs],
        scratch_shapes=[pltpu.VMEM((POOL_HALO, C_WIDTH), F32)],
        compiler_params=_params(),
        name="odd_in",
    )(x, mod, *consts, cos_t, sin_t)


def kernel(x, c, ada_w, ada_b, ln_mix_g, ln_mix_b, ln_ffn_g, ln_ffn_b, ev_w_in, ev_conv_w, ev_w_out,
           od_w_in, pool_w, pool_scale, q_norm_g, w_uq, kv_norm_g, w_ukv, od_w_out,
           ffn_w_up, ffn_conv_w, ffn_w_down):
    bsz, seq, d = x.shape
    assert bsz == 1 and d == D_MODEL and seq % TM == 0 and seq % TK_ATT == 0
    assert SHORT_CONV == 3 and FFN_CONV == 3
    xs = x.reshape(seq, d)
    mods = _adaln(c, ada_w, ada_b)
    for l in range(DEPTH):
        mod = mods[l]
        if l % 2 == 0:
            e = l // 2
            q, k, v, qi, ki, wi, yb = _even_in(xs, mod, _even_weights(ev_w_in[e]), ev_conv_w[e])
            bias = _select(qi, ki, wi)
            ya = _flash(q, k, v, bias, k_per_head=False, alibi=True)
            xs = _out_ln(ya, yb, xs, mod, ev_w_out[e].astype(BF16), ln_mix_g[l], ln_mix_b[l])
        else:
            o = l // 2
            cos_t, sin_t = _rope_lane_tables(seq)
            ws = _odd_weights(od_w_in[o], w_uq[o], w_ukv[o])
            yc, q, k, v = _odd_in(xs, mod, *ws, pool_w[o], pool_scale[o], q_norm_g[o], kv_norm_g[o],
                                  cos_t, sin_t)
            yd = _flash(q, k, v, None, k_per_head=True, alibi=False)
            xs = _out_ln(yc, yd, xs, mod, od_w_out[o].astype(BF16), ln_mix_g[l], ln_mix_b[l])
        xs = _ffn(xs, mod, ffn_w_up[l].astype(BF16), ffn_conv_w[l], ffn_w_down[l].astype(BF16),
                  ln_ffn_g[l], ln_ffn_b[l])
    return xs.reshape(bsz, seq, d)
```

```python
import functools

import numpy as np
import jax
import jax.numpy as jnp
from jax import lax
from jax.experimental import pallas as pl
from jax.experimental.pallas import tpu as pltpu

D_MODEL = 1024
DEPTH = 2
CHUNK = 64
N_HEADS = 8
A_HEAD_DIM = 64
A_WIDTH = N_HEADS * A_HEAD_DIM
IDX_DIM = 64
TOPK = 256
B_WIDTH = 512
SHORT_CONV = 3
C_WIDTH = 512
POOL_WINDOWS = (2, 4, 8, 16)
C_GROUP_DIM = C_WIDTH // len(POOL_WINDOWS)
D_NOPE = 64
D_ROPE = 32
D_V = 64
Q_LORA = 384
KV_LORA = 256
ROPE_BASE = 10000.0
D_FF = 2816
FFN_CONV = 3
LN_EPS = 1e-5
RMS_EPS = 1e-6
DN_ALPHA = (2 * DEPTH) ** 0.25
NEG = -1e30
IDX_W_SCALE = N_HEADS ** -0.5 * IDX_DIM ** -0.5
LOG2E = 1.4426950408889634
DSA_Q_SCALE = A_HEAD_DIM ** -0.5 * LOG2E
MLA_Q_SCALE = (D_NOPE + D_ROPE) ** -0.5 * LOG2E
ALIBI_SLOPES = tuple(2.0 ** (-8.0 * (i + 1) / N_HEADS) for i in range(N_HEADS))

LANE = 128
SUBLANE = 8
BF16_ROWS = 16
VMEM_LIMIT_BYTES = 56 * 1024 * 1024

TM = 512
FFN_CHUNK = 256
TQ_ATT = 256
TK_SEL = 512
TK_ATT = 512
QK_AHEAD = 4
POOL_HALO = 16

INT_MIN = -2 ** 31
F32 = jnp.float32
BF16 = jnp.bfloat16


def _params():
    return pltpu.CompilerParams(dimension_semantics=("arbitrary",),
                                vmem_limit_bytes=VMEM_LIMIT_BYTES)


def _dot(a, b):
    return jnp.dot(a, b, preferred_element_type=F32)


def _dot_nt(a, b):
    return lax.dot_general(a, b, (((1,), (1,)), ((), ())), preferred_element_type=F32)


def _layer_norm(z, g, b):
    mu = jnp.mean(z, axis=-1, keepdims=True)
    zc = z - mu
    var = jnp.mean(zc * zc, axis=-1, keepdims=True)
    return zc * lax.rsqrt(var + LN_EPS) * g + b


def _rms_norm(z, g):
    return z * lax.rsqrt(jnp.mean(z * z, axis=-1, keepdims=True) + RMS_EPS) * g


def _causal_conv3(u, prev, w):
    w0, w1, w2 = w[0:1], w[1:2], w[2:3]
    y = pltpu.roll(u, 2, 0) * w0 + pltpu.roll(u, 1, 0) * w1 + u * w2
    head = u[0:SUBLANE]
    r = lax.broadcasted_iota(jnp.int32, (SUBLANE, 1), 0)
    h1 = jnp.where(r == 0, prev[7:8], pltpu.roll(head, 1, 0))
    h2 = jnp.where(r == 0, prev[6:7], jnp.where(r == 1, prev[7:8], pltpu.roll(head, 2, 0)))
    yh = h2 * w0 + h1 * w1 + head * w2
    return jnp.concatenate([yh, y[SUBLANE:]], axis=0)


def _fold_rows(c):
    r = c.shape[0]
    while r > SUBLANE:
        r //= 2
        c = c[0:r] + c[r:2 * r]
    return c


def _adaln_kernel(c_ref, w_ref, b_ref, o_ref):
    c = c_ref[...]
    cond = c * jax.nn.sigmoid(c)
    o_ref[0] = jnp.sum(cond * w_ref[0], axis=0, keepdims=True) + b_ref[0]


def _adaln(c, ada_w, ada_b):
    depth, d, n = ada_w.shape
    tn = 1536
    return pl.pallas_call(
        _adaln_kernel,
        grid=(depth, n // tn),
        in_specs=[pl.BlockSpec((d, 1), lambda l, j: (0, 0)),
                  pl.BlockSpec((1, d, tn), lambda l, j: (l, 0, j)),
                  pl.BlockSpec((1, 1, tn), lambda l, j: (l, 0, j))],
        out_specs=pl.BlockSpec((1, 1, tn), lambda l, j: (l, 0, j)),
        out_shape=jax.ShapeDtypeStruct((depth, 1, n), F32),
        compiler_params=pltpu.CompilerParams(dimension_semantics=("arbitrary", "arbitrary"),
                                             vmem_limit_bytes=VMEM_LIMIT_BYTES),
        name="adaln",
    )(c.reshape(d, 1), ada_w, ada_b.reshape(depth, 1, n))


_E_Q = (0, 1024)
_E_K = (1024, 1536)
_E_QI = (1536, 2560)
_E_KI = (2560, 2688)
_E_BG = (2688, 3200)
_E_CG = (3200, 3712)
_E_XB = (3712, 4224)
_E_COLS = 4224
_ET_V = (0, 512)
_ET_WI = (512, 528)
_ET_ROWS = 528


def _pad_heads_alternating(w):
    d = w.shape[0]
    w4 = w.reshape(d, N_HEADS // 2, 2, 1, 64)
    eye = jnp.eye(2, dtype=w.dtype).reshape(1, 1, 2, 2, 1)
    return (w4 * eye).reshape(d, N_HEADS * LANE)


def _even_weights(w_in):
    d = w_in.shape[0]
    a = A_WIDTH
    q = _pad_heads_alternating(w_in[:, 0:a])
    k = w_in[:, a:2 * a]
    v = w_in[:, 2 * a:3 * a]
    qi = _pad_heads_alternating(w_in[:, 3 * a:4 * a])
    o = 4 * a
    ki = w_in[:, o:o + IDX_DIM]
    wi = w_in[:, o + IDX_DIM:o + IDX_DIM + N_HEADS]
    o = o + IDX_DIM + N_HEADS
    rest = w_in[:, o:o + 3 * B_WIDTH]
    w = jnp.concatenate([q, k, qi, ki, ki, rest], axis=1)
    assert w.shape[1] == _E_COLS
    wt = jnp.concatenate([v.T, wi.T, jnp.zeros((BF16_ROWS - N_HEADS, d), w_in.dtype)], axis=0)
    assert wt.shape[0] == _ET_ROWS
    return w.astype(BF16), wt.astype(BF16)


def _even_in_kernel(x_ref, mod_ref, w_ref, wt_ref, cw_ref,
                    q_ref, k_ref, vt_ref, qi_ref, ki_ref, wit_ref, yb_ref, carry_ref):
    d = D_MODEL

    @pl.when(pl.program_id(0) == 0)
    def _():
        carry_ref[...] = jnp.zeros_like(carry_ref)

    h = (x_ref[...] * (1.0 + mod_ref[:, d:2 * d]) + mod_ref[:, 0:d]).astype(BF16)

    def proj(cols):
        return _dot(h, w_ref[:, cols[0]:cols[1]])

    def proj_t(rows):
        return _dot_nt(wt_ref[rows[0]:rows[1], :], h)

    q_ref[...] = (proj(_E_Q) * DSA_Q_SCALE).astype(BF16)
    k_ref[...] = proj(_E_K).astype(BF16)
    vt_ref[...] = proj_t(_ET_V).astype(BF16)
    qi_ref[...] = proj(_E_QI).astype(BF16)
    ki_ref[...] = proj(_E_KI).astype(BF16)
    wit_ref[...] = proj_t(_ET_WI) * IDX_W_SCALE
    g = proj(_E_CG) * proj(_E_XB)
    y = _causal_conv3(g, carry_ref[...], cw_ref[...])
    yb_ref[...] = (proj(_E_BG) * y).astype(BF16)
    carry_ref[...] = g[TM - SUBLANE:TM]


def _even_in(x, mod, w, wt, conv_w):
    s, d = x.shape
    row = lambda n: pl.BlockSpec((TM, n), lambda i: (i, 0))
    col = lambda n: pl.BlockSpec((n, TM), lambda i: (0, i))
    full = lambda a: pl.BlockSpec(a.shape, lambda i: (0,) * a.ndim)
    rows_out = lambda n, t: (row(n), jax.ShapeDtypeStruct((s, n), t))
    cols_out = lambda n, t: (col(n), jax.ShapeDtypeStruct((n, s), t))
    outs = [rows_out(N_HEADS * LANE, BF16), rows_out(A_WIDTH, BF16), cols_out(A_WIDTH, BF16),
            rows_out(N_HEADS * LANE, BF16), rows_out(LANE, BF16), cols_out(BF16_ROWS, F32),
            rows_out(B_WIDTH, BF16)]
    return pl.pallas_call(
        _even_in_kernel,
        grid=(s // TM,),
        in_specs=[row(d), full(mod), full(w), full(wt), full(conv_w)],
        out_specs=[o[0] for o in outs],
        out_shape=[o[1] for o in outs],
        scratch_shapes=[pltpu.VMEM((SUBLANE, B_WIDTH), F32)],
        compiler_params=_params(),
        name="even_in",
    )(x, mod, w, wt, conv_w)


def _select_kernel(qi_ref, ki_ref, wt_ref, bias_ref, keys_ref, *, seq):
    tq, tk = TQ_ATT, TK_SEL
    i = pl.program_id(0)
    start = i * tq
    n_tiles = (start + tq + tk - 1) // tk
    tpos = start + lax.broadcasted_iota(jnp.int32, (1, tq), 1)
    limit = (tpos // CHUNK + 1) * CHUNK
    int_min = jnp.int32(INT_MIN)

    def key_pos(off):
        return off + lax.broadcasted_iota(jnp.int32, (tk, tq), 0)

    def score_tile(j, carry):
        off = pl.multiple_of(j * tk, tk)
        kt = ki_ref[pl.ds(off, tk), :]
        score = jnp.zeros((tk, tq), F32)
        for h in range(N_HEADS):
            rel = _dot_nt(kt, qi_ref[:, h * LANE:(h + 1) * LANE])
            score = score + wt_ref[h:h + 1, :] * jnp.maximum(rel, 0.0)
        bits = pltpu.bitcast(score, jnp.int32)
        key = bits ^ ((bits >> 31) & jnp.int32(0x7FFFFFFF))
        key = jnp.where(bits == int_min, 0, key)
        keys_ref[pl.ds(off, tk), :] = jnp.where(key_pos(off) < limit, key, int_min)
        return carry

    lax.fori_loop(0, n_tiles, score_tile, 0)

    def count(pred):
        def body(j, acc):
            off = pl.multiple_of(j * tk, tk)
            kk = keys_ref[pl.ds(off, tk), :]
            return acc + _fold_rows(jnp.where(pred(kk, off), 1.0, 0.0))
        acc = lax.fori_loop(0, n_tiles, body, jnp.zeros((SUBLANE, tq), F32))
        return jnp.sum(acc, axis=0, keepdims=True)

    def bit_step(b, thr):
        cand = thr + jnp.left_shift(jnp.int32(1), 31 - b)
        cnt = count(lambda kk, off: kk >= cand)
        return jnp.where(cnt >= float(TOPK), cand, thr)

    thr = lax.fori_loop(0, 32, bit_step, jnp.full((1, tq), INT_MIN, jnp.int32))

    cnt_ge = count(lambda kk, off: kk >= thr)
    cnt_gt = count(lambda kk, off: kk > thr)
    room = float(TOPK) - cnt_gt
    sentinel = thr == int_min
    has_ties = jnp.where(sentinel, 0.0, jnp.where(cnt_ge > float(TOPK), 1.0, 0.0))

    def tie_cutoff():
        def cut_step(b, cut):
            cand = cut + jnp.left_shift(jnp.int32(1), 14 - b)
            cnt = count(lambda kk, off: jnp.where(kk == thr, key_pos(off), 32767) < cand)
            return jnp.where(cnt <= room, cand, cut)
        return lax.fori_loop(0, 15, cut_step, jnp.zeros((1, tq), jnp.int32))

    cut = lax.cond(jnp.max(has_ties) > 0.0, tie_cutoff,
                   lambda: jnp.full((1, tq), 32767, jnp.int32))
    cut = jnp.where(sentinel, 0, cut)

    def write_tile(j, carry):
        off = pl.multiple_of(j * tk, tk)
        kk = keys_ref[pl.ds(off, tk), :]
        tie = jnp.where(key_pos(off) < cut, 0.0, NEG)
        bias = jnp.where(kk > thr, 0.0, jnp.where(kk == thr, tie, NEG))
        bias_ref[pl.ds(off, tk), :] = bias.astype(BF16)
        return carry

    lax.fori_loop(0, n_tiles, write_tile, 0)

    def fill_tile(j, carry):
        off = pl.multiple_of(j * tk, tk)
        bias_ref[pl.ds(off, tk), :] = jnp.full((tk, tq), NEG, BF16)
        return carry

    lax.fori_loop(n_tiles, seq // tk, fill_tile, 0)


def _select(qi, ki, wit):
    s = qi.shape[0]
    assert s % TK_SEL == 0 and s % TQ_ATT == 0 and 2 * TOPK <= s <= 32767
    return pl.pallas_call(
        functools.partial(_select_kernel, seq=s),
        grid=(s // TQ_ATT,),
        in_specs=[pl.BlockSpec((TQ_ATT, N_HEADS * LANE), lambda i: (i, 0)),
                  pl.BlockSpec((s, LANE), lambda i: (0, 0)),
                  pl.BlockSpec((BF16_ROWS, TQ_ATT), lambda i: (0, i))],
        out_specs=pl.BlockSpec((s, TQ_ATT), lambda i: (0, i)),
        out_shape=jax.ShapeDtypeStruct((s, s), BF16),
        scratch_shapes=[pltpu.VMEM((s, TQ_ATT), jnp.int32)],
        compiler_params=_params(),
        name="dsa_select",
    )(qi, ki, wit)


def _flash_kernel(qi_of, kj_of, *refs, k_per_head, alibi, use_bias):
    if use_bias:
        q_ref, k_ref, vt_ref, bias_ref, o_ref, m_ref, l_ref, acc_ref = refs
    else:
        q_ref, k_ref, vt_ref, o_ref, m_ref, l_ref, acc_ref = refs
    tq, tk = TQ_ATT, TK_ATT
    p = pl.program_id(0)
    qi = qi_of[p]
    kj = kj_of[p]
    last = ((qi + 1) * tq - 1) // tk

    @pl.when(kj == 0)
    def _():
        m_ref[...] = jnp.full_like(m_ref, NEG)
        l_ref[...] = jnp.zeros_like(l_ref)
        acc_ref[...] = jnp.zeros_like(acc_ref)

    tpos = qi * tq + lax.broadcasted_iota(jnp.int32, (tk, tq), 1)
    spos = kj * tk + lax.broadcasted_iota(jnp.int32, (tk, tq), 0)
    if use_bias:
        base = bias_ref[...].astype(F32)
    else:
        limit = (tpos // CHUNK + 1) * CHUNK
        base = jnp.where(spos < limit, 0.0, NEG)
    if alibi:
        dist = jnp.abs(tpos - spos).astype(F32)

    def logits(h):
        g = h if k_per_head else h // 2
        s = _dot_nt(k_ref[:, g * LANE:(g + 1) * LANE], q_ref[:, h * LANE:(h + 1) * LANE])
        if alibi:
            return s + (base - (ALIBI_SLOPES[h] * LOG2E) * dist)
        return s + base

    pending = [logits(h) for h in range(QK_AHEAD)]
    for h in range(N_HEADS):
        rows = slice(h * D_V, (h + 1) * D_V)
        if h + QK_AHEAD < N_HEADS:
            pending.append(logits(h + QK_AHEAD))
        s = pending.pop(0)
        m_prev = m_ref[h:h + 1, :]
        m_new = jnp.maximum(m_prev, jnp.max(s, axis=0, keepdims=True))
        alpha = jnp.exp2(m_prev - m_new)
        pexp = jnp.exp2(s - m_new)
        l_ref[h:h + 1, :] = alpha * l_ref[h:h + 1, :] + jnp.sum(pexp, axis=0, keepdims=True)
        acc_ref[rows, :] = alpha * acc_ref[rows, :] + _dot(vt_ref[rows, :], pexp.astype(BF16))
        m_ref[h:h + 1, :] = m_new

    @pl.when(kj == last)
    def _():
        outs = [acc_ref[h * D_V:(h + 1) * D_V, :] / l_ref[h:h + 1, :] for h in range(N_HEADS)]
        o_ref[...] = jnp.concatenate(outs, axis=0).T.astype(BF16)


def _flash(q, k, vt, bias, *, k_per_head, alibi):
    s = q.shape[0]
    nq = s // TQ_ATT
    pairs = [(i, j) for i in range(nq) for j in range(((i + 1) * TQ_ATT - 1) // TK_ATT + 1)]
    qi_of = jnp.asarray(np.array([a for a, _ in pairs], np.int32))
    kj_of = jnp.asarray(np.array([b for _, b in pairs], np.int32))
    use_bias = bias is not None
    in_specs = [pl.BlockSpec((TQ_ATT, N_HEADS * LANE), lambda p, qi, kj: (qi[p], 0)),
                pl.BlockSpec((TK_ATT, k.shape[1]), lambda p, qi, kj: (kj[p], 0)),
                pl.BlockSpec((N_HEADS * D_V, TK_ATT), lambda p, qi, kj: (0, kj[p]))]
    args = [q, k, vt]
    if use_bias:
        in_specs.append(pl.BlockSpec((TK_ATT, TQ_ATT), lambda p, qi, kj: (kj[p], qi[p])))
        args.append(bias)
    kern = functools.partial(_flash_kernel, k_per_head=k_per_head, alibi=alibi, use_bias=use_bias)
    return pl.pallas_call(
        kern,
        grid_spec=pltpu.PrefetchScalarGridSpec(
            num_scalar_prefetch=2,
            grid=(len(pairs),),
            in_specs=in_specs,
            out_specs=pl.BlockSpec((TQ_ATT, N_HEADS * D_V), lambda p, qi, kj: (qi[p], 0)),
            scratch_shapes=[pltpu.VMEM((N_HEADS, TQ_ATT), F32),
                            pltpu.VMEM((N_HEADS, TQ_ATT), F32),
                            pltpu.VMEM((N_HEADS * D_V, TQ_ATT), F32)]),
        out_shape=jax.ShapeDtypeStruct((s, N_HEADS * D_V), BF16),
        compiler_params=_params(),
        name="flash_dsa" if use_bias else "flash_mla",
    )(qi_of, kj_of, *args)


def _out_ln_kernel(ya_ref, yb_ref, x_ref, mod_ref, w_ref, g_ref, b_ref, o_ref):
    d = D_MODEL
    na = ya_ref.shape[1]
    y = _dot(ya_ref[...], w_ref[0:na, :]) + _dot(yb_ref[...], w_ref[na:, :])
    z = DN_ALPHA * x_ref[...] + (1.0 + mod_ref[:, 2 * d:3 * d]) * y
    o_ref[...] = _layer_norm(z, g_ref[...], b_ref[...])


def _out_ln(ya, yb, x, mod, w, g, b):
    s, d = x.shape
    row = lambda n: pl.BlockSpec((TM, n), lambda i: (i, 0))
    full = lambda a: pl.BlockSpec(a.shape, lambda i: (0,) * a.ndim)
    g = g.reshape(1, d)
    b = b.reshape(1, d)
    return pl.pallas_call(
        _out_ln_kernel,
        grid=(s // TM,),
        in_specs=[row(ya.shape[1]), row(yb.shape[1]), row(d), full(mod), full(w), full(g), full(b)],
        out_specs=row(d),
        out_shape=jax.ShapeDtypeStruct((s, d), F32),
        compiler_params=_params(),
        name="out_ln",
    )(ya, yb, x, mod, w, g, b)


def _ffn_kernel(x_ref, mod_ref, wup_ref, cw_ref, wdn_ref, g_ref, b_ref, o_ref, carry_ref, act_ref):
    d = D_MODEL

    @pl.when(pl.program_id(0) == 0)
    def _():
        carry_ref[...] = jnp.zeros_like(carry_ref)

    x = x_ref[...]
    h = (x * (1.0 + mod_ref[:, 4 * d:5 * d]) + mod_ref[:, 3 * d:4 * d]).astype(BF16)
    for c in range(D_FF // FFN_CHUNK):
        va = c * FFN_CHUNK
        ga = D_FF + va
        uv = _dot(h, wup_ref[:, va:va + FFN_CHUNK])
        ug = _dot(h, wup_ref[:, ga:ga + FFN_CHUNK])
        val = _causal_conv3(uv, carry_ref[:, va:va + FFN_CHUNK], cw_ref[:, va:va + FFN_CHUNK])
        gate = _causal_conv3(ug, carry_ref[:, ga:ga + FFN_CHUNK], cw_ref[:, ga:ga + FFN_CHUNK])
        carry_ref[:, va:va + FFN_CHUNK] = uv[TM - SUBLANE:TM]
        carry_ref[:, ga:ga + FFN_CHUNK] = ug[TM - SUBLANE:TM]
        act_ref[:, va:va + FFN_CHUNK] = (gate * jax.nn.sigmoid(gate) * val).astype(BF16)
    y = _dot(act_ref[...], wdn_ref[...])
    z = DN_ALPHA * x + (1.0 + mod_ref[:, 5 * d:6 * d]) * y
    o_ref[...] = _layer_norm(z, g_ref[...], b_ref[...])


def _ffn(x, mod, w_up, conv_w, w_down, g, b):
    s, d = x.shape
    assert D_FF % FFN_CHUNK == 0
    row = lambda n: pl.BlockSpec((TM, n), lambda i: (i, 0))
    full = lambda a: pl.BlockSpec(a.shape, lambda i: (0,) * a.ndim)
    once = lambda a: pl.BlockSpec(a.shape, lambda i: (0,) * a.ndim, pipeline_mode=pl.Buffered(1))
    g = g.reshape(1, d)
    b = b.reshape(1, d)
    return pl.pallas_call(
        _ffn_kernel,
        grid=(s // TM,),
        in_specs=[row(d), full(mod), once(w_up), full(conv_w), once(w_down), full(g), full(b)],
        out_specs=row(d),
        out_shape=jax.ShapeDtypeStruct((s, d), F32),
        scratch_shapes=[pltpu.VMEM((SUBLANE, 2 * D_FF), F32), pltpu.VMEM((TM, D_FF), BF16)],
        compiler_params=_params(),
        name="conv_ffn",
    )(x, mod, w_up, conv_w, w_down, g, b)


_O_U = (0, 512)
_O_QLAT = (512, 896)
_O_KVLAT = (896, 1152)
_O_KR = (1152, 1280)
_O_KR_ROT = (1280, 1408)
_O_COLS = 1408


def _rotate_half_cols(w):
    half = w.shape[-1] // 2
    return jnp.concatenate([-w[..., half:], w[..., :half]], axis=-1)


def _odd_weights(w_in, w_uq, w_ukv):
    d = w_in.shape[0]
    o = C_WIDTH + Q_LORA + KV_LORA
    kr = w_in[:, o:o + D_ROPE]
    zl = jnp.zeros((d, D_NOPE), w_in.dtype)
    zr = jnp.zeros((d, LANE - D_NOPE - D_ROPE), w_in.dtype)
    w1 = jnp.concatenate([w_in[:, 0:o], zl, kr, zr, zl, _rotate_half_cols(kr), zr], axis=1)
    assert w1.shape[1] == _O_COLS
    dq = D_NOPE + D_ROPE
    uq = w_uq.reshape(Q_LORA, N_HEADS, dq)
    padq = jnp.zeros((Q_LORA, N_HEADS, LANE - dq), w_uq.dtype)
    wq1 = jnp.concatenate([uq, padq], axis=2).reshape(Q_LORA, N_HEADS * LANE)
    wq2 = jnp.concatenate([jnp.zeros((Q_LORA, N_HEADS, D_NOPE), w_uq.dtype),
                           _rotate_half_cols(uq[:, :, D_NOPE:]), padq], axis=2
                          ).reshape(Q_LORA, N_HEADS * LANE)
    ukv = w_ukv.reshape(KV_LORA, N_HEADS, D_NOPE + D_V)
    wk = jnp.concatenate([ukv[:, :, :D_NOPE], jnp.zeros((KV_LORA, N_HEADS, LANE - D_NOPE), w_ukv.dtype)],
                         axis=2).reshape(KV_LORA, N_HEADS * LANE)
    wvt = ukv[:, :, D_NOPE:].reshape(KV_LORA, N_HEADS * D_V).T
    return w1.astype(BF16), wq1.astype(BF16), wq2.astype(BF16), wk.astype(BF16), wvt.astype(BF16)


def _rope_lane_tables(seq):
    inv = ROPE_BASE ** (-jnp.arange(0, D_ROPE, 2, dtype=F32) / D_ROPE)
    ang = jnp.arange(seq, dtype=F32)[:, None] * inv[None, :]
    cos, sin = jnp.cos(ang), jnp.sin(ang)
    pad = jnp.zeros((seq, LANE - D_NOPE - D_ROPE), F32)
    cos_t = jnp.concatenate([jnp.ones((seq, D_NOPE), F32), cos, cos, pad], axis=1)
    sin_t = jnp.concatenate([jnp.zeros((seq, D_NOPE), F32), sin, sin, pad], axis=1)
    return cos_t, sin_t


def _odd_in_kernel(x_ref, mod_ref, w1_ref, wq1_ref, wq2_ref, wk_ref, wvt_ref, pw_ref, ps_ref,
                   qg_ref, kvg_ref, cos_ref, sin_ref,
                   yc_ref, q_ref, k_ref, vt_ref, carry_ref):
    d = D_MODEL
    i = pl.program_id(0)

    @pl.when(i == 0)
    def _():
        carry_ref[...] = jnp.zeros_like(carry_ref)

    h = (x_ref[...] * (1.0 + mod_ref[:, d:2 * d]) + mod_ref[:, 0:d]).astype(BF16)

    def proj(cols):
        return _dot(h, w1_ref[:, cols[0]:cols[1]])

    cos = cos_ref[...]
    sin = sin_ref[...]

    u = proj(_O_U)
    ext = jnp.concatenate([carry_ref[...], u], axis=0)
    pos = i * TM + lax.broadcasted_iota(jnp.int32, (TM, 1), 0)
    for g, win in enumerate(POOL_WINDOWS):
        sl = slice(g * C_GROUP_DIM, (g + 1) * C_GROUP_DIM)
        acc = ext[:, sl]
        shift = 1
        while shift < win:
            acc = acc + pltpu.roll(acc, shift, 0)
            shift *= 2
        cnt = jnp.minimum(pos + 1, win).astype(F32)
        pooled = acc[POOL_HALO:] / cnt - u[:, sl]
        mixed = _dot(pooled.astype(BF16), pw_ref[g])
        yc_ref[:, sl] = (mixed * ps_ref[:, sl]).astype(BF16)
    carry_ref[...] = u[TM - POOL_HALO:TM]

    r = _rms_norm(proj(_O_QLAT), qg_ref[...]).astype(BF16)
    qa = _dot(r, wq1_ref[...])
    qb = _dot(r, wq2_ref[...])
    kr = proj(_O_KR) * cos + proj(_O_KR_ROT) * sin
    rk = _rms_norm(proj(_O_KVLAT), kvg_ref[...]).astype(BF16)
    kn = _dot(rk, wk_ref[...])
    for hd in range(N_HEADS):
        sl = slice(hd * LANE, (hd + 1) * LANE)
        q_ref[:, sl] = ((qa[:, sl] * cos + qb[:, sl] * sin) * MLA_Q_SCALE).astype(BF16)
        k_ref[:, sl] = (kn[:, sl] + kr).astype(BF16)
    vt_ref[...] = _dot_nt(wvt_ref[...], rk).astype(BF16)


def _odd_in(x, mod, w1, wq1, wq2, wk, wvt, pool_w, pool_scale, q_norm_g, kv_norm_g, cos_t, sin_t):
    s, d = x.shape
    row = lambda n: pl.BlockSpec((TM, n), lambda i: (i, 0))
    col = lambda n: pl.BlockSpec((n, TM), lambda i: (0, i))
    full = lambda a: pl.BlockSpec(a.shape, lambda i: (0,) * a.ndim)
    consts = [w1, wq1, wq2, wk, wvt, pool_w.astype(BF16), pool_scale.reshape(1, C_WIDTH),
              q_norm_g.reshape(1, Q_LORA), kv_norm_g.reshape(1, KV_LORA)]
    rows_out = lambda n, t: (row(n), jax.ShapeDtypeStruct((s, n), t))
    cols_out = lambda n, t: (col(n), jax.ShapeDtypeStruct((n, s), t))
    outs = [rows_out(C_WIDTH, BF16), rows_out(N_HEADS * LANE, BF16), rows_out(N_HEADS * LANE, BF16),
            cols_out(N_HEADS * D_V, BF16)]
    return pl.pallas_call(
        _odd_in_kernel,
        grid=(s // TM,),
        in_specs=[row(d), full(mod)] + [full(a) for a in consts] + [row(LANE), row(LANE)],
        out_specs=[o[0] for o in outs],
        out_shape=[o[1] for o in outs],
        scratch_shapes=[pltpu.VMEM((POOL_HALO, C_WIDTH), F32)],
        compiler_params=_params(),
        name="odd_in",
    )(x, mod, *consts, cos_t, sin_t)


def kernel(x, c, ada_w, ada_b, ln_mix_g, ln_mix_b, ln_ffn_g, ln_ffn_b, ev_w_in, ev_conv_w, ev_w_out,
           od_w_in, pool_w, pool_scale, q_norm_g, w_uq, kv_norm_g, w_ukv, od_w_out,
           ffn_w_up, ffn_conv_w, ffn_w_down):
    bsz, seq, d = x.shape
    assert bsz == 1 and d == D_MODEL and seq % TM == 0 and seq % TK_ATT == 0
    assert SHORT_CONV == 3 and FFN_CONV == 3
    xs = x.reshape(seq, d)
    mods = _adaln(c, ada_w, ada_b)
    for l in range(DEPTH):
        mod = mods[l]
        if l % 2 == 0:
            e = l // 2
            w, wt = _even_weights(ev_w_in[e])
            q, k, vt, qi, ki, wit, yb = _even_in(xs, mod, w, wt, ev_conv_w[e])
            bias = _select(qi, ki, wit)
            ya = _flash(q, k, vt, bias, k_per_head=False, alibi=True)
            xs = _out_ln(ya, yb, xs, mod, ev_w_out[e].astype(BF16), ln_mix_g[l], ln_mix_b[l])
        else:
            o = l // 2
            cos_t, sin_t = _rope_lane_tables(seq)
            ws = _odd_weights(od_w_in[o], w_uq[o], w_ukv[o])
            yc, q, k, vt = _odd_in(xs, mod, *ws, pool_w[o], pool_scale[o], q_norm_g[o], kv_norm_g[o],
                                   cos_t, sin_t)
            yd = _flash(q, k, vt, None, k_per_head=True, alibi=False)
            xs = _out_ln(yc, yd, xs, mod, od_w_out[o].astype(BF16), ln_mix_g[l], ln_mix_b[l])
        xs = _ffn(xs, mod, ffn_w_up[l].astype(BF16), ffn_conv_w[l], ffn_w_down[l].astype(BF16),
                  ln_ffn_g[l], ln_ffn_b[l])
    return xs.reshape(bsz, seq, d)
```

```python
import functools

import numpy as np
import jax
import jax.numpy as jnp
from jax import lax
from jax.experimental import pallas as pl
from jax.experimental.pallas import tpu as pltpu

D_MODEL = 1024
DEPTH = 2
CHUNK = 64
N_HEADS = 8
A_HEAD_DIM = 64
A_WIDTH = N_HEADS * A_HEAD_DIM
IDX_DIM = 64
TOPK = 256
B_WIDTH = 512
SHORT_CONV = 3
C_WIDTH = 512
POOL_WINDOWS = (2, 4, 8, 16)
C_GROUP_DIM = C_WIDTH // len(POOL_WINDOWS)
D_NOPE = 64
D_ROPE = 32
D_V = 64
Q_LORA = 384
KV_LORA = 256
ROPE_BASE = 10000.0
D_FF = 2816
FFN_CONV = 3
LN_EPS = 1e-5
RMS_EPS = 1e-6
DN_ALPHA = (2 * DEPTH) ** 0.25
NEG = -1e30
IDX_W_SCALE = N_HEADS ** -0.5 * IDX_DIM ** -0.5
LOG2E = 1.4426950408889634
DSA_Q_SCALE = A_HEAD_DIM ** -0.5 * LOG2E
MLA_Q_SCALE = (D_NOPE + D_ROPE) ** -0.5 * LOG2E
ALIBI_SLOPES = tuple(2.0 ** (-8.0 * (i + 1) / N_HEADS) for i in range(N_HEADS))

LANE = 128
SUBLANE = 8
BF16_ROWS = 16
VMEM_LIMIT_BYTES = 56 * 1024 * 1024

TM = 512
FFN_CHUNK = 256
TQ_ATT = 256
TK_SEL = 512
TK_ATT = 512
QK_AHEAD = 4
POOL_HALO = 16

INT_MIN = -2 ** 31
F32 = jnp.float32
BF16 = jnp.bfloat16


def _params():
    return pltpu.CompilerParams(dimension_semantics=("arbitrary",),
                                vmem_limit_bytes=VMEM_LIMIT_BYTES)


def _dot(a, b):
    return jnp.dot(a, b, preferred_element_type=F32)


def _dot_nt(a, b):
    return lax.dot_general(a, b, (((1,), (1,)), ((), ())), preferred_element_type=F32)


def _layer_norm(z, g, b):
    mu = jnp.mean(z, axis=-1, keepdims=True)
    zc = z - mu
    var = jnp.mean(zc * zc, axis=-1, keepdims=True)
    return zc * lax.rsqrt(var + LN_EPS) * g + b


def _rms_norm(z, g):
    return z * lax.rsqrt(jnp.mean(z * z, axis=-1, keepdims=True) + RMS_EPS) * g


def _causal_conv3(u, prev, w):
    w0, w1, w2 = w[0:1], w[1:2], w[2:3]
    y = pltpu.roll(u, 2, 0) * w0 + pltpu.roll(u, 1, 0) * w1 + u * w2
    head = u[0:SUBLANE]
    r = lax.broadcasted_iota(jnp.int32, (SUBLANE, 1), 0)
    h1 = jnp.where(r == 0, prev[7:8], pltpu.roll(head, 1, 0))
    h2 = jnp.where(r == 0, prev[6:7], jnp.where(r == 1, prev[7:8], pltpu.roll(head, 2, 0)))
    yh = h2 * w0 + h1 * w1 + head * w2
    return jnp.concatenate([yh, y[SUBLANE:]], axis=0)


def _fold_rows(c):
    r = c.shape[0]
    while r > SUBLANE:
        r //= 2
        c = c[0:r] + c[r:2 * r]
    return c


def _adaln_kernel(c_ref, w_ref, b_ref, o_ref):
    c = c_ref[...]
    cond = c * jax.nn.sigmoid(c)
    o_ref[0] = jnp.sum(cond * w_ref[0], axis=0, keepdims=True) + b_ref[0]


def _adaln(c, ada_w, ada_b):
    depth, d, n = ada_w.shape
    tn = 1536
    return pl.pallas_call(
        _adaln_kernel,
        grid=(depth, n // tn),
        in_specs=[pl.BlockSpec((d, 1), lambda l, j: (0, 0)),
                  pl.BlockSpec((1, d, tn), lambda l, j: (l, 0, j)),
                  pl.BlockSpec((1, 1, tn), lambda l, j: (l, 0, j))],
        out_specs=pl.BlockSpec((1, 1, tn), lambda l, j: (l, 0, j)),
        out_shape=jax.ShapeDtypeStruct((depth, 1, n), F32),
        compiler_params=pltpu.CompilerParams(dimension_semantics=("arbitrary", "arbitrary"),
                                             vmem_limit_bytes=VMEM_LIMIT_BYTES),
        name="adaln",
    )(c.reshape(d, 1), ada_w, ada_b.reshape(depth, 1, n))


_E_Q = (0, 1024)
_E_K = (1024, 2048)
_E_QI = (2048, 3072)
_E_KI = (3072, 3200)
_E_BG = (3200, 3712)
_E_CG = (3712, 4224)
_E_XB = (4224, 4736)
_E_COLS = 4736
_ET_V = (0, 512)
_ET_WI = (512, 528)
_ET_ROWS = 528
_FEAT_LANE = A_HEAD_DIM
_N_LOG2E_TERMS = 3
_POS_RADIX = 128


def _pad_heads(w):
    d = w.shape[0]
    w3 = w.reshape(d, N_HEADS, A_HEAD_DIM)
    return jnp.concatenate([w3, jnp.zeros_like(w3)], axis=2).reshape(d, N_HEADS * LANE)


def _alibi_q_features():
    terms, rest = [], np.float64(LOG2E)
    for _ in range(_N_LOG2E_TERMS):
        t = np.float64(np.asarray(rest, np.float32).astype(jnp.bfloat16).astype(np.float32))
        terms.append(t)
        rest = rest - t
    row = np.zeros((N_HEADS, LANE), np.float32)
    for h in range(N_HEADS):
        for n, t in enumerate(terms):
            row[h, _FEAT_LANE + n] = ALIBI_SLOPES[h] * _POS_RADIX * t
            row[h, _FEAT_LANE + _N_LOG2E_TERMS + n] = ALIBI_SLOPES[h] * t
    return jnp.asarray(row.reshape(1, N_HEADS * LANE))


def _even_weights(w_in):
    d = w_in.shape[0]
    a = A_WIDTH
    q = _pad_heads(w_in[:, 0:a])
    k = _pad_heads(w_in[:, a:2 * a])
    v = w_in[:, 2 * a:3 * a]
    qi = _pad_heads(w_in[:, 3 * a:4 * a])
    o = 4 * a
    ki = w_in[:, o:o + IDX_DIM]
    wi = w_in[:, o + IDX_DIM:o + IDX_DIM + N_HEADS]
    o = o + IDX_DIM + N_HEADS
    rest = w_in[:, o:o + 3 * B_WIDTH]
    w = jnp.concatenate([q, k, qi, ki, ki, rest], axis=1)
    assert w.shape[1] == _E_COLS
    wt = jnp.concatenate([v.T, wi.T, jnp.zeros((BF16_ROWS - N_HEADS, d), w_in.dtype)], axis=0)
    assert wt.shape[0] == _ET_ROWS
    return w.astype(BF16), wt.astype(BF16)


def _even_in_kernel(x_ref, mod_ref, w_ref, wt_ref, cw_ref, qf_ref,
                    q_ref, k_ref, vt_ref, qi_ref, ki_ref, wit_ref, yb_ref, carry_ref):
    d = D_MODEL
    i = pl.program_id(0)

    @pl.when(i == 0)
    def _():
        carry_ref[...] = jnp.zeros_like(carry_ref)

    h = (x_ref[...] * (1.0 + mod_ref[:, d:2 * d]) + mod_ref[:, 0:d]).astype(BF16)

    def proj(cols):
        return _dot(h, w_ref[:, cols[0]:cols[1]])

    def proj_t(rows):
        return _dot_nt(wt_ref[rows[0]:rows[1], :], h)

    q_ref[...] = (proj(_E_Q) * DSA_Q_SCALE + qf_ref[...]).astype(BF16)
    pos = i * TM + lax.broadcasted_iota(jnp.int32, (TM, LANE), 0)
    lane = lax.broadcasted_iota(jnp.int32, (TM, LANE), 1) - _FEAT_LANE
    pos_hi = (pos // _POS_RADIX).astype(F32)
    pos_lo = (pos % _POS_RADIX).astype(F32)
    kfeat = jnp.where(lane < 0, 0.0,
                      jnp.where(lane < _N_LOG2E_TERMS, pos_hi,
                                jnp.where(lane < 2 * _N_LOG2E_TERMS, pos_lo, 0.0)))
    kproj = proj(_E_K)
    for hd in range(N_HEADS):
        sl = slice(hd * LANE, (hd + 1) * LANE)
        k_ref[:, sl] = (kproj[:, sl] + kfeat).astype(BF16)
    vt_ref[...] = proj_t(_ET_V).astype(BF16)
    qi_ref[...] = proj(_E_QI).astype(BF16)
    ki_ref[...] = proj(_E_KI).astype(BF16)
    wit_ref[...] = proj_t(_ET_WI) * IDX_W_SCALE
    g = proj(_E_CG) * proj(_E_XB)
    y = _causal_conv3(g, carry_ref[...], cw_ref[...])
    yb_ref[...] = (proj(_E_BG) * y).astype(BF16)
    carry_ref[...] = g[TM - SUBLANE:TM]


def _even_in(x, mod, w, wt, conv_w, qfeat):
    s, d = x.shape
    row = lambda n: pl.BlockSpec((TM, n), lambda i: (i, 0))
    col = lambda n: pl.BlockSpec((n, TM), lambda i: (0, i))
    full = lambda a: pl.BlockSpec(a.shape, lambda i: (0,) * a.ndim)
    rows_out = lambda n, t: (row(n), jax.ShapeDtypeStruct((s, n), t))
    cols_out = lambda n, t: (col(n), jax.ShapeDtypeStruct((n, s), t))
    outs = [rows_out(N_HEADS * LANE, BF16), rows_out(N_HEADS * LANE, BF16), cols_out(A_WIDTH, BF16),
            rows_out(N_HEADS * LANE, BF16), rows_out(LANE, BF16), cols_out(BF16_ROWS, F32),
            rows_out(B_WIDTH, BF16)]
    return pl.pallas_call(
        _even_in_kernel,
        grid=(s // TM,),
        in_specs=[row(d), full(mod), full(w), full(wt), full(conv_w), full(qfeat)],
        out_specs=[o[0] for o in outs],
        out_shape=[o[1] for o in outs],
        scratch_shapes=[pltpu.VMEM((SUBLANE, B_WIDTH), F32)],
        compiler_params=_params(),
        name="even_in",
    )(x, mod, w, wt, conv_w, qfeat)


def _select_kernel(qi_ref, ki_ref, wt_ref, bias_ref, hi_ref, lo_ref, *, seq):
    tq, tk = TQ_ATT, TK_SEL
    i16 = jnp.int16
    i = pl.program_id(0)
    start = i * tq
    n_tiles = (start + tq + tk - 1) // tk
    tpos = start + lax.broadcasted_iota(jnp.int32, (1, tq), 1)
    limit = (tpos // CHUNK + 1) * CHUNK
    int_min = jnp.int32(INT_MIN)
    dmin, dmax = -2 ** 15, 2 ** 15 - 1
    zero16, one16 = jnp.zeros((), i16), jnp.ones((), i16)
    sel0, seln = jnp.zeros((), BF16), jnp.full((), NEG, BF16)

    def key_pos(off):
        return off + lax.broadcasted_iota(jnp.int32, (tk, tq), 0)

    def tile(ref, j):
        return ref[pl.ds(pl.multiple_of(j * tk, tk), tk), :]

    def score_tile(j, carry, *, masked):
        off = pl.multiple_of(j * tk, tk)
        kt = ki_ref[pl.ds(off, tk), :]
        score = jnp.zeros((tk, tq), F32)
        for h in range(N_HEADS):
            rel = _dot_nt(kt, qi_ref[:, h * LANE:(h + 1) * LANE])
            score = score + wt_ref[h:h + 1, :] * jnp.maximum(rel, 0.0)
        bits = pltpu.bitcast(score, jnp.int32)
        key = bits ^ ((bits >> 31) & jnp.int32(0x7FFFFFFF))
        key = jnp.where(bits == int_min, 0, key)
        if masked:
            key = jnp.where(key_pos(off) < limit, key, int_min)
        hi_ref[pl.ds(off, tk), :] = (key >> 16).astype(i16)
        lo_ref[pl.ds(off, tk), :] = key.astype(i16) ^ jnp.asarray(dmin, i16)
        return carry

    n_before = start // tk
    lax.fori_loop(0, n_before, functools.partial(score_tile, masked=False), 0)
    lax.fori_loop(n_before, n_tiles, functools.partial(score_tile, masked=True), 0)

    def count(flag):
        def body(j, acc):
            c = flag(j)
            r = tk
            while r > BF16_ROWS:
                r //= 2
                c = c[0:r] + c[r:2 * r]
            return acc + c
        acc = lax.fori_loop(0, n_tiles, body, jnp.zeros((BF16_ROWS, tq), i16))
        return jnp.sum(acc.astype(jnp.int32).astype(F32), axis=0, keepdims=True)

    def radix_search(ref, need):
        def bit_step(b, thr):
            cand = thr + jnp.left_shift(jnp.int32(1), 15 - b)
            c16 = cand.astype(i16)
            cnt = count(lambda j: jnp.where(tile(ref, j) >= c16, one16, zero16))
            return jnp.where(cnt >= need, cand, thr)
        return lax.fori_loop(0, 16, bit_step, jnp.full((1, tq), dmin, jnp.int32))

    topk = float(TOPK)
    t_hi = radix_search(hi_ref, topk)
    t_hi16 = t_hi.astype(i16)
    n_above = count(lambda j: jnp.where(tile(hi_ref, j) > t_hi16, one16, zero16))

    def keep_members(j, carry):
        off = pl.multiple_of(j * tk, tk)
        lo_ref[pl.ds(off, tk), :] = jnp.where(hi_ref[pl.ds(off, tk), :] == t_hi16,
                                              lo_ref[pl.ds(off, tk), :], jnp.asarray(dmin, i16))
        return carry

    lax.fori_loop(0, n_tiles, keep_members, 0)
    t_lo = radix_search(lo_ref, topk - n_above)
    t_lo16 = t_lo.astype(i16)
    n_gt = n_above + count(lambda j: jnp.where(tile(lo_ref, j) > t_lo16, one16, zero16))
    n_ge = n_above + count(lambda j: jnp.where(tile(hi_ref, j) == t_hi16,
                                               jnp.where(tile(lo_ref, j) >= t_lo16, one16, zero16),
                                               zero16))
    room = topk - n_gt
    sentinel = t_hi == dmin
    has_ties = jnp.where(sentinel, 0.0, jnp.where(n_ge > topk, 1.0, 0.0))
    any_ties = jnp.max(has_ties) > 0.0

    def pos16(j):
        return key_pos(j * tk).astype(i16)

    def tie_cutoff():
        def cut_step(b, cut):
            cand = cut + jnp.left_shift(jnp.int32(1), 14 - b)
            c16 = cand.astype(i16)

            def flag(j):
                tied = jnp.where(tile(hi_ref, j) == t_hi16,
                                 jnp.where(tile(lo_ref, j) == t_lo16, pos16(j), jnp.asarray(dmax, i16)),
                                 jnp.asarray(dmax, i16))
                return jnp.where(tied < c16, one16, zero16)
            return jnp.where(count(flag) <= room, cand, cut)
        return lax.fori_loop(0, 15, cut_step, jnp.zeros((1, tq), jnp.int32))

    def write_plain():
        lo_min16 = jnp.where(sentinel, dmax, t_lo).astype(i16)

        def write_tile(j, carry):
            off = pl.multiple_of(j * tk, tk)
            hi = hi_ref[pl.ds(off, tk), :]
            inner = jnp.where(lo_ref[pl.ds(off, tk), :] >= lo_min16, sel0, seln)
            bias_ref[pl.ds(off, tk), :] = jnp.where(hi > t_hi16, sel0,
                                                    jnp.where(hi == t_hi16, inner, seln))
            return carry
        lax.fori_loop(0, n_tiles, write_tile, 0)

    def write_with_ties():
        cut16 = jnp.where(sentinel, 0, tie_cutoff()).astype(i16)

        def write_tile(j, carry):
            off = pl.multiple_of(j * tk, tk)
            hi = hi_ref[pl.ds(off, tk), :]
            lo = lo_ref[pl.ds(off, tk), :]
            tie = jnp.where(pos16(j) < cut16, sel0, seln)
            inner = jnp.where(lo > t_lo16, sel0, jnp.where(lo == t_lo16, tie, seln))
            bias_ref[pl.ds(off, tk), :] = jnp.where(hi > t_hi16, sel0,
                                                    jnp.where(hi == t_hi16, inner, seln))
            return carry
        lax.fori_loop(0, n_tiles, write_tile, 0)

    lax.cond(any_ties, write_with_ties, write_plain)

    def fill_tile(j, carry):
        off = pl.multiple_of(j * tk, tk)
        bias_ref[pl.ds(off, tk), :] = jnp.full((tk, tq), NEG, BF16)
        return carry

    lax.fori_loop(n_tiles, seq // tk, fill_tile, 0)


def _select(qi, ki, wit):
    s = qi.shape[0]
    assert s % TK_SEL == 0 and s % TQ_ATT == 0 and 2 * TOPK <= s <= 32767
    return pl.pallas_call(
        functools.partial(_select_kernel, seq=s),
        grid=(s // TQ_ATT,),
        in_specs=[pl.BlockSpec((TQ_ATT, N_HEADS * LANE), lambda i: (i, 0)),
                  pl.BlockSpec((s, LANE), lambda i: (0, 0)),
                  pl.BlockSpec((BF16_ROWS, TQ_ATT), lambda i: (0, i))],
        out_specs=pl.BlockSpec((s, TQ_ATT), lambda i: (0, i)),
        out_shape=jax.ShapeDtypeStruct((s, s), BF16),
        scratch_shapes=[pltpu.VMEM((s, TQ_ATT), jnp.int16), pltpu.VMEM((s, TQ_ATT), jnp.int16)],
        compiler_params=_params(),
        name="dsa_select",
    )(qi, ki, wit)


def _flash_kernel(qi_of, kj_of, *refs, alibi, use_bias):
    if use_bias:
        q_ref, k_ref, vt_ref, bias_ref, o_ref, m_ref, l_ref, acc_ref = refs
    else:
        q_ref, k_ref, vt_ref, o_ref, m_ref, l_ref, acc_ref = refs
    tq, tk = TQ_ATT, TK_ATT
    p = pl.program_id(0)
    qi = qi_of[p]
    kj = kj_of[p]
    last = ((qi + 1) * tq - 1) // tk

    @pl.when(kj == 0)
    def _():
        m_ref[...] = jnp.full_like(m_ref, NEG)
        l_ref[...] = jnp.zeros_like(l_ref)
        acc_ref[...] = jnp.zeros_like(acc_ref)

    def step(diagonal):
        base = bias_ref[...].astype(F32) if use_bias else None
        if diagonal:
            tpos = qi * tq + lax.broadcasted_iota(jnp.int32, (tk, tq), 1)
            spos = kj * tk + lax.broadcasted_iota(jnp.int32, (tk, tq), 0)
            if not use_bias:
                base = jnp.where(spos < (tpos // CHUNK + 1) * CHUNK, 0.0, NEG)
            if alibi:
                ahead = jnp.maximum(spos - tpos, 0).astype(F32)

        def logits(h):
            s = _dot_nt(k_ref[:, h * LANE:(h + 1) * LANE], q_ref[:, h * LANE:(h + 1) * LANE])
            if diagonal and alibi:
                return s + (base - (2.0 * ALIBI_SLOPES[h] * LOG2E) * ahead)
            return s if base is None else s + base

        pending = [logits(h) for h in range(QK_AHEAD)]
        for h in range(N_HEADS):
            rows = slice(h * D_V, (h + 1) * D_V)
            if h + QK_AHEAD < N_HEADS:
                pending.append(logits(h + QK_AHEAD))
            s = pending.pop(0)
            m_prev = m_ref[h:h + 1, :]
            m_new = jnp.maximum(m_prev, jnp.max(s, axis=0, keepdims=True))
            alpha = jnp.exp2(m_prev - m_new)
            pexp = jnp.exp2(s - m_new)
            l_ref[h:h + 1, :] = alpha * l_ref[h:h + 1, :] + jnp.sum(pexp, axis=0, keepdims=True)
            acc_ref[rows, :] = alpha * acc_ref[rows, :] + _dot(vt_ref[rows, :], pexp.astype(BF16))
            m_ref[h:h + 1, :] = m_new

    @pl.when(kj != last)
    def _():
        step(False)

    @pl.when(kj == last)
    def _():
        step(True)
        outs = [acc_ref[h * D_V:(h + 1) * D_V, :] / l_ref[h:h + 1, :] for h in range(N_HEADS)]
        o_ref[...] = jnp.concatenate(outs, axis=0).T.astype(BF16)


def _flash(q, k, vt, bias, *, alibi):
    s = q.shape[0]
    nq = s // TQ_ATT
    pairs = [(i, j) for i in range(nq) for j in range(((i + 1) * TQ_ATT - 1) // TK_ATT + 1)]
    qi_of = jnp.asarray(np.array([a for a, _ in pairs], np.int32))
    kj_of = jnp.asarray(np.array([b for _, b in pairs], np.int32))
    use_bias = bias is not None
    in_specs = [pl.BlockSpec((TQ_ATT, N_HEADS * LANE), lambda p, qi, kj: (qi[p], 0)),
                pl.BlockSpec((TK_ATT, k.shape[1]), lambda p, qi, kj: (kj[p], 0)),
                pl.BlockSpec((N_HEADS * D_V, TK_ATT), lambda p, qi, kj: (0, kj[p]))]
    args = [q, k, vt]
    if use_bias:
        in_specs.append(pl.BlockSpec((TK_ATT, TQ_ATT), lambda p, qi, kj: (kj[p], qi[p])))
        args.append(bias)
    kern = functools.partial(_flash_kernel, alibi=alibi, use_bias=use_bias)
    return pl.pallas_call(
        kern,
        grid_spec=pltpu.PrefetchScalarGridSpec(
            num_scalar_prefetch=2,
            grid=(len(pairs),),
            in_specs=in_specs,
            out_specs=pl.BlockSpec((TQ_ATT, N_HEADS * D_V), lambda p, qi, kj: (qi[p], 0)),
            scratch_shapes=[pltpu.VMEM((N_HEADS, TQ_ATT), F32),
                            pltpu.VMEM((N_HEADS, TQ_ATT), F32),
                            pltpu.VMEM((N_HEADS * D_V, TQ_ATT), F32)]),
        out_shape=jax.ShapeDtypeStruct((s, N_HEADS * D_V), BF16),
        compiler_params=_params(),
        name="flash_dsa" if use_bias else "flash_mla",
    )(qi_of, kj_of, *args)


def _out_ln_kernel(ya_ref, yb_ref, x_ref, mod_ref, w_ref, g_ref, b_ref, o_ref):
    d = D_MODEL
    na = ya_ref.shape[1]
    y = _dot(ya_ref[...], w_ref[0:na, :]) + _dot(yb_ref[...], w_ref[na:, :])
    z = DN_ALPHA * x_ref[...] + (1.0 + mod_ref[:, 2 * d:3 * d]) * y
    o_ref[...] = _layer_norm(z, g_ref[...], b_ref[...])


def _out_ln(ya, yb, x, mod, w, g, b):
    s, d = x.shape
    row = lambda n: pl.BlockSpec((TM, n), lambda i: (i, 0))
    full = lambda a: pl.BlockSpec(a.shape, lambda i: (0,) * a.ndim)
    g = g.reshape(1, d)
    b = b.reshape(1, d)
    return pl.pallas_call(
        _out_ln_kernel,
        grid=(s // TM,),
        in_specs=[row(ya.shape[1]), row(yb.shape[1]), row(d), full(mod), full(w), full(g), full(b)],
        out_specs=row(d),
        out_shape=jax.ShapeDtypeStruct((s, d), F32),
        compiler_params=_params(),
        name="out_ln",
    )(ya, yb, x, mod, w, g, b)


def _ffn_kernel(x_ref, mod_ref, wup_ref, cw_ref, wdn_ref, g_ref, b_ref, o_ref, carry_ref, act_ref):
    d = D_MODEL

    @pl.when(pl.program_id(0) == 0)
    def _():
        carry_ref[...] = jnp.zeros_like(carry_ref)

    x = x_ref[...]
    h = (x * (1.0 + mod_ref[:, 4 * d:5 * d]) + mod_ref[:, 3 * d:4 * d]).astype(BF16)
    for c in range(D_FF // FFN_CHUNK):
        va = c * FFN_CHUNK
        ga = D_FF + va
        uv = _dot(h, wup_ref[:, va:va + FFN_CHUNK])
        ug = _dot(h, wup_ref[:, ga:ga + FFN_CHUNK])
        val = _causal_conv3(uv, carry_ref[:, va:va + FFN_CHUNK], cw_ref[:, va:va + FFN_CHUNK])
        gate = _causal_conv3(ug, carry_ref[:, ga:ga + FFN_CHUNK], cw_ref[:, ga:ga + FFN_CHUNK])
        carry_ref[:, va:va + FFN_CHUNK] = uv[TM - SUBLANE:TM]
        carry_ref[:, ga:ga + FFN_CHUNK] = ug[TM - SUBLANE:TM]
        act_ref[:, va:va + FFN_CHUNK] = (gate * jax.nn.sigmoid(gate) * val).astype(BF16)
    y = _dot(act_ref[...], wdn_ref[...])
    z = DN_ALPHA * x + (1.0 + mod_ref[:, 5 * d:6 * d]) * y
    o_ref[...] = _layer_norm(z, g_ref[...], b_ref[...])


def _ffn(x, mod, w_up, conv_w, w_down, g, b):
    s, d = x.shape
    assert D_FF % FFN_CHUNK == 0
    row = lambda n: pl.BlockSpec((TM, n), lambda i: (i, 0))
    full = lambda a: pl.BlockSpec(a.shape, lambda i: (0,) * a.ndim)
    once = lambda a: pl.BlockSpec(a.shape, lambda i: (0,) * a.ndim, pipeline_mode=pl.Buffered(1))
    g = g.reshape(1, d)
    b = b.reshape(1, d)
    return pl.pallas_call(
        _ffn_kernel,
        grid=(s // TM,),
        in_specs=[row(d), full(mod), once(w_up), full(conv_w), once(w_down), full(g), full(b)],
        out_specs=row(d),
        out_shape=jax.ShapeDtypeStruct((s, d), F32),
        scratch_shapes=[pltpu.VMEM((SUBLANE, 2 * D_FF), F32), pltpu.VMEM((TM, D_FF), BF16)],
        compiler_params=_params(),
        name="conv_ffn",
    )(x, mod, w_up, conv_w, w_down, g, b)


_O_U = (0, 512)
_O_QLAT = (512, 896)
_O_KVLAT = (896, 1152)
_O_KR = (1152, 1280)
_O_KR_ROT = (1280, 1408)
_O_COLS = 1408


def _rotate_half_cols(w):
    half = w.shape[-1] // 2
    return jnp.concatenate([-w[..., half:], w[..., :half]], axis=-1)


def _odd_weights(w_in, w_uq, w_ukv):
    d = w_in.shape[0]
    o = C_WIDTH + Q_LORA + KV_LORA
    kr = w_in[:, o:o + D_ROPE]
    zl = jnp.zeros((d, D_NOPE), w_in.dtype)
    zr = jnp.zeros((d, LANE - D_NOPE - D_ROPE), w_in.dtype)
    w1 = jnp.concatenate([w_in[:, 0:o], zl, kr, zr, zl, _rotate_half_cols(kr), zr], axis=1)
    assert w1.shape[1] == _O_COLS
    dq = D_NOPE + D_ROPE
    uq = w_uq.reshape(Q_LORA, N_HEADS, dq)
    padq = jnp.zeros((Q_LORA, N_HEADS, LANE - dq), w_uq.dtype)
    wq1 = jnp.concatenate([uq, padq], axis=2).reshape(Q_LORA, N_HEADS * LANE)
    wq2 = jnp.concatenate([jnp.zeros((Q_LORA, N_HEADS, D_NOPE), w_uq.dtype),
                           _rotate_half_cols(uq[:, :, D_NOPE:]), padq], axis=2
                          ).reshape(Q_LORA, N_HEADS * LANE)
    ukv = w_ukv.reshape(KV_LORA, N_HEADS, D_NOPE + D_V)
    wk = jnp.concatenate([ukv[:, :, :D_NOPE], jnp.zeros((KV_LORA, N_HEADS, LANE - D_NOPE), w_ukv.dtype)],
                         axis=2).reshape(KV_LORA, N_HEADS * LANE)
    wvt = ukv[:, :, D_NOPE:].reshape(KV_LORA, N_HEADS * D_V).T
    return w1.astype(BF16), wq1.astype(BF16), wq2.astype(BF16), wk.astype(BF16), wvt.astype(BF16)


def _rope_lane_tables(seq):
    inv = ROPE_BASE ** (-jnp.arange(0, D_ROPE, 2, dtype=F32) / D_ROPE)
    ang = jnp.arange(seq, dtype=F32)[:, None] * inv[None, :]
    cos, sin = jnp.cos(ang), jnp.sin(ang)
    pad = jnp.zeros((seq, LANE - D_NOPE - D_ROPE), F32)
    cos_t = jnp.concatenate([jnp.ones((seq, D_NOPE), F32), cos, cos, pad], axis=1)
    sin_t = jnp.concatenate([jnp.zeros((seq, D_NOPE), F32), sin, sin, pad], axis=1)
    return cos_t, sin_t


def _odd_in_kernel(x_ref, mod_ref, w1_ref, wq1_ref, wq2_ref, wk_ref, wvt_ref, pw_ref, ps_ref,
                   qg_ref, kvg_ref, cos_ref, sin_ref,
                   yc_ref, q_ref, k_ref, vt_ref, carry_ref):
    d = D_MODEL
    i = pl.program_id(0)

    @pl.when(i == 0)
    def _():
        carry_ref[...] = jnp.zeros_like(carry_ref)

    h = (x_ref[...] * (1.0 + mod_ref[:, d:2 * d]) + mod_ref[:, 0:d]).astype(BF16)

    def proj(cols):
        return _dot(h, w1_ref[:, cols[0]:cols[1]])

    cos = cos_ref[...]
    sin = sin_ref[...]

    u = proj(_O_U)
    ext = jnp.concatenate([carry_ref[...], u], axis=0)
    pos = i * TM + lax.broadcasted_iota(jnp.int32, (TM, 1), 0)
    for g, win in enumerate(POOL_WINDOWS):
        sl = slice(g * C_GROUP_DIM, (g + 1) * C_GROUP_DIM)
        acc = ext[:, sl]
        shift = 1
        while shift < win:
            acc = acc + pltpu.roll(acc, shift, 0)
            shift *= 2
        cnt = jnp.minimum(pos + 1, win).astype(F32)
        pooled = acc[POOL_HALO:] / cnt - u[:, sl]
        mixed = _dot(pooled.astype(BF16), pw_ref[g])
        yc_ref[:, sl] = (mixed * ps_ref[:, sl]).astype(BF16)
    carry_ref[...] = u[TM - POOL_HALO:TM]

    r = _rms_norm(proj(_O_QLAT), qg_ref[...]).astype(BF16)
    qa = _dot(r, wq1_ref[...])
    qb = _dot(r, wq2_ref[...])
    kr = proj(_O_KR) * cos + proj(_O_KR_ROT) * sin
    rk = _rms_norm(proj(_O_KVLAT), kvg_ref[...]).astype(BF16)
    kn = _dot(rk, wk_ref[...])
    for hd in range(N_HEADS):
        sl = slice(hd * LANE, (hd + 1) * LANE)
        q_ref[:, sl] = ((qa[:, sl] * cos + qb[:, sl] * sin) * MLA_Q_SCALE).astype(BF16)
        k_ref[:, sl] = (kn[:, sl] + kr).astype(BF16)
    vt_ref[...] = _dot_nt(wvt_ref[...], rk).astype(BF16)


def _odd_in(x, mod, w1, wq1, wq2, wk, wvt, pool_w, pool_scale, q_norm_g, kv_norm_g, cos_t, sin_t):
    s, d = x.shape
    row = lambda n: pl.BlockSpec((TM, n), lambda i: (i, 0))
    col = lambda n: pl.BlockSpec((n, TM), lambda i: (0, i))
    full = lambda a: pl.BlockSpec(a.shape, lambda i: (0,) * a.ndim)
    consts = [w1, wq1, wq2, wk, wvt, pool_w.astype(BF16), pool_scale.reshape(1, C_WIDTH),
              q_norm_g.reshape(1, Q_LORA), kv_norm_g.reshape(1, KV_LORA)]
    rows_out = lambda n, t: (row(n), jax.ShapeDtypeStruct((s, n), t))
    cols_out = lambda n, t: (col(n), jax.ShapeDtypeStruct((n, s), t))
    outs = [rows_out(C_WIDTH, BF16), rows_out(N_HEADS * LANE, BF16), rows_out(N_HEADS * LANE, BF16),
            cols_out(N_HEADS * D_V, BF16)]
    return pl.pallas_call(
        _odd_in_kernel,
        grid=(s // TM,),
        in_specs=[row(d), full(mod)] + [full(a) for a in consts] + [row(LANE), row(LANE)],
        out_specs=[o[0] for o in outs],
        out_shape=[o[1] for o in outs],
        scratch_shapes=[pltpu.VMEM((POOL_HALO, C_WIDTH), F32)],
        compiler_params=_params(),
        name="odd_in",
    )(x, mod, *consts, cos_t, sin_t)


def kernel(x, c, ada_w, ada_b, ln_mix_g, ln_mix_b, ln_ffn_g, ln_ffn_b, ev_w_in, ev_conv_w, ev_w_out,
           od_w_in, pool_w, pool_scale, q_norm_g, w_uq, kv_norm_g, w_ukv, od_w_out,
           ffn_w_up, ffn_conv_w, ffn_w_down):
    bsz, seq, d = x.shape
    assert bsz == 1 and d == D_MODEL and seq % TM == 0 and seq % TK_ATT == 0
    assert SHORT_CONV == 3 and FFN_CONV == 3
    xs = x.reshape(seq, d)
    mods = _adaln(c, ada_w, ada_b)
    for l in range(DEPTH):
        mod = mods[l]
        if l % 2 == 0:
            e = l // 2
            w, wt = _even_weights(ev_w_in[e])
            q, k, vt, qi, ki, wit, yb = _even_in(xs, mod, w, wt, ev_conv_w[e], _alibi_q_features())
            bias = _select(qi, ki, wit)
            ya = _flash(q, k, vt, bias, alibi=True)
            xs = _out_ln(ya, yb, xs, mod, ev_w_out[e].astype(BF16), ln_mix_g[l], ln_mix_b[l])
        else:
            o = l // 2
            cos_t, sin_t = _rope_lane_tables(seq)
            ws = _odd_weights(od_w_in[o], w_uq[o], w_ukv[o])
            yc, q, k, vt = _odd_in(xs, mod, *ws, pool_w[o], pool_scale[o], q_norm_g[o], kv_norm_g[o],
                                   cos_t, sin_t)
            yd = _flash(q, k, vt, None, alibi=False)
            xs = _out_ln(yc, yd, xs, mod, od_w_out[o].astype(BF16), ln_mix_g[l], ln_mix_b[l])
        xs = _ffn(xs, mod, ffn_w_up[l].astype(BF16), ffn_conv_w[l], ffn_w_down[l].astype(BF16),
                  ln_ffn_g[l], ln_ffn_b[l])
    return xs.reshape(bsz, seq, d)
```

```python
import functools

import numpy as np
import jax
import jax.numpy as jnp
from jax import lax
from jax.experimental import pallas as pl
from jax.experimental.pallas import tpu as pltpu

D_MODEL = 1024
DEPTH = 2
CHUNK = 64
N_HEADS = 8
A_HEAD_DIM = 64
A_WIDTH = N_HEADS * A_HEAD_DIM
IDX_DIM = 64
TOPK = 256
B_WIDTH = 512
SHORT_CONV = 3
C_WIDTH = 512
POOL_WINDOWS = (2, 4, 8, 16)
C_GROUP_DIM = C_WIDTH // len(POOL_WINDOWS)
D_NOPE = 64
D_ROPE = 32
D_V = 64
Q_LORA = 384
KV_LORA = 256
ROPE_BASE = 10000.0
D_FF = 2816
FFN_CONV = 3
LN_EPS = 1e-5
RMS_EPS = 1e-6
DN_ALPHA = (2 * DEPTH) ** 0.25
NEG = -1e30
IDX_W_SCALE = N_HEADS ** -0.5 * IDX_DIM ** -0.5
LOG2E = 1.4426950408889634
DSA_Q_SCALE = A_HEAD_DIM ** -0.5 * LOG2E
MLA_Q_SCALE = (D_NOPE + D_ROPE) ** -0.5 * LOG2E
ALIBI_SLOPES = tuple(2.0 ** (-8.0 * (i + 1) / N_HEADS) for i in range(N_HEADS))

LANE = 128
SUBLANE = 8
BF16_ROWS = 16
V_ROWS = D_V + BF16_ROWS
VMEM_LIMIT_BYTES = 56 * 1024 * 1024

TM = 512
FFN_CHUNK = 256
TQ_ATT = 256
TK_SEL = 512
TK_ATT = 1024
QK_AHEAD = 4
POOL_HALO = 16

INT_MIN = -2 ** 31
F32 = jnp.float32
BF16 = jnp.bfloat16


def _params():
    return pltpu.CompilerParams(dimension_semantics=("arbitrary",),
                                vmem_limit_bytes=VMEM_LIMIT_BYTES)


def _dot(a, b):
    return jnp.dot(a, b, preferred_element_type=F32)


def _dot_nt(a, b):
    return lax.dot_general(a, b, (((1,), (1,)), ((), ())), preferred_element_type=F32)


def _layer_norm(z, g, b):
    mu = jnp.mean(z, axis=-1, keepdims=True)
    zc = z - mu
    var = jnp.mean(zc * zc, axis=-1, keepdims=True)
    return zc * lax.rsqrt(var + LN_EPS) * g + b


def _rms_norm(z, g):
    return z * lax.rsqrt(jnp.mean(z * z, axis=-1, keepdims=True) + RMS_EPS) * g


def _causal_conv3(u, prev, w):
    w0, w1, w2 = w[0:1], w[1:2], w[2:3]
    y = pltpu.roll(u, 2, 0) * w0 + pltpu.roll(u, 1, 0) * w1 + u * w2
    head = u[0:SUBLANE]
    r = lax.broadcasted_iota(jnp.int32, (SUBLANE, 1), 0)
    h1 = jnp.where(r == 0, prev[7:8], pltpu.roll(head, 1, 0))
    h2 = jnp.where(r == 0, prev[6:7], jnp.where(r == 1, prev[7:8], pltpu.roll(head, 2, 0)))
    yh = h2 * w0 + h1 * w1 + head * w2
    return jnp.concatenate([yh, y[SUBLANE:]], axis=0)


def _store_values_t(vt_ref, vt):
    t = vt.shape[1]
    ones_row = jnp.where(lax.broadcasted_iota(jnp.int32, (BF16_ROWS, t), 0) == 0, 1.0, 0.0)
    for h in range(N_HEADS):
        vt_ref[h * V_ROWS:h * V_ROWS + D_V, :] = vt[h * D_V:(h + 1) * D_V, :].astype(BF16)
        vt_ref[h * V_ROWS + D_V:(h + 1) * V_ROWS, :] = ones_row.astype(BF16)


def _adaln_kernel(c_ref, w_ref, b_ref, o_ref):
    c = c_ref[...]
    cond = c * jax.nn.sigmoid(c)
    o_ref[0] = jnp.sum(cond * w_ref[0], axis=0, keepdims=True) + b_ref[0]


def _adaln(c, ada_w, ada_b):
    depth, d, n = ada_w.shape
    tn = 1536
    return pl.pallas_call(
        _adaln_kernel,
        grid=(depth, n // tn),
        in_specs=[pl.BlockSpec((d, 1), lambda l, j: (0, 0)),
                  pl.BlockSpec((1, d, tn), lambda l, j: (l, 0, j)),
                  pl.BlockSpec((1, 1, tn), lambda l, j: (l, 0, j))],
        out_specs=pl.BlockSpec((1, 1, tn), lambda l, j: (l, 0, j)),
        out_shape=jax.ShapeDtypeStruct((depth, 1, n), F32),
        compiler_params=pltpu.CompilerParams(dimension_semantics=("arbitrary", "arbitrary"),
                                             vmem_limit_bytes=VMEM_LIMIT_BYTES),
        name="adaln",
    )(c.reshape(d, 1), ada_w, ada_b.reshape(depth, 1, n))


_E_Q = (0, 1024)
_E_K = (1024, 2048)
_E_QI = (2048, 3072)
_E_KI = (3072, 3200)
_E_BG = (3200, 3712)
_E_CG = (3712, 4224)
_E_XB = (4224, 4736)
_E_COLS = 4736
_ET_V = (0, 512)
_ET_WI = (512, 528)
_ET_ROWS = 528
_FEAT_LANE = A_HEAD_DIM
_N_LOG2E_TERMS = 3
_POS_RADIX = 128


def _pad_heads(w):
    d = w.shape[0]
    w3 = w.reshape(d, N_HEADS, A_HEAD_DIM)
    return jnp.concatenate([w3, jnp.zeros_like(w3)], axis=2).reshape(d, N_HEADS * LANE)


def _alibi_q_features():
    terms, rest = [], np.float64(LOG2E)
    for _ in range(_N_LOG2E_TERMS):
        t = np.float64(np.asarray(rest, np.float32).astype(jnp.bfloat16).astype(np.float32))
        terms.append(t)
        rest = rest - t
    row = np.zeros((N_HEADS, LANE), np.float32)
    for h in range(N_HEADS):
        for n, t in enumerate(terms):
            row[h, _FEAT_LANE + n] = ALIBI_SLOPES[h] * _POS_RADIX * t
            row[h, _FEAT_LANE + _N_LOG2E_TERMS + n] = ALIBI_SLOPES[h] * t
    return jnp.asarray(row.reshape(1, N_HEADS * LANE))


def _even_weights(w_in):
    d = w_in.shape[0]
    a = A_WIDTH
    q = _pad_heads(w_in[:, 0:a])
    k = _pad_heads(w_in[:, a:2 * a])
    v = w_in[:, 2 * a:3 * a]
    qi = _pad_heads(w_in[:, 3 * a:4 * a])
    o = 4 * a
    ki = w_in[:, o:o + IDX_DIM]
    wi = w_in[:, o + IDX_DIM:o + IDX_DIM + N_HEADS]
    o = o + IDX_DIM + N_HEADS
    rest = w_in[:, o:o + 3 * B_WIDTH]
    w = jnp.concatenate([q, k, qi, ki, ki, rest], axis=1)
    assert w.shape[1] == _E_COLS
    wt = jnp.concatenate([v.T, wi.T, jnp.zeros((BF16_ROWS - N_HEADS, d), w_in.dtype)], axis=0)
    assert wt.shape[0] == _ET_ROWS
    return w.astype(BF16), wt.astype(BF16)


def _even_in_kernel(x_ref, mod_ref, w_ref, wt_ref, cw_ref, qf_ref,
                    q_ref, k_ref, vt_ref, qi_ref, ki_ref, wit_ref, yb_ref, carry_ref):
    d = D_MODEL
    i = pl.program_id(0)

    @pl.when(i == 0)
    def _():
        carry_ref[...] = jnp.zeros_like(carry_ref)

    h = (x_ref[...] * (1.0 + mod_ref[:, d:2 * d]) + mod_ref[:, 0:d]).astype(BF16)

    def proj(cols):
        return _dot(h, w_ref[:, cols[0]:cols[1]])

    def proj_t(rows):
        return _dot_nt(wt_ref[rows[0]:rows[1], :], h)

    q_ref[...] = (proj(_E_Q) * DSA_Q_SCALE + qf_ref[...]).astype(BF16)
    pos = i * TM + lax.broadcasted_iota(jnp.int32, (TM, LANE), 0)
    lane = lax.broadcasted_iota(jnp.int32, (TM, LANE), 1) - _FEAT_LANE
    pos_hi = (pos // _POS_RADIX).astype(F32)
    pos_lo = (pos % _POS_RADIX).astype(F32)
    kfeat = jnp.where(lane < 0, 0.0,
                      jnp.where(lane < _N_LOG2E_TERMS, pos_hi,
                                jnp.where(lane < 2 * _N_LOG2E_TERMS, pos_lo, 0.0)))
    kproj = proj(_E_K)
    for hd in range(N_HEADS):
        sl = slice(hd * LANE, (hd + 1) * LANE)
        k_ref[:, sl] = (kproj[:, sl] + kfeat).astype(BF16)
    _store_values_t(vt_ref, proj_t(_ET_V))
    qi_ref[...] = proj(_E_QI).astype(BF16)
    ki_ref[...] = proj(_E_KI).astype(BF16)
    wit_ref[...] = proj_t(_ET_WI) * IDX_W_SCALE
    g = proj(_E_CG) * proj(_E_XB)
    y = _causal_conv3(g, carry_ref[...], cw_ref[...])
    yb_ref[...] = (proj(_E_BG) * y).astype(BF16)
    carry_ref[...] = g[TM - SUBLANE:TM]


def _even_in(x, mod, w, wt, conv_w, qfeat):
    s, d = x.shape
    row = lambda n: pl.BlockSpec((TM, n), lambda i: (i, 0))
    col = lambda n: pl.BlockSpec((n, TM), lambda i: (0, i))
    full = lambda a: pl.BlockSpec(a.shape, lambda i: (0,) * a.ndim)
    rows_out = lambda n, t: (row(n), jax.ShapeDtypeStruct((s, n), t))
    cols_out = lambda n, t: (col(n), jax.ShapeDtypeStruct((n, s), t))
    outs = [rows_out(N_HEADS * LANE, BF16), rows_out(N_HEADS * LANE, BF16), cols_out(N_HEADS * V_ROWS, BF16),
            rows_out(N_HEADS * LANE, BF16), rows_out(LANE, BF16), cols_out(BF16_ROWS, F32),
            rows_out(B_WIDTH, BF16)]
    return pl.pallas_call(
        _even_in_kernel,
        grid=(s // TM,),
        in_specs=[row(d), full(mod), full(w), full(wt), full(conv_w), full(qfeat)],
        out_specs=[o[0] for o in outs],
        out_shape=[o[1] for o in outs],
        scratch_shapes=[pltpu.VMEM((SUBLANE, B_WIDTH), F32)],
        compiler_params=_params(),
        name="even_in",
    )(x, mod, w, wt, conv_w, qfeat)


def _select_kernel(qi_ref, ki_ref, wt_ref, bias_ref, hi_ref, lo_ref, *, seq):
    tq, tk = TQ_ATT, TK_SEL
    i16 = jnp.int16
    i = pl.program_id(0)
    start = i * tq
    n_tiles = (start + tq + tk - 1) // tk
    tpos = start + lax.broadcasted_iota(jnp.int32, (1, tq), 1)
    limit = (tpos // CHUNK + 1) * CHUNK
    int_min = jnp.int32(INT_MIN)
    dmin, dmax = -2 ** 15, 2 ** 15 - 1
    zero16, one16 = jnp.zeros((), i16), jnp.ones((), i16)
    sel0, seln = jnp.zeros((), BF16), jnp.full((), NEG, BF16)

    def key_pos(off):
        return off + lax.broadcasted_iota(jnp.int32, (tk, tq), 0)

    def tile(ref, j):
        return ref[pl.ds(pl.multiple_of(j * tk, tk), tk), :]

    def score_tile(j, carry, *, masked):
        off = pl.multiple_of(j * tk, tk)
        kt = ki_ref[pl.ds(off, tk), :]
        score = jnp.zeros((tk, tq), F32)
        for h in range(N_HEADS):
            rel = _dot_nt(kt, qi_ref[:, h * LANE:(h + 1) * LANE])
            score = score + wt_ref[h:h + 1, :] * jnp.maximum(rel, 0.0)
        bits = pltpu.bitcast(score, jnp.int32)
        key = bits ^ ((bits >> 31) & jnp.int32(0x7FFFFFFF))
        key = jnp.where(bits == int_min, 0, key)
        if masked:
            key = jnp.where(key_pos(off) < limit, key, int_min)
        hi_ref[pl.ds(off, tk), :] = (key >> 16).astype(i16)
        lo_ref[pl.ds(off, tk), :] = key.astype(i16) ^ jnp.asarray(dmin, i16)
        return carry

    n_before = start // tk
    lax.fori_loop(0, n_before, functools.partial(score_tile, masked=False), 0)
    lax.fori_loop(n_before, n_tiles, functools.partial(score_tile, masked=True), 0)

    def count(flag):
        def body(j, acc):
            c = flag(j)
            r = tk
            while r > BF16_ROWS:
                r //= 2
                c = c[0:r] + c[r:2 * r]
            return acc + c
        acc = lax.fori_loop(0, n_tiles, body, jnp.zeros((BF16_ROWS, tq), i16))
        return jnp.sum(acc.astype(jnp.int32).astype(F32), axis=0, keepdims=True)

    def radix_search(ref, need):
        def bit_step(b, thr):
            cand = thr + jnp.left_shift(jnp.int32(1), 15 - b)
            c16 = cand.astype(i16)
            cnt = count(lambda j: jnp.where(tile(ref, j) >= c16, one16, zero16))
            return jnp.where(cnt >= need, cand, thr)
        return lax.fori_loop(0, 16, bit_step, jnp.full((1, tq), dmin, jnp.int32))

    topk = float(TOPK)
    t_hi = radix_search(hi_ref, topk)
    t_hi16 = t_hi.astype(i16)
    n_above = count(lambda j: jnp.where(tile(hi_ref, j) > t_hi16, one16, zero16))

    def keep_members(j, carry):
        off = pl.multiple_of(j * tk, tk)
        lo_ref[pl.ds(off, tk), :] = jnp.where(hi_ref[pl.ds(off, tk), :] == t_hi16,
                                              lo_ref[pl.ds(off, tk), :], jnp.asarray(dmin, i16))
        return carry

    lax.fori_loop(0, n_tiles, keep_members, 0)
    t_lo = radix_search(lo_ref, topk - n_above)
    t_lo16 = t_lo.astype(i16)
    n_gt = n_above + count(lambda j: jnp.where(tile(lo_ref, j) > t_lo16, one16, zero16))
    n_ge = n_above + count(lambda j: jnp.where(tile(hi_ref, j) == t_hi16,
                                               jnp.where(tile(lo_ref, j) >= t_lo16, one16, zero16),
                                               zero16))
    room = topk - n_gt
    sentinel = t_hi == dmin
    has_ties = jnp.where(sentinel, 0.0, jnp.where(n_ge > topk, 1.0, 0.0))
    any_ties = jnp.max(has_ties) > 0.0

    def pos16(j):
        return key_pos(j * tk).astype(i16)

    def tie_cutoff():
        def cut_step(b, cut):
            cand = cut + jnp.left_shift(jnp.int32(1), 14 - b)
            c16 = cand.astype(i16)

            def flag(j):
                tied = jnp.where(tile(hi_ref, j) == t_hi16,
                                 jnp.where(tile(lo_ref, j) == t_lo16, pos16(j), jnp.asarray(dmax, i16)),
                                 jnp.asarray(dmax, i16))
                return jnp.where(tied < c16, one16, zero16)
            return jnp.where(count(flag) <= room, cand, cut)
        return lax.fori_loop(0, 15, cut_step, jnp.zeros((1, tq), jnp.int32))

    def write_plain():
        lo_min16 = jnp.where(sentinel, dmax, t_lo).astype(i16)

        def write_tile(j, carry):
            off = pl.multiple_of(j * tk, tk)
            hi = hi_ref[pl.ds(off, tk), :]
            inner = jnp.where(lo_ref[pl.ds(off, tk), :] >= lo_min16, sel0, seln)
            bias_ref[pl.ds(off, tk), :] = jnp.where(hi > t_hi16, sel0,
                                                    jnp.where(hi == t_hi16, inner, seln))
            return carry
        lax.fori_loop(0, n_tiles, write_tile, 0)

    def write_with_ties():
        cut16 = jnp.where(sentinel, 0, tie_cutoff()).astype(i16)

        def write_tile(j, carry):
            off = pl.multiple_of(j * tk, tk)
            hi = hi_ref[pl.ds(off, tk), :]
            lo = lo_ref[pl.ds(off, tk), :]
            tie = jnp.where(pos16(j) < cut16, sel0, seln)
            inner = jnp.where(lo > t_lo16, sel0, jnp.where(lo == t_lo16, tie, seln))
            bias_ref[pl.ds(off, tk), :] = jnp.where(hi > t_hi16, sel0,
                                                    jnp.where(hi == t_hi16, inner, seln))
            return carry
        lax.fori_loop(0, n_tiles, write_tile, 0)

    lax.cond(any_ties, write_with_ties, write_plain)

    def fill_tile(j, carry):
        off = pl.multiple_of(j * tk, tk)
        bias_ref[pl.ds(off, tk), :] = jnp.full((tk, tq), NEG, BF16)
        return carry

    lax.fori_loop(n_tiles, seq // tk, fill_tile, 0)


def _select(qi, ki, wit):
    s = qi.shape[0]
    assert s % TK_SEL == 0 and s % TQ_ATT == 0 and 2 * TOPK <= s <= 32767
    return pl.pallas_call(
        functools.partial(_select_kernel, seq=s),
        grid=(s // TQ_ATT,),
        in_specs=[pl.BlockSpec((TQ_ATT, N_HEADS * LANE), lambda i: (i, 0)),
                  pl.BlockSpec((s, LANE), lambda i: (0, 0)),
                  pl.BlockSpec((BF16_ROWS, TQ_ATT), lambda i: (0, i))],
        out_specs=pl.BlockSpec((s, TQ_ATT), lambda i: (0, i)),
        out_shape=jax.ShapeDtypeStruct((s, s), BF16),
        scratch_shapes=[pltpu.VMEM((s, TQ_ATT), jnp.int16), pltpu.VMEM((s, TQ_ATT), jnp.int16)],
        compiler_params=_params(),
        name="dsa_select",
    )(qi, ki, wit)


def _flash_kernel(qi_of, kj_of, *refs, alibi, use_bias):
    if use_bias:
        q_ref, k_ref, vt_ref, bias_ref, o_ref, m_ref, acc_ref = refs
    else:
        q_ref, k_ref, vt_ref, o_ref, m_ref, acc_ref = refs
    tq, tk = TQ_ATT, TK_ATT
    p = pl.program_id(0)
    qi = qi_of[p]
    kj = kj_of[p]
    last = ((qi + 1) * tq - 1) // tk

    @pl.when(kj == 0)
    def _():
        m_ref[...] = jnp.full_like(m_ref, NEG)
        acc_ref[...] = jnp.zeros_like(acc_ref)

    def step(diagonal):
        base = bias_ref[...].astype(F32) if use_bias else None
        if diagonal:
            tpos = qi * tq + lax.broadcasted_iota(jnp.int32, (tk, tq), 1)
            spos = kj * tk + lax.broadcasted_iota(jnp.int32, (tk, tq), 0)
            if not use_bias:
                base = jnp.where(spos < (tpos // CHUNK + 1) * CHUNK, 0.0, NEG)
            if alibi:
                ahead = jnp.maximum(spos - tpos, 0).astype(F32)

        def logits(h):
            s = _dot_nt(k_ref[:, h * LANE:(h + 1) * LANE], q_ref[:, h * LANE:(h + 1) * LANE])
            if diagonal and alibi:
                return s + (base - (2.0 * ALIBI_SLOPES[h] * LOG2E) * ahead)
            return s if base is None else s + base

        pending = [logits(h) for h in range(QK_AHEAD)]
        for h in range(N_HEADS):
            rows = slice(h * V_ROWS, (h + 1) * V_ROWS)
            if h + QK_AHEAD < N_HEADS:
                pending.append(logits(h + QK_AHEAD))
            s = pending.pop(0)
            m_prev = m_ref[h:h + 1, :]
            m_new = jnp.maximum(m_prev, jnp.max(s, axis=0, keepdims=True))
            alpha = jnp.exp2(m_prev - m_new)
            pexp = jnp.exp2(s - m_new)
            acc_ref[rows, :] = alpha * acc_ref[rows, :] + _dot(vt_ref[rows, :], pexp.astype(BF16))
            m_ref[h:h + 1, :] = m_new

    @pl.when(kj != last)
    def _():
        step(False)

    @pl.when(kj == last)
    def _():
        step(True)
        outs = [acc_ref[h * V_ROWS:h * V_ROWS + D_V, :] / acc_ref[h * V_ROWS + D_V:h * V_ROWS + D_V + 1, :]
                for h in range(N_HEADS)]
        o_ref[...] = jnp.concatenate(outs, axis=0).T.astype(BF16)


def _flash(q, k, vt, bias, *, alibi):
    s = q.shape[0]
    nq = s // TQ_ATT
    pairs = [(i, j) for i in range(nq) for j in range(((i + 1) * TQ_ATT - 1) // TK_ATT + 1)]
    qi_of = jnp.asarray(np.array([a for a, _ in pairs], np.int32))
    kj_of = jnp.asarray(np.array([b for _, b in pairs], np.int32))
    use_bias = bias is not None
    in_specs = [pl.BlockSpec((TQ_ATT, N_HEADS * LANE), lambda p, qi, kj: (qi[p], 0)),
                pl.BlockSpec((TK_ATT, k.shape[1]), lambda p, qi, kj: (kj[p], 0)),
                pl.BlockSpec((N_HEADS * V_ROWS, TK_ATT), lambda p, qi, kj: (0, kj[p]))]
    args = [q, k, vt]
    if use_bias:
        in_specs.append(pl.BlockSpec((TK_ATT, TQ_ATT), lambda p, qi, kj: (kj[p], qi[p])))
        args.append(bias)
    kern = functools.partial(_flash_kernel, alibi=alibi, use_bias=use_bias)
    return pl.pallas_call(
        kern,
        grid_spec=pltpu.PrefetchScalarGridSpec(
            num_scalar_prefetch=2,
            grid=(len(pairs),),
            in_specs=in_specs,
            out_specs=pl.BlockSpec((TQ_ATT, N_HEADS * D_V), lambda p, qi, kj: (qi[p], 0)),
            scratch_shapes=[pltpu.VMEM((N_HEADS, TQ_ATT), F32),
                            pltpu.VMEM((N_HEADS * V_ROWS, TQ_ATT), F32)]),
        out_shape=jax.ShapeDtypeStruct((s, N_HEADS * D_V), BF16),
        compiler_params=_params(),
        name="flash_dsa" if use_bias else "flash_mla",
    )(qi_of, kj_of, *args)


def _out_ln_kernel(ya_ref, yb_ref, x_ref, mod_ref, w_ref, g_ref, b_ref, o_ref):
    d = D_MODEL
    na = ya_ref.shape[1]
    y = _dot(ya_ref[...], w_ref[0:na, :]) + _dot(yb_ref[...], w_ref[na:, :])
    z = DN_ALPHA * x_ref[...] + (1.0 + mod_ref[:, 2 * d:3 * d]) * y
    o_ref[...] = _layer_norm(z, g_ref[...], b_ref[...])


def _out_ln(ya, yb, x, mod, w, g, b):
    s, d = x.shape
    row = lambda n: pl.BlockSpec((TM, n), lambda i: (i, 0))
    full = lambda a: pl.BlockSpec(a.shape, lambda i: (0,) * a.ndim)
    g = g.reshape(1, d)
    b = b.reshape(1, d)
    return pl.pallas_call(
        _out_ln_kernel,
        grid=(s // TM,),
        in_specs=[row(ya.shape[1]), row(yb.shape[1]), row(d), full(mod), full(w), full(g), full(b)],
        out_specs=row(d),
        out_shape=jax.ShapeDtypeStruct((s, d), F32),
        compiler_params=_params(),
        name="out_ln",
    )(ya, yb, x, mod, w, g, b)


def _ffn_kernel(x_ref, mod_ref, wup_ref, cw_ref, wdn_ref, g_ref, b_ref, o_ref, carry_ref, act_ref):
    d = D_MODEL

    @pl.when(pl.program_id(0) == 0)
    def _():
        carry_ref[...] = jnp.zeros_like(carry_ref)

    x = x_ref[...]
    h = (x * (1.0 + mod_ref[:, 4 * d:5 * d]) + mod_ref[:, 3 * d:4 * d]).astype(BF16)
    for c in range(D_FF // FFN_CHUNK):
        va = c * FFN_CHUNK
        ga = D_FF + va
        uv = _dot(h, wup_ref[:, va:va + FFN_CHUNK])
        ug = _dot(h, wup_ref[:, ga:ga + FFN_CHUNK])
        val = _causal_conv3(uv, carry_ref[:, va:va + FFN_CHUNK], cw_ref[:, va:va + FFN_CHUNK])
        gate = _causal_conv3(ug, carry_ref[:, ga:ga + FFN_CHUNK], cw_ref[:, ga:ga + FFN_CHUNK])
        carry_ref[:, va:va + FFN_CHUNK] = uv[TM - SUBLANE:TM]
        carry_ref[:, ga:ga + FFN_CHUNK] = ug[TM - SUBLANE:TM]
        act_ref[:, va:va + FFN_CHUNK] = (gate * jax.nn.sigmoid(gate) * val).astype(BF16)
    y = _dot(act_ref[...], wdn_ref[...])
    z = DN_ALPHA * x + (1.0 + mod_ref[:, 5 * d:6 * d]) * y
    o_ref[...] = _layer_norm(z, g_ref[...], b_ref[...])


def _ffn(x, mod, w_up, conv_w, w_down, g, b):
    s, d = x.shape
    assert D_FF % FFN_CHUNK == 0
    row = lambda n: pl.BlockSpec((TM, n), lambda i: (i, 0))
    full = lambda a: pl.BlockSpec(a.shape, lambda i: (0,) * a.ndim)
    once = lambda a: pl.BlockSpec(a.shape, lambda i: (0,) * a.ndim, pipeline_mode=pl.Buffered(1))
    g = g.reshape(1, d)
    b = b.reshape(1, d)
    return pl.pallas_call(
        _ffn_kernel,
        grid=(s // TM,),
        in_specs=[row(d), full(mod), once(w_up), full(conv_w), once(w_down), full(g), full(b)],
        out_specs=row(d),
        out_shape=jax.ShapeDtypeStruct((s, d), F32),
        scratch_shapes=[pltpu.VMEM((SUBLANE, 2 * D_FF), F32), pltpu.VMEM((TM, D_FF), BF16)],
        compiler_params=_params(),
        name="conv_ffn",
    )(x, mod, w_up, conv_w, w_down, g, b)


_O_U = (0, 512)
_O_QLAT = (512, 896)
_O_KVLAT = (896, 1152)
_O_KR = (1152, 1280)
_O_KR_ROT = (1280, 1408)
_O_COLS = 1408


def _rotate_half_cols(w):
    half = w.shape[-1] // 2
    return jnp.concatenate([-w[..., half:], w[..., :half]], axis=-1)


def _odd_weights(w_in, w_uq, w_ukv):
    d = w_in.shape[0]
    o = C_WIDTH + Q_LORA + KV_LORA
    kr = w_in[:, o:o + D_ROPE]
    zl = jnp.zeros((d, D_NOPE), w_in.dtype)
    zr = jnp.zeros((d, LANE - D_NOPE - D_ROPE), w_in.dtype)
    w1 = jnp.concatenate([w_in[:, 0:o], zl, kr, zr, zl, _rotate_half_cols(kr), zr], axis=1)
    assert w1.shape[1] == _O_COLS
    dq = D_NOPE + D_ROPE
    uq = w_uq.reshape(Q_LORA, N_HEADS, dq)
    padq = jnp.zeros((Q_LORA, N_HEADS, LANE - dq), w_uq.dtype)
    wq1 = jnp.concatenate([uq, padq], axis=2).reshape(Q_LORA, N_HEADS * LANE)
    wq2 = jnp.concatenate([jnp.zeros((Q_LORA, N_HEADS, D_NOPE), w_uq.dtype),
                           _rotate_half_cols(uq[:, :, D_NOPE:]), padq], axis=2
                          ).reshape(Q_LORA, N_HEADS * LANE)
    ukv = w_ukv.reshape(KV_LORA, N_HEADS, D_NOPE + D_V)
    wk = jnp.concatenate([ukv[:, :, :D_NOPE], jnp.zeros((KV_LORA, N_HEADS, LANE - D_NOPE), w_ukv.dtype)],
                         axis=2).reshape(KV_LORA, N_HEADS * LANE)
    wvt = ukv[:, :, D_NOPE:].reshape(KV_LORA, N_HEADS * D_V).T
    return w1.astype(BF16), wq1.astype(BF16), wq2.astype(BF16), wk.astype(BF16), wvt.astype(BF16)


def _rope_lane_tables(seq):
    inv = ROPE_BASE ** (-jnp.arange(0, D_ROPE, 2, dtype=F32) / D_ROPE)
    ang = jnp.arange(seq, dtype=F32)[:, None] * inv[None, :]
    cos, sin = jnp.cos(ang), jnp.sin(ang)
    pad = jnp.zeros((seq, LANE - D_NOPE - D_ROPE), F32)
    cos_t = jnp.concatenate([jnp.ones((seq, D_NOPE), F32), cos, cos, pad], axis=1)
    sin_t = jnp.concatenate([jnp.zeros((seq, D_NOPE), F32), sin, sin, pad], axis=1)
    return cos_t, sin_t


def _odd_in_kernel(x_ref, mod_ref, w1_ref, wq1_ref, wq2_ref, wk_ref, wvt_ref, pw_ref, ps_ref,
                   qg_ref, kvg_ref, cos_ref, sin_ref,
                   yc_ref, q_ref, k_ref, vt_ref, carry_ref):
    d = D_MODEL
    i = pl.program_id(0)

    @pl.when(i == 0)
    def _():
        carry_ref[...] = jnp.zeros_like(carry_ref)

    h = (x_ref[...] * (1.0 + mod_ref[:, d:2 * d]) + mod_ref[:, 0:d]).astype(BF16)

    def proj(cols):
        return _dot(h, w1_ref[:, cols[0]:cols[1]])

    cos = cos_ref[...]
    sin = sin_ref[...]

    u = proj(_O_U)
    ext = jnp.concatenate([carry_ref[...], u], axis=0)
    pos = i * TM + lax.broadcasted_iota(jnp.int32, (TM, 1), 0)
    for g, win in enumerate(POOL_WINDOWS):
        sl = slice(g * C_GROUP_DIM, (g + 1) * C_GROUP_DIM)
        acc = ext[:, sl]
        shift = 1
        while shift < win:
            acc = acc + pltpu.roll(acc, shift, 0)
            shift *= 2
        cnt = jnp.minimum(pos + 1, win).astype(F32)
        pooled = acc[POOL_HALO:] / cnt - u[:, sl]
        mixed = _dot(pooled.astype(BF16), pw_ref[g])
        yc_ref[:, sl] = (mixed * ps_ref[:, sl]).astype(BF16)
    carry_ref[...] = u[TM - POOL_HALO:TM]

    r = _rms_norm(proj(_O_QLAT), qg_ref[...]).astype(BF16)
    qa = _dot(r, wq1_ref[...])
    qb = _dot(r, wq2_ref[...])
    kr = proj(_O_KR) * cos + proj(_O_KR_ROT) * sin
    rk = _rms_norm(proj(_O_KVLAT), kvg_ref[...]).astype(BF16)
    kn = _dot(rk, wk_ref[...])
    for hd in range(N_HEADS):
        sl = slice(hd * LANE, (hd + 1) * LANE)
        q_ref[:, sl] = ((qa[:, sl] * cos + qb[:, sl] * sin) * MLA_Q_SCALE).astype(BF16)
        k_ref[:, sl] = (kn[:, sl] + kr).astype(BF16)
    _store_values_t(vt_ref, _dot_nt(wvt_ref[...], rk))


def _odd_in(x, mod, w1, wq1, wq2, wk, wvt, pool_w, pool_scale, q_norm_g, kv_norm_g, cos_t, sin_t):
    s, d = x.shape
    row = lambda n: pl.BlockSpec((TM, n), lambda i: (i, 0))
    col = lambda n: pl.BlockSpec((n, TM), lambda i: (0, i))
    full = lambda a: pl.BlockSpec(a.shape, lambda i: (0,) * a.ndim)
    consts = [w1, wq1, wq2, wk, wvt, pool_w.astype(BF16), pool_scale.reshape(1, C_WIDTH),
              q_norm_g.reshape(1, Q_LORA), kv_norm_g.reshape(1, KV_LORA)]
    rows_out = lambda n, t: (row(n), jax.ShapeDtypeStruct((s, n), t))
    cols_out = lambda n, t: (col(n), jax.ShapeDtypeStruct((n, s), t))
    outs = [rows_out(C_WIDTH, BF16), rows_out(N_HEADS * LANE, BF16), rows_out(N_HEADS * LANE, BF16),
            cols_out(N_HEADS * V_ROWS, BF16)]
    return pl.pallas_call(
        _odd_in_kernel,
        grid=(s // TM,),
        in_specs=[row(d), full(mod)] + [full(a) for a in consts] + [row(LANE), row(LANE)],
        out_specs=[o[0] for o in outs],
        out_shape=[o[1] for o in outs],
        scratch_shapes=[pltpu.VMEM((POOL_HALO, C_WIDTH), F32)],
        compiler_params=_params(),
        name="odd_in",
    )(x, mod, *consts, cos_t, sin_t)


def kernel(x, c, ada_w, ada_b, ln_mix_g, ln_mix_b, ln_ffn_g, ln_ffn_b, ev_w_in, ev_conv_w, ev_w_out,
           od_w_in, pool_w, pool_scale, q_norm_g, w_uq, kv_norm_g, w_ukv, od_w_out,
           ffn_w_up, ffn_conv_w, ffn_w_down):
    bsz, seq, d = x.shape
    assert bsz == 1 and d == D_MODEL and seq % TM == 0 and seq % TK_ATT == 0
    assert SHORT_CONV == 3 and FFN_CONV == 3
    xs = x.reshape(seq, d)
    mods = _adaln(c, ada_w, ada_b)
    for l in range(DEPTH):
        mod = mods[l]
        if l % 2 == 0:
            e = l // 2
            w, wt = _even_weights(ev_w_in[e])
            q, k, vt, qi, ki, wit, yb = _even_in(xs, mod, w, wt, ev_conv_w[e], _alibi_q_features())
            bias = _select(qi, ki, wit)
            ya = _flash(q, k, vt, bias, alibi=True)
            xs = _out_ln(ya, yb, xs, mod, ev_w_out[e].astype(BF16), ln_mix_g[l], ln_mix_b[l])
        else:
            o = l // 2
            cos_t, sin_t = _rope_lane_tables(seq)
            ws = _odd_weights(od_w_in[o], w_uq[o], w_ukv[o])
            yc, q, k, vt = _odd_in(xs, mod, *ws, pool_w[o], pool_scale[o], q_norm_g[o], kv_norm_g[o],
                                   cos_t, sin_t)
            yd = _flash(q, k, vt, None, alibi=False)
            xs = _out_ln(yc, yd, xs, mod, od_w_out[o].astype(BF16), ln_mix_g[l], ln_mix_b[l])
        xs = _ffn(xs, mod, ffn_w_up[l].astype(BF16), ffn_conv_w[l], ffn_w_down[l].astype(BF16),
                  ln_ffn_g[l], ln_ffn_b[l])
    return xs.reshape(bsz, seq, d)
```

```python
import functools

import numpy as np
import jax
import jax.numpy as jnp
from jax import lax
from jax.experimental import pallas as pl
from jax.experimental.pallas import tpu as pltpu

D_MODEL = 1024
DEPTH = 2
CHUNK = 64
N_HEADS = 8
A_HEAD_DIM = 64
A_WIDTH = N_HEADS * A_HEAD_DIM
IDX_DIM = 64
TOPK = 256
B_WIDTH = 512
SHORT_CONV = 3
C_WIDTH = 512
POOL_WINDOWS = (2, 4, 8, 16)
C_GROUP_DIM = C_WIDTH // len(POOL_WINDOWS)
D_NOPE = 64
D_ROPE = 32
D_V = 64
Q_LORA = 384
KV_LORA = 256
ROPE_BASE = 10000.0
D_FF = 2816
FFN_CONV = 3
LN_EPS = 1e-5
RMS_EPS = 1e-6
DN_ALPHA = (2 * DEPTH) ** 0.25
NEG = -1e30
IDX_W_SCALE = N_HEADS ** -0.5 * IDX_DIM ** -0.5
LOG2E = 1.4426950408889634
DSA_Q_SCALE = A_HEAD_DIM ** -0.5 * LOG2E
MLA_Q_SCALE = (D_NOPE + D_ROPE) ** -0.5 * LOG2E
ALIBI_SLOPES = tuple(2.0 ** (-8.0 * (i + 1) / N_HEADS) for i in range(N_HEADS))

LANE = 128
SUBLANE = 8
BF16_ROWS = 16
V_ROWS = D_V + BF16_ROWS
VMEM_LIMIT_BYTES = 56 * 1024 * 1024

TM = 512
FFN_CHUNK = 256
TQ_ATT = 256
TK_SEL = 512
SEL_BLOCK = 4
TK_ATT = 1024
QK_AHEAD = 4
POOL_HALO = 16

INT_MIN = -2 ** 31
F32 = jnp.float32
BF16 = jnp.bfloat16


def _params():
    return pltpu.CompilerParams(dimension_semantics=("arbitrary",),
                                vmem_limit_bytes=VMEM_LIMIT_BYTES)


def _dot(a, b):
    return jnp.dot(a, b, preferred_element_type=F32)


def _dot_nt(a, b):
    return lax.dot_general(a, b, (((1,), (1,)), ((), ())), preferred_element_type=F32)


def _layer_norm(z, g, b):
    mu = jnp.mean(z, axis=-1, keepdims=True)
    zc = z - mu
    var = jnp.mean(zc * zc, axis=-1, keepdims=True)
    return zc * lax.rsqrt(var + LN_EPS) * g + b


def _rms_norm(z, g):
    return z * lax.rsqrt(jnp.mean(z * z, axis=-1, keepdims=True) + RMS_EPS) * g


def _causal_conv3(u, prev, w):
    w0, w1, w2 = w[0:1], w[1:2], w[2:3]
    y = pltpu.roll(u, 2, 0) * w0 + pltpu.roll(u, 1, 0) * w1 + u * w2
    head = u[0:SUBLANE]
    r = lax.broadcasted_iota(jnp.int32, (SUBLANE, 1), 0)
    h1 = jnp.where(r == 0, prev[7:8], pltpu.roll(head, 1, 0))
    h2 = jnp.where(r == 0, prev[6:7], jnp.where(r == 1, prev[7:8], pltpu.roll(head, 2, 0)))
    yh = h2 * w0 + h1 * w1 + head * w2
    return jnp.concatenate([yh, y[SUBLANE:]], axis=0)


def _store_values_t(vt_ref, vt):
    t = vt.shape[1]
    ones_row = jnp.where(lax.broadcasted_iota(jnp.int32, (BF16_ROWS, t), 0) == 0, 1.0, 0.0)
    for h in range(N_HEADS):
        vt_ref[h * V_ROWS:h * V_ROWS + D_V, :] = vt[h * D_V:(h + 1) * D_V, :].astype(BF16)
        vt_ref[h * V_ROWS + D_V:(h + 1) * V_ROWS, :] = ones_row.astype(BF16)


def _adaln_kernel(c_ref, w_ref, b_ref, o_ref):
    c = c_ref[...]
    cond = c * jax.nn.sigmoid(c)
    o_ref[0] = jnp.sum(cond * w_ref[0], axis=0, keepdims=True) + b_ref[0]


def _adaln(c, ada_w, ada_b):
    depth, d, n = ada_w.shape
    tn = 1536
    return pl.pallas_call(
        _adaln_kernel,
        grid=(depth, n // tn),
        in_specs=[pl.BlockSpec((d, 1), lambda l, j: (0, 0)),
                  pl.BlockSpec((1, d, tn), lambda l, j: (l, 0, j)),
                  pl.BlockSpec((1, 1, tn), lambda l, j: (l, 0, j))],
        out_specs=pl.BlockSpec((1, 1, tn), lambda l, j: (l, 0, j)),
        out_shape=jax.ShapeDtypeStruct((depth, 1, n), F32),
        compiler_params=pltpu.CompilerParams(dimension_semantics=("arbitrary", "arbitrary"),
                                             vmem_limit_bytes=VMEM_LIMIT_BYTES),
        name="adaln",
    )(c.reshape(d, 1), ada_w, ada_b.reshape(depth, 1, n))


_E_Q = (0, 1024)
_E_K = (1024, 2048)
_E_QI = (2048, 3072)
_E_KI = (3072, 3200)
_E_BG = (3200, 3712)
_E_CG = (3712, 4224)
_E_XB = (4224, 4736)
_E_COLS = 4736
_ET_V = (0, 512)
_ET_WI = (512, 528)
_ET_ROWS = 528
_FEAT_LANE = A_HEAD_DIM
_N_LOG2E_TERMS = 3
_POS_RADIX = 128


def _pad_heads(w):
    d = w.shape[0]
    w3 = w.reshape(d, N_HEADS, A_HEAD_DIM)
    return jnp.concatenate([w3, jnp.zeros_like(w3)], axis=2).reshape(d, N_HEADS * LANE)


def _alibi_q_features():
    terms, rest = [], np.float64(LOG2E)
    for _ in range(_N_LOG2E_TERMS):
        t = np.float64(np.asarray(rest, np.float32).astype(jnp.bfloat16).astype(np.float32))
        terms.append(t)
        rest = rest - t
    row = np.zeros((N_HEADS, LANE), np.float32)
    for h in range(N_HEADS):
        for n, t in enumerate(terms):
            row[h, _FEAT_LANE + n] = ALIBI_SLOPES[h] * _POS_RADIX * t
            row[h, _FEAT_LANE + _N_LOG2E_TERMS + n] = ALIBI_SLOPES[h] * t
    return jnp.asarray(row.reshape(1, N_HEADS * LANE))


def _even_weights(w_in):
    d = w_in.shape[0]
    a = A_WIDTH
    q = _pad_heads(w_in[:, 0:a])
    k = _pad_heads(w_in[:, a:2 * a])
    v = w_in[:, 2 * a:3 * a]
    qi = _pad_heads(w_in[:, 3 * a:4 * a])
    o = 4 * a
    ki = w_in[:, o:o + IDX_DIM]
    wi = w_in[:, o + IDX_DIM:o + IDX_DIM + N_HEADS]
    o = o + IDX_DIM + N_HEADS
    rest = w_in[:, o:o + 3 * B_WIDTH]
    w = jnp.concatenate([q, k, qi, ki, ki, rest], axis=1)
    assert w.shape[1] == _E_COLS
    wt = jnp.concatenate([v.T, wi.T, jnp.zeros((BF16_ROWS - N_HEADS, d), w_in.dtype)], axis=0)
    assert wt.shape[0] == _ET_ROWS
    return w.astype(BF16), wt.astype(BF16)


def _even_in_kernel(x_ref, mod_ref, w_ref, wt_ref, cw_ref, qf_ref,
                    q_ref, k_ref, vt_ref, qi_ref, ki_ref, wit_ref, yb_ref, carry_ref):
    d = D_MODEL
    i = pl.program_id(0)

    @pl.when(i == 0)
    def _():
        carry_ref[...] = jnp.zeros_like(carry_ref)

    h = (x_ref[...] * (1.0 + mod_ref[:, d:2 * d]) + mod_ref[:, 0:d]).astype(BF16)

    def proj(cols):
        return _dot(h, w_ref[:, cols[0]:cols[1]])

    def proj_t(rows):
        return _dot_nt(wt_ref[rows[0]:rows[1], :], h)

    q_ref[...] = (proj(_E_Q) * DSA_Q_SCALE + qf_ref[...]).astype(BF16)
    pos = i * TM + lax.broadcasted_iota(jnp.int32, (TM, LANE), 0)
    lane = lax.broadcasted_iota(jnp.int32, (TM, LANE), 1) - _FEAT_LANE
    pos_hi = (pos // _POS_RADIX).astype(F32)
    pos_lo = (pos % _POS_RADIX).astype(F32)
    kfeat = jnp.where(lane < 0, 0.0,
                      jnp.where(lane < _N_LOG2E_TERMS, pos_hi,
                                jnp.where(lane < 2 * _N_LOG2E_TERMS, pos_lo, 0.0)))
    kproj = proj(_E_K)
    for hd in range(N_HEADS):
        sl = slice(hd * LANE, (hd + 1) * LANE)
        k_ref[:, sl] = (kproj[:, sl] + kfeat).astype(BF16)
    _store_values_t(vt_ref, proj_t(_ET_V))
    qi_ref[...] = proj(_E_QI).astype(BF16)
    ki_ref[...] = proj(_E_KI).astype(BF16)
    wit_ref[...] = proj_t(_ET_WI) * IDX_W_SCALE
    g = proj(_E_CG) * proj(_E_XB)
    y = _causal_conv3(g, carry_ref[...], cw_ref[...])
    yb_ref[...] = (proj(_E_BG) * y).astype(BF16)
    carry_ref[...] = g[TM - SUBLANE:TM]


def _even_in(x, mod, w, wt, conv_w, qfeat):
    s, d = x.shape
    row = lambda n: pl.BlockSpec((TM, n), lambda i: (i, 0))
    col = lambda n: pl.BlockSpec((n, TM), lambda i: (0, i))
    full = lambda a: pl.BlockSpec(a.shape, lambda i: (0,) * a.ndim)
    rows_out = lambda n, t: (row(n), jax.ShapeDtypeStruct((s, n), t))
    cols_out = lambda n, t: (col(n), jax.ShapeDtypeStruct((n, s), t))
    outs = [rows_out(N_HEADS * LANE, BF16), rows_out(N_HEADS * LANE, BF16), cols_out(N_HEADS * V_ROWS, BF16),
            rows_out(N_HEADS * LANE, BF16), rows_out(LANE, BF16), cols_out(BF16_ROWS, F32),
            rows_out(B_WIDTH, BF16)]
    return pl.pallas_call(
        _even_in_kernel,
        grid=(s // TM,),
        in_specs=[row(d), full(mod), full(w), full(wt), full(conv_w), full(qfeat)],
        out_specs=[o[0] for o in outs],
        out_shape=[o[1] for o in outs],
        scratch_shapes=[pltpu.VMEM((SUBLANE, B_WIDTH), F32)],
        compiler_params=_params(),
        name="even_in",
    )(x, mod, w, wt, conv_w, qfeat)


def _select_kernel(qi_ref, ki_ref, wt_ref, bias_ref, hi_ref, lo_ref, *, seq):
    tq, tk = TQ_ATT, TK_SEL
    i16 = jnp.int16
    i = pl.program_id(0)
    start = i * tq
    n_tiles = (start + tq + tk - 1) // tk
    tpos = start + lax.broadcasted_iota(jnp.int32, (1, tq), 1)
    limit = (tpos // CHUNK + 1) * CHUNK
    int_min = jnp.int32(INT_MIN)
    dmin, dmax = -2 ** 15, 2 ** 15 - 1
    zero16, one16 = jnp.zeros((), i16), jnp.ones((), i16)
    sel0, seln = jnp.zeros((), BF16), jnp.full((), NEG, BF16)

    def key_pos(off):
        return off + lax.broadcasted_iota(jnp.int32, (tk, tq), 0)

    def score_tile(j, carry, *, masked):
        off = pl.multiple_of(j * tk, tk)
        kt = ki_ref[pl.ds(off, tk), :]
        score = jnp.zeros((tk, tq), F32)
        for h in range(N_HEADS):
            rel = _dot_nt(kt, qi_ref[:, h * LANE:(h + 1) * LANE])
            score = score + wt_ref[h:h + 1, :] * jnp.maximum(rel, 0.0)
        bits = pltpu.bitcast(score, jnp.int32)
        key = bits ^ ((bits >> 31) & jnp.int32(0x7FFFFFFF))
        key = jnp.where(bits == int_min, 0, key)
        if masked:
            key = jnp.where(key_pos(off) < limit, key, int_min)
        hi_ref[pl.ds(off, tk), :] = (key >> 16).astype(i16)
        lo_ref[pl.ds(off, tk), :] = key.astype(i16) ^ jnp.asarray(dmin, i16)
        return carry

    n_before = start // tk
    lax.fori_loop(0, n_before, functools.partial(score_tile, masked=False), 0)
    lax.fori_loop(n_before, n_tiles, functools.partial(score_tile, masked=True), 0)

    def rows_at(ref, off, rows):
        return ref[pl.ds(off, rows), :]

    def for_key_blocks(fn, carry):
        big = SEL_BLOCK * tk
        n_big = n_tiles // SEL_BLOCK
        carry = lax.fori_loop(0, n_big, lambda j, c: fn(pl.multiple_of(j * big, big), big, c), carry)
        return lax.fori_loop(n_big * SEL_BLOCK, n_tiles,
                             lambda j, c: fn(pl.multiple_of(j * tk, tk), tk, c), carry)

    def count(flag):
        def body(off, rows, acc):
            c = flag(off, rows)
            while rows > BF16_ROWS:
                rows //= 2
                c = c[0:rows] + c[rows:2 * rows]
            return acc + c
        acc = for_key_blocks(body, jnp.zeros((BF16_ROWS, tq), i16))
        return jnp.sum(acc.astype(jnp.int32).astype(F32), axis=0, keepdims=True)

    def radix_search(ref, need):
        def bit_step(b, thr):
            cand = thr + jnp.left_shift(jnp.int32(1), 15 - b)
            c16 = cand.astype(i16)
            cnt = count(lambda off, rows: jnp.where(rows_at(ref, off, rows) >= c16, one16, zero16))
            return jnp.where(cnt >= need, cand, thr)
        return lax.fori_loop(0, 16, bit_step, jnp.full((1, tq), dmin, jnp.int32))

    topk = float(TOPK)
    t_hi = radix_search(hi_ref, topk)
    t_hi16 = t_hi.astype(i16)
    n_above = count(lambda off, rows: jnp.where(rows_at(hi_ref, off, rows) > t_hi16, one16, zero16))

    def keep_members(off, rows, carry):
        lo_ref[pl.ds(off, rows), :] = jnp.where(rows_at(hi_ref, off, rows) == t_hi16,
                                                rows_at(lo_ref, off, rows), jnp.asarray(dmin, i16))
        return carry

    for_key_blocks(keep_members, 0)
    t_lo = radix_search(lo_ref, topk - n_above)
    t_lo16 = t_lo.astype(i16)
    n_gt = n_above + count(lambda off, rows: jnp.where(rows_at(lo_ref, off, rows) > t_lo16, one16, zero16))
    n_ge = n_above + count(lambda off, rows: jnp.where(
        rows_at(hi_ref, off, rows) == t_hi16,
        jnp.where(rows_at(lo_ref, off, rows) >= t_lo16, one16, zero16), zero16))
    room = topk - n_gt
    sentinel = t_hi == dmin
    has_ties = jnp.where(sentinel, 0.0, jnp.where(n_ge > topk, 1.0, 0.0))
    any_ties = jnp.max(has_ties) > 0.0

    def pos16(off, rows):
        return (off + lax.broadcasted_iota(jnp.int32, (rows, tq), 0)).astype(i16)

    def tie_cutoff():
        def cut_step(b, cut):
            cand = cut + jnp.left_shift(jnp.int32(1), 14 - b)
            c16 = cand.astype(i16)

            def flag(off, rows):
                tied = jnp.where(rows_at(hi_ref, off, rows) == t_hi16,
                                 jnp.where(rows_at(lo_ref, off, rows) == t_lo16, pos16(off, rows),
                                           jnp.asarray(dmax, i16)),
                                 jnp.asarray(dmax, i16))
                return jnp.where(tied < c16, one16, zero16)
            return jnp.where(count(flag) <= room, cand, cut)
        return lax.fori_loop(0, 15, cut_step, jnp.zeros((1, tq), jnp.int32))

    def write_plain():
        lo_min16 = jnp.where(sentinel, dmax, t_lo).astype(i16)

        def write_rows(off, rows, carry):
            hi = rows_at(hi_ref, off, rows)
            inner = jnp.where(rows_at(lo_ref, off, rows) >= lo_min16, sel0, seln)
            bias_ref[pl.ds(off, rows), :] = jnp.where(hi > t_hi16, sel0,
                                                      jnp.where(hi == t_hi16, inner, seln))
            return carry
        for_key_blocks(write_rows, 0)

    def write_with_ties():
        cut16 = jnp.where(sentinel, 0, tie_cutoff()).astype(i16)

        def write_rows(off, rows, carry):
            hi = rows_at(hi_ref, off, rows)
            lo = rows_at(lo_ref, off, rows)
            tie = jnp.where(pos16(off, rows) < cut16, sel0, seln)
            inner = jnp.where(lo > t_lo16, sel0, jnp.where(lo == t_lo16, tie, seln))
            bias_ref[pl.ds(off, rows), :] = jnp.where(hi > t_hi16, sel0,
                                                      jnp.where(hi == t_hi16, inner, seln))
            return carry
        for_key_blocks(write_rows, 0)

    lax.cond(any_ties, write_with_ties, write_plain)

    def fill_tile(j, carry):
        off = pl.multiple_of(j * tk, tk)
        bias_ref[pl.ds(off, tk), :] = jnp.full((tk, tq), NEG, BF16)
        return carry

    lax.fori_loop(n_tiles, seq // tk, fill_tile, 0)


def _select(qi, ki, wit):
    s = qi.shape[0]
    assert s % TK_SEL == 0 and s % TQ_ATT == 0 and 2 * TOPK <= s <= 32767
    return pl.pallas_call(
        functools.partial(_select_kernel, seq=s),
        grid=(s // TQ_ATT,),
        in_specs=[pl.BlockSpec((TQ_ATT, N_HEADS * LANE), lambda i: (i, 0)),
                  pl.BlockSpec((s, LANE), lambda i: (0, 0)),
                  pl.BlockSpec((BF16_ROWS, TQ_ATT), lambda i: (0, i))],
        out_specs=pl.BlockSpec((s, TQ_ATT), lambda i: (0, i)),
        out_shape=jax.ShapeDtypeStruct((s, s), BF16),
        scratch_shapes=[pltpu.VMEM((s, TQ_ATT), jnp.int16), pltpu.VMEM((s, TQ_ATT), jnp.int16)],
        compiler_params=_params(),
        name="dsa_select",
    )(qi, ki, wit)


def _flash_kernel(qi_of, kj_of, *refs, alibi, use_bias):
    if use_bias:
        q_ref, k_ref, vt_ref, bias_ref, o_ref, m_ref, acc_ref = refs
    else:
        q_ref, k_ref, vt_ref, o_ref, m_ref, acc_ref = refs
    tq, tk = TQ_ATT, TK_ATT
    p = pl.program_id(0)
    qi = qi_of[p]
    kj = kj_of[p]
    last = ((qi + 1) * tq - 1) // tk

    @pl.when(kj == 0)
    def _():
        m_ref[...] = jnp.full_like(m_ref, NEG)
        acc_ref[...] = jnp.zeros_like(acc_ref)

    def step(diagonal):
        base = bias_ref[...].astype(F32) if use_bias else None
        if diagonal:
            tpos = qi * tq + lax.broadcasted_iota(jnp.int32, (tk, tq), 1)
            spos = kj * tk + lax.broadcasted_iota(jnp.int32, (tk, tq), 0)
            if not use_bias:
                base = jnp.where(spos < (tpos // CHUNK + 1) * CHUNK, 0.0, NEG)
            if alibi:
                ahead = jnp.maximum(spos - tpos, 0).astype(F32)

        def logits(h):
            s = _dot_nt(k_ref[:, h * LANE:(h + 1) * LANE], q_ref[:, h * LANE:(h + 1) * LANE])
            if diagonal and alibi:
                return s + (base - (2.0 * ALIBI_SLOPES[h] * LOG2E) * ahead)
            return s if base is None else s + base

        pending = [logits(h) for h in range(QK_AHEAD)]
        for h in range(N_HEADS):
            rows = slice(h * V_ROWS, (h + 1) * V_ROWS)
            if h + QK_AHEAD < N_HEADS:
                pending.append(logits(h + QK_AHEAD))
            s = pending.pop(0)
            m_prev = m_ref[h:h + 1, :]
            m_new = jnp.maximum(m_prev, jnp.max(s, axis=0, keepdims=True))
            alpha = jnp.exp2(m_prev - m_new)
            pexp = jnp.exp2(s - m_new)
            acc_ref[rows, :] = alpha * acc_ref[rows, :] + _dot(vt_ref[rows, :], pexp.astype(BF16))
            m_ref[h:h + 1, :] = m_new

    @pl.when(kj != last)
    def _():
        step(False)

    @pl.when(kj == last)
    def _():
        step(True)
        outs = [acc_ref[h * V_ROWS:h * V_ROWS + D_V, :] / acc_ref[h * V_ROWS + D_V:h * V_ROWS + D_V + 1, :]
                for h in range(N_HEADS)]
        o_ref[...] = jnp.concatenate(outs, axis=0).T.astype(BF16)


def _flash(q, k, vt, bias, *, alibi):
    s = q.shape[0]
    nq = s // TQ_ATT
    pairs = [(i, j) for i in range(nq) for j in range(((i + 1) * TQ_ATT - 1) // TK_ATT + 1)]
    qi_of = jnp.asarray(np.array([a for a, _ in pairs], np.int32))
    kj_of = jnp.asarray(np.array([b for _, b in pairs], np.int32))
    use_bias = bias is not None
    in_specs = [pl.BlockSpec((TQ_ATT, N_HEADS * LANE), lambda p, qi, kj: (qi[p], 0)),
                pl.BlockSpec((TK_ATT, k.shape[1]), lambda p, qi, kj: (kj[p], 0)),
                pl.BlockSpec((N_HEADS * V_ROWS, TK_ATT), lambda p, qi, kj: (0, kj[p]))]
    args = [q, k, vt]
    if use_bias:
        in_specs.append(pl.BlockSpec((TK_ATT, TQ_ATT), lambda p, qi, kj: (kj[p], qi[p])))
        args.append(bias)
    kern = functools.partial(_flash_kernel, alibi=alibi, use_bias=use_bias)
    return pl.pallas_call(
        kern,
        grid_spec=pltpu.PrefetchScalarGridSpec(
            num_scalar_prefetch=2,
            grid=(len(pairs),),
            in_specs=in_specs,
            out_specs=pl.BlockSpec((TQ_ATT, N_HEADS * D_V), lambda p, qi, kj: (qi[p], 0)),
            scratch_shapes=[pltpu.VMEM((N_HEADS, TQ_ATT), F32),
                            pltpu.VMEM((N_HEADS * V_ROWS, TQ_ATT), F32)]),
        out_shape=jax.ShapeDtypeStruct((s, N_HEADS * D_V), BF16),
        compiler_params=_params(),
        name="flash_dsa" if use_bias else "flash_mla",
    )(qi_of, kj_of, *args)


def _out_ln_kernel(ya_ref, yb_ref, x_ref, mod_ref, w_ref, g_ref, b_ref, o_ref):
    d = D_MODEL
    na = ya_ref.shape[1]
    y = _dot(ya_ref[...], w_ref[0:na, :]) + _dot(yb_ref[...], w_ref[na:, :])
    z = DN_ALPHA * x_ref[...] + (1.0 + mod_ref[:, 2 * d:3 * d]) * y
    o_ref[...] = _layer_norm(z, g_ref[...], b_ref[...])


def _out_ln(ya, yb, x, mod, w, g, b):
    s, d = x.shape
    row = lambda n: pl.BlockSpec((TM, n), lambda i: (i, 0))
    full = lambda a: pl.BlockSpec(a.shape, lambda i: (0,) * a.ndim)
    g = g.reshape(1, d)
    b = b.reshape(1, d)
    return pl.pallas_call(
        _out_ln_kernel,
        grid=(s // TM,),
        in_specs=[row(ya.shape[1]), row(yb.shape[1]), row(d), full(mod), full(w), full(g), full(b)],
        out_specs=row(d),
        out_shape=jax.ShapeDtypeStruct((s, d), F32),
        compiler_params=_params(),
        name="out_ln",
    )(ya, yb, x, mod, w, g, b)


def _ffn_kernel(x_ref, mod_ref, wup_ref, cw_ref, wdn_ref, g_ref, b_ref, o_ref, carry_ref, act_ref):
    d = D_MODEL

    @pl.when(pl.program_id(0) == 0)
    def _():
        carry_ref[...] = jnp.zeros_like(carry_ref)

    x = x_ref[...]
    h = (x * (1.0 + mod_ref[:, 4 * d:5 * d]) + mod_ref[:, 3 * d:4 * d]).astype(BF16)
    for c in range(D_FF // FFN_CHUNK):
        va = c * FFN_CHUNK
        ga = D_FF + va
        uv = _dot(h, wup_ref[:, va:va + FFN_CHUNK])
        ug = _dot(h, wup_ref[:, ga:ga + FFN_CHUNK])
        val = _causal_conv3(uv, carry_ref[:, va:va + FFN_CHUNK], cw_ref[:, va:va + FFN_CHUNK])
        gate = _causal_conv3(ug, carry_ref[:, ga:ga + FFN_CHUNK], cw_ref[:, ga:ga + FFN_CHUNK])
        carry_ref[:, va:va + FFN_CHUNK] = uv[TM - SUBLANE:TM]
        carry_ref[:, ga:ga + FFN_CHUNK] = ug[TM - SUBLANE:TM]
        act_ref[:, va:va + FFN_CHUNK] = (gate * jax.nn.sigmoid(gate) * val).astype(BF16)
    y = _dot(act_ref[...], wdn_ref[...])
    z = DN_ALPHA * x + (1.0 + mod_ref[:, 5 * d:6 * d]) * y
    o_ref[...] = _layer_norm(z, g_ref[...], b_ref[...])


def _ffn(x, mod, w_up, conv_w, w_down, g, b):
    s, d = x.shape
    assert D_FF % FFN_CHUNK == 0
    row = lambda n: pl.BlockSpec((TM, n), lambda i: (i, 0))
    full = lambda a: pl.BlockSpec(a.shape, lambda i: (0,) * a.ndim)
    once = lambda a: pl.BlockSpec(a.shape, lambda i: (0,) * a.ndim, pipeline_mode=pl.Buffered(1))
    g = g.reshape(1, d)
    b = b.reshape(1, d)
    return pl.pallas_call(
        _ffn_kernel,
        grid=(s // TM,),
        in_specs=[row(d), full(mod), once(w_up), full(conv_w), once(w_down), full(g), full(b)],
        out_specs=row(d),
        out_shape=jax.ShapeDtypeStruct((s, d), F32),
        scratch_shapes=[pltpu.VMEM((SUBLANE, 2 * D_FF), F32), pltpu.VMEM((TM, D_FF), BF16)],
        compiler_params=_params(),
        name="conv_ffn",
    )(x, mod, w_up, conv_w, w_down, g, b)


_O_U = (0, 512)
_O_QLAT = (512, 896)
_O_KVLAT = (896, 1152)
_O_KR = (1152, 1280)
_O_KR_ROT = (1280, 1408)
_O_COLS = 1408


def _rotate_half_cols(w):
    half = w.shape[-1] // 2
    return jnp.concatenate([-w[..., half:], w[..., :half]], axis=-1)


def _odd_weights(w_in, w_uq, w_ukv):
    d = w_in.shape[0]
    o = C_WIDTH + Q_LORA + KV_LORA
    kr = w_in[:, o:o + D_ROPE]
    zl = jnp.zeros((d, D_NOPE), w_in.dtype)
    zr = jnp.zeros((d, LANE - D_NOPE - D_ROPE), w_in.dtype)
    w1 = jnp.concatenate([w_in[:, 0:o], zl, kr, zr, zl, _rotate_half_cols(kr), zr], axis=1)
    assert w1.shape[1] == _O_COLS
    dq = D_NOPE + D_ROPE
    uq = w_uq.reshape(Q_LORA, N_HEADS, dq)
    padq = jnp.zeros((Q_LORA, N_HEADS, LANE - dq), w_uq.dtype)
    wq1 = jnp.concatenate([uq, padq], axis=2).reshape(Q_LORA, N_HEADS * LANE)
    wq2 = jnp.concatenate([jnp.zeros((Q_LORA, N_HEADS, D_NOPE), w_uq.dtype),
                           _rotate_half_cols(uq[:, :, D_NOPE:]), padq], axis=2
                          ).reshape(Q_LORA, N_HEADS * LANE)
    ukv = w_ukv.reshape(KV_LORA, N_HEADS, D_NOPE + D_V)
    wk = jnp.concatenate([ukv[:, :, :D_NOPE], jnp.zeros((KV_LORA, N_HEADS, LANE - D_NOPE), w_ukv.dtype)],
                         axis=2).reshape(KV_LORA, N_HEADS * LANE)
    wvt = ukv[:, :, D_NOPE:].reshape(KV_LORA, N_HEADS * D_V).T
    return w1.astype(BF16), wq1.astype(BF16), wq2.astype(BF16), wk.astype(BF16), wvt.astype(BF16)


def _rope_lane_tables(seq):
    inv = ROPE_BASE ** (-jnp.arange(0, D_ROPE, 2, dtype=F32) / D_ROPE)
    ang = jnp.arange(seq, dtype=F32)[:, None] * inv[None, :]
    cos, sin = jnp.cos(ang), jnp.sin(ang)
    pad = jnp.zeros((seq, LANE - D_NOPE - D_ROPE), F32)
    cos_t = jnp.concatenate([jnp.ones((seq, D_NOPE), F32), cos, cos, pad], axis=1)
    sin_t = jnp.concatenate([jnp.zeros((seq, D_NOPE), F32), sin, sin, pad], axis=1)
    return cos_t, sin_t


def _odd_in_kernel(x_ref, mod_ref, w1_ref, wq1_ref, wq2_ref, wk_ref, wvt_ref, pw_ref, ps_ref,
                   qg_ref, kvg_ref, cos_ref, sin_ref,
                   yc_ref, q_ref, k_ref, vt_ref, carry_ref):
    d = D_MODEL
    i = pl.program_id(0)

    @pl.when(i == 0)
    def _():
        carry_ref[...] = jnp.zeros_like(carry_ref)

    h = (x_ref[...] * (1.0 + mod_ref[:, d:2 * d]) + mod_ref[:, 0:d]).astype(BF16)

    def proj(cols):
        return _dot(h, w1_ref[:, cols[0]:cols[1]])

    cos = cos_ref[...]
    sin = sin_ref[...]

    u = proj(_O_U)
    ext = jnp.concatenate([carry_ref[...], u], axis=0)
    pos = i * TM + lax.broadcasted_iota(jnp.int32, (TM, 1), 0)
    for g, win in enumerate(POOL_WINDOWS):
        sl = slice(g * C_GROUP_DIM, (g + 1) * C_GROUP_DIM)
        acc = ext[:, sl]
        shift = 1
        while shift < win:
            acc = acc + pltpu.roll(acc, shift, 0)
            shift *= 2
        cnt = jnp.minimum(pos + 1, win).astype(F32)
        pooled = acc[POOL_HALO:] / cnt - u[:, sl]
        mixed = _dot(pooled.astype(BF16), pw_ref[g])
        yc_ref[:, sl] = (mixed * ps_ref[:, sl]).astype(BF16)
    carry_ref[...] = u[TM - POOL_HALO:TM]

    r = _rms_norm(proj(_O_QLAT), qg_ref[...]).astype(BF16)
    qa = _dot(r, wq1_ref[...])
    qb = _dot(r, wq2_ref[...])
    kr = proj(_O_KR) * cos + proj(_O_KR_ROT) * sin
    rk = _rms_norm(proj(_O_KVLAT), kvg_ref[...]).astype(BF16)
    kn = _dot(rk, wk_ref[...])
    for hd in range(N_HEADS):
        sl = slice(hd * LANE, (hd + 1) * LANE)
        q_ref[:, sl] = ((qa[:, sl] * cos + qb[:, sl] * sin) * MLA_Q_SCALE).astype(BF16)
        k_ref[:, sl] = (kn[:, sl] + kr).astype(BF16)
    _store_values_t(vt_ref, _dot_nt(wvt_ref[...], rk))


def _odd_in(x, mod, w1, wq1, wq2, wk, wvt, pool_w, pool_scale, q_norm_g, kv_norm_g, cos_t, sin_t):
    s, d = x.shape
    row = lambda n: pl.BlockSpec((TM, n), lambda i: (i, 0))
    col = lambda n: pl.BlockSpec((n, TM), lambda i: (0, i))
    full = lambda a: pl.BlockSpec(a.shape, lambda i: (0,) * a.ndim)
    consts = [w1, wq1, wq2, wk, wvt, pool_w.astype(BF16), pool_scale.reshape(1, C_WIDTH),
              q_norm_g.reshape(1, Q_LORA), kv_norm_g.reshape(1, KV_LORA)]
    rows_out = lambda n, t: (row(n), jax.ShapeDtypeStruct((s, n), t))
    cols_out = lambda n, t: (col(n), jax.ShapeDtypeStruct((n, s), t))
    outs = [rows_out(C_WIDTH, BF16), rows_out(N_HEADS * LANE, BF16), rows_out(N_HEADS * LANE, BF16),
            cols_out(N_HEADS * V_ROWS, BF16)]
    return pl.pallas_call(
        _odd_in_kernel,
        grid=(s // TM,),
        in_specs=[row(d), full(mod)] + [full(a) for a in consts] + [row(LANE), row(LANE)],
        out_specs=[o[0] for o in outs],
        out_shape=[o[1] for o in outs],
        scratch_shapes=[pltpu.VMEM((POOL_HALO, C_WIDTH), F32)],
        compiler_params=_params(),
        name="odd_in",
    )(x, mod, *consts, cos_t, sin_t)


def kernel(x, c, ada_w, ada_b, ln_mix_g, ln_mix_b, ln_ffn_g, ln_ffn_b, ev_w_in, ev_conv_w, ev_w_out,
           od_w_in, pool_w, pool_scale, q_norm_g, w_uq, kv_norm_g, w_ukv, od_w_out,
           ffn_w_up, ffn_conv_w, ffn_w_down):
    bsz, seq, d = x.shape
    assert bsz == 1 and d == D_MODEL and seq % TM == 0 and seq % TK_ATT == 0
    assert SHORT_CONV == 3 and FFN_CONV == 3
    xs = x.reshape(seq, d)
    mods = _adaln(c, ada_w, ada_b)
    for l in range(DEPTH):
        mod = mods[l]
        if l % 2 == 0:
            e = l // 2
            w, wt = _even_weights(ev_w_in[e])
            q, k, vt, qi, ki, wit, yb = _even_in(xs, mod, w, wt, ev_conv_w[e], _alibi_q_features())
            bias = _select(qi, ki, wit)
            ya = _flash(q, k, vt, bias, alibi=True)
            xs = _out_ln(ya, yb, xs, mod, ev_w_out[e].astype(BF16), ln_mix_g[l], ln_mix_b[l])
        else:
            o = l // 2
            cos_t, sin_t = _rope_lane_tables(seq)
            ws = _odd_weights(od_w_in[o], w_uq[o], w_ukv[o])
            yc, q, k, vt = _odd_in(xs, mod, *ws, pool_w[o], pool_scale[o], q_norm_g[o], kv_norm_g[o],
                                   cos_t, sin_t)
            yd = _flash(q, k, vt, None, alibi=False)
            xs = _out_ln(yc, yd, xs, mod, od_w_out[o].astype(BF16), ln_mix_g[l], ln_mix_b[l])
        xs = _ffn(xs, mod, ffn_w_up[l].astype(BF16), ffn_conv_w[l], ffn_w_down[l].astype(BF16),
                  ln_ffn_g[l], ln_ffn_b[l])
    return xs.reshape(bsz, seq, d)
```

```python
import functools

import numpy as np
import jax
import jax.numpy as jnp
from jax import lax
from jax.experimental import pallas as pl
from jax.experimental.pallas import tpu as pltpu

D_MODEL = 1024
DEPTH = 2
CHUNK = 64
N_HEADS = 8
A_HEAD_DIM = 64
A_WIDTH = N_HEADS * A_HEAD_DIM
IDX_DIM = 64
TOPK = 256
B_WIDTH = 512
SHORT_CONV = 3
C_WIDTH = 512
POOL_WINDOWS = (2, 4, 8, 16)
C_GROUP_DIM = C_WIDTH // len(POOL_WINDOWS)
D_NOPE = 64
D_ROPE = 32
D_V = 64
Q_LORA = 384
KV_LORA = 256
ROPE_BASE = 10000.0
D_FF = 2816
FFN_CONV = 3
LN_EPS = 1e-5
RMS_EPS = 1e-6
DN_ALPHA = (2 * DEPTH) ** 0.25
NEG = -1e30
IDX_W_SCALE = N_HEADS ** -0.5 * IDX_DIM ** -0.5
LOG2E = 1.4426950408889634
DSA_Q_SCALE = A_HEAD_DIM ** -0.5 * LOG2E
MLA_Q_SCALE = (D_NOPE + D_ROPE) ** -0.5 * LOG2E
ALIBI_SLOPES = tuple(2.0 ** (-8.0 * (i + 1) / N_HEADS) for i in range(N_HEADS))

LANE = 128
SUBLANE = 8
BF16_ROWS = 16
V_ROWS = D_V + BF16_ROWS
VMEM_LIMIT_BYTES = 56 * 1024 * 1024

TM = 512
FFN_CHUNK = 256
TQ_ATT = 256
TK_SEL = 512
TK_ATT = 1024
QK_AHEAD = 4
POOL_HALO = 16

INT_MIN = -2 ** 31
F32 = jnp.float32
BF16 = jnp.bfloat16


def _params():
    return pltpu.CompilerParams(dimension_semantics=("arbitrary",),
                                vmem_limit_bytes=VMEM_LIMIT_BYTES)


def _dot(a, b):
    return jnp.dot(a, b, preferred_element_type=F32)


def _dot_nt(a, b):
    return lax.dot_general(a, b, (((1,), (1,)), ((), ())), preferred_element_type=F32)


def _layer_norm(z, g, b):
    mu = jnp.mean(z, axis=-1, keepdims=True)
    zc = z - mu
    var = jnp.mean(zc * zc, axis=-1, keepdims=True)
    return zc * lax.rsqrt(var + LN_EPS) * g + b


def _rms_norm(z, g):
    return z * lax.rsqrt(jnp.mean(z * z, axis=-1, keepdims=True) + RMS_EPS) * g


def _causal_conv3(u, prev, w):
    w0, w1, w2 = w[0:1], w[1:2], w[2:3]
    y = pltpu.roll(u, 2, 0) * w0 + pltpu.roll(u, 1, 0) * w1 + u * w2
    head = u[0:SUBLANE]
    r = lax.broadcasted_iota(jnp.int32, (SUBLANE, 1), 0)
    h1 = jnp.where(r == 0, prev[7:8], pltpu.roll(head, 1, 0))
    h2 = jnp.where(r == 0, prev[6:7], jnp.where(r == 1, prev[7:8], pltpu.roll(head, 2, 0)))
    yh = h2 * w0 + h1 * w1 + head * w2
    return jnp.concatenate([yh, y[SUBLANE:]], axis=0)


def _fold_rows(c, rows_out):
    rows = c.shape[0]
    while rows > rows_out:
        rows //= 2
        c = c[0:rows] + c[rows:2 * rows]
    return c


def _store_values_t(vt_ref, vt):
    t = vt.shape[1]
    ones_row = jnp.where(lax.broadcasted_iota(jnp.int32, (BF16_ROWS, t), 0) == 0, 1.0, 0.0)
    for h in range(N_HEADS):
        vt_ref[h * V_ROWS:h * V_ROWS + D_V, :] = vt[h * D_V:(h + 1) * D_V, :].astype(BF16)
        vt_ref[h * V_ROWS + D_V:(h + 1) * V_ROWS, :] = ones_row.astype(BF16)


def _adaln_kernel(c_ref, w_ref, b_ref, o_ref):
    c = c_ref[...]
    cond = c * jax.nn.sigmoid(c)
    o_ref[0] = jnp.sum(cond * w_ref[0], axis=0, keepdims=True) + b_ref[0]


def _adaln(c, ada_w, ada_b):
    depth, d, n = ada_w.shape
    tn = 1536
    return pl.pallas_call(
        _adaln_kernel,
        grid=(depth, n // tn),
        in_specs=[pl.BlockSpec((d, 1), lambda l, j: (0, 0)),
                  pl.BlockSpec((1, d, tn), lambda l, j: (l, 0, j)),
                  pl.BlockSpec((1, 1, tn), lambda l, j: (l, 0, j))],
        out_specs=pl.BlockSpec((1, 1, tn), lambda l, j: (l, 0, j)),
        out_shape=jax.ShapeDtypeStruct((depth, 1, n), F32),
        compiler_params=pltpu.CompilerParams(dimension_semantics=("arbitrary", "arbitrary"),
                                             vmem_limit_bytes=VMEM_LIMIT_BYTES),
        name="adaln",
    )(c.reshape(d, 1), ada_w, ada_b.reshape(depth, 1, n))


_E_Q = (0, 1024)
_E_K = (1024, 2048)
_E_QI = (2048, 3072)
_E_KI = (3072, 3200)
_E_BG = (3200, 3712)
_E_CG = (3712, 4224)
_E_XB = (4224, 4736)
_E_COLS = 4736
_ET_V = (0, 512)
_ET_WI = (512, 528)
_ET_ROWS = 528
_FEAT_LANE = A_HEAD_DIM
_N_LOG2E_TERMS = 3
_POS_RADIX = 128


def _pad_heads(w):
    d = w.shape[0]
    w3 = w.reshape(d, N_HEADS, A_HEAD_DIM)
    return jnp.concatenate([w3, jnp.zeros_like(w3)], axis=2).reshape(d, N_HEADS * LANE)


def _alibi_q_features():
    terms, rest = [], np.float64(LOG2E)
    for _ in range(_N_LOG2E_TERMS):
        t = np.float64(np.asarray(rest, np.float32).astype(jnp.bfloat16).astype(np.float32))
        terms.append(t)
        rest = rest - t
    row = np.zeros((N_HEADS, LANE), np.float32)
    for h in range(N_HEADS):
        for n, t in enumerate(terms):
            row[h, _FEAT_LANE + n] = ALIBI_SLOPES[h] * _POS_RADIX * t
            row[h, _FEAT_LANE + _N_LOG2E_TERMS + n] = ALIBI_SLOPES[h] * t
    return jnp.asarray(row.reshape(1, N_HEADS * LANE))


def _even_weights(w_in):
    d = w_in.shape[0]
    a = A_WIDTH
    q = _pad_heads(w_in[:, 0:a])
    k = _pad_heads(w_in[:, a:2 * a])
    v = w_in[:, 2 * a:3 * a]
    qi = _pad_heads(w_in[:, 3 * a:4 * a])
    o = 4 * a
    ki = w_in[:, o:o + IDX_DIM]
    wi = w_in[:, o + IDX_DIM:o + IDX_DIM + N_HEADS]
    o = o + IDX_DIM + N_HEADS
    rest = w_in[:, o:o + 3 * B_WIDTH]
    w = jnp.concatenate([q, k, qi, ki, ki, rest], axis=1)
    assert w.shape[1] == _E_COLS
    wt = jnp.concatenate([v.T, wi.T, jnp.zeros((BF16_ROWS - N_HEADS, d), w_in.dtype)], axis=0)
    assert wt.shape[0] == _ET_ROWS
    return w.astype(BF16), wt.astype(BF16)


def _even_in_kernel(x_ref, mod_ref, w_ref, wt_ref, cw_ref, qf_ref,
                    q_ref, k_ref, vt_ref, qi_ref, ki_ref, wit_ref, yb_ref, carry_ref):
    d = D_MODEL
    i = pl.program_id(0)

    @pl.when(i == 0)
    def _():
        carry_ref[...] = jnp.zeros_like(carry_ref)

    h = (x_ref[...] * (1.0 + mod_ref[:, d:2 * d]) + mod_ref[:, 0:d]).astype(BF16)

    def proj(cols):
        return _dot(h, w_ref[:, cols[0]:cols[1]])

    def proj_t(rows):
        return _dot_nt(wt_ref[rows[0]:rows[1], :], h)

    q_ref[...] = (proj(_E_Q) * DSA_Q_SCALE + qf_ref[...]).astype(BF16)
    pos = i * TM + lax.broadcasted_iota(jnp.int32, (TM, LANE), 0)
    lane = lax.broadcasted_iota(jnp.int32, (TM, LANE), 1) - _FEAT_LANE
    pos_hi = (pos // _POS_RADIX).astype(F32)
    pos_lo = (pos % _POS_RADIX).astype(F32)
    kfeat = jnp.where(lane < 0, 0.0,
                      jnp.where(lane < _N_LOG2E_TERMS, pos_hi,
                                jnp.where(lane < 2 * _N_LOG2E_TERMS, pos_lo, 0.0)))
    kproj = proj(_E_K)
    for hd in range(N_HEADS):
        sl = slice(hd * LANE, (hd + 1) * LANE)
        k_ref[:, sl] = (kproj[:, sl] + kfeat).astype(BF16)
    _store_values_t(vt_ref, proj_t(_ET_V))
    qi_ref[...] = proj(_E_QI).astype(BF16)
    ki_ref[...] = proj(_E_KI).astype(BF16)
    wit_ref[...] = proj_t(_ET_WI) * IDX_W_SCALE
    g = proj(_E_CG) * proj(_E_XB)
    y = _causal_conv3(g, carry_ref[...], cw_ref[...])
    yb_ref[...] = (proj(_E_BG) * y).astype(BF16)
    carry_ref[...] = g[TM - SUBLANE:TM]


def _even_in(x, mod, w, wt, conv_w, qfeat):
    s, d = x.shape
    row = lambda n: pl.BlockSpec((TM, n), lambda i: (i, 0))
    col = lambda n: pl.BlockSpec((n, TM), lambda i: (0, i))
    full = lambda a: pl.BlockSpec(a.shape, lambda i: (0,) * a.ndim)
    rows_out = lambda n, t: (row(n), jax.ShapeDtypeStruct((s, n), t))
    cols_out = lambda n, t: (col(n), jax.ShapeDtypeStruct((n, s), t))
    outs = [rows_out(N_HEADS * LANE, BF16), rows_out(N_HEADS * LANE, BF16), cols_out(N_HEADS * V_ROWS, BF16),
            rows_out(N_HEADS * LANE, BF16), rows_out(LANE, BF16), cols_out(BF16_ROWS, F32),
            rows_out(B_WIDTH, BF16)]
    return pl.pallas_call(
        _even_in_kernel,
        grid=(s // TM,),
        in_specs=[row(d), full(mod), full(w), full(wt), full(conv_w), full(qfeat)],
        out_specs=[o[0] for o in outs],
        out_shape=[o[1] for o in outs],
        scratch_shapes=[pltpu.VMEM((SUBLANE, B_WIDTH), F32)],
        compiler_params=_params(),
        name="even_in",
    )(x, mod, w, wt, conv_w, qfeat)


def _select_kernel(qi_ref, ki_ref, wt_ref, bias_ref, hi_ref, lo_ref, m1_ref, m2_ref, mc_ref, *, seq):
    tq, tk = TQ_ATT, TK_SEL
    i16 = jnp.int16
    i = pl.program_id(0)
    start = i * tq
    n_tiles = (start + tq + tk - 1) // tk
    tpos = start + lax.broadcasted_iota(jnp.int32, (1, tq), 1)
    limit = (tpos // CHUNK + 1) * CHUNK
    int_min = jnp.int32(INT_MIN)
    dmin, dmax = -2 ** 15, 2 ** 15 - 1
    zero16, one16 = jnp.zeros((), i16), jnp.ones((), i16)
    sel0, seln = jnp.zeros((), BF16), jnp.full((), NEG, BF16)

    def key_pos(off):
        return off + lax.broadcasted_iota(jnp.int32, (tk, tq), 0)

    def score_tile(j, carry, *, masked):
        off = pl.multiple_of(j * tk, tk)
        kt = ki_ref[pl.ds(off, tk), :]
        score = jnp.zeros((tk, tq), F32)
        for h in range(N_HEADS):
            rel = _dot_nt(kt, qi_ref[:, h * LANE:(h + 1) * LANE])
            score = score + wt_ref[h:h + 1, :] * jnp.maximum(rel, 0.0)
        bits = pltpu.bitcast(score, jnp.int32)
        key = bits ^ ((bits >> 31) & jnp.int32(0x7FFFFFFF))
        key = jnp.where(bits == int_min, 0, key)
        if masked:
            key = jnp.where(key_pos(off) < limit, key, int_min)
        hi_ref[pl.ds(off, tk), :] = (key >> 16).astype(i16)
        lo_ref[pl.ds(off, tk), :] = key.astype(i16) ^ jnp.asarray(dmin, i16)
        return carry

    n_before = start // tk
    lax.fori_loop(0, n_before, functools.partial(score_tile, masked=False), 0)
    lax.fori_loop(n_before, n_tiles, functools.partial(score_tile, masked=True), 0)

    def rows_at(ref, off, rows):
        return ref[pl.ds(off, rows), :]

    def for_key_tiles(fn, carry):
        return lax.fori_loop(0, n_tiles, lambda j, c: fn(pl.multiple_of(j * tk, tk), tk, c), carry)

    def per_query(c):
        return jnp.sum(c.astype(jnp.int32).astype(F32), axis=0, keepdims=True)

    def count(flag):
        def body(off, rows, acc):
            return acc + _fold_rows(flag(off, rows), BF16_ROWS)
        return per_query(for_key_tiles(body, jnp.zeros((BF16_ROWS, tq), i16)))

    def radix_search(count_ge, need):
        def bit_step(b, thr):
            cand = thr + jnp.left_shift(jnp.int32(1), 15 - b)
            return jnp.where(count_ge(cand.astype(i16)) >= need, cand, thr)
        return lax.fori_loop(0, 16, bit_step, jnp.full((1, tq), dmin, jnp.int32))

    topk = float(TOPK)
    t_hi = radix_search(lambda c16: count(
        lambda off, rows: jnp.where(rows_at(hi_ref, off, rows) >= c16, one16, zero16)), topk)
    t_hi16 = t_hi.astype(i16)
    n_above = count(lambda off, rows: jnp.where(rows_at(hi_ref, off, rows) > t_hi16, one16, zero16))
    need_lo = topk - n_above

    m1_ref[...] = jnp.full((tk, tq), dmin, i16)
    m2_ref[...] = jnp.full((tk, tq), dmin, i16)
    mc_ref[...] = jnp.zeros((tk, tq), i16)

    def fold_members(off, rows, carry):
        member = rows_at(hi_ref, off, rows) == t_hi16
        x = jnp.where(member, rows_at(lo_ref, off, rows), jnp.asarray(dmin, i16))
        lo_ref[pl.ds(off, rows), :] = x
        a, b = m1_ref[...], m2_ref[...]
        above = x > a
        m1_ref[...] = jnp.where(above, x, a)
        second = jnp.where(above, a, x)
        m2_ref[...] = jnp.where(second > b, second, b)
        mc_ref[...] = mc_ref[...] + jnp.where(member, one16, zero16)
        return carry

    for_key_tiles(fold_members, 0)

    def count_slots(flag):
        return per_query(_fold_rows(flag(m1_ref[...]) + flag(m2_ref[...]), BF16_ROWS))

    def level2_slots():
        t = radix_search(lambda c16: count_slots(lambda m: jnp.where(m >= c16, one16, zero16)), need_lo)
        t16 = t.astype(i16)
        n_members = per_query(_fold_rows(mc_ref[...], BF16_ROWS))
        gt = count_slots(lambda m: jnp.where(m > t16, one16, zero16))
        ge = jnp.where(t == dmin, n_members, count_slots(lambda m: jnp.where(m >= t16, one16, zero16)))
        return t, gt, ge

    def level2_full():
        t = radix_search(lambda c16: count(
            lambda off, rows: jnp.where(rows_at(lo_ref, off, rows) >= c16, one16, zero16)), need_lo)
        t16 = t.astype(i16)
        gt = count(lambda off, rows: jnp.where(rows_at(lo_ref, off, rows) > t16, one16, zero16))
        ge = count(lambda off, rows: jnp.where(
            rows_at(hi_ref, off, rows) == t_hi16,
            jnp.where(rows_at(lo_ref, off, rows) >= t16, one16, zero16), zero16))
        return t, gt, ge

    crowded = per_query(_fold_rows(jnp.where(mc_ref[...] > jnp.asarray(2, i16), one16, zero16), BF16_ROWS))
    t_lo, n_gt_lo, n_ge_lo = lax.cond(jnp.max(crowded) > 0.0, level2_full, level2_slots)
    t_lo16 = t_lo.astype(i16)
    n_gt = n_above + n_gt_lo
    n_ge = n_above + n_ge_lo
    room = topk - n_gt
    sentinel = t_hi == dmin
    has_ties = jnp.where(sentinel, 0.0, jnp.where(n_ge > topk, 1.0, 0.0))
    any_ties = jnp.max(has_ties) > 0.0

    def pos16(off, rows):
        return (off + lax.broadcasted_iota(jnp.int32, (rows, tq), 0)).astype(i16)

    def tie_cutoff():
        def cut_step(b, cut):
            cand = cut + jnp.left_shift(jnp.int32(1), 14 - b)
            c16 = cand.astype(i16)

            def flag(off, rows):
                tied = jnp.where(rows_at(hi_ref, off, rows) == t_hi16,
                                 jnp.where(rows_at(lo_ref, off, rows) == t_lo16, pos16(off, rows),
                                           jnp.asarray(dmax, i16)),
                                 jnp.asarray(dmax, i16))
                return jnp.where(tied < c16, one16, zero16)
            return jnp.where(count(flag) <= room, cand, cut)
        return lax.fori_loop(0, 15, cut_step, jnp.zeros((1, tq), jnp.int32))

    def write_plain():
        lo_min16 = jnp.where(sentinel, dmax, t_lo).astype(i16)

        def write_rows(off, rows, carry):
            hi = rows_at(hi_ref, off, rows)
            inner = jnp.where(rows_at(lo_ref, off, rows) >= lo_min16, sel0, seln)
            bias_ref[pl.ds(off, rows), :] = jnp.where(hi > t_hi16, sel0,
                                                      jnp.where(hi == t_hi16, inner, seln))
            return carry
        for_key_tiles(write_rows, 0)

    def write_with_ties():
        cut16 = jnp.where(sentinel, 0, tie_cutoff()).astype(i16)

        def write_rows(off, rows, carry):
            hi = rows_at(hi_ref, off, rows)
            lo = rows_at(lo_ref, off, rows)
            tie = jnp.where(pos16(off, rows) < cut16, sel0, seln)
            inner = jnp.where(lo > t_lo16, sel0, jnp.where(lo == t_lo16, tie, seln))
            bias_ref[pl.ds(off, rows), :] = jnp.where(hi > t_hi16, sel0,
                                                      jnp.where(hi == t_hi16, inner, seln))
            return carry
        for_key_tiles(write_rows, 0)

    lax.cond(any_ties, write_with_ties, write_plain)

    def fill_tile(j, carry):
        off = pl.multiple_of(j * tk, tk)
        bias_ref[pl.ds(off, tk), :] = jnp.full((tk, tq), NEG, BF16)
        return carry

    lax.fori_loop(n_tiles, seq // tk, fill_tile, 0)


def _select(qi, ki, wit):
    s = qi.shape[0]
    assert s % TK_SEL == 0 and s % TQ_ATT == 0 and 2 * TOPK <= s <= 32767
    return pl.pallas_call(
        functools.partial(_select_kernel, seq=s),
        grid=(s // TQ_ATT,),
        in_specs=[pl.BlockSpec((TQ_ATT, N_HEADS * LANE), lambda i: (i, 0)),
                  pl.BlockSpec((s, LANE), lambda i: (0, 0)),
                  pl.BlockSpec((BF16_ROWS, TQ_ATT), lambda i: (0, i))],
        out_specs=pl.BlockSpec((s, TQ_ATT), lambda i: (0, i)),
        out_shape=jax.ShapeDtypeStruct((s, s), BF16),
        scratch_shapes=[pltpu.VMEM((s, TQ_ATT), jnp.int16), pltpu.VMEM((s, TQ_ATT), jnp.int16)]
        + [pltpu.VMEM((TK_SEL, TQ_ATT), jnp.int16)] * 3,
        compiler_params=_params(),
        name="dsa_select",
    )(qi, ki, wit)


def _flash_kernel(qi_of, kj_of, *refs, alibi, use_bias):
    if use_bias:
        q_ref, k_ref, vt_ref, bias_ref, o_ref, m_ref, acc_ref = refs
    else:
        q_ref, k_ref, vt_ref, o_ref, m_ref, acc_ref = refs
    tq, tk = TQ_ATT, TK_ATT
    p = pl.program_id(0)
    qi = qi_of[p]
    kj = kj_of[p]
    last = ((qi + 1) * tq - 1) // tk

    @pl.when(kj == 0)
    def _():
        m_ref[...] = jnp.full_like(m_ref, NEG)
        acc_ref[...] = jnp.zeros_like(acc_ref)

    def step(diagonal):
        base = bias_ref[...].astype(F32) if use_bias else None
        if diagonal:
            tpos = qi * tq + lax.broadcasted_iota(jnp.int32, (tk, tq), 1)
            spos = kj * tk + lax.broadcasted_iota(jnp.int32, (tk, tq), 0)
            if not use_bias:
                base = jnp.where(spos < (tpos // CHUNK + 1) * CHUNK, 0.0, NEG)
            if alibi:
                ahead = jnp.maximum(spos - tpos, 0).astype(F32)

        def logits(h):
            s = _dot_nt(k_ref[:, h * LANE:(h + 1) * LANE], q_ref[:, h * LANE:(h + 1) * LANE])
            if diagonal and alibi:
                return s + (base - (2.0 * ALIBI_SLOPES[h] * LOG2E) * ahead)
            return s if base is None else s + base

        pending = [logits(h) for h in range(QK_AHEAD)]
        for h in range(N_HEADS):
            rows = slice(h * V_ROWS, (h + 1) * V_ROWS)
            if h + QK_AHEAD < N_HEADS:
                pending.append(logits(h + QK_AHEAD))
            s = pending.pop(0)
            m_prev = m_ref[h:h + 1, :]
            m_new = jnp.maximum(m_prev, jnp.max(s, axis=0, keepdims=True))
            alpha = jnp.exp2(m_prev - m_new)
            pexp = jnp.exp2(s - m_new)
            acc_ref[rows, :] = alpha * acc_ref[rows, :] + _dot(vt_ref[rows, :], pexp.astype(BF16))
            m_ref[h:h + 1, :] = m_new

    @pl.when(kj != last)
    def _():
        step(False)

    @pl.when(kj == last)
    def _():
        step(True)
        outs = [acc_ref[h * V_ROWS:h * V_ROWS + D_V, :] / acc_ref[h * V_ROWS + D_V:h * V_ROWS + D_V + 1, :]
                for h in range(N_HEADS)]
        o_ref[...] = jnp.concatenate(outs, axis=0).T.astype(BF16)


def _flash(q, k, vt, bias, *, alibi):
    s = q.shape[0]
    nq = s // TQ_ATT
    pairs = [(i, j) for i in range(nq) for j in range(((i + 1) * TQ_ATT - 1) // TK_ATT + 1)]
    qi_of = jnp.asarray(np.array([a for a, _ in pairs], np.int32))
    kj_of = jnp.asarray(np.array([b for _, b in pairs], np.int32))
    use_bias = bias is not None
    in_specs = [pl.BlockSpec((TQ_ATT, N_HEADS * LANE), lambda p, qi, kj: (qi[p], 0)),
                pl.BlockSpec((TK_ATT, k.shape[1]), lambda p, qi, kj: (kj[p], 0)),
                pl.BlockSpec((N_HEADS * V_ROWS, TK_ATT), lambda p, qi, kj: (0, kj[p]))]
    args = [q, k, vt]
    if use_bias:
        in_specs.append(pl.BlockSpec((TK_ATT, TQ_ATT), lambda p, qi, kj: (kj[p], qi[p])))
        args.append(bias)
    kern = functools.partial(_flash_kernel, alibi=alibi, use_bias=use_bias)
    return pl.pallas_call(
        kern,
        grid_spec=pltpu.PrefetchScalarGridSpec(
            num_scalar_prefetch=2,
            grid=(len(pairs),),
            in_specs=in_specs,
            out_specs=pl.BlockSpec((TQ_ATT, N_HEADS * D_V), lambda p, qi, kj: (qi[p], 0)),
            scratch_shapes=[pltpu.VMEM((N_HEADS, TQ_ATT), F32),
                            pltpu.VMEM((N_HEADS * V_ROWS, TQ_ATT), F32)]),
        out_shape=jax.ShapeDtypeStruct((s, N_HEADS * D_V), BF16),
        compiler_params=_params(),
        name="flash_dsa" if use_bias else "flash_mla",
    )(qi_of, kj_of, *args)


def _out_ln_kernel(ya_ref, yb_ref, x_ref, mod_ref, w_ref, g_ref, b_ref, o_ref):
    d = D_MODEL
    na = ya_ref.shape[1]
    y = _dot(ya_ref[...], w_ref[0:na, :]) + _dot(yb_ref[...], w_ref[na:, :])
    z = DN_ALPHA * x_ref[...] + (1.0 + mod_ref[:, 2 * d:3 * d]) * y
    o_ref[...] = _layer_norm(z, g_ref[...], b_ref[...])


def _out_ln(ya, yb, x, mod, w, g, b):
    s, d = x.shape
    row = lambda n: pl.BlockSpec((TM, n), lambda i: (i, 0))
    full = lambda a: pl.BlockSpec(a.shape, lambda i: (0,) * a.ndim)
    g = g.reshape(1, d)
    b = b.reshape(1, d)
    return pl.pallas_call(
        _out_ln_kernel,
        grid=(s // TM,),
        in_specs=[row(ya.shape[1]), row(yb.shape[1]), row(d), full(mod), full(w), full(g), full(b)],
        out_specs=row(d),
        out_shape=jax.ShapeDtypeStruct((s, d), F32),
        compiler_params=_params(),
        name="out_ln",
    )(ya, yb, x, mod, w, g, b)


def _ffn_kernel(x_ref, mod_ref, wup_ref, cw_ref, wdn_ref, g_ref, b_ref, o_ref, carry_ref, act_ref):
    d = D_MODEL

    @pl.when(pl.program_id(0) == 0)
    def _():
        carry_ref[...] = jnp.zeros_like(carry_ref)

    x = x_ref[...]
    h = (x * (1.0 + mod_ref[:, 4 * d:5 * d]) + mod_ref[:, 3 * d:4 * d]).astype(BF16)
    for c in range(D_FF // FFN_CHUNK):
        va = c * FFN_CHUNK
        ga = D_FF + va
        uv = _dot(h, wup_ref[:, va:va + FFN_CHUNK])
        ug = _dot(h, wup_ref[:, ga:ga + FFN_CHUNK])
        val = _causal_conv3(uv, carry_ref[:, va:va + FFN_CHUNK], cw_ref[:, va:va + FFN_CHUNK])
        gate = _causal_conv3(ug, carry_ref[:, ga:ga + FFN_CHUNK], cw_ref[:, ga:ga + FFN_CHUNK])
        carry_ref[:, va:va + FFN_CHUNK] = uv[TM - SUBLANE:TM]
        carry_ref[:, ga:ga + FFN_CHUNK] = ug[TM - SUBLANE:TM]
        act_ref[:, va:va + FFN_CHUNK] = (gate * jax.nn.sigmoid(gate) * val).astype(BF16)
    y = _dot(act_ref[...], wdn_ref[...])
    z = DN_ALPHA * x + (1.0 + mod_ref[:, 5 * d:6 * d]) * y
    o_ref[...] = _layer_norm(z, g_ref[...], b_ref[...])


def _ffn(x, mod, w_up, conv_w, w_down, g, b):
    s, d = x.shape
    assert D_FF % FFN_CHUNK == 0
    row = lambda n: pl.BlockSpec((TM, n), lambda i: (i, 0))
    full = lambda a: pl.BlockSpec(a.shape, lambda i: (0,) * a.ndim)
    once = lambda a: pl.BlockSpec(a.shape, lambda i: (0,) * a.ndim, pipeline_mode=pl.Buffered(1))
    g = g.reshape(1, d)
    b = b.reshape(1, d)
    return pl.pallas_call(
        _ffn_kernel,
        grid=(s // TM,),
        in_specs=[row(d), full(mod), once(w_up), full(conv_w), once(w_down), full(g), full(b)],
        out_specs=row(d),
        out_shape=jax.ShapeDtypeStruct((s, d), F32),
        scratch_shapes=[pltpu.VMEM((SUBLANE, 2 * D_FF), F32), pltpu.VMEM((TM, D_FF), BF16)],
        compiler_params=_params(),
        name="conv_ffn",
    )(x, mod, w_up, conv_w, w_down, g, b)


_O_U = (0, 512)
_O_QLAT = (512, 896)
_O_KVLAT = (896, 1152)
_O_KR = (1152, 1280)
_O_KR_ROT = (1280, 1408)
_O_COLS = 1408


def _rotate_half_cols(w):
    half = w.shape[-1] // 2
    return jnp.concatenate([-w[..., half:], w[..., :half]], axis=-1)


def _odd_weights(w_in, w_uq, w_ukv):
    d = w_in.shape[0]
    o = C_WIDTH + Q_LORA + KV_LORA
    kr = w_in[:, o:o + D_ROPE]
    zl = jnp.zeros((d, D_NOPE), w_in.dtype)
    zr = jnp.zeros((d, LANE - D_NOPE - D_ROPE), w_in.dtype)
    w1 = jnp.concatenate([w_in[:, 0:o], zl, kr, zr, zl, _rotate_half_cols(kr), zr], axis=1)
    assert w1.shape[1] == _O_COLS
    dq = D_NOPE + D_ROPE
    uq = w_uq.reshape(Q_LORA, N_HEADS, dq)
    padq = jnp.zeros((Q_LORA, N_HEADS, LANE - dq), w_uq.dtype)
    wq1 = jnp.concatenate([uq, padq], axis=2).reshape(Q_LORA, N_HEADS * LANE)
    wq2 = jnp.concatenate([jnp.zeros((Q_LORA, N_HEADS, D_NOPE), w_uq.dtype),
                           _rotate_half_cols(uq[:, :, D_NOPE:]), padq], axis=2
                          ).reshape(Q_LORA, N_HEADS * LANE)
    ukv = w_ukv.reshape(KV_LORA, N_HEADS, D_NOPE + D_V)
    wk = jnp.concatenate([ukv[:, :, :D_NOPE], jnp.zeros((KV_LORA, N_HEADS, LANE - D_NOPE), w_ukv.dtype)],
                         axis=2).reshape(KV_LORA, N_HEADS * LANE)
    wvt = ukv[:, :, D_NOPE:].reshape(KV_LORA, N_HEADS * D_V).T
    return w1.astype(BF16), wq1.astype(BF16), wq2.astype(BF16), wk.astype(BF16), wvt.astype(BF16)


def _rope_lane_tables(seq):
    inv = ROPE_BASE ** (-jnp.arange(0, D_ROPE, 2, dtype=F32) / D_ROPE)
    ang = jnp.arange(seq, dtype=F32)[:, None] * inv[None, :]
    cos, sin = jnp.cos(ang), jnp.sin(ang)
    pad = jnp.zeros((seq, LANE - D_NOPE - D_ROPE), F32)
    cos_t = jnp.concatenate([jnp.ones((seq, D_NOPE), F32), cos, cos, pad], axis=1)
    sin_t = jnp.concatenate([jnp.zeros((seq, D_NOPE), F32), sin, sin, pad], axis=1)
    return cos_t, sin_t


def _odd_in_kernel(x_ref, mod_ref, w1_ref, wq1_ref, wq2_ref, wk_ref, wvt_ref, pw_ref, ps_ref,
                   qg_ref, kvg_ref, cos_ref, sin_ref,
                   yc_ref, q_ref, k_ref, vt_ref, carry_ref):
    d = D_MODEL
    i = pl.program_id(0)

    @pl.when(i == 0)
    def _():
        carry_ref[...] = jnp.zeros_like(carry_ref)

    h = (x_ref[...] * (1.0 + mod_ref[:, d:2 * d]) + mod_ref[:, 0:d]).astype(BF16)

    def proj(cols):
        return _dot(h, w1_ref[:, cols[0]:cols[1]])

    cos = cos_ref[...]
    sin = sin_ref[...]

    u = proj(_O_U)
    ext = jnp.concatenate([carry_ref[...], u], axis=0)
    pos = i * TM + lax.broadcasted_iota(jnp.int32, (TM, 1), 0)
    for g, win in enumerate(POOL_WINDOWS):
        sl = slice(g * C_GROUP_DIM, (g + 1) * C_GROUP_DIM)
        acc = ext[:, sl]
        shift = 1
        while shift < win:
            acc = acc + pltpu.roll(acc, shift, 0)
            shift *= 2
        cnt = jnp.minimum(pos + 1, win).astype(F32)
        pooled = acc[POOL_HALO:] / cnt - u[:, sl]
        mixed = _dot(pooled.astype(BF16), pw_ref[g])
        yc_ref[:, sl] = (mixed * ps_ref[:, sl]).astype(BF16)
    carry_ref[...] = u[TM - POOL_HALO:TM]

    r = _rms_norm(proj(_O_QLAT), qg_ref[...]).astype(BF16)
    qa = _dot(r, wq1_ref[...])
    qb = _dot(r, wq2_ref[...])
    kr = proj(_O_KR) * cos + proj(_O_KR_ROT) * sin
    rk = _rms_norm(proj(_O_KVLAT), kvg_ref[...]).astype(BF16)
    kn = _dot(rk, wk_ref[...])
    for hd in range(N_HEADS):
        sl = slice(hd * LANE, (hd + 1) * LANE)
        q_ref[:, sl] = ((qa[:, sl] * cos + qb[:, sl] * sin) * MLA_Q_SCALE).astype(BF16)
        k_ref[:, sl] = (kn[:, sl] + kr).astype(BF16)
    _store_values_t(vt_ref, _dot_nt(wvt_ref[...], rk))


def _odd_in(x, mod, w1, wq1, wq2, wk, wvt, pool_w, pool_scale, q_norm_g, kv_norm_g, cos_t, sin_t):
    s, d = x.shape
    row = lambda n: pl.BlockSpec((TM, n), lambda i: (i, 0))
    col = lambda n: pl.BlockSpec((n, TM), lambda i: (0, i))
    full = lambda a: pl.BlockSpec(a.shape, lambda i: (0,) * a.ndim)
    consts = [w1, wq1, wq2, wk, wvt, pool_w.astype(BF16), pool_scale.reshape(1, C_WIDTH),
              q_norm_g.reshape(1, Q_LORA), kv_norm_g.reshape(1, KV_LORA)]
    rows_out = lambda n, t: (row(n), jax.ShapeDtypeStruct((s, n), t))
    cols_out = lambda n, t: (col(n), jax.ShapeDtypeStruct((n, s), t))
    outs = [rows_out(C_WIDTH, BF16), rows_out(N_HEADS * LANE, BF16), rows_out(N_HEADS * LANE, BF16),
            cols_out(N_HEADS * V_ROWS, BF16)]
    return pl.pallas_call(
        _odd_in_kernel,
        grid=(s // TM,),
        in_specs=[row(d), full(mod)] + [full(a) for a in consts] + [row(LANE), row(LANE)],
        out_specs=[o[0] for o in outs],
        out_shape=[o[1] for o in outs],
        scratch_shapes=[pltpu.VMEM((POOL_HALO, C_WIDTH), F32)],
        compiler_params=_params(),
        name="odd_in",
    )(x, mod, *consts, cos_t, sin_t)


def kernel(x, c, ada_w, ada_b, ln_mix_g, ln_mix_b, ln_ffn_g, ln_ffn_b, ev_w_in, ev_conv_w, ev_w_out,
           od_w_in, pool_w, pool_scale, q_norm_g, w_uq, kv_norm_g, w_ukv, od_w_out,
           ffn_w_up, ffn_conv_w, ffn_w_down):
    bsz, seq, d = x.shape
    assert bsz == 1 and d == D_MODEL and seq % TM == 0 and seq % TK_ATT == 0
    assert SHORT_CONV == 3 and FFN_CONV == 3
    xs = x.reshape(seq, d)
    mods = _adaln(c, ada_w, ada_b)
    for l in range(DEPTH):
        mod = mods[l]
        if l % 2 == 0:
            e = l // 2
            w, wt = _even_weights(ev_w_in[e])
            q, k, vt, qi, ki, wit, yb = _even_in(xs, mod, w, wt, ev_conv_w[e], _alibi_q_features())
            bias = _select(qi, ki, wit)
            ya = _flash(q, k, vt, bias, alibi=True)
            xs = _out_ln(ya, yb, xs, mod, ev_w_out[e].astype(BF16), ln_mix_g[l], ln_mix_b[l])
        else:
            o = l // 2
            cos_t, sin_t = _rope_lane_tables(seq)
            ws = _odd_weights(od_w_in[o], w_uq[o], w_ukv[o])
            yc, q, k, vt = _odd_in(xs, mod, *ws, pool_w[o], pool_scale[o], q_norm_g[o], kv_norm_g[o],
                                   cos_t, sin_t)
            yd = _flash(q, k, vt, None, alibi=False)
            xs = _out_ln(yc, yd, xs, mod, od_w_out[o].astype(BF16), ln_mix_g[l], ln_mix_b[l])
        xs = _ffn(xs, mod, ffn_w_up[l].astype(BF16), ffn_conv_w[l], ffn_w_down[l].astype(BF16),
                  ln_ffn_g[l], ln_ffn_b[l])
    return xs.reshape(bsz, seq, d)
```

```python
import functools

import numpy as np
import jax
import jax.numpy as jnp
from jax import lax
from jax.experimental import pallas as pl
from jax.experimental.pallas import tpu as pltpu

D_MODEL = 1024
DEPTH = 2
CHUNK = 64
N_HEADS = 8
A_HEAD_DIM = 64
A_WIDTH = N_HEADS * A_HEAD_DIM
IDX_DIM = 64
TOPK = 256
B_WIDTH = 512
SHORT_CONV = 3
C_WIDTH = 512
POOL_WINDOWS = (2, 4, 8, 16)
C_GROUP_DIM = C_WIDTH // len(POOL_WINDOWS)
D_NOPE = 64
D_ROPE = 32
D_V = 64
Q_LORA = 384
KV_LORA = 256
ROPE_BASE = 10000.0
D_FF = 2816
FFN_CONV = 3
LN_EPS = 1e-5
RMS_EPS = 1e-6
DN_ALPHA = (2 * DEPTH) ** 0.25
NEG = -1e30
IDX_W_SCALE = N_HEADS ** -0.5 * IDX_DIM ** -0.5
LOG2E = 1.4426950408889634
DSA_Q_SCALE = A_HEAD_DIM ** -0.5 * LOG2E
MLA_Q_SCALE = (D_NOPE + D_ROPE) ** -0.5 * LOG2E
ALIBI_SLOPES = tuple(2.0 ** (-8.0 * (i + 1) / N_HEADS) for i in range(N_HEADS))

LANE = 128
SUBLANE = 8
BF16_ROWS = 16
V_ROWS = D_V + BF16_ROWS
VMEM_LIMIT_BYTES = 56 * 1024 * 1024

TM = 512
FFN_CHUNK = 256
TQ_SEL = 256
TQ_ATT = 512
TK_SEL = 512
TK_ATT = 1024
QK_AHEAD = 4
POOL_HALO = 16

INT_MIN = -2 ** 31
F32 = jnp.float32
BF16 = jnp.bfloat16


def _params():
    return pltpu.CompilerParams(dimension_semantics=("arbitrary",),
                                vmem_limit_bytes=VMEM_LIMIT_BYTES)


def _dot(a, b):
    return jnp.dot(a, b, preferred_element_type=F32)


def _dot_nt(a, b):
    return lax.dot_general(a, b, (((1,), (1,)), ((), ())), preferred_element_type=F32)


def _layer_norm(z, g, b):
    mu = jnp.mean(z, axis=-1, keepdims=True)
    zc = z - mu
    var = jnp.mean(zc * zc, axis=-1, keepdims=True)
    return zc * lax.rsqrt(var + LN_EPS) * g + b


def _rms_norm(z, g):
    return z * lax.rsqrt(jnp.mean(z * z, axis=-1, keepdims=True) + RMS_EPS) * g


def _causal_conv3(u, prev, w):
    w0, w1, w2 = w[0:1], w[1:2], w[2:3]
    y = pltpu.roll(u, 2, 0) * w0 + pltpu.roll(u, 1, 0) * w1 + u * w2
    head = u[0:SUBLANE]
    r = lax.broadcasted_iota(jnp.int32, (SUBLANE, 1), 0)
    h1 = jnp.where(r == 0, prev[7:8], pltpu.roll(head, 1, 0))
    h2 = jnp.where(r == 0, prev[6:7], jnp.where(r == 1, prev[7:8], pltpu.roll(head, 2, 0)))
    yh = h2 * w0 + h1 * w1 + head * w2
    return jnp.concatenate([yh, y[SUBLANE:]], axis=0)


def _fold_rows(c, rows_out):
    rows = c.shape[0]
    while rows > rows_out:
        rows //= 2
        c = c[0:rows] + c[rows:2 * rows]
    return c


def _store_values_t(vt_ref, vt):
    t = vt.shape[1]
    ones_row = jnp.where(lax.broadcasted_iota(jnp.int32, (BF16_ROWS, t), 0) == 0, 1.0, 0.0)
    for h in range(N_HEADS):
        vt_ref[h * V_ROWS:h * V_ROWS + D_V, :] = vt[h * D_V:(h + 1) * D_V, :].astype(BF16)
        vt_ref[h * V_ROWS + D_V:(h + 1) * V_ROWS, :] = ones_row.astype(BF16)


def _adaln_kernel(c_ref, w_ref, b_ref, o_ref):
    c = c_ref[...]
    cond = c * jax.nn.sigmoid(c)
    o_ref[0] = jnp.sum(cond * w_ref[0], axis=0, keepdims=True) + b_ref[0]


def _adaln(c, ada_w, ada_b):
    depth, d, n = ada_w.shape
    tn = 1536
    return pl.pallas_call(
        _adaln_kernel,
        grid=(depth, n // tn),
        in_specs=[pl.BlockSpec((d, 1), lambda l, j: (0, 0)),
                  pl.BlockSpec((1, d, tn), lambda l, j: (l, 0, j)),
                  pl.BlockSpec((1, 1, tn), lambda l, j: (l, 0, j))],
        out_specs=pl.BlockSpec((1, 1, tn), lambda l, j: (l, 0, j)),
        out_shape=jax.ShapeDtypeStruct((depth, 1, n), F32),
        compiler_params=pltpu.CompilerParams(dimension_semantics=("arbitrary", "arbitrary"),
                                             vmem_limit_bytes=VMEM_LIMIT_BYTES),
        name="adaln",
    )(c.reshape(d, 1), ada_w, ada_b.reshape(depth, 1, n))


_E_Q = (0, 1024)
_E_K = (1024, 2048)
_E_QI = (2048, 3072)
_E_KI = (3072, 3200)
_E_BG = (3200, 3712)
_E_CG = (3712, 4224)
_E_XB = (4224, 4736)
_E_COLS = 4736
_ET_V = (0, 512)
_ET_WI = (512, 528)
_ET_ROWS = 528
_FEAT_LANE = A_HEAD_DIM
_N_LOG2E_TERMS = 3
_POS_RADIX = 128


def _pad_heads(w):
    d = w.shape[0]
    w3 = w.reshape(d, N_HEADS, A_HEAD_DIM)
    return jnp.concatenate([w3, jnp.zeros_like(w3)], axis=2).reshape(d, N_HEADS * LANE)


def _alibi_q_features():
    terms, rest = [], np.float64(LOG2E)
    for _ in range(_N_LOG2E_TERMS):
        t = np.float64(np.asarray(rest, np.float32).astype(jnp.bfloat16).astype(np.float32))
        terms.append(t)
        rest = rest - t
    row = np.zeros((N_HEADS, LANE), np.float32)
    for h in range(N_HEADS):
        for n, t in enumerate(terms):
            row[h, _FEAT_LANE + n] = ALIBI_SLOPES[h] * _POS_RADIX * t
            row[h, _FEAT_LANE + _N_LOG2E_TERMS + n] = ALIBI_SLOPES[h] * t
    return jnp.asarray(row.reshape(1, N_HEADS * LANE))


def _even_weights(w_in):
    d = w_in.shape[0]
    a = A_WIDTH
    q = _pad_heads(w_in[:, 0:a])
    k = _pad_heads(w_in[:, a:2 * a])
    v = w_in[:, 2 * a:3 * a]
    qi = _pad_heads(w_in[:, 3 * a:4 * a])
    o = 4 * a
    ki = w_in[:, o:o + IDX_DIM]
    wi = w_in[:, o + IDX_DIM:o + IDX_DIM + N_HEADS]
    o = o + IDX_DIM + N_HEADS
    rest = w_in[:, o:o + 3 * B_WIDTH]
    w = jnp.concatenate([q, k, qi, ki, ki, rest], axis=1)
    assert w.shape[1] == _E_COLS
    wt = jnp.concatenate([v.T, wi.T, jnp.zeros((BF16_ROWS - N_HEADS, d), w_in.dtype)], axis=0)
    assert wt.shape[0] == _ET_ROWS
    return w.astype(BF16), wt.astype(BF16)


def _even_in_kernel(x_ref, mod_ref, w_ref, wt_ref, cw_ref, qf_ref,
                    q_ref, k_ref, vt_ref, qi_ref, ki_ref, wit_ref, yb_ref, carry_ref):
    d = D_MODEL
    i = pl.program_id(0)

    @pl.when(i == 0)
    def _():
        carry_ref[...] = jnp.zeros_like(carry_ref)

    h = (x_ref[...] * (1.0 + mod_ref[:, d:2 * d]) + mod_ref[:, 0:d]).astype(BF16)

    def proj(cols):
        return _dot(h, w_ref[:, cols[0]:cols[1]])

    def proj_t(rows):
        return _dot_nt(wt_ref[rows[0]:rows[1], :], h)

    q_ref[...] = (proj(_E_Q) * DSA_Q_SCALE + qf_ref[...]).astype(BF16)
    pos = i * TM + lax.broadcasted_iota(jnp.int32, (TM, LANE), 0)
    lane = lax.broadcasted_iota(jnp.int32, (TM, LANE), 1) - _FEAT_LANE
    pos_hi = (pos // _POS_RADIX).astype(F32)
    pos_lo = (pos % _POS_RADIX).astype(F32)
    kfeat = jnp.where(lane < 0, 0.0,
                      jnp.where(lane < _N_LOG2E_TERMS, pos_hi,
                                jnp.where(lane < 2 * _N_LOG2E_TERMS, pos_lo, 0.0)))
    kproj = proj(_E_K)
    for hd in range(N_HEADS):
        sl = slice(hd * LANE, (hd + 1) * LANE)
        k_ref[:, sl] = (kproj[:, sl] + kfeat).astype(BF16)
    _store_values_t(vt_ref, proj_t(_ET_V))
    qi_ref[...] = proj(_E_QI).astype(BF16)
    ki_ref[...] = proj(_E_KI).astype(BF16)
    wit_ref[...] = proj_t(_ET_WI) * IDX_W_SCALE
    g = proj(_E_CG) * proj(_E_XB)
    y = _causal_conv3(g, carry_ref[...], cw_ref[...])
    yb_ref[...] = (proj(_E_BG) * y).astype(BF16)
    carry_ref[...] = g[TM - SUBLANE:TM]


def _even_in(x, mod, w, wt, conv_w, qfeat):
    s, d = x.shape
    row = lambda n: pl.BlockSpec((TM, n), lambda i: (i, 0))
    col = lambda n: pl.BlockSpec((n, TM), lambda i: (0, i))
    full = lambda a: pl.BlockSpec(a.shape, lambda i: (0,) * a.ndim)
    rows_out = lambda n, t: (row(n), jax.ShapeDtypeStruct((s, n), t))
    cols_out = lambda n, t: (col(n), jax.ShapeDtypeStruct((n, s), t))
    outs = [rows_out(N_HEADS * LANE, BF16), rows_out(N_HEADS * LANE, BF16), cols_out(N_HEADS * V_ROWS, BF16),
            rows_out(N_HEADS * LANE, BF16), rows_out(LANE, BF16), cols_out(BF16_ROWS, F32),
            rows_out(B_WIDTH, BF16)]
    return pl.pallas_call(
        _even_in_kernel,
        grid=(s // TM,),
        in_specs=[row(d), full(mod), full(w), full(wt), full(conv_w), full(qfeat)],
        out_specs=[o[0] for o in outs],
        out_shape=[o[1] for o in outs],
        scratch_shapes=[pltpu.VMEM((SUBLANE, B_WIDTH), F32)],
        compiler_params=_params(),
        name="even_in",
    )(x, mod, w, wt, conv_w, qfeat)


def _select_kernel(qi_ref, ki_ref, wt_ref, bias_ref, hi_ref, lo_ref, m1_ref, m2_ref, mc_ref, *, seq):
    tq, tk = TQ_SEL, TK_SEL
    i16 = jnp.int16
    i = pl.program_id(0)
    start = i * tq
    n_tiles = (start + tq + tk - 1) // tk
    tpos = start + lax.broadcasted_iota(jnp.int32, (1, tq), 1)
    limit = (tpos // CHUNK + 1) * CHUNK
    int_min = jnp.int32(INT_MIN)
    dmin, dmax = -2 ** 15, 2 ** 15 - 1
    zero16, one16 = jnp.zeros((), i16), jnp.ones((), i16)
    sel0, seln = jnp.zeros((), BF16), jnp.full((), NEG, BF16)

    def key_pos(off):
        return off + lax.broadcasted_iota(jnp.int32, (tk, tq), 0)

    def score_tile(j, carry, *, masked, tiles=1):
        for t in range(tiles):
            score_rows(pl.multiple_of((j * tiles + t) * tk, tk), masked)
        return carry

    def score_rows(off, masked):
        kt = ki_ref[pl.ds(off, tk), :]
        score = jnp.zeros((tk, tq), F32)
        for h in range(N_HEADS):
            rel = _dot_nt(kt, qi_ref[:, h * LANE:(h + 1) * LANE])
            score = score + wt_ref[h:h + 1, :] * jnp.maximum(rel, 0.0)
        bits = pltpu.bitcast(score, jnp.int32)
        key = bits ^ ((bits >> 31) & jnp.int32(0x7FFFFFFF))
        key = jnp.where(bits == int_min, 0, key)
        if masked:
            key = jnp.where(key_pos(off) < limit, key, int_min)
        hi_ref[pl.ds(off, tk), :] = (key >> 16).astype(i16)
        lo_ref[pl.ds(off, tk), :] = key.astype(i16) ^ jnp.asarray(dmin, i16)

    n_before = start // tk
    n_pairs = n_before // 2
    lax.fori_loop(0, n_pairs, functools.partial(score_tile, masked=False, tiles=2), 0)
    lax.fori_loop(2 * n_pairs, n_before, functools.partial(score_tile, masked=False), 0)
    lax.fori_loop(n_before, n_tiles, functools.partial(score_tile, masked=True), 0)

    def rows_at(ref, off, rows):
        return ref[pl.ds(off, rows), :]

    def for_key_tiles(fn, carry):
        return lax.fori_loop(0, n_tiles, lambda j, c: fn(pl.multiple_of(j * tk, tk), tk, c), carry)

    def per_query(c):
        return jnp.sum(c.astype(jnp.int32).astype(F32), axis=0, keepdims=True)

    def count(flag):
        def body(off, rows, acc):
            return acc + _fold_rows(flag(off, rows), BF16_ROWS)
        return per_query(for_key_tiles(body, jnp.zeros((BF16_ROWS, tq), i16)))

    def radix_search(count_ge, need):
        def bit_step(b, thr):
            cand = thr + jnp.left_shift(jnp.int32(1), 15 - b)
            return jnp.where(count_ge(cand.astype(i16)) >= need, cand, thr)
        return lax.fori_loop(0, 16, bit_step, jnp.full((1, tq), dmin, jnp.int32))

    topk = float(TOPK)
    t_hi = radix_search(lambda c16: count(
        lambda off, rows: jnp.where(rows_at(hi_ref, off, rows) >= c16, one16, zero16)), topk)
    t_hi16 = t_hi.astype(i16)
    n_above = count(lambda off, rows: jnp.where(rows_at(hi_ref, off, rows) > t_hi16, one16, zero16))
    need_lo = topk - n_above

    m1_ref[...] = jnp.full((tk, tq), dmin, i16)
    m2_ref[...] = jnp.full((tk, tq), dmin, i16)
    mc_ref[...] = jnp.zeros((tk, tq), i16)

    def fold_members(off, rows, carry):
        member = rows_at(hi_ref, off, rows) == t_hi16
        x = jnp.where(member, rows_at(lo_ref, off, rows), jnp.asarray(dmin, i16))
        lo_ref[pl.ds(off, rows), :] = x
        a, b = m1_ref[...], m2_ref[...]
        above = x > a
        m1_ref[...] = jnp.where(above, x, a)
        second = jnp.where(above, a, x)
        m2_ref[...] = jnp.where(second > b, second, b)
        mc_ref[...] = mc_ref[...] + jnp.where(member, one16, zero16)
        return carry

    for_key_tiles(fold_members, 0)

    def count_slots(flag):
        return per_query(_fold_rows(flag(m1_ref[...]) + flag(m2_ref[...]), BF16_ROWS))

    def level2_slots():
        t = radix_search(lambda c16: count_slots(lambda m: jnp.where(m >= c16, one16, zero16)), need_lo)
        t16 = t.astype(i16)
        n_members = per_query(_fold_rows(mc_ref[...], BF16_ROWS))
        gt = count_slots(lambda m: jnp.where(m > t16, one16, zero16))
        ge = jnp.where(t == dmin, n_members, count_slots(lambda m: jnp.where(m >= t16, one16, zero16)))
        return t, gt, ge

    def level2_full():
        t = radix_search(lambda c16: count(
            lambda off, rows: jnp.where(rows_at(lo_ref, off, rows) >= c16, one16, zero16)), need_lo)
        t16 = t.astype(i16)
        gt = count(lambda off, rows: jnp.where(rows_at(lo_ref, off, rows) > t16, one16, zero16))
        ge = count(lambda off, rows: jnp.where(
            rows_at(hi_ref, off, rows) == t_hi16,
            jnp.where(rows_at(lo_ref, off, rows) >= t16, one16, zero16), zero16))
        return t, gt, ge

    crowded = per_query(_fold_rows(jnp.where(mc_ref[...] > jnp.asarray(2, i16), one16, zero16), BF16_ROWS))
    t_lo, n_gt_lo, n_ge_lo = lax.cond(jnp.max(crowded) > 0.0, level2_full, level2_slots)
    t_lo16 = t_lo.astype(i16)
    n_gt = n_above + n_gt_lo
    n_ge = n_above + n_ge_lo
    room = topk - n_gt
    sentinel = t_hi == dmin
    has_ties = jnp.where(sentinel, 0.0, jnp.where(n_ge > topk, 1.0, 0.0))
    any_ties = jnp.max(has_ties) > 0.0

    def pos16(off, rows):
        return (off + lax.broadcasted_iota(jnp.int32, (rows, tq), 0)).astype(i16)

    def tie_cutoff():
        def cut_step(b, cut):
            cand = cut + jnp.left_shift(jnp.int32(1), 14 - b)
            c16 = cand.astype(i16)

            def flag(off, rows):
                tied = jnp.where(rows_at(hi_ref, off, rows) == t_hi16,
                                 jnp.where(rows_at(lo_ref, off, rows) == t_lo16, pos16(off, rows),
                                           jnp.asarray(dmax, i16)),
                                 jnp.asarray(dmax, i16))
                return jnp.where(tied < c16, one16, zero16)
            return jnp.where(count(flag) <= room, cand, cut)
        return lax.fori_loop(0, 15, cut_step, jnp.zeros((1, tq), jnp.int32))

    def write_plain():
        lo_min16 = jnp.where(sentinel, dmax, t_lo).astype(i16)

        def write_rows(off, rows, carry):
            hi = rows_at(hi_ref, off, rows)
            inner = jnp.where(rows_at(lo_ref, off, rows) >= lo_min16, sel0, seln)
            bias_ref[pl.ds(off, rows), :] = jnp.where(hi > t_hi16, sel0,
                                                      jnp.where(hi == t_hi16, inner, seln))
            return carry
        for_key_tiles(write_rows, 0)

    def write_with_ties():
        cut16 = jnp.where(sentinel, 0, tie_cutoff()).astype(i16)

        def write_rows(off, rows, carry):
            hi = rows_at(hi_ref, off, rows)
            lo = rows_at(lo_ref, off, rows)
            tie = jnp.where(pos16(off, rows) < cut16, sel0, seln)
            inner = jnp.where(lo > t_lo16, sel0, jnp.where(lo == t_lo16, tie, seln))
            bias_ref[pl.ds(off, rows), :] = jnp.where(hi > t_hi16, sel0,
                                                      jnp.where(hi == t_hi16, inner, seln))
            return carry
        for_key_tiles(write_rows, 0)

    lax.cond(any_ties, write_with_ties, write_plain)

    def fill_tile(j, carry):
        off = pl.multiple_of(j * tk, tk)
        bias_ref[pl.ds(off, tk), :] = jnp.full((tk, tq), NEG, BF16)
        return carry

    lax.fori_loop(n_tiles, seq // tk, fill_tile, 0)


def _select(qi, ki, wit):
    s = qi.shape[0]
    assert s % TK_SEL == 0 and s % TQ_SEL == 0 and 2 * TOPK <= s <= 32767
    return pl.pallas_call(
        functools.partial(_select_kernel, seq=s),
        grid=(s // TQ_SEL,),
        in_specs=[pl.BlockSpec((TQ_SEL, N_HEADS * LANE), lambda i: (i, 0)),
                  pl.BlockSpec((s, LANE), lambda i: (0, 0)),
                  pl.BlockSpec((BF16_ROWS, TQ_SEL), lambda i: (0, i))],
        out_specs=pl.BlockSpec((s, TQ_SEL), lambda i: (0, i)),
        out_shape=jax.ShapeDtypeStruct((s, s), BF16),
        scratch_shapes=[pltpu.VMEM((s, TQ_SEL), jnp.int16), pltpu.VMEM((s, TQ_SEL), jnp.int16)]
        + [pltpu.VMEM((TK_SEL, TQ_SEL), jnp.int16)] * 3,
        compiler_params=_params(),
        name="dsa_select",
    )(qi, ki, wit)


def _flash_kernel(qi_of, kj_of, *refs, alibi, use_bias):
    if use_bias:
        q_ref, k_ref, vt_ref, bias_ref, o_ref, m_ref, acc_ref = refs
    else:
        q_ref, k_ref, vt_ref, o_ref, m_ref, acc_ref = refs
    tq, tk = TQ_ATT, TK_ATT
    p = pl.program_id(0)
    qi = qi_of[p]
    kj = kj_of[p]
    last = ((qi + 1) * tq - 1) // tk

    @pl.when(kj == 0)
    def _():
        m_ref[...] = jnp.full_like(m_ref, NEG)
        acc_ref[...] = jnp.zeros_like(acc_ref)

    def step(diagonal):
        base = bias_ref[...].astype(F32) if use_bias else None
        if diagonal:
            tpos = qi * tq + lax.broadcasted_iota(jnp.int32, (tk, tq), 1)
            spos = kj * tk + lax.broadcasted_iota(jnp.int32, (tk, tq), 0)
            if not use_bias:
                base = jnp.where(spos < (tpos // CHUNK + 1) * CHUNK, 0.0, NEG)
            if alibi:
                ahead = jnp.maximum(spos - tpos, 0).astype(F32)

        def logits(h):
            s = _dot_nt(k_ref[:, h * LANE:(h + 1) * LANE], q_ref[:, h * LANE:(h + 1) * LANE])
            if diagonal and alibi:
                return s + (base - (2.0 * ALIBI_SLOPES[h] * LOG2E) * ahead)
            return s if base is None else s + base

        pending = [logits(h) for h in range(QK_AHEAD)]
        for h in range(N_HEADS):
            rows = slice(h * V_ROWS, (h + 1) * V_ROWS)
            if h + QK_AHEAD < N_HEADS:
                pending.append(logits(h + QK_AHEAD))
            s = pending.pop(0)
            m_prev = m_ref[h:h + 1, :]
            m_new = jnp.maximum(m_prev, jnp.max(s, axis=0, keepdims=True))
            alpha = jnp.exp2(m_prev - m_new)
            pexp = jnp.exp2(s - m_new)
            acc_ref[rows, :] = alpha * acc_ref[rows, :] + _dot(vt_ref[rows, :], pexp.astype(BF16))
            m_ref[h:h + 1, :] = m_new

    @pl.when(kj != last)
    def _():
        step(False)

    @pl.when(kj == last)
    def _():
        step(True)
        outs = [acc_ref[h * V_ROWS:h * V_ROWS + D_V, :] / acc_ref[h * V_ROWS + D_V:h * V_ROWS + D_V + 1, :]
                for h in range(N_HEADS)]
        o_ref[...] = jnp.concatenate(outs, axis=0).T.astype(BF16)


def _flash(q, k, vt, bias, *, alibi):
    s = q.shape[0]
    nq = s // TQ_ATT
    pairs = [(i, j) for i in range(nq) for j in range(((i + 1) * TQ_ATT - 1) // TK_ATT + 1)]
    qi_of = jnp.asarray(np.array([a for a, _ in pairs], np.int32))
    kj_of = jnp.asarray(np.array([b for _, b in pairs], np.int32))
    use_bias = bias is not None
    in_specs = [pl.BlockSpec((TQ_ATT, N_HEADS * LANE), lambda p, qi, kj: (qi[p], 0)),
                pl.BlockSpec((TK_ATT, k.shape[1]), lambda p, qi, kj: (kj[p], 0)),
                pl.BlockSpec((N_HEADS * V_ROWS, TK_ATT), lambda p, qi, kj: (0, kj[p]))]
    args = [q, k, vt]
    if use_bias:
        in_specs.append(pl.BlockSpec((TK_ATT, TQ_ATT), lambda p, qi, kj: (kj[p], qi[p])))
        args.append(bias)
    kern = functools.partial(_flash_kernel, alibi=alibi, use_bias=use_bias)
    return pl.pallas_call(
        kern,
        grid_spec=pltpu.PrefetchScalarGridSpec(
            num_scalar_prefetch=2,
            grid=(len(pairs),),
            in_specs=in_specs,
            out_specs=pl.BlockSpec((TQ_ATT, N_HEADS * D_V), lambda p, qi, kj: (qi[p], 0)),
            scratch_shapes=[pltpu.VMEM((N_HEADS, TQ_ATT), F32),
                            pltpu.VMEM((N_HEADS * V_ROWS, TQ_ATT), F32)]),
        out_shape=jax.ShapeDtypeStruct((s, N_HEADS * D_V), BF16),
        compiler_params=_params(),
        name="flash_dsa" if use_bias else "flash_mla",
    )(qi_of, kj_of, *args)


def _out_ln_kernel(ya_ref, yb_ref, x_ref, mod_ref, w_ref, g_ref, b_ref, o_ref):
    d = D_MODEL
    na = ya_ref.shape[1]
    y = _dot(ya_ref[...], w_ref[0:na, :]) + _dot(yb_ref[...], w_ref[na:, :])
    z = DN_ALPHA * x_ref[...] + (1.0 + mod_ref[:, 2 * d:3 * d]) * y
    o_ref[...] = _layer_norm(z, g_ref[...], b_ref[...])


def _out_ln(ya, yb, x, mod, w, g, b):
    s, d = x.shape
    row = lambda n: pl.BlockSpec((TM, n), lambda i: (i, 0))
    full = lambda a: pl.BlockSpec(a.shape, lambda i: (0,) * a.ndim)
    g = g.reshape(1, d)
    b = b.reshape(1, d)
    return pl.pallas_call(
        _out_ln_kernel,
        grid=(s // TM,),
        in_specs=[row(ya.shape[1]), row(yb.shape[1]), row(d), full(mod), full(w), full(g), full(b)],
        out_specs=row(d),
        out_shape=jax.ShapeDtypeStruct((s, d), F32),
        compiler_params=_params(),
        name="out_ln",
    )(ya, yb, x, mod, w, g, b)


def _ffn_kernel(x_ref, mod_ref, wup_ref, cw_ref, wdn_ref, g_ref, b_ref, o_ref, carry_ref, act_ref):
    d = D_MODEL

    @pl.when(pl.program_id(0) == 0)
    def _():
        carry_ref[...] = jnp.zeros_like(carry_ref)

    x = x_ref[...]
    h = (x * (1.0 + mod_ref[:, 4 * d:5 * d]) + mod_ref[:, 3 * d:4 * d]).astype(BF16)
    for c in range(D_FF // FFN_CHUNK):
        va = c * FFN_CHUNK
        ga = D_FF + va
        uv = _dot(h, wup_ref[:, va:va + FFN_CHUNK])
        ug = _dot(h, wup_ref[:, ga:ga + FFN_CHUNK])
        val = _causal_conv3(uv, carry_ref[:, va:va + FFN_CHUNK], cw_ref[:, va:va + FFN_CHUNK])
        gate = _causal_conv3(ug, carry_ref[:, ga:ga + FFN_CHUNK], cw_ref[:, ga:ga + FFN_CHUNK])
        carry_ref[:, va:va + FFN_CHUNK] = uv[TM - SUBLANE:TM]
        carry_ref[:, ga:ga + FFN_CHUNK] = ug[TM - SUBLANE:TM]
        act_ref[:, va:va + FFN_CHUNK] = (gate * jax.nn.sigmoid(gate) * val).astype(BF16)
    y = _dot(act_ref[...], wdn_ref[...])
    z = DN_ALPHA * x + (1.0 + mod_ref[:, 5 * d:6 * d]) * y
    o_ref[...] = _layer_norm(z, g_ref[...], b_ref[...])


def _ffn(x, mod, w_up, conv_w, w_down, g, b):
    s, d = x.shape
    assert D_FF % FFN_CHUNK == 0
    row = lambda n: pl.BlockSpec((TM, n), lambda i: (i, 0))
    full = lambda a: pl.BlockSpec(a.shape, lambda i: (0,) * a.ndim)
    once = lambda a: pl.BlockSpec(a.shape, lambda i: (0,) * a.ndim, pipeline_mode=pl.Buffered(1))
    g = g.reshape(1, d)
    b = b.reshape(1, d)
    return pl.pallas_call(
        _ffn_kernel,
        grid=(s // TM,),
        in_specs=[row(d), full(mod), once(w_up), full(conv_w), once(w_down), full(g), full(b)],
        out_specs=row(d),
        out_shape=jax.ShapeDtypeStruct((s, d), F32),
        scratch_shapes=[pltpu.VMEM((SUBLANE, 2 * D_FF), F32), pltpu.VMEM((TM, D_FF), BF16)],
        compiler_params=_params(),
        name="conv_ffn",
    )(x, mod, w_up, conv_w, w_down, g, b)


_O_U = (0, 512)
_O_QLAT = (512, 896)
_O_KVLAT = (896, 1152)
_O_KR = (1152, 1280)
_O_KR_ROT = (1280, 1408)
_O_COLS = 1408


def _rotate_half_cols(w):
    half = w.shape[-1] // 2
    return jnp.concatenate([-w[..., half:], w[..., :half]], axis=-1)


def _odd_weights(w_in, w_uq, w_ukv):
    d = w_in.shape[0]
    o = C_WIDTH + Q_LORA + KV_LORA
    kr = w_in[:, o:o + D_ROPE]
    zl = jnp.zeros((d, D_NOPE), w_in.dtype)
    zr = jnp.zeros((d, LANE - D_NOPE - D_ROPE), w_in.dtype)
    w1 = jnp.concatenate([w_in[:, 0:o], zl, kr, zr, zl, _rotate_half_cols(kr), zr], axis=1)
    assert w1.shape[1] == _O_COLS
    dq = D_NOPE + D_ROPE
    uq = w_uq.reshape(Q_LORA, N_HEADS, dq)
    padq = jnp.zeros((Q_LORA, N_HEADS, LANE - dq), w_uq.dtype)
    wq1 = jnp.concatenate([uq, padq], axis=2).reshape(Q_LORA, N_HEADS * LANE)
    wq2 = jnp.concatenate([jnp.zeros((Q_LORA, N_HEADS, D_NOPE), w_uq.dtype),
                           _rotate_half_cols(uq[:, :, D_NOPE:]), padq], axis=2
                          ).reshape(Q_LORA, N_HEADS * LANE)
    ukv = w_ukv.reshape(KV_LORA, N_HEADS, D_NOPE + D_V)
    wk = jnp.concatenate([ukv[:, :, :D_NOPE], jnp.zeros((KV_LORA, N_HEADS, LANE - D_NOPE), w_ukv.dtype)],
                         axis=2).reshape(KV_LORA, N_HEADS * LANE)
    wvt = ukv[:, :, D_NOPE:].reshape(KV_LORA, N_HEADS * D_V).T
    return w1.astype(BF16), wq1.astype(BF16), wq2.astype(BF16), wk.astype(BF16), wvt.astype(BF16)


def _rope_lane_tables(seq):
    inv = ROPE_BASE ** (-jnp.arange(0, D_ROPE, 2, dtype=F32) / D_ROPE)
    ang = jnp.arange(seq, dtype=F32)[:, None] * inv[None, :]
    cos, sin = jnp.cos(ang), jnp.sin(ang)
    pad = jnp.zeros((seq, LANE - D_NOPE - D_ROPE), F32)
    cos_t = jnp.concatenate([jnp.ones((seq, D_NOPE), F32), cos, cos, pad], axis=1)
    sin_t = jnp.concatenate([jnp.zeros((seq, D_NOPE), F32), sin, sin, pad], axis=1)
    return cos_t, sin_t


def _odd_in_kernel(x_ref, mod_ref, w1_ref, wq1_ref, wq2_ref, wk_ref, wvt_ref, pw_ref, ps_ref,
                   qg_ref, kvg_ref, cos_ref, sin_ref,
                   yc_ref, q_ref, k_ref, vt_ref, carry_ref):
    d = D_MODEL
    i = pl.program_id(0)

    @pl.when(i == 0)
    def _():
        carry_ref[...] = jnp.zeros_like(carry_ref)

    h = (x_ref[...] * (1.0 + mod_ref[:, d:2 * d]) + mod_ref[:, 0:d]).astype(BF16)

    def proj(cols):
        return _dot(h, w1_ref[:, cols[0]:cols[1]])

    cos = cos_ref[...]
    sin = sin_ref[...]

    u = proj(_O_U)
    ext = jnp.concatenate([carry_ref[...], u], axis=0)
    pos = i * TM + lax.broadcasted_iota(jnp.int32, (TM, 1), 0)
    for g, win in enumerate(POOL_WINDOWS):
        sl = slice(g * C_GROUP_DIM, (g + 1) * C_GROUP_DIM)
        acc = ext[:, sl]
        shift = 1
        while shift < win:
            acc = acc + pltpu.roll(acc, shift, 0)
            shift *= 2
        cnt = jnp.minimum(pos + 1, win).astype(F32)
        pooled = acc[POOL_HALO:] / cnt - u[:, sl]
        mixed = _dot(pooled.astype(BF16), pw_ref[g])
        yc_ref[:, sl] = (mixed * ps_ref[:, sl]).astype(BF16)
    carry_ref[...] = u[TM - POOL_HALO:TM]

    r = _rms_norm(proj(_O_QLAT), qg_ref[...]).astype(BF16)
    qa = _dot(r, wq1_ref[...])
    qb = _dot(r, wq2_ref[...])
    kr = proj(_O_KR) * cos + proj(_O_KR_ROT) * sin
    rk = _rms_norm(proj(_O_KVLAT), kvg_ref[...]).astype(BF16)
    kn = _dot(rk, wk_ref[...])
    for hd in range(N_HEADS):
        sl = slice(hd * LANE, (hd + 1) * LANE)
        q_ref[:, sl] = ((qa[:, sl] * cos + qb[:, sl] * sin) * MLA_Q_SCALE).astype(BF16)
        k_ref[:, sl] = (kn[:, sl] + kr).astype(BF16)
    _store_values_t(vt_ref, _dot_nt(wvt_ref[...], rk))


def _odd_in(x, mod, w1, wq1, wq2, wk, wvt, pool_w, pool_scale, q_norm_g, kv_norm_g, cos_t, sin_t):
    s, d = x.shape
    row = lambda n: pl.BlockSpec((TM, n), lambda i: (i, 0))
    col = lambda n: pl.BlockSpec((n, TM), lambda i: (0, i))
    full = lambda a: pl.BlockSpec(a.shape, lambda i: (0,) * a.ndim)
    consts = [w1, wq1, wq2, wk, wvt, pool_w.astype(BF16), pool_scale.reshape(1, C_WIDTH),
              q_norm_g.reshape(1, Q_LORA), kv_norm_g.reshape(1, KV_LORA)]
    rows_out = lambda n, t: (row(n), jax.ShapeDtypeStruct((s, n), t))
    cols_out = lambda n, t: (col(n), jax.ShapeDtypeStruct((n, s), t))
    outs = [rows_out(C_WIDTH, BF16), rows_out(N_HEADS * LANE, BF16), rows_out(N_HEADS * LANE, BF16),
            cols_out(N_HEADS * V_ROWS, BF16)]
    return pl.pallas_call(
        _odd_in_kernel,
        grid=(s // TM,),
        in_specs=[row(d), full(mod)] + [full(a) for a in consts] + [row(LANE), row(LANE)],
        out_specs=[o[0] for o in outs],
        out_shape=[o[1] for o in outs],
        scratch_shapes=[pltpu.VMEM((POOL_HALO, C_WIDTH), F32)],
        compiler_params=_params(),
        name="odd_in",
    )(x, mod, *consts, cos_t, sin_t)


def kernel(x, c, ada_w, ada_b, ln_mix_g, ln_mix_b, ln_ffn_g, ln_ffn_b, ev_w_in, ev_conv_w, ev_w_out,
           od_w_in, pool_w, pool_scale, q_norm_g, w_uq, kv_norm_g, w_ukv, od_w_out,
           ffn_w_up, ffn_conv_w, ffn_w_down):
    bsz, seq, d = x.shape
    assert bsz == 1 and d == D_MODEL and seq % TM == 0 and seq % TK_ATT == 0
    assert SHORT_CONV == 3 and FFN_CONV == 3
    xs = x.reshape(seq, d)
    mods = _adaln(c, ada_w, ada_b)
    for l in range(DEPTH):
        mod = mods[l]
        if l % 2 == 0:
            e = l // 2
            w, wt = _even_weights(ev_w_in[e])
            q, k, vt, qi, ki, wit, yb = _even_in(xs, mod, w, wt, ev_conv_w[e], _alibi_q_features())
            bias = _select(qi, ki, wit)
            ya = _flash(q, k, vt, bias, alibi=True)
            xs = _out_ln(ya, yb, xs, mod, ev_w_out[e].astype(BF16), ln_mix_g[l], ln_mix_b[l])
        else:
            o = l // 2
            cos_t, sin_t = _rope_lane_tables(seq)
            ws = _odd_weights(od_w_in[o], w_uq[o], w_ukv[o])
            yc, q, k, vt = _odd_in(xs, mod, *ws, pool_w[o], pool_scale[o], q_norm_g[o], kv_norm_g[o],
                                   cos_t, sin_t)
            yd = _flash(q, k, vt, None, alibi=False)
            xs = _out_ln(yc, yd, xs, mod, od_w_out[o].astype(BF16), ln_mix_g[l], ln_mix_b[l])
        xs = _ffn(xs, mod, ffn_w_up[l].astype(BF16), ffn_conv_w[l], ffn_w_down[l].astype(BF16),
                  ln_ffn_g[l], ln_ffn_b[l])
    return xs.reshape(bsz, seq, d)
```

```python
import functools

import numpy as np
import jax
import jax.numpy as jnp
from jax import lax
from jax.experimental import pallas as pl
from jax.experimental.pallas import tpu as pltpu

D_MODEL = 1024
DEPTH = 2
CHUNK = 64
N_HEADS = 8
A_HEAD_DIM = 64
A_WIDTH = N_HEADS * A_HEAD_DIM
IDX_DIM = 64
TOPK = 256
B_WIDTH = 512
SHORT_CONV = 3
C_WIDTH = 512
POOL_WINDOWS = (2, 4, 8, 16)
C_GROUP_DIM = C_WIDTH // len(POOL_WINDOWS)
D_NOPE = 64
D_ROPE = 32
D_V = 64
Q_LORA = 384
KV_LORA = 256
ROPE_BASE = 10000.0
D_FF = 2816
FFN_CONV = 3
LN_EPS = 1e-5
RMS_EPS = 1e-6
DN_ALPHA = (2 * DEPTH) ** 0.25
NEG = -1e30
IDX_W_SCALE = N_HEADS ** -0.5 * IDX_DIM ** -0.5
LOG2E = 1.4426950408889634
DSA_Q_SCALE = A_HEAD_DIM ** -0.5 * LOG2E
MLA_Q_SCALE = (D_NOPE + D_ROPE) ** -0.5 * LOG2E
ALIBI_SLOPES = tuple(2.0 ** (-8.0 * (i + 1) / N_HEADS) for i in range(N_HEADS))

LANE = 128
SUBLANE = 8
BF16_ROWS = 16
V_ROWS = D_V + BF16_ROWS
VMEM_LIMIT_BYTES = 56 * 1024 * 1024

TM = 512
FFN_CHUNK = 256
TQ_SEL = 256
TQ_ATT = 512
TK_SEL = 512
TK_ATT = 1024
QK_AHEAD = 4
POOL_HALO = 16

INT_MIN = -2 ** 31
F32 = jnp.float32
BF16 = jnp.bfloat16


def _params():
    return pltpu.CompilerParams(dimension_semantics=("arbitrary",),
                                vmem_limit_bytes=VMEM_LIMIT_BYTES)


def _dot(a, b):
    return jnp.dot(a, b, preferred_element_type=F32)


def _dot_nt(a, b):
    return lax.dot_general(a, b, (((1,), (1,)), ((), ())), preferred_element_type=F32)


def _layer_norm(z, g, b):
    mu = jnp.mean(z, axis=-1, keepdims=True)
    zc = z - mu
    var = jnp.mean(zc * zc, axis=-1, keepdims=True)
    return zc * lax.rsqrt(var + LN_EPS) * g + b


def _rms_norm(z, g):
    return z * lax.rsqrt(jnp.mean(z * z, axis=-1, keepdims=True) + RMS_EPS) * g


def _causal_conv3(u, prev, w):
    w0, w1, w2 = w[0:1], w[1:2], w[2:3]
    y = pltpu.roll(u, 2, 0) * w0 + pltpu.roll(u, 1, 0) * w1 + u * w2
    head = u[0:SUBLANE]
    r = lax.broadcasted_iota(jnp.int32, (SUBLANE, 1), 0)
    h1 = jnp.where(r == 0, prev[7:8], pltpu.roll(head, 1, 0))
    h2 = jnp.where(r == 0, prev[6:7], jnp.where(r == 1, prev[7:8], pltpu.roll(head, 2, 0)))
    yh = h2 * w0 + h1 * w1 + head * w2
    return jnp.concatenate([yh, y[SUBLANE:]], axis=0)


def _fold_rows(c, rows_out):
    rows = c.shape[0]
    while rows > rows_out:
        rows //= 2
        c = c[0:rows] + c[rows:2 * rows]
    return c


def _store_values_t(vt_ref, vt):
    t = vt.shape[1]
    ones_row = jnp.where(lax.broadcasted_iota(jnp.int32, (BF16_ROWS, t), 0) == 0, 1.0, 0.0)
    for h in range(N_HEADS):
        vt_ref[h * V_ROWS:h * V_ROWS + D_V, :] = vt[h * D_V:(h + 1) * D_V, :].astype(BF16)
        vt_ref[h * V_ROWS + D_V:(h + 1) * V_ROWS, :] = ones_row.astype(BF16)


def _adaln_kernel(c_ref, w_ref, b_ref, o_ref):
    c = c_ref[...]
    cond = c * jax.nn.sigmoid(c)
    o_ref[0] = jnp.sum(cond * w_ref[0], axis=0, keepdims=True) + b_ref[0]


def _adaln(c, ada_w, ada_b):
    depth, d, n = ada_w.shape
    tn = 1536
    return pl.pallas_call(
        _adaln_kernel,
        grid=(depth, n // tn),
        in_specs=[pl.BlockSpec((d, 1), lambda l, j: (0, 0)),
                  pl.BlockSpec((1, d, tn), lambda l, j: (l, 0, j)),
                  pl.BlockSpec((1, 1, tn), lambda l, j: (l, 0, j))],
        out_specs=pl.BlockSpec((1, 1, tn), lambda l, j: (l, 0, j)),
        out_shape=jax.ShapeDtypeStruct((depth, 1, n), F32),
        compiler_params=pltpu.CompilerParams(dimension_semantics=("arbitrary", "arbitrary"),
                                             vmem_limit_bytes=VMEM_LIMIT_BYTES),
        name="adaln",
    )(c.reshape(d, 1), ada_w, ada_b.reshape(depth, 1, n))


_E_Q = (0, 1024)
_E_K = (1024, 2048)
_E_QI = (2048, 3072)
_E_KI = (3072, 3200)
_E_BG = (3200, 3712)
_E_CG = (3712, 4224)
_E_XB = (4224, 4736)
_E_COLS = 4736
_ET_V = (0, 512)
_ET_WI = (512, 528)
_ET_ROWS = 528
_FEAT_LANE = A_HEAD_DIM
_N_LOG2E_TERMS = 3
_POS_RADIX = 128


def _pad_heads(w):
    d = w.shape[0]
    w3 = w.reshape(d, N_HEADS, A_HEAD_DIM)
    return jnp.concatenate([w3, jnp.zeros_like(w3)], axis=2).reshape(d, N_HEADS * LANE)


def _alibi_q_features():
    terms, rest = [], np.float64(LOG2E)
    for _ in range(_N_LOG2E_TERMS):
        t = np.float64(np.asarray(rest, np.float32).astype(jnp.bfloat16).astype(np.float32))
        terms.append(t)
        rest = rest - t
    row = np.zeros((N_HEADS, LANE), np.float32)
    for h in range(N_HEADS):
        for n, t in enumerate(terms):
            row[h, _FEAT_LANE + n] = ALIBI_SLOPES[h] * _POS_RADIX * t
            row[h, _FEAT_LANE + _N_LOG2E_TERMS + n] = ALIBI_SLOPES[h] * t
    return jnp.asarray(row.reshape(1, N_HEADS * LANE))


def _even_weights(w_in):
    d = w_in.shape[0]
    a = A_WIDTH
    q = _pad_heads(w_in[:, 0:a])
    k = _pad_heads(w_in[:, a:2 * a])
    v = w_in[:, 2 * a:3 * a]
    qi = _pad_heads(w_in[:, 3 * a:4 * a])
    o = 4 * a
    ki = w_in[:, o:o + IDX_DIM]
    wi = w_in[:, o + IDX_DIM:o + IDX_DIM + N_HEADS]
    o = o + IDX_DIM + N_HEADS
    rest = w_in[:, o:o + 3 * B_WIDTH]
    w = jnp.concatenate([q, k, qi, ki, ki, rest], axis=1)
    assert w.shape[1] == _E_COLS
    wt = jnp.concatenate([v.T, wi.T, jnp.zeros((BF16_ROWS - N_HEADS, d), w_in.dtype)], axis=0)
    assert wt.shape[0] == _ET_ROWS
    return w.astype(BF16), wt.astype(BF16)


def _even_in_kernel(x_ref, mod_ref, w_ref, wt_ref, cw_ref, qf_ref,
                    q_ref, k_ref, vt_ref, qi_ref, ki_ref, wit_ref, yb_ref, carry_ref):
    d = D_MODEL
    i = pl.program_id(0)

    @pl.when(i == 0)
    def _():
        carry_ref[...] = jnp.zeros_like(carry_ref)

    h = (x_ref[...] * (1.0 + mod_ref[:, d:2 * d]) + mod_ref[:, 0:d]).astype(BF16)

    def proj(cols):
        return _dot(h, w_ref[:, cols[0]:cols[1]])

    def proj_t(rows):
        return _dot_nt(wt_ref[rows[0]:rows[1], :], h)

    q_ref[...] = (proj(_E_Q) * DSA_Q_SCALE + qf_ref[...]).astype(BF16)
    pos = i * TM + lax.broadcasted_iota(jnp.int32, (TM, LANE), 0)
    lane = lax.broadcasted_iota(jnp.int32, (TM, LANE), 1) - _FEAT_LANE
    pos_hi = (pos // _POS_RADIX).astype(F32)
    pos_lo = (pos % _POS_RADIX).astype(F32)
    kfeat = jnp.where(lane < 0, 0.0,
                      jnp.where(lane < _N_LOG2E_TERMS, pos_hi,
                                jnp.where(lane < 2 * _N_LOG2E_TERMS, pos_lo, 0.0)))
    kproj = proj(_E_K)
    for hd in range(N_HEADS):
        sl = slice(hd * LANE, (hd + 1) * LANE)
        k_ref[:, sl] = (kproj[:, sl] + kfeat).astype(BF16)
    _store_values_t(vt_ref, proj_t(_ET_V))
    qi_ref[...] = proj(_E_QI).astype(BF16)
    ki_ref[...] = proj(_E_KI).astype(BF16)
    wit_ref[...] = proj_t(_ET_WI) * IDX_W_SCALE
    g = proj(_E_CG) * proj(_E_XB)
    y = _causal_conv3(g, carry_ref[...], cw_ref[...])
    yb_ref[...] = (proj(_E_BG) * y).astype(BF16)
    carry_ref[...] = g[TM - SUBLANE:TM]


def _even_in(x, mod, w, wt, conv_w, qfeat):
    s, d = x.shape
    row = lambda n: pl.BlockSpec((TM, n), lambda i: (i, 0))
    col = lambda n: pl.BlockSpec((n, TM), lambda i: (0, i))
    full = lambda a: pl.BlockSpec(a.shape, lambda i: (0,) * a.ndim)
    rows_out = lambda n, t: (row(n), jax.ShapeDtypeStruct((s, n), t))
    cols_out = lambda n, t: (col(n), jax.ShapeDtypeStruct((n, s), t))
    outs = [rows_out(N_HEADS * LANE, BF16), rows_out(N_HEADS * LANE, BF16), cols_out(N_HEADS * V_ROWS, BF16),
            rows_out(N_HEADS * LANE, BF16), rows_out(LANE, BF16), cols_out(BF16_ROWS, F32),
            rows_out(B_WIDTH, BF16)]
    return pl.pallas_call(
        _even_in_kernel,
        grid=(s // TM,),
        in_specs=[row(d), full(mod), full(w), full(wt), full(conv_w), full(qfeat)],
        out_specs=[o[0] for o in outs],
        out_shape=[o[1] for o in outs],
        scratch_shapes=[pltpu.VMEM((SUBLANE, B_WIDTH), F32)],
        compiler_params=_params(),
        name="even_in",
    )(x, mod, w, wt, conv_w, qfeat)


def _select_kernel(qi_ref, ki_ref, wt_ref, bias_ref, hi_ref, lo_ref, m1_ref, m2_ref, mc_ref, *, seq):
    tq, tk = TQ_SEL, TK_SEL
    i16 = jnp.int16
    i = pl.program_id(0)
    start = i * tq
    n_tiles = (start + tq + tk - 1) // tk
    tpos = start + lax.broadcasted_iota(jnp.int32, (1, tq), 1)
    limit = (tpos // CHUNK + 1) * CHUNK
    int_min = jnp.int32(INT_MIN)
    dmin, dmax = -2 ** 15, 2 ** 15 - 1
    zero16, one16 = jnp.zeros((), i16), jnp.ones((), i16)
    sel0, seln = jnp.zeros((), BF16), jnp.full((), NEG, BF16)

    def key_pos(off):
        return off + lax.broadcasted_iota(jnp.int32, (tk, tq), 0)

    def score_tile(j, carry, *, masked, tiles=1):
        for t in range(tiles):
            score_rows(pl.multiple_of((j * tiles + t) * tk, tk), masked)
        return carry

    def score_rows(off, masked):
        kt = ki_ref[pl.ds(off, tk), :]
        score = jnp.zeros((tk, tq), F32)
        for h in range(N_HEADS):
            rel = _dot_nt(kt, qi_ref[:, h * LANE:(h + 1) * LANE])
            score = score + wt_ref[h:h + 1, :] * jnp.maximum(rel, 0.0)
        bits = pltpu.bitcast(score, jnp.int32)
        key = bits ^ ((bits >> 31) & jnp.int32(0x7FFFFFFF))
        key = jnp.where(bits == int_min, 0, key)
        if masked:
            key = jnp.where(key_pos(off) < limit, key, int_min)
        hi_ref[pl.ds(off, tk), :] = (key >> 16).astype(i16)
        lo_ref[pl.ds(off, tk), :] = key.astype(i16) ^ jnp.asarray(dmin, i16)

    n_before = start // tk
    n_pairs = n_before // 2
    lax.fori_loop(0, n_pairs, functools.partial(score_tile, masked=False, tiles=2), 0)
    lax.fori_loop(2 * n_pairs, n_before, functools.partial(score_tile, masked=False), 0)
    lax.fori_loop(n_before, n_tiles, functools.partial(score_tile, masked=True), 0)

    def rows_at(ref, off, rows):
        return ref[pl.ds(off, rows), :]

    def for_key_tiles(fn, carry):
        return lax.fori_loop(0, n_tiles, lambda j, c: fn(pl.multiple_of(j * tk, tk), tk, c), carry)

    def per_query(c):
        return jnp.sum(c.astype(jnp.int32).astype(F32), axis=0, keepdims=True)

    def count(flag):
        def body(off, rows, acc):
            return acc + _fold_rows(flag(off, rows), BF16_ROWS)
        return per_query(for_key_tiles(body, jnp.zeros((BF16_ROWS, tq), i16)))

    def radix_search(count_ge, need):
        def bit_step(b, thr):
            cand = thr + jnp.left_shift(jnp.int32(1), 15 - b)
            return jnp.where(count_ge(cand.astype(i16)) >= need, cand, thr)
        return lax.fori_loop(0, 16, bit_step, jnp.full((1, tq), dmin, jnp.int32))

    topk = float(TOPK)
    t_hi = radix_search(lambda c16: count(
        lambda off, rows: jnp.where(rows_at(hi_ref, off, rows) >= c16, one16, zero16)), topk)
    t_hi16 = t_hi.astype(i16)
    n_above = count(lambda off, rows: jnp.where(rows_at(hi_ref, off, rows) > t_hi16, one16, zero16))
    need_lo = topk - n_above

    m1_ref[...] = jnp.full((tk, tq), dmin, i16)
    m2_ref[...] = jnp.full((tk, tq), dmin, i16)
    mc_ref[...] = jnp.zeros((tk, tq), i16)

    def fold_members(off, rows, carry):
        member = rows_at(hi_ref, off, rows) == t_hi16
        x = jnp.where(member, rows_at(lo_ref, off, rows), jnp.asarray(dmin, i16))
        lo_ref[pl.ds(off, rows), :] = x
        a, b = m1_ref[...], m2_ref[...]
        above = x > a
        m1_ref[...] = jnp.where(above, x, a)
        second = jnp.where(above, a, x)
        m2_ref[...] = jnp.where(second > b, second, b)
        mc_ref[...] = mc_ref[...] + jnp.where(member, one16, zero16)
        return carry

    for_key_tiles(fold_members, 0)

    def count_slots(flag):
        return per_query(_fold_rows(flag(m1_ref[...]) + flag(m2_ref[...]), BF16_ROWS))

    def level2_slots():
        t = radix_search(lambda c16: count_slots(lambda m: jnp.where(m >= c16, one16, zero16)), need_lo)
        t16 = t.astype(i16)
        n_members = per_query(_fold_rows(mc_ref[...], BF16_ROWS))
        gt = count_slots(lambda m: jnp.where(m > t16, one16, zero16))
        ge = jnp.where(t == dmin, n_members, count_slots(lambda m: jnp.where(m >= t16, one16, zero16)))
        return t, gt, ge

    def level2_full():
        t = radix_search(lambda c16: count(
            lambda off, rows: jnp.where(rows_at(lo_ref, off, rows) >= c16, one16, zero16)), need_lo)
        t16 = t.astype(i16)
        gt = count(lambda off, rows: jnp.where(rows_at(lo_ref, off, rows) > t16, one16, zero16))
        ge = count(lambda off, rows: jnp.where(
            rows_at(hi_ref, off, rows) == t_hi16,
            jnp.where(rows_at(lo_ref, off, rows) >= t16, one16, zero16), zero16))
        return t, gt, ge

    crowded = per_query(_fold_rows(jnp.where(mc_ref[...] > jnp.asarray(2, i16), one16, zero16), BF16_ROWS))
    t_lo, n_gt_lo, n_ge_lo = lax.cond(jnp.max(crowded) > 0.0, level2_full, level2_slots)
    t_lo16 = t_lo.astype(i16)
    n_gt = n_above + n_gt_lo
    n_ge = n_above + n_ge_lo
    room = topk - n_gt
    sentinel = t_hi == dmin
    has_ties = jnp.where(sentinel, 0.0, jnp.where(n_ge > topk, 1.0, 0.0))
    any_ties = jnp.max(has_ties) > 0.0

    def pos16(off, rows):
        return (off + lax.broadcasted_iota(jnp.int32, (rows, tq), 0)).astype(i16)

    def tie_cutoff():
        def cut_step(b, cut):
            cand = cut + jnp.left_shift(jnp.int32(1), 14 - b)
            c16 = cand.astype(i16)

            def flag(off, rows):
                tied = jnp.where(rows_at(hi_ref, off, rows) == t_hi16,
                                 jnp.where(rows_at(lo_ref, off, rows) == t_lo16, pos16(off, rows),
                                           jnp.asarray(dmax, i16)),
                                 jnp.asarray(dmax, i16))
                return jnp.where(tied < c16, one16, zero16)
            return jnp.where(count(flag) <= room, cand, cut)
        return lax.fori_loop(0, 15, cut_step, jnp.zeros((1, tq), jnp.int32))

    def write_plain():
        lo_min16 = jnp.where(sentinel, dmax, t_lo).astype(i16)

        def write_rows(off, rows, carry):
            hi = rows_at(hi_ref, off, rows)
            inner = jnp.where(rows_at(lo_ref, off, rows) >= lo_min16, sel0, seln)
            bias_ref[pl.ds(off, rows), :] = jnp.where(hi > t_hi16, sel0,
                                                      jnp.where(hi == t_hi16, inner, seln))
            return carry
        for_key_tiles(write_rows, 0)

    def write_with_ties():
        cut16 = jnp.where(sentinel, 0, tie_cutoff()).astype(i16)

        def write_rows(off, rows, carry):
            hi = rows_at(hi_ref, off, rows)
            lo = rows_at(lo_ref, off, rows)
            tie = jnp.where(pos16(off, rows) < cut16, sel0, seln)
            inner = jnp.where(lo > t_lo16, sel0, jnp.where(lo == t_lo16, tie, seln))
            bias_ref[pl.ds(off, rows), :] = jnp.where(hi > t_hi16, sel0,
                                                      jnp.where(hi == t_hi16, inner, seln))
            return carry
        for_key_tiles(write_rows, 0)

    lax.cond(any_ties, write_with_ties, write_plain)

    def fill_tile(j, carry):
        off = pl.multiple_of(j * tk, tk)
        bias_ref[pl.ds(off, tk), :] = jnp.full((tk, tq), NEG, BF16)
        return carry

    lax.fori_loop(n_tiles, seq // tk, fill_tile, 0)


def _select(qi, ki, wit):
    s = qi.shape[0]
    assert s % TK_SEL == 0 and s % TQ_SEL == 0 and 2 * TOPK <= s <= 32767
    return pl.pallas_call(
        functools.partial(_select_kernel, seq=s),
        grid=(s // TQ_SEL,),
        in_specs=[pl.BlockSpec((TQ_SEL, N_HEADS * LANE), lambda i: (i, 0)),
                  pl.BlockSpec((s, LANE), lambda i: (0, 0)),
                  pl.BlockSpec((BF16_ROWS, TQ_SEL), lambda i: (0, i))],
        out_specs=pl.BlockSpec((s, TQ_SEL), lambda i: (0, i)),
        out_shape=jax.ShapeDtypeStruct((s, s), BF16),
        scratch_shapes=[pltpu.VMEM((s, TQ_SEL), jnp.int16), pltpu.VMEM((s, TQ_SEL), jnp.int16)]
        + [pltpu.VMEM((TK_SEL, TQ_SEL), jnp.int16)] * 3,
        compiler_params=_params(),
        name="dsa_select",
    )(qi, ki, wit)


def _flash_kernel(qi_of, kj_of, *refs, alibi, use_bias):
    if use_bias:
        q_ref, k_ref, vt_ref, bias_ref, o_ref, m_ref, acc_ref = refs
    else:
        q_ref, k_ref, vt_ref, o_ref, m_ref, acc_ref = refs
    tq, tk = TQ_ATT, TK_ATT
    p = pl.program_id(0)
    qi = qi_of[p]
    kj = kj_of[p]
    last = ((qi + 1) * tq - 1) // tk

    @pl.when(kj == 0)
    def _():
        m_ref[...] = jnp.full_like(m_ref, NEG)
        acc_ref[...] = jnp.zeros_like(acc_ref)

    def step(diagonal):
        base = bias_ref[...].astype(F32) if use_bias else None
        if diagonal:
            tpos = qi * tq + lax.broadcasted_iota(jnp.int32, (tk, tq), 1)
            spos = kj * tk + lax.broadcasted_iota(jnp.int32, (tk, tq), 0)
            if not use_bias:
                base = jnp.where(spos < (tpos // CHUNK + 1) * CHUNK, 0.0, NEG)
            if alibi:
                ahead = jnp.maximum(spos - tpos, 0).astype(F32)

        def logits(h):
            s = _dot_nt(k_ref[:, h * LANE:(h + 1) * LANE], q_ref[:, h * LANE:(h + 1) * LANE])
            if diagonal and alibi:
                return s + (base - (2.0 * ALIBI_SLOPES[h] * LOG2E) * ahead)
            return s if base is None else s + base

        pending = [logits(h) for h in range(QK_AHEAD)]
        for h in range(N_HEADS):
            rows = slice(h * V_ROWS, (h + 1) * V_ROWS)
            if h + QK_AHEAD < N_HEADS:
                pending.append(logits(h + QK_AHEAD))
            s = pending.pop(0)
            m_prev = m_ref[h:h + 1, :]
            m_new = jnp.maximum(m_prev, jnp.max(s, axis=0, keepdims=True))
            alpha = jnp.exp2(m_prev - m_new)
            pexp = jnp.exp2(s - m_new)
            acc_ref[rows, :] = alpha * acc_ref[rows, :] + _dot(vt_ref[rows, :], pexp.astype(BF16))
            m_ref[h:h + 1, :] = m_new

    @pl.when(kj != last)
    def _():
        step(False)

    @pl.when(kj == last)
    def _():
        step(True)
        outs = [acc_ref[h * V_ROWS:h * V_ROWS + D_V, :] / acc_ref[h * V_ROWS + D_V:h * V_ROWS + D_V + 1, :]
                for h in range(N_HEADS)]
        o_ref[...] = jnp.concatenate(outs, axis=0).T.astype(BF16)


def _flash(q, k, vt, bias, *, alibi):
    s = q.shape[0]
    nq = s // TQ_ATT
    pairs = [(i, j) for i in range(nq) for j in range(((i + 1) * TQ_ATT - 1) // TK_ATT + 1)]
    qi_of = jnp.asarray(np.array([a for a, _ in pairs], np.int32))
    kj_of = jnp.asarray(np.array([b for _, b in pairs], np.int32))
    use_bias = bias is not None
    in_specs = [pl.BlockSpec((TQ_ATT, N_HEADS * LANE), lambda p, qi, kj: (qi[p], 0)),
                pl.BlockSpec((TK_ATT, k.shape[1]), lambda p, qi, kj: (kj[p], 0)),
                pl.BlockSpec((N_HEADS * V_ROWS, TK_ATT), lambda p, qi, kj: (0, kj[p]))]
    args = [q, k, vt]
    if use_bias:
        in_specs.append(pl.BlockSpec((TK_ATT, TQ_ATT), lambda p, qi, kj: (kj[p], qi[p])))
        args.append(bias)
    kern = functools.partial(_flash_kernel, alibi=alibi, use_bias=use_bias)
    return pl.pallas_call(
        kern,
        grid_spec=pltpu.PrefetchScalarGridSpec(
            num_scalar_prefetch=2,
            grid=(len(pairs),),
            in_specs=in_specs,
            out_specs=pl.BlockSpec((TQ_ATT, N_HEADS * D_V), lambda p, qi, kj: (qi[p], 0)),
            scratch_shapes=[pltpu.VMEM((N_HEADS, TQ_ATT), F32),
                            pltpu.VMEM((N_HEADS * V_ROWS, TQ_ATT), F32)]),
        out_shape=jax.ShapeDtypeStruct((s, N_HEADS * D_V), BF16),
        compiler_params=_params(),
        name="flash_dsa" if use_bias else "flash_mla",
    )(qi_of, kj_of, *args)


def _mix_ffn_kernel(ya_ref, yb_ref, x_ref, mod_ref, wo_ref, gm_ref, bm_ref,
                    wup_ref, cw_ref, wdn_ref, gf_ref, bf_ref, o_ref, carry_ref, act_ref):
    d = D_MODEL

    @pl.when(pl.program_id(0) == 0)
    def _():
        carry_ref[...] = jnp.zeros_like(carry_ref)

    na = ya_ref.shape[1]
    y = _dot(ya_ref[...], wo_ref[0:na, :]) + _dot(yb_ref[...], wo_ref[na:, :])
    x = _layer_norm(DN_ALPHA * x_ref[...] + (1.0 + mod_ref[:, 2 * d:3 * d]) * y, gm_ref[...], bm_ref[...])
    h = (x * (1.0 + mod_ref[:, 4 * d:5 * d]) + mod_ref[:, 3 * d:4 * d]).astype(BF16)
    for c in range(D_FF // FFN_CHUNK):
        va = c * FFN_CHUNK
        ga = D_FF + va
        uv = _dot(h, wup_ref[:, va:va + FFN_CHUNK])
        ug = _dot(h, wup_ref[:, ga:ga + FFN_CHUNK])
        val = _causal_conv3(uv, carry_ref[:, va:va + FFN_CHUNK], cw_ref[:, va:va + FFN_CHUNK])
        gate = _causal_conv3(ug, carry_ref[:, ga:ga + FFN_CHUNK], cw_ref[:, ga:ga + FFN_CHUNK])
        carry_ref[:, va:va + FFN_CHUNK] = uv[TM - SUBLANE:TM]
        carry_ref[:, ga:ga + FFN_CHUNK] = ug[TM - SUBLANE:TM]
        act_ref[:, va:va + FFN_CHUNK] = (gate * jax.nn.sigmoid(gate) * val).astype(BF16)
    y = _dot(act_ref[...], wdn_ref[...])
    z = DN_ALPHA * x + (1.0 + mod_ref[:, 5 * d:6 * d]) * y
    o_ref[...] = _layer_norm(z, gf_ref[...], bf_ref[...])


def _mix_ffn(ya, yb, x, mod, w_out, g_mix, b_mix, w_up, conv_w, w_down, g_ffn, b_ffn):
    s, d = x.shape
    assert D_FF % FFN_CHUNK == 0
    row = lambda n: pl.BlockSpec((TM, n), lambda i: (i, 0))
    full = lambda a: pl.BlockSpec(a.shape, lambda i: (0,) * a.ndim)
    once = lambda a: pl.BlockSpec(a.shape, lambda i: (0,) * a.ndim, pipeline_mode=pl.Buffered(1))
    vec = lambda a: a.reshape(1, d)
    args = [ya, yb, x, mod, w_out, vec(g_mix), vec(b_mix), w_up, conv_w, w_down, vec(g_ffn), vec(b_ffn)]
    specs = [row(ya.shape[1]), row(yb.shape[1]), row(d), full(mod), once(w_out), full(args[5]), full(args[6]),
             once(w_up), full(conv_w), once(w_down), full(args[10]), full(args[11])]
    return pl.pallas_call(
        _mix_ffn_kernel,
        grid=(s // TM,),
        in_specs=specs,
        out_specs=row(d),
        out_shape=jax.ShapeDtypeStruct((s, d), F32),
        scratch_shapes=[pltpu.VMEM((SUBLANE, 2 * D_FF), F32), pltpu.VMEM((TM, D_FF), BF16)],
        compiler_params=_params(),
        name="mix_ffn",
    )(*args)


_O_U = (0, 512)
_O_QLAT = (512, 896)
_O_KVLAT = (896, 1152)
_O_KR = (1152, 1280)
_O_KR_ROT = (1280, 1408)
_O_COLS = 1408


def _rotate_half_cols(w):
    half = w.shape[-1] // 2
    return jnp.concatenate([-w[..., half:], w[..., :half]], axis=-1)


def _odd_weights(w_in, w_uq, w_ukv):
    d = w_in.shape[0]
    o = C_WIDTH + Q_LORA + KV_LORA
    kr = w_in[:, o:o + D_ROPE]
    zl = jnp.zeros((d, D_NOPE), w_in.dtype)
    zr = jnp.zeros((d, LANE - D_NOPE - D_ROPE), w_in.dtype)
    w1 = jnp.concatenate([w_in[:, 0:o], zl, kr, zr, zl, _rotate_half_cols(kr), zr], axis=1)
    assert w1.shape[1] == _O_COLS
    dq = D_NOPE + D_ROPE
    uq = w_uq.reshape(Q_LORA, N_HEADS, dq)
    padq = jnp.zeros((Q_LORA, N_HEADS, LANE - dq), w_uq.dtype)
    wq1 = jnp.concatenate([uq, padq], axis=2).reshape(Q_LORA, N_HEADS * LANE)
    wq2 = jnp.concatenate([jnp.zeros((Q_LORA, N_HEADS, D_NOPE), w_uq.dtype),
                           _rotate_half_cols(uq[:, :, D_NOPE:]), padq], axis=2
                          ).reshape(Q_LORA, N_HEADS * LANE)
    ukv = w_ukv.reshape(KV_LORA, N_HEADS, D_NOPE + D_V)
    wk = jnp.concatenate([ukv[:, :, :D_NOPE], jnp.zeros((KV_LORA, N_HEADS, LANE - D_NOPE), w_ukv.dtype)],
                         axis=2).reshape(KV_LORA, N_HEADS * LANE)
    wvt = ukv[:, :, D_NOPE:].reshape(KV_LORA, N_HEADS * D_V).T
    return w1.astype(BF16), wq1.astype(BF16), wq2.astype(BF16), wk.astype(BF16), wvt.astype(BF16)


def _rope_lane_tables(seq):
    inv = ROPE_BASE ** (-jnp.arange(0, D_ROPE, 2, dtype=F32) / D_ROPE)
    ang = jnp.arange(seq, dtype=F32)[:, None] * inv[None, :]
    cos, sin = jnp.cos(ang), jnp.sin(ang)
    pad = jnp.zeros((seq, LANE - D_NOPE - D_ROPE), F32)
    cos_t = jnp.concatenate([jnp.ones((seq, D_NOPE), F32), cos, cos, pad], axis=1)
    sin_t = jnp.concatenate([jnp.zeros((seq, D_NOPE), F32), sin, sin, pad], axis=1)
    return cos_t, sin_t


def _odd_in_kernel(x_ref, mod_ref, w1_ref, wq1_ref, wq2_ref, wk_ref, wvt_ref, pw_ref, ps_ref,
                   qg_ref, kvg_ref, cos_ref, sin_ref,
                   yc_ref, q_ref, k_ref, vt_ref, carry_ref):
    d = D_MODEL
    i = pl.program_id(0)

    @pl.when(i == 0)
    def _():
        carry_ref[...] = jnp.zeros_like(carry_ref)

    h = (x_ref[...] * (1.0 + mod_ref[:, d:2 * d]) + mod_ref[:, 0:d]).astype(BF16)

    def proj(cols):
        return _dot(h, w1_ref[:, cols[0]:cols[1]])

    cos = cos_ref[...]
    sin = sin_ref[...]

    u = proj(_O_U)
    ext = jnp.concatenate([carry_ref[...], u], axis=0)
    pos = i * TM + lax.broadcasted_iota(jnp.int32, (TM, 1), 0)
    for g, win in enumerate(POOL_WINDOWS):
        sl = slice(g * C_GROUP_DIM, (g + 1) * C_GROUP_DIM)
        acc = ext[:, sl]
        shift = 1
        while shift < win:
            acc = acc + pltpu.roll(acc, shift, 0)
            shift *= 2
        cnt = jnp.minimum(pos + 1, win).astype(F32)
        pooled = acc[POOL_HALO:] / cnt - u[:, sl]
        mixed = _dot(pooled.astype(BF16), pw_ref[g])
        yc_ref[:, sl] = (mixed * ps_ref[:, sl]).astype(BF16)
    carry_ref[...] = u[TM - POOL_HALO:TM]

    r = _rms_norm(proj(_O_QLAT), qg_ref[...]).astype(BF16)
    qa = _dot(r, wq1_ref[...])
    qb = _dot(r, wq2_ref[...])
    kr = proj(_O_KR) * cos + proj(_O_KR_ROT) * sin
    rk = _rms_norm(proj(_O_KVLAT), kvg_ref[...]).astype(BF16)
    kn = _dot(rk, wk_ref[...])
    for hd in range(N_HEADS):
        sl = slice(hd * LANE, (hd + 1) * LANE)
        q_ref[:, sl] = ((qa[:, sl] * cos + qb[:, sl] * sin) * MLA_Q_SCALE).astype(BF16)
        k_ref[:, sl] = (kn[:, sl] + kr).astype(BF16)
    _store_values_t(vt_ref, _dot_nt(wvt_ref[...], rk))


def _odd_in(x, mod, w1, wq1, wq2, wk, wvt, pool_w, pool_scale, q_norm_g, kv_norm_g, cos_t, sin_t):
    s, d = x.shape
    row = lambda n: pl.BlockSpec((TM, n), lambda i: (i, 0))
    col = lambda n: pl.BlockSpec((n, TM), lambda i: (0, i))
    full = lambda a: pl.BlockSpec(a.shape, lambda i: (0,) * a.ndim)
    consts = [w1, wq1, wq2, wk, wvt, pool_w.astype(BF16), pool_scale.reshape(1, C_WIDTH),
              q_norm_g.reshape(1, Q_LORA), kv_norm_g.reshape(1, KV_LORA)]
    rows_out = lambda n, t: (row(n), jax.ShapeDtypeStruct((s, n), t))
    cols_out = lambda n, t: (col(n), jax.ShapeDtypeStruct((n, s), t))
    outs = [rows_out(C_WIDTH, BF16), rows_out(N_HEADS * LANE, BF16), rows_out(N_HEADS * LANE, BF16),
            cols_out(N_HEADS * V_ROWS, BF16)]
    return pl.pallas_call(
        _odd_in_kernel,
        grid=(s // TM,),
        in_specs=[row(d), full(mod)] + [full(a) for a in consts] + [row(LANE), row(LANE)],
        out_specs=[o[0] for o in outs],
        out_shape=[o[1] for o in outs],
        scratch_shapes=[pltpu.VMEM((POOL_HALO, C_WIDTH), F32)],
        compiler_params=_params(),
        name="odd_in",
    )(x, mod, *consts, cos_t, sin_t)


def kernel(x, c, ada_w, ada_b, ln_mix_g, ln_mix_b, ln_ffn_g, ln_ffn_b, ev_w_in, ev_conv_w, ev_w_out,
           od_w_in, pool_w, pool_scale, q_norm_g, w_uq, kv_norm_g, w_ukv, od_w_out,
           ffn_w_up, ffn_conv_w, ffn_w_down):
    bsz, seq, d = x.shape
    assert bsz == 1 and d == D_MODEL and seq % TM == 0 and seq % TK_ATT == 0
    assert SHORT_CONV == 3 and FFN_CONV == 3
    xs = x.reshape(seq, d)
    mods = _adaln(c, ada_w, ada_b)
    for l in range(DEPTH):
        mod = mods[l]
        if l % 2 == 0:
            e = l // 2
            w, wt = _even_weights(ev_w_in[e])
            q, k, vt, qi, ki, wit, yb = _even_in(xs, mod, w, wt, ev_conv_w[e], _alibi_q_features())
            bias = _select(qi, ki, wit)
            mixed = (_flash(q, k, vt, bias, alibi=True), yb)
            w_out = ev_w_out[e]
        else:
            o = l // 2
            cos_t, sin_t = _rope_lane_tables(seq)
            ws = _odd_weights(od_w_in[o], w_uq[o], w_ukv[o])
            yc, q, k, vt = _odd_in(xs, mod, *ws, pool_w[o], pool_scale[o], q_norm_g[o], kv_norm_g[o],
                                   cos_t, sin_t)
            mixed = (yc, _flash(q, k, vt, None, alibi=False))
            w_out = od_w_out[o]
        xs = _mix_ffn(*mixed, xs, mod, w_out.astype(BF16), ln_mix_g[l], ln_mix_b[l],
                      ffn_w_up[l].astype(BF16), ffn_conv_w[l], ffn_w_down[l].astype(BF16),
                      ln_ffn_g[l], ln_ffn_b[l])
    return xs.reshape(bsz, seq, d)
```

```python
import functools

import numpy as np
import jax
import jax.numpy as jnp
from jax import lax
from jax.experimental import pallas as pl
from jax.experimental.pallas import tpu as pltpu

D_MODEL = 1024
DEPTH = 2
CHUNK = 64
N_HEADS = 8
A_HEAD_DIM = 64
A_WIDTH = N_HEADS * A_HEAD_DIM
IDX_DIM = 64
TOPK = 256
B_WIDTH = 512
SHORT_CONV = 3
C_WIDTH = 512
POOL_WINDOWS = (2, 4, 8, 16)
C_GROUP_DIM = C_WIDTH // len(POOL_WINDOWS)
D_NOPE = 64
D_ROPE = 32
D_V = 64
Q_LORA = 384
KV_LORA = 256
ROPE_BASE = 10000.0
D_FF = 2816
FFN_CONV = 3
LN_EPS = 1e-5
RMS_EPS = 1e-6
DN_ALPHA = (2 * DEPTH) ** 0.25
NEG = -1e30
IDX_W_SCALE = N_HEADS ** -0.5 * IDX_DIM ** -0.5
LOG2E = 1.4426950408889634
DSA_Q_SCALE = A_HEAD_DIM ** -0.5 * LOG2E
MLA_Q_SCALE = (D_NOPE + D_ROPE) ** -0.5 * LOG2E
ALIBI_SLOPES = tuple(2.0 ** (-8.0 * (i + 1) / N_HEADS) for i in range(N_HEADS))

LANE = 128
SUBLANE = 8
BF16_ROWS = 16
V_ROWS = D_V + BF16_ROWS
VMEM_LIMIT_BYTES = 56 * 1024 * 1024

TM = 512
FFN_CHUNK = 256
TQ_SEL = 256
TQ_ATT = 512
TK_SEL = 512
TK_ATT = 1024
QK_AHEAD = 4
POOL_HALO = 16

INT_MIN = -2 ** 31
F32 = jnp.float32
BF16 = jnp.bfloat16


def _params():
    return pltpu.CompilerParams(dimension_semantics=("arbitrary",),
                                vmem_limit_bytes=VMEM_LIMIT_BYTES)


def _dot(a, b):
    return jnp.dot(a, b, preferred_element_type=F32)


def _dot_nt(a, b):
    return lax.dot_general(a, b, (((1,), (1,)), ((), ())), preferred_element_type=F32)


def _layer_norm(z, g, b):
    mu = jnp.mean(z, axis=-1, keepdims=True)
    zc = z - mu
    var = jnp.mean(zc * zc, axis=-1, keepdims=True)
    return zc * lax.rsqrt(var + LN_EPS) * g + b


def _rms_norm(z, g):
    return z * lax.rsqrt(jnp.mean(z * z, axis=-1, keepdims=True) + RMS_EPS) * g


def _causal_conv3(u, prev, w):
    w0, w1, w2 = w[0:1], w[1:2], w[2:3]
    y = pltpu.roll(u, 2, 0) * w0 + pltpu.roll(u, 1, 0) * w1 + u * w2
    head = u[0:SUBLANE]
    r = lax.broadcasted_iota(jnp.int32, (SUBLANE, 1), 0)
    h1 = jnp.where(r == 0, prev[7:8], pltpu.roll(head, 1, 0))
    h2 = jnp.where(r == 0, prev[6:7], jnp.where(r == 1, prev[7:8], pltpu.roll(head, 2, 0)))
    yh = h2 * w0 + h1 * w1 + head * w2
    return jnp.concatenate([yh, y[SUBLANE:]], axis=0)


def _fold_rows(c, rows_out):
    rows = c.shape[0]
    while rows > rows_out:
        rows //= 2
        c = c[0:rows] + c[rows:2 * rows]
    return c


def _store_values_t(vt_ref, vt):
    t = vt.shape[1]
    ones_row = jnp.where(lax.broadcasted_iota(jnp.int32, (BF16_ROWS, t), 0) == 0, 1.0, 0.0)
    for h in range(N_HEADS):
        vt_ref[h * V_ROWS:h * V_ROWS + D_V, :] = vt[h * D_V:(h + 1) * D_V, :].astype(BF16)
        vt_ref[h * V_ROWS + D_V:(h + 1) * V_ROWS, :] = ones_row.astype(BF16)


def _adaln_kernel(c_ref, w_ref, b_ref, o_ref):
    c = c_ref[...]
    cond = c * jax.nn.sigmoid(c)
    o_ref[0] = jnp.sum(cond * w_ref[0], axis=0, keepdims=True) + b_ref[0]


def _adaln(c, ada_w, ada_b):
    depth, d, n = ada_w.shape
    tn = 1536
    return pl.pallas_call(
        _adaln_kernel,
        grid=(depth, n // tn),
        in_specs=[pl.BlockSpec((d, 1), lambda l, j: (0, 0)),
                  pl.BlockSpec((1, d, tn), lambda l, j: (l, 0, j)),
                  pl.BlockSpec((1, 1, tn), lambda l, j: (l, 0, j))],
        out_specs=pl.BlockSpec((1, 1, tn), lambda l, j: (l, 0, j)),
        out_shape=jax.ShapeDtypeStruct((depth, 1, n), F32),
        compiler_params=pltpu.CompilerParams(dimension_semantics=("arbitrary", "arbitrary"),
                                             vmem_limit_bytes=VMEM_LIMIT_BYTES),
        name="adaln",
    )(c.reshape(d, 1), ada_w, ada_b.reshape(depth, 1, n))


_E_Q = (0, 1024)
_E_K = (1024, 2048)
_E_QI = (2048, 3072)
_E_KI = (3072, 3200)
_E_BG = (3200, 3712)
_E_CG = (3712, 4224)
_E_XB = (4224, 4736)
_E_COLS = 4736
_ET_V = (0, 512)
_ET_WI = (512, 528)
_ET_ROWS = 528
_FEAT_LANE = A_HEAD_DIM
_N_LOG2E_TERMS = 3
_POS_RADIX = 128


def _pad_heads(w):
    d = w.shape[0]
    w3 = w.reshape(d, N_HEADS, A_HEAD_DIM)
    return jnp.concatenate([w3, jnp.zeros_like(w3)], axis=2).reshape(d, N_HEADS * LANE)


def _alibi_q_features():
    terms, rest = [], np.float64(LOG2E)
    for _ in range(_N_LOG2E_TERMS):
        t = np.float64(np.asarray(rest, np.float32).astype(jnp.bfloat16).astype(np.float32))
        terms.append(t)
        rest = rest - t
    row = np.zeros((N_HEADS, LANE), np.float32)
    for h in range(N_HEADS):
        for n, t in enumerate(terms):
            row[h, _FEAT_LANE + n] = ALIBI_SLOPES[h] * _POS_RADIX * t
            row[h, _FEAT_LANE + _N_LOG2E_TERMS + n] = ALIBI_SLOPES[h] * t
    return jnp.asarray(row.reshape(1, N_HEADS * LANE))


def _even_weights(w_in):
    d = w_in.shape[0]
    a = A_WIDTH
    q = _pad_heads(w_in[:, 0:a])
    k = _pad_heads(w_in[:, a:2 * a])
    v = w_in[:, 2 * a:3 * a]
    qi = _pad_heads(w_in[:, 3 * a:4 * a])
    o = 4 * a
    ki = w_in[:, o:o + IDX_DIM]
    wi = w_in[:, o + IDX_DIM:o + IDX_DIM + N_HEADS]
    o = o + IDX_DIM + N_HEADS
    rest = w_in[:, o:o + 3 * B_WIDTH]
    w = jnp.concatenate([q, k, qi, ki, ki, rest], axis=1)
    assert w.shape[1] == _E_COLS
    wt = jnp.concatenate([v.T, wi.T, jnp.zeros((BF16_ROWS - N_HEADS, d), w_in.dtype)], axis=0)
    assert wt.shape[0] == _ET_ROWS
    return w.astype(BF16), wt.astype(BF16)


def _even_in_kernel(x_ref, mod_ref, w_ref, wt_ref, cw_ref, qf_ref,
                    q_ref, k_ref, vt_ref, qi_ref, ki_ref, wit_ref, yb_ref, carry_ref):
    d = D_MODEL
    i = pl.program_id(0)

    @pl.when(i == 0)
    def _():
        carry_ref[...] = jnp.zeros_like(carry_ref)

    h = (x_ref[...] * (1.0 + mod_ref[:, d:2 * d]) + mod_ref[:, 0:d]).astype(BF16)

    def proj(cols):
        return _dot(h, w_ref[:, cols[0]:cols[1]])

    def proj_t(rows):
        return _dot_nt(wt_ref[rows[0]:rows[1], :], h)

    q_ref[...] = (proj(_E_Q) * DSA_Q_SCALE + qf_ref[...]).astype(BF16)
    pos = i * TM + lax.broadcasted_iota(jnp.int32, (TM, LANE), 0)
    lane = lax.broadcasted_iota(jnp.int32, (TM, LANE), 1) - _FEAT_LANE
    pos_hi = (pos // _POS_RADIX).astype(F32)
    pos_lo = (pos % _POS_RADIX).astype(F32)
    kfeat = jnp.where(lane < 0, 0.0,
                      jnp.where(lane < _N_LOG2E_TERMS, pos_hi,
                                jnp.where(lane < 2 * _N_LOG2E_TERMS, pos_lo, 0.0)))
    kproj = proj(_E_K)
    for hd in range(N_HEADS):
        sl = slice(hd * LANE, (hd + 1) * LANE)
        k_ref[:, sl] = (kproj[:, sl] + kfeat).astype(BF16)
    _store_values_t(vt_ref, proj_t(_ET_V))
    qi_ref[...] = proj(_E_QI).astype(BF16)
    ki_ref[...] = proj(_E_KI).astype(BF16)
    wit_ref[...] = proj_t(_ET_WI) * IDX_W_SCALE
    g = proj(_E_CG) * proj(_E_XB)
    y = _causal_conv3(g, carry_ref[...], cw_ref[...])
    yb_ref[...] = (proj(_E_BG) * y).astype(BF16)
    carry_ref[...] = g[TM - SUBLANE:TM]


def _even_in(x, mod, w, wt, conv_w, qfeat):
    s, d = x.shape
    row = lambda n: pl.BlockSpec((TM, n), lambda i: (i, 0))
    col = lambda n: pl.BlockSpec((n, TM), lambda i: (0, i))
    full = lambda a: pl.BlockSpec(a.shape, lambda i: (0,) * a.ndim)
    rows_out = lambda n, t: (row(n), jax.ShapeDtypeStruct((s, n), t))
    cols_out = lambda n, t: (col(n), jax.ShapeDtypeStruct((n, s), t))
    outs = [rows_out(N_HEADS * LANE, BF16), rows_out(N_HEADS * LANE, BF16), cols_out(N_HEADS * V_ROWS, BF16),
            rows_out(N_HEADS * LANE, BF16), rows_out(LANE, BF16), cols_out(BF16_ROWS, F32),
            rows_out(B_WIDTH, BF16)]
    return pl.pallas_call(
        _even_in_kernel,
        grid=(s // TM,),
        in_specs=[row(d), full(mod), full(w), full(wt), full(conv_w), full(qfeat)],
        out_specs=[o[0] for o in outs],
        out_shape=[o[1] for o in outs],
        scratch_shapes=[pltpu.VMEM((SUBLANE, B_WIDTH), F32)],
        compiler_params=_params(),
        name="even_in",
    )(x, mod, w, wt, conv_w, qfeat)


def _select_kernel(qi_ref, ki_ref, wt_ref, bias_ref, hi_ref, lo_ref, m1_ref, m2_ref, mc_ref, my_ref, tc_ref,
                   *, seq):
    tq, tk = TQ_SEL, TK_SEL
    i16 = jnp.int16
    i = pl.program_id(0)
    start = i * tq
    n_tiles = (start + tq + tk - 1) // tk
    tpos = start + lax.broadcasted_iota(jnp.int32, (1, tq), 1)
    limit = (tpos // CHUNK + 1) * CHUNK
    int_min = jnp.int32(INT_MIN)
    dmin, dmax = -2 ** 15, 2 ** 15 - 1
    zero16, one16 = jnp.zeros((), i16), jnp.ones((), i16)
    sel0, seln = jnp.zeros((), BF16), jnp.full((), NEG, BF16)

    def key_pos(off):
        return off + lax.broadcasted_iota(jnp.int32, (tk, tq), 0)

    def score_tile(j, carry, *, masked, tiles=1):
        for t in range(tiles):
            score_rows(pl.multiple_of((j * tiles + t) * tk, tk), masked)
        return carry

    def score_rows(off, masked):
        kt = ki_ref[pl.ds(off, tk), :]
        score = jnp.zeros((tk, tq), F32)
        for h in range(N_HEADS):
            rel = _dot_nt(kt, qi_ref[:, h * LANE:(h + 1) * LANE])
            score = score + wt_ref[h:h + 1, :] * jnp.maximum(rel, 0.0)
        bits = pltpu.bitcast(score, jnp.int32)
        key = bits ^ ((bits >> 31) & jnp.int32(0x7FFFFFFF))
        key = jnp.where(bits == int_min, 0, key)
        if masked:
            key = jnp.where(key_pos(off) < limit, key, int_min)
        hi_ref[pl.ds(off, tk), :] = (key >> 16).astype(i16)
        lo_ref[pl.ds(off, tk), :] = key.astype(i16) ^ jnp.asarray(dmin, i16)

    n_before = start // tk
    n_pairs = n_before // 2
    lax.fori_loop(0, n_pairs, functools.partial(score_tile, masked=False, tiles=2), 0)
    lax.fori_loop(2 * n_pairs, n_before, functools.partial(score_tile, masked=False), 0)
    lax.fori_loop(n_before, n_tiles, functools.partial(score_tile, masked=True), 0)

    def rows_at(ref, off, rows):
        return ref[pl.ds(off, rows), :]

    def for_key_tiles(fn, carry):
        return lax.fori_loop(0, n_tiles, lambda j, c: fn(pl.multiple_of(j * tk, tk), tk, c), carry)

    def per_query(c):
        return jnp.sum(c.astype(jnp.int32).astype(F32), axis=0, keepdims=True)

    def count(flag):
        def body(off, rows, acc):
            return acc + _fold_rows(flag(off, rows), BF16_ROWS)
        return per_query(for_key_tiles(body, jnp.zeros((BF16_ROWS, tq), i16)))

    def radix_search(count_ge, need):
        def bit_step(b, thr):
            cand = thr + jnp.left_shift(jnp.int32(1), 15 - b)
            return jnp.where(count_ge(cand.astype(i16)) >= need, cand, thr)
        return lax.fori_loop(0, 16, bit_step, jnp.full((1, tq), dmin, jnp.int32))

    topk = float(TOPK)
    t_hi = radix_search(lambda c16: count(
        lambda off, rows: jnp.where(rows_at(hi_ref, off, rows) >= c16, one16, zero16)), topk)
    t_hi16 = t_hi.astype(i16)
    n_above = count(lambda off, rows: jnp.where(rows_at(hi_ref, off, rows) > t_hi16, one16, zero16))
    need_lo = topk - n_above

    m1_ref[...] = jnp.full((tk, tq), dmin, i16)
    m2_ref[...] = jnp.full((tk, tq), dmin, i16)
    mc_ref[...] = jnp.zeros((tk, tq), i16)

    def fold_members(off, rows, carry):
        member = rows_at(hi_ref, off, rows) == t_hi16
        x = jnp.where(member, rows_at(lo_ref, off, rows), jnp.asarray(dmin, i16))
        lo_ref[pl.ds(off, rows), :] = x
        a, b = m1_ref[...], m2_ref[...]
        above = x > a
        m1_ref[...] = jnp.where(above, x, a)
        second = jnp.where(above, a, x)
        m2_ref[...] = jnp.where(second > b, second, b)
        mc_ref[...] = mc_ref[...] + jnp.where(member, one16, zero16)
        return carry

    for_key_tiles(fold_members, 0)

    def count_slots(flag):
        return per_query(_fold_rows(flag(m1_ref[...]) + flag(m2_ref[...]), BF16_ROWS))

    def level2_slots():
        t = radix_search(lambda c16: count_slots(lambda m: jnp.where(m >= c16, one16, zero16)), need_lo)
        t16 = t.astype(i16)
        n_members = per_query(_fold_rows(mc_ref[...], BF16_ROWS))
        gt = count_slots(lambda m: jnp.where(m > t16, one16, zero16))
        ge = jnp.where(t == dmin, n_members, count_slots(lambda m: jnp.where(m >= t16, one16, zero16)))
        return t, gt, ge

    def level2_full():
        t = radix_search(lambda c16: count(
            lambda off, rows: jnp.where(rows_at(lo_ref, off, rows) >= c16, one16, zero16)), need_lo)
        t16 = t.astype(i16)
        gt = count(lambda off, rows: jnp.where(rows_at(lo_ref, off, rows) > t16, one16, zero16))
        ge = count(lambda off, rows: jnp.where(
            rows_at(hi_ref, off, rows) == t_hi16,
            jnp.where(rows_at(lo_ref, off, rows) >= t16, one16, zero16), zero16))
        return t, gt, ge

    crowded = per_query(_fold_rows(jnp.where(mc_ref[...] > jnp.asarray(2, i16), one16, zero16), BF16_ROWS))
    t_lo, n_gt_lo, n_ge_lo = lax.cond(jnp.max(crowded) > 0.0, level2_full, level2_slots)
    t_lo16 = t_lo.astype(i16)
    n_gt = n_above + n_gt_lo
    n_ge = n_above + n_ge_lo
    room = topk - n_gt
    sentinel = t_hi == dmin
    has_ties = jnp.where(sentinel, 0.0, jnp.where(n_ge > topk, 1.0, 0.0))
    any_ties = jnp.max(has_ties) > 0.0

    def pos16(off, rows):
        return (off + lax.broadcasted_iota(jnp.int32, (rows, tq), 0)).astype(i16)

    def tie_cutoff():
        def tied_rows(j):
            off = pl.multiple_of(j * tk, tk)
            return jnp.where(rows_at(hi_ref, off, tk) == t_hi16,
                             jnp.where(rows_at(lo_ref, off, tk) == t_lo16, one16, zero16), zero16)

        def count_tile(j, carry):
            tc_ref[pl.ds(j, 1), :] = per_query(_fold_rows(tied_rows(j), BF16_ROWS))
            return carry
        lax.fori_loop(0, n_tiles, count_tile, 0)

        def find_tile(j, carry):
            seen, tile_of, left = carry
            here = tc_ref[pl.ds(j, 1), :]
            crossing = jnp.logical_and(seen < room, seen + here >= room)
            return (seen + here, jnp.where(crossing, j, tile_of), jnp.where(crossing, room - seen, left))
        _, tile_of, left = lax.fori_loop(
            0, n_tiles, find_tile, (jnp.zeros((1, tq), F32), jnp.zeros((1, tq), jnp.int32), room))

        my_ref[...] = jnp.zeros((tk, tq), i16)
        tile_of16 = tile_of.astype(i16)

        def gather_tile(j, carry):
            mine = jnp.where(tile_of16 == j.astype(i16), jnp.asarray(-1, i16), zero16)
            my_ref[...] = my_ref[...] | (tied_rows(j) & mine)
            return carry
        lax.fori_loop(0, n_tiles, gather_tile, 0)

        row16 = lax.broadcasted_iota(jnp.int32, (tk, tq), 0).astype(i16)
        row_bits = tk.bit_length()

        def row_step(b, cut):
            cand = cut + jnp.left_shift(jnp.int32(1), row_bits - 1 - b)
            below = jnp.where(row16 < cand.astype(i16), my_ref[...], zero16)
            return jnp.where(per_query(_fold_rows(below, BF16_ROWS)) <= left, cand, cut)
        row_cut = lax.fori_loop(0, row_bits, row_step, jnp.zeros((1, tq), jnp.int32))
        return tile_of * tk + jnp.minimum(row_cut, tk)

    def write_plain():
        lo_min16 = jnp.where(sentinel, dmax, t_lo).astype(i16)

        def write_rows(off, rows, carry):
            hi = rows_at(hi_ref, off, rows)
            inner = jnp.where(rows_at(lo_ref, off, rows) >= lo_min16, sel0, seln)
            bias_ref[pl.ds(off, rows), :] = jnp.where(hi > t_hi16, sel0,
                                                      jnp.where(hi == t_hi16, inner, seln))
            return carry
        for_key_tiles(write_rows, 0)

    def write_with_ties():
        cut16 = jnp.where(sentinel, 0, tie_cutoff()).astype(i16)

        def write_rows(off, rows, carry):
            hi = rows_at(hi_ref, off, rows)
            lo = rows_at(lo_ref, off, rows)
            tie = jnp.where(pos16(off, rows) < cut16, sel0, seln)
            inner = jnp.where(lo > t_lo16, sel0, jnp.where(lo == t_lo16, tie, seln))
            bias_ref[pl.ds(off, rows), :] = jnp.where(hi > t_hi16, sel0,
                                                      jnp.where(hi == t_hi16, inner, seln))
            return carry
        for_key_tiles(write_rows, 0)

    lax.cond(any_ties, write_with_ties, write_plain)

    def fill_tile(j, carry):
        off = pl.multiple_of(j * tk, tk)
        bias_ref[pl.ds(off, tk), :] = jnp.full((tk, tq), NEG, BF16)
        return carry

    lax.fori_loop(n_tiles, seq // tk, fill_tile, 0)


def _select(qi, ki, wit):
    s = qi.shape[0]
    assert s % TK_SEL == 0 and s % TQ_SEL == 0 and 2 * TOPK <= s <= 32767
    return pl.pallas_call(
        functools.partial(_select_kernel, seq=s),
        grid=(s // TQ_SEL,),
        in_specs=[pl.BlockSpec((TQ_SEL, N_HEADS * LANE), lambda i: (i, 0)),
                  pl.BlockSpec((s, LANE), lambda i: (0, 0)),
                  pl.BlockSpec((BF16_ROWS, TQ_SEL), lambda i: (0, i))],
        out_specs=pl.BlockSpec((s, TQ_SEL), lambda i: (0, i)),
        out_shape=jax.ShapeDtypeStruct((s, s), BF16),
        scratch_shapes=[pltpu.VMEM((s, TQ_SEL), jnp.int16), pltpu.VMEM((s, TQ_SEL), jnp.int16)]
        + [pltpu.VMEM((TK_SEL, TQ_SEL), jnp.int16)] * 4 + [pltpu.VMEM((s // TK_SEL, TQ_SEL), F32)],
        compiler_params=_params(),
        name="dsa_select",
    )(qi, ki, wit)


def _flash_kernel(qi_of, kj_of, *refs, alibi, use_bias):
    if use_bias:
        q_ref, k_ref, vt_ref, bias_ref, o_ref, m_ref, acc_ref = refs
    else:
        q_ref, k_ref, vt_ref, o_ref, m_ref, acc_ref = refs
    tq, tk = TQ_ATT, TK_ATT
    p = pl.program_id(0)
    qi = qi_of[p]
    kj = kj_of[p]
    last = ((qi + 1) * tq - 1) // tk

    @pl.when(kj == 0)
    def _():
        m_ref[...] = jnp.full_like(m_ref, NEG)
        acc_ref[...] = jnp.zeros_like(acc_ref)

    def step(diagonal):
        base = bias_ref[...].astype(F32) if use_bias else None
        if diagonal:
            tpos = qi * tq + lax.broadcasted_iota(jnp.int32, (tk, tq), 1)
            spos = kj * tk + lax.broadcasted_iota(jnp.int32, (tk, tq), 0)
            if not use_bias:
                base = jnp.where(spos < (tpos // CHUNK + 1) * CHUNK, 0.0, NEG)
            if alibi:
                ahead = jnp.maximum(spos - tpos, 0).astype(F32)

        def logits(h):
            s = _dot_nt(k_ref[:, h * LANE:(h + 1) * LANE], q_ref[:, h * LANE:(h + 1) * LANE])
            if diagonal and alibi:
                return s + (base - (2.0 * ALIBI_SLOPES[h] * LOG2E) * ahead)
            return s if base is None else s + base

        pending = [logits(h) for h in range(QK_AHEAD)]
        for h in range(N_HEADS):
            rows = slice(h * V_ROWS, (h + 1) * V_ROWS)
            if h + QK_AHEAD < N_HEADS:
                pending.append(logits(h + QK_AHEAD))
            s = pending.pop(0)
            m_prev = m_ref[h:h + 1, :]
            m_new = jnp.maximum(m_prev, jnp.max(s, axis=0, keepdims=True))
            alpha = jnp.exp2(m_prev - m_new)
            pexp = jnp.exp2(s - m_new)
            acc_ref[rows, :] = alpha * acc_ref[rows, :] + _dot(vt_ref[rows, :], pexp.astype(BF16))
            m_ref[h:h + 1, :] = m_new

    @pl.when(kj != last)
    def _():
        step(False)

    @pl.when(kj == last)
    def _():
        step(True)
        outs = [acc_ref[h * V_ROWS:h * V_ROWS + D_V, :] / acc_ref[h * V_ROWS + D_V:h * V_ROWS + D_V + 1, :]
                for h in range(N_HEADS)]
        o_ref[...] = jnp.concatenate(outs, axis=0).T.astype(BF16)


def _flash(q, k, vt, bias, *, alibi):
    s = q.shape[0]
    nq = s // TQ_ATT
    pairs = [(i, j) for i in range(nq) for j in range(((i + 1) * TQ_ATT - 1) // TK_ATT + 1)]
    qi_of = jnp.asarray(np.array([a for a, _ in pairs], np.int32))
    kj_of = jnp.asarray(np.array([b for _, b in pairs], np.int32))
    use_bias = bias is not None
    in_specs = [pl.BlockSpec((TQ_ATT, N_HEADS * LANE), lambda p, qi, kj: (qi[p], 0)),
                pl.BlockSpec((TK_ATT, k.shape[1]), lambda p, qi, kj: (kj[p], 0)),
                pl.BlockSpec((N_HEADS * V_ROWS, TK_ATT), lambda p, qi, kj: (0, kj[p]))]
    args = [q, k, vt]
    if use_bias:
        in_specs.append(pl.BlockSpec((TK_ATT, TQ_ATT), lambda p, qi, kj: (kj[p], qi[p])))
        args.append(bias)
    kern = functools.partial(_flash_kernel, alibi=alibi, use_bias=use_bias)
    return pl.pallas_call(
        kern,
        grid_spec=pltpu.PrefetchScalarGridSpec(
            num_scalar_prefetch=2,
            grid=(len(pairs),),
            in_specs=in_specs,
            out_specs=pl.BlockSpec((TQ_ATT, N_HEADS * D_V), lambda p, qi, kj: (qi[p], 0)),
            scratch_shapes=[pltpu.VMEM((N_HEADS, TQ_ATT), F32),
                            pltpu.VMEM((N_HEADS * V_ROWS, TQ_ATT), F32)]),
        out_shape=jax.ShapeDtypeStruct((s, N_HEADS * D_V), BF16),
        compiler_params=_params(),
        name="flash_dsa" if use_bias else "flash_mla",
    )(qi_of, kj_of, *args)


def _mix_ffn_kernel(ya_ref, yb_ref, x_ref, mod_ref, wo_ref, gm_ref, bm_ref,
                    wup_ref, cw_ref, wdn_ref, gf_ref, bf_ref, o_ref, carry_ref, act_ref):
    d = D_MODEL

    @pl.when(pl.program_id(0) == 0)
    def _():
        carry_ref[...] = jnp.zeros_like(carry_ref)

    na = ya_ref.shape[1]
    y = _dot(ya_ref[...], wo_ref[0:na, :]) + _dot(yb_ref[...], wo_ref[na:, :])
    x = _layer_norm(DN_ALPHA * x_ref[...] + (1.0 + mod_ref[:, 2 * d:3 * d]) * y, gm_ref[...], bm_ref[...])
    h = (x * (1.0 + mod_ref[:, 4 * d:5 * d]) + mod_ref[:, 3 * d:4 * d]).astype(BF16)
    for c in range(D_FF // FFN_CHUNK):
        va = c * FFN_CHUNK
        ga = D_FF + va
        uv = _dot(h, wup_ref[:, va:va + FFN_CHUNK])
        ug = _dot(h, wup_ref[:, ga:ga + FFN_CHUNK])
        val = _causal_conv3(uv, carry_ref[:, va:va + FFN_CHUNK], cw_ref[:, va:va + FFN_CHUNK])
        gate = _causal_conv3(ug, carry_ref[:, ga:ga + FFN_CHUNK], cw_ref[:, ga:ga + FFN_CHUNK])
        carry_ref[:, va:va + FFN_CHUNK] = uv[TM - SUBLANE:TM]
        carry_ref[:, ga:ga + FFN_CHUNK] = ug[TM - SUBLANE:TM]
        act_ref[:, va:va + FFN_CHUNK] = (gate * jax.nn.sigmoid(gate) * val).astype(BF16)
    y = _dot(act_ref[...], wdn_ref[...])
    z = DN_ALPHA * x + (1.0 + mod_ref[:, 5 * d:6 * d]) * y
    o_ref[...] = _layer_norm(z, gf_ref[...], bf_ref[...])


def _mix_ffn(ya, yb, x, mod, w_out, g_mix, b_mix, w_up, conv_w, w_down, g_ffn, b_ffn):
    s, d = x.shape
    assert D_FF % FFN_CHUNK == 0
    row = lambda n: pl.BlockSpec((TM, n), lambda i: (i, 0))
    full = lambda a: pl.BlockSpec(a.shape, lambda i: (0,) * a.ndim)
    once = lambda a: pl.BlockSpec(a.shape, lambda i: (0,) * a.ndim, pipeline_mode=pl.Buffered(1))
    vec = lambda a: a.reshape(1, d)
    args = [ya, yb, x, mod, w_out, vec(g_mix), vec(b_mix), w_up, conv_w, w_down, vec(g_ffn), vec(b_ffn)]
    specs = [row(ya.shape[1]), row(yb.shape[1]), row(d), full(mod), once(w_out), full(args[5]), full(args[6]),
             once(w_up), full(conv_w), once(w_down), full(args[10]), full(args[11])]
    return pl.pallas_call(
        _mix_ffn_kernel,
        grid=(s // TM,),
        in_specs=specs,
        out_specs=row(d),
        out_shape=jax.ShapeDtypeStruct((s, d), F32),
        scratch_shapes=[pltpu.VMEM((SUBLANE, 2 * D_FF), F32), pltpu.VMEM((TM, D_FF), BF16)],
        compiler_params=_params(),
        name="mix_ffn",
    )(*args)


_O_U = (0, 512)
_O_QLAT = (512, 896)
_O_KVLAT = (896, 1152)
_O_KR = (1152, 1280)
_O_KR_ROT = (1280, 1408)
_O_COLS = 1408


def _rotate_half_cols(w):
    half = w.shape[-1] // 2
    return jnp.concatenate([-w[..., half:], w[..., :half]], axis=-1)


def _odd_weights(w_in, w_uq, w_ukv):
    d = w_in.shape[0]
    o = C_WIDTH + Q_LORA + KV_LORA
    kr = w_in[:, o:o + D_ROPE]
    zl = jnp.zeros((d, D_NOPE), w_in.dtype)
    zr = jnp.zeros((d, LANE - D_NOPE - D_ROPE), w_in.dtype)
    w1 = jnp.concatenate([w_in[:, 0:o], zl, kr, zr, zl, _rotate_half_cols(kr), zr], axis=1)
    assert w1.shape[1] == _O_COLS
    dq = D_NOPE + D_ROPE
    uq = w_uq.reshape(Q_LORA, N_HEADS, dq)
    padq = jnp.zeros((Q_LORA, N_HEADS, LANE - dq), w_uq.dtype)
    wq1 = jnp.concatenate([uq, padq], axis=2).reshape(Q_LORA, N_HEADS * LANE)
    wq2 = jnp.concatenate([jnp.zeros((Q_LORA, N_HEADS, D_NOPE), w_uq.dtype),
                           _rotate_half_cols(uq[:, :, D_NOPE:]), padq], axis=2
                          ).reshape(Q_LORA, N_HEADS * LANE)
    ukv = w_ukv.reshape(KV_LORA, N_HEADS, D_NOPE + D_V)
    wk = jnp.concatenate([ukv[:, :, :D_NOPE], jnp.zeros((KV_LORA, N_HEADS, LANE - D_NOPE), w_ukv.dtype)],
                         axis=2).reshape(KV_LORA, N_HEADS * LANE)
    wvt = ukv[:, :, D_NOPE:].reshape(KV_LORA, N_HEADS * D_V).T
    return w1.astype(BF16), wq1.astype(BF16), wq2.astype(BF16), wk.astype(BF16), wvt.astype(BF16)


def _rope_lane_tables(seq):
    inv = ROPE_BASE ** (-jnp.arange(0, D_ROPE, 2, dtype=F32) / D_ROPE)
    ang = jnp.arange(seq, dtype=F32)[:, None] * inv[None, :]
    cos, sin = jnp.cos(ang), jnp.sin(ang)
    pad = jnp.zeros((seq, LANE - D_NOPE - D_ROPE), F32)
    cos_t = jnp.concatenate([jnp.ones((seq, D_NOPE), F32), cos, cos, pad], axis=1)
    sin_t = jnp.concatenate([jnp.zeros((seq, D_NOPE), F32), sin, sin, pad], axis=1)
    return cos_t, sin_t


def _odd_in_kernel(x_ref, mod_ref, w1_ref, wq1_ref, wq2_ref, wk_ref, wvt_ref, pw_ref, ps_ref,
                   qg_ref, kvg_ref, cos_ref, sin_ref,
                   yc_ref, q_ref, k_ref, vt_ref, carry_ref):
    d = D_MODEL
    i = pl.program_id(0)

    @pl.when(i == 0)
    def _():
        carry_ref[...] = jnp.zeros_like(carry_ref)

    h = (x_ref[...] * (1.0 + mod_ref[:, d:2 * d]) + mod_ref[:, 0:d]).astype(BF16)

    def proj(cols):
        return _dot(h, w1_ref[:, cols[0]:cols[1]])

    cos = cos_ref[...]
    sin = sin_ref[...]

    u = proj(_O_U)
    ext = jnp.concatenate([carry_ref[...], u], axis=0)
    pos = i * TM + lax.broadcasted_iota(jnp.int32, (TM, 1), 0)
    for g, win in enumerate(POOL_WINDOWS):
        sl = slice(g * C_GROUP_DIM, (g + 1) * C_GROUP_DIM)
        acc = ext[:, sl]
        shift = 1
        while shift < win:
            acc = acc + pltpu.roll(acc, shift, 0)
            shift *= 2
        cnt = jnp.minimum(pos + 1, win).astype(F32)
        pooled = acc[POOL_HALO:] / cnt - u[:, sl]
        mixed = _dot(pooled.astype(BF16), pw_ref[g])
        yc_ref[:, sl] = (mixed * ps_ref[:, sl]).astype(BF16)
    carry_ref[...] = u[TM - POOL_HALO:TM]

    r = _rms_norm(proj(_O_QLAT), qg_ref[...]).astype(BF16)
    qa = _dot(r, wq1_ref[...])
    qb = _dot(r, wq2_ref[...])
    kr = proj(_O_KR) * cos + proj(_O_KR_ROT) * sin
    rk = _rms_norm(proj(_O_KVLAT), kvg_ref[...]).astype(BF16)
    kn = _dot(rk, wk_ref[...])
    for hd in range(N_HEADS):
        sl = slice(hd * LANE, (hd + 1) * LANE)
        q_ref[:, sl] = ((qa[:, sl] * cos + qb[:, sl] * sin) * MLA_Q_SCALE).astype(BF16)
        k_ref[:, sl] = (kn[:, sl] + kr).astype(BF16)
    _store_values_t(vt_ref, _dot_nt(wvt_ref[...], rk))


def _odd_in(x, mod, w1, wq1, wq2, wk, wvt, pool_w, pool_scale, q_norm_g, kv_norm_g, cos_t, sin_t):
    s, d = x.shape
    row = lambda n: pl.BlockSpec((TM, n), lambda i: (i, 0))
    col = lambda n: pl.BlockSpec((n, TM), lambda i: (0, i))
    full = lambda a: pl.BlockSpec(a.shape, lambda i: (0,) * a.ndim)
    consts = [w1, wq1, wq2, wk, wvt, pool_w.astype(BF16), pool_scale.reshape(1, C_WIDTH),
              q_norm_g.reshape(1, Q_LORA), kv_norm_g.reshape(1, KV_LORA)]
    rows_out = lambda n, t: (row(n), jax.ShapeDtypeStruct((s, n), t))
    cols_out = lambda n, t: (col(n), jax.ShapeDtypeStruct((n, s), t))
    outs = [rows_out(C_WIDTH, BF16), rows_out(N_HEADS * LANE, BF16), rows_out(N_HEADS * LANE, BF16),
            cols_out(N_HEADS * V_ROWS, BF16)]
    return pl.pallas_call(
        _odd_in_kernel,
        grid=(s // TM,),
        in_specs=[row(d), full(mod)] + [full(a) for a in consts] + [row(LANE), row(LANE)],
        out_specs=[o[0] for o in outs],
        out_shape=[o[1] for o in outs],
        scratch_shapes=[pltpu.VMEM((POOL_HALO, C_WIDTH), F32)],
        compiler_params=_params(),
        name="odd_in",
    )(x, mod, *consts, cos_t, sin_t)


def kernel(x, c, ada_w, ada_b, ln_mix_g, ln_mix_b, ln_ffn_g, ln_ffn_b, ev_w_in, ev_conv_w, ev_w_out,
           od_w_in, pool_w, pool_scale, q_norm_g, w_uq, kv_norm_g, w_ukv, od_w_out,
           ffn_w_up, ffn_conv_w, ffn_w_down):
    bsz, seq, d = x.shape
    assert bsz == 1 and d == D_MODEL and seq % TM == 0 and seq % TK_ATT == 0
    assert SHORT_CONV == 3 and FFN_CONV == 3
    xs = x.reshape(seq, d)
    mods = _adaln(c, ada_w, ada_b)
    for l in range(DEPTH):
        mod = mods[l]
        if l % 2 == 0:
            e = l // 2
            w, wt = _even_weights(ev_w_in[e])
            q, k, vt, qi, ki, wit, yb = _even_in(xs, mod, w, wt, ev_conv_w[e], _alibi_q_features())
            bias = _select(qi, ki, wit)
            mixed = (_flash(q, k, vt, bias, alibi=True), yb)
            w_out = ev_w_out[e]
        else:
            o = l // 2
            cos_t, sin_t = _rope_lane_tables(seq)
            ws = _odd_weights(od_w_in[o], w_uq[o], w_ukv[o])
            yc, q, k, vt = _odd_in(xs, mod, *ws, pool_w[o], pool_scale[o], q_norm_g[o], kv_norm_g[o],
                                   cos_t, sin_t)
            mixed = (yc, _flash(q, k, vt, None, alibi=False))
            w_out = od_w_out[o]
        xs = _mix_ffn(*mixed, xs, mod, w_out.astype(BF16), ln_mix_g[l], ln_mix_b[l],
                      ffn_w_up[l].astype(BF16), ffn_conv_w[l], ffn_w_down[l].astype(BF16),
                      ln_ffn_g[l], ln_ffn_b[l])
    return xs.reshape(bsz, seq, d)
```

```python
import functools

import numpy as np
import jax
import jax.numpy as jnp
from jax import lax
from jax.experimental import pallas as pl
from jax.experimental.pallas import tpu as pltpu

D_MODEL = 1024
DEPTH = 2
CHUNK = 64
N_HEADS = 8
A_HEAD_DIM = 64
A_WIDTH = N_HEADS * A_HEAD_DIM
IDX_DIM = 64
TOPK = 256
B_WIDTH = 512
SHORT_CONV = 3
C_WIDTH = 512
POOL_WINDOWS = (2, 4, 8, 16)
C_GROUP_DIM = C_WIDTH // len(POOL_WINDOWS)
D_NOPE = 64
D_ROPE = 32
D_V = 64
Q_LORA = 384
KV_LORA = 256
ROPE_BASE = 10000.0
D_FF = 2816
FFN_CONV = 3
LN_EPS = 1e-5
RMS_EPS = 1e-6
DN_ALPHA = (2 * DEPTH) ** 0.25
NEG = -1e30
IDX_W_SCALE = N_HEADS ** -0.5 * IDX_DIM ** -0.5
LOG2E = 1.4426950408889634
DSA_Q_SCALE = A_HEAD_DIM ** -0.5 * LOG2E
MLA_Q_SCALE = (D_NOPE + D_ROPE) ** -0.5 * LOG2E
ALIBI_SLOPES = tuple(2.0 ** (-8.0 * (i + 1) / N_HEADS) for i in range(N_HEADS))

LANE = 128
SUBLANE = 8
BF16_ROWS = 16
V_ROWS = D_V + BF16_ROWS
VMEM_LIMIT_BYTES = 56 * 1024 * 1024

TM = 512
FFN_CHUNK = 256
TQ_SEL = 256
TQ_ATT = 512
TK_SEL = 512
SCORE_GROUP = 4
TK_ATT = 1024
QK_AHEAD = 4
POOL_HALO = 16

INT_MIN = -2 ** 31
F32 = jnp.float32
BF16 = jnp.bfloat16


def _params():
    return pltpu.CompilerParams(dimension_semantics=("arbitrary",),
                                vmem_limit_bytes=VMEM_LIMIT_BYTES)


def _dot(a, b):
    return jnp.dot(a, b, preferred_element_type=F32)


def _dot_nt(a, b):
    return lax.dot_general(a, b, (((1,), (1,)), ((), ())), preferred_element_type=F32)


def _layer_norm(z, g, b):
    mu = jnp.mean(z, axis=-1, keepdims=True)
    zc = z - mu
    var = jnp.mean(zc * zc, axis=-1, keepdims=True)
    return zc * lax.rsqrt(var + LN_EPS) * g + b


def _rms_norm(z, g):
    return z * lax.rsqrt(jnp.mean(z * z, axis=-1, keepdims=True) + RMS_EPS) * g


def _causal_conv3(u, prev, w):
    w0, w1, w2 = w[0:1], w[1:2], w[2:3]
    y = pltpu.roll(u, 2, 0) * w0 + pltpu.roll(u, 1, 0) * w1 + u * w2
    head = u[0:SUBLANE]
    r = lax.broadcasted_iota(jnp.int32, (SUBLANE, 1), 0)
    h1 = jnp.where(r == 0, prev[7:8], pltpu.roll(head, 1, 0))
    h2 = jnp.where(r == 0, prev[6:7], jnp.where(r == 1, prev[7:8], pltpu.roll(head, 2, 0)))
    yh = h2 * w0 + h1 * w1 + head * w2
    return jnp.concatenate([yh, y[SUBLANE:]], axis=0)


def _fold_rows(c, rows_out):
    rows = c.shape[0]
    while rows > rows_out:
        rows //= 2
        c = c[0:rows] + c[rows:2 * rows]
    return c


def _store_values_t(vt_ref, vt):
    t = vt.shape[1]
    ones_row = jnp.where(lax.broadcasted_iota(jnp.int32, (BF16_ROWS, t), 0) == 0, 1.0, 0.0)
    for h in range(N_HEADS):
        vt_ref[h * V_ROWS:h * V_ROWS + D_V, :] = vt[h * D_V:(h + 1) * D_V, :].astype(BF16)
        vt_ref[h * V_ROWS + D_V:(h + 1) * V_ROWS, :] = ones_row.astype(BF16)


def _adaln_kernel(c_ref, w_ref, b_ref, o_ref):
    c = c_ref[...]
    cond = c * jax.nn.sigmoid(c)
    o_ref[0] = jnp.sum(cond * w_ref[0], axis=0, keepdims=True) + b_ref[0]


def _adaln(c, ada_w, ada_b):
    depth, d, n = ada_w.shape
    tn = 1536
    return pl.pallas_call(
        _adaln_kernel,
        grid=(depth, n // tn),
        in_specs=[pl.BlockSpec((d, 1), lambda l, j: (0, 0)),
                  pl.BlockSpec((1, d, tn), lambda l, j: (l, 0, j)),
                  pl.BlockSpec((1, 1, tn), lambda l, j: (l, 0, j))],
        out_specs=pl.BlockSpec((1, 1, tn), lambda l, j: (l, 0, j)),
        out_shape=jax.ShapeDtypeStruct((depth, 1, n), F32),
        compiler_params=pltpu.CompilerParams(dimension_semantics=("arbitrary", "arbitrary"),
                                             vmem_limit_bytes=VMEM_LIMIT_BYTES),
        name="adaln",
    )(c.reshape(d, 1), ada_w, ada_b.reshape(depth, 1, n))


_E_Q = (0, 1024)
_E_K = (1024, 2048)
_E_QI = (2048, 3072)
_E_KI = (3072, 3200)
_E_BG = (3200, 3712)
_E_CG = (3712, 4224)
_E_XB = (4224, 4736)
_E_COLS = 4736
_ET_V = (0, 512)
_ET_WI = (512, 528)
_ET_ROWS = 528
_FEAT_LANE = A_HEAD_DIM
_N_LOG2E_TERMS = 3
_POS_RADIX = 128


def _pad_heads(w):
    d = w.shape[0]
    w3 = w.reshape(d, N_HEADS, A_HEAD_DIM)
    return jnp.concatenate([w3, jnp.zeros_like(w3)], axis=2).reshape(d, N_HEADS * LANE)


def _alibi_q_features():
    terms, rest = [], np.float64(LOG2E)
    for _ in range(_N_LOG2E_TERMS):
        t = np.float64(np.asarray(rest, np.float32).astype(jnp.bfloat16).astype(np.float32))
        terms.append(t)
        rest = rest - t
    row = np.zeros((N_HEADS, LANE), np.float32)
    for h in range(N_HEADS):
        for n, t in enumerate(terms):
            row[h, _FEAT_LANE + n] = ALIBI_SLOPES[h] * _POS_RADIX * t
            row[h, _FEAT_LANE + _N_LOG2E_TERMS + n] = ALIBI_SLOPES[h] * t
    return jnp.asarray(row.reshape(1, N_HEADS * LANE))


def _even_weights(w_in):
    d = w_in.shape[0]
    a = A_WIDTH
    q = _pad_heads(w_in[:, 0:a])
    k = _pad_heads(w_in[:, a:2 * a])
    v = w_in[:, 2 * a:3 * a]
    qi = _pad_heads(w_in[:, 3 * a:4 * a])
    o = 4 * a
    ki = w_in[:, o:o + IDX_DIM]
    wi = w_in[:, o + IDX_DIM:o + IDX_DIM + N_HEADS]
    o = o + IDX_DIM + N_HEADS
    rest = w_in[:, o:o + 3 * B_WIDTH]
    w = jnp.concatenate([q, k, qi, ki, ki, rest], axis=1)
    assert w.shape[1] == _E_COLS
    wt = jnp.concatenate([v.T, wi.T, jnp.zeros((BF16_ROWS - N_HEADS, d), w_in.dtype)], axis=0)
    assert wt.shape[0] == _ET_ROWS
    return w.astype(BF16), wt.astype(BF16)


def _even_in_kernel(x_ref, mod_ref, w_ref, wt_ref, cw_ref, qf_ref,
                    q_ref, k_ref, vt_ref, qi_ref, ki_ref, wit_ref, yb_ref, carry_ref):
    d = D_MODEL
    i = pl.program_id(0)

    @pl.when(i == 0)
    def _():
        carry_ref[...] = jnp.zeros_like(carry_ref)

    h = (x_ref[...] * (1.0 + mod_ref[:, d:2 * d]) + mod_ref[:, 0:d]).astype(BF16)

    def proj(cols):
        return _dot(h, w_ref[:, cols[0]:cols[1]])

    def proj_t(rows):
        return _dot_nt(wt_ref[rows[0]:rows[1], :], h)

    q_ref[...] = (proj(_E_Q) * DSA_Q_SCALE + qf_ref[...]).astype(BF16)
    pos = i * TM + lax.broadcasted_iota(jnp.int32, (TM, LANE), 0)
    lane = lax.broadcasted_iota(jnp.int32, (TM, LANE), 1) - _FEAT_LANE
    pos_hi = (pos // _POS_RADIX).astype(F32)
    pos_lo = (pos % _POS_RADIX).astype(F32)
    kfeat = jnp.where(lane < 0, 0.0,
                      jnp.where(lane < _N_LOG2E_TERMS, pos_hi,
                                jnp.where(lane < 2 * _N_LOG2E_TERMS, pos_lo, 0.0)))
    kproj = proj(_E_K)
    for hd in range(N_HEADS):
        sl = slice(hd * LANE, (hd + 1) * LANE)
        k_ref[:, sl] = (kproj[:, sl] + kfeat).astype(BF16)
    _store_values_t(vt_ref, proj_t(_ET_V))
    qi_ref[...] = proj(_E_QI).astype(BF16)
    ki_ref[...] = proj(_E_KI).astype(BF16)
    wit_ref[...] = proj_t(_ET_WI) * IDX_W_SCALE
    g = proj(_E_CG) * proj(_E_XB)
    y = _causal_conv3(g, carry_ref[...], cw_ref[...])
    yb_ref[...] = (proj(_E_BG) * y).astype(BF16)
    carry_ref[...] = g[TM - SUBLANE:TM]


def _even_in(x, mod, w, wt, conv_w, qfeat):
    s, d = x.shape
    row = lambda n: pl.BlockSpec((TM, n), lambda i: (i, 0))
    col = lambda n: pl.BlockSpec((n, TM), lambda i: (0, i))
    full = lambda a: pl.BlockSpec(a.shape, lambda i: (0,) * a.ndim)
    rows_out = lambda n, t: (row(n), jax.ShapeDtypeStruct((s, n), t))
    cols_out = lambda n, t: (col(n), jax.ShapeDtypeStruct((n, s), t))
    outs = [rows_out(N_HEADS * LANE, BF16), rows_out(N_HEADS * LANE, BF16), cols_out(N_HEADS * V_ROWS, BF16),
            rows_out(N_HEADS * LANE, BF16), rows_out(LANE, BF16), cols_out(BF16_ROWS, F32),
            rows_out(B_WIDTH, BF16)]
    return pl.pallas_call(
        _even_in_kernel,
        grid=(s // TM,),
        in_specs=[row(d), full(mod), full(w), full(wt), full(conv_w), full(qfeat)],
        out_specs=[o[0] for o in outs],
        out_shape=[o[1] for o in outs],
        scratch_shapes=[pltpu.VMEM((SUBLANE, B_WIDTH), F32)],
        compiler_params=_params(),
        name="even_in",
    )(x, mod, w, wt, conv_w, qfeat)


def _select_kernel(qi_ref, ki_ref, wt_ref, bias_ref, hi_ref, lo_ref, m1_ref, m2_ref, mc_ref, my_ref, tc_ref,
                   *, seq):
    tq, tk = TQ_SEL, TK_SEL
    i16 = jnp.int16
    i = pl.program_id(0)
    start = i * tq
    n_tiles = (start + tq + tk - 1) // tk
    tpos = start + lax.broadcasted_iota(jnp.int32, (1, tq), 1)
    limit = (tpos // CHUNK + 1) * CHUNK
    int_min = jnp.int32(INT_MIN)
    dmin, dmax = -2 ** 15, 2 ** 15 - 1
    zero16, one16 = jnp.zeros((), i16), jnp.ones((), i16)
    sel0, seln = jnp.zeros((), BF16), jnp.full((), NEG, BF16)

    def key_pos(off):
        return off + lax.broadcasted_iota(jnp.int32, (tk, tq), 0)

    def score_tile(j, carry, *, masked, tiles=1):
        for t in range(tiles):
            score_rows(pl.multiple_of((j * tiles + t) * tk, tk), masked)
        return carry

    def score_rows(off, masked):
        kt = ki_ref[pl.ds(off, tk), :]
        score = jnp.zeros((tk, tq), F32)
        for h in range(N_HEADS):
            rel = _dot_nt(kt, qi_ref[:, h * LANE:(h + 1) * LANE])
            score = score + wt_ref[h:h + 1, :] * jnp.maximum(rel, 0.0)
        bits = pltpu.bitcast(score, jnp.int32)
        key = bits ^ ((bits >> 31) & jnp.int32(0x7FFFFFFF))
        key = jnp.where(bits == int_min, 0, key)
        if masked:
            key = jnp.where(key_pos(off) < limit, key, int_min)
        hi_ref[pl.ds(off, tk), :] = (key >> 16).astype(i16)
        lo_ref[pl.ds(off, tk), :] = key.astype(i16) ^ jnp.asarray(dmin, i16)

    n_before = start // tk
    n_groups = n_before // SCORE_GROUP
    lax.fori_loop(0, n_groups, functools.partial(score_tile, masked=False, tiles=SCORE_GROUP), 0)
    lax.fori_loop(SCORE_GROUP * n_groups, n_before, functools.partial(score_tile, masked=False), 0)
    lax.fori_loop(n_before, n_tiles, functools.partial(score_tile, masked=True), 0)

    def rows_at(ref, off, rows):
        return ref[pl.ds(off, rows), :]

    def for_key_tiles(fn, carry):
        return lax.fori_loop(0, n_tiles, lambda j, c: fn(pl.multiple_of(j * tk, tk), tk, c), carry)

    def per_query(c):
        return jnp.sum(c.astype(jnp.int32).astype(F32), axis=0, keepdims=True)

    def count(flag):
        def body(off, rows, acc):
            return acc + _fold_rows(flag(off, rows), BF16_ROWS)
        return per_query(for_key_tiles(body, jnp.zeros((BF16_ROWS, tq), i16)))

    def radix_search(count_ge, need):
        def bit_step(b, thr):
            cand = thr + jnp.left_shift(jnp.int32(1), 15 - b)
            return jnp.where(count_ge(cand.astype(i16)) >= need, cand, thr)
        return lax.fori_loop(0, 16, bit_step, jnp.full((1, tq), dmin, jnp.int32))

    topk = float(TOPK)
    t_hi = radix_search(lambda c16: count(
        lambda off, rows: jnp.where(rows_at(hi_ref, off, rows) >= c16, one16, zero16)), topk)
    t_hi16 = t_hi.astype(i16)
    n_above = count(lambda off, rows: jnp.where(rows_at(hi_ref, off, rows) > t_hi16, one16, zero16))
    need_lo = topk - n_above

    m1_ref[...] = jnp.full((tk, tq), dmin, i16)
    m2_ref[...] = jnp.full((tk, tq), dmin, i16)
    mc_ref[...] = jnp.zeros((tk, tq), i16)

    def fold_members(off, rows, carry):
        member = rows_at(hi_ref, off, rows) == t_hi16
        x = jnp.where(member, rows_at(lo_ref, off, rows), jnp.asarray(dmin, i16))
        lo_ref[pl.ds(off, rows), :] = x
        a, b = m1_ref[...], m2_ref[...]
        above = x > a
        m1_ref[...] = jnp.where(above, x, a)
        second = jnp.where(above, a, x)
        m2_ref[...] = jnp.where(second > b, second, b)
        mc_ref[...] = mc_ref[...] + jnp.where(member, one16, zero16)
        return carry

    for_key_tiles(fold_members, 0)

    def count_slots(flag):
        return per_query(_fold_rows(flag(m1_ref[...]) + flag(m2_ref[...]), BF16_ROWS))

    def level2_slots():
        t = radix_search(lambda c16: count_slots(lambda m: jnp.where(m >= c16, one16, zero16)), need_lo)
        t16 = t.astype(i16)
        n_members = per_query(_fold_rows(mc_ref[...], BF16_ROWS))
        gt = count_slots(lambda m: jnp.where(m > t16, one16, zero16))
        ge = jnp.where(t == dmin, n_members, count_slots(lambda m: jnp.where(m >= t16, one16, zero16)))
        return t, gt, ge

    def level2_full():
        t = radix_search(lambda c16: count(
            lambda off, rows: jnp.where(rows_at(lo_ref, off, rows) >= c16, one16, zero16)), need_lo)
        t16 = t.astype(i16)
        gt = count(lambda off, rows: jnp.where(rows_at(lo_ref, off, rows) > t16, one16, zero16))
        ge = count(lambda off, rows: jnp.where(
            rows_at(hi_ref, off, rows) == t_hi16,
            jnp.where(rows_at(lo_ref, off, rows) >= t16, one16, zero16), zero16))
        return t, gt, ge

    crowded = per_query(_fold_rows(jnp.where(mc_ref[...] > jnp.asarray(2, i16), one16, zero16), BF16_ROWS))
    t_lo, n_gt_lo, n_ge_lo = lax.cond(jnp.max(crowded) > 0.0, level2_full, level2_slots)
    t_lo16 = t_lo.astype(i16)
    n_gt = n_above + n_gt_lo
    n_ge = n_above + n_ge_lo
    room = topk - n_gt
    sentinel = t_hi == dmin
    has_ties = jnp.where(sentinel, 0.0, jnp.where(n_ge > topk, 1.0, 0.0))
    any_ties = jnp.max(has_ties) > 0.0

    def pos16(off, rows):
        return (off + lax.broadcasted_iota(jnp.int32, (rows, tq), 0)).astype(i16)

    def tie_cutoff():
        def tied_rows(j):
            off = pl.multiple_of(j * tk, tk)
            return jnp.where(rows_at(hi_ref, off, tk) == t_hi16,
                             jnp.where(rows_at(lo_ref, off, tk) == t_lo16, one16, zero16), zero16)

        def count_tile(j, carry):
            tc_ref[pl.ds(j, 1), :] = per_query(_fold_rows(tied_rows(j), BF16_ROWS))
            return carry
        lax.fori_loop(0, n_tiles, count_tile, 0)

        def find_tile(j, carry):
            seen, tile_of, left = carry
            here = tc_ref[pl.ds(j, 1), :]
            crossing = jnp.logical_and(seen < room, seen + here >= room)
            return (seen + here, jnp.where(crossing, j, tile_of), jnp.where(crossing, room - seen, left))
        _, tile_of, left = lax.fori_loop(
            0, n_tiles, find_tile, (jnp.zeros((1, tq), F32), jnp.zeros((1, tq), jnp.int32), room))

        my_ref[...] = jnp.zeros((tk, tq), i16)
        tile_of16 = tile_of.astype(i16)

        def gather_tile(j, carry):
            mine = jnp.where(tile_of16 == j.astype(i16), jnp.asarray(-1, i16), zero16)
            my_ref[...] = my_ref[...] | (tied_rows(j) & mine)
            return carry
        lax.fori_loop(0, n_tiles, gather_tile, 0)

        row16 = lax.broadcasted_iota(jnp.int32, (tk, tq), 0).astype(i16)
        row_bits = tk.bit_length()

        def row_step(b, cut):
            cand = cut + jnp.left_shift(jnp.int32(1), row_bits - 1 - b)
            below = jnp.where(row16 < cand.astype(i16), my_ref[...], zero16)
            return jnp.where(per_query(_fold_rows(below, BF16_ROWS)) <= left, cand, cut)
        row_cut = lax.fori_loop(0, row_bits, row_step, jnp.zeros((1, tq), jnp.int32))
        return tile_of * tk + jnp.minimum(row_cut, tk)

    def write_plain():
        lo_min16 = jnp.where(sentinel, dmax, t_lo).astype(i16)

        def write_rows(off, rows, carry):
            hi = rows_at(hi_ref, off, rows)
            inner = jnp.where(rows_at(lo_ref, off, rows) >= lo_min16, sel0, seln)
            bias_ref[pl.ds(off, rows), :] = jnp.where(hi > t_hi16, sel0,
                                                      jnp.where(hi == t_hi16, inner, seln))
            return carry
        for_key_tiles(write_rows, 0)

    def write_with_ties():
        cut16 = jnp.where(sentinel, 0, tie_cutoff()).astype(i16)

        def write_rows(off, rows, carry):
            hi = rows_at(hi_ref, off, rows)
            lo = rows_at(lo_ref, off, rows)
            tie = jnp.where(pos16(off, rows) < cut16, sel0, seln)
            inner = jnp.where(lo > t_lo16, sel0, jnp.where(lo == t_lo16, tie, seln))
            bias_ref[pl.ds(off, rows), :] = jnp.where(hi > t_hi16, sel0,
                                                      jnp.where(hi == t_hi16, inner, seln))
            return carry
        for_key_tiles(write_rows, 0)

    lax.cond(any_ties, write_with_ties, write_plain)

    def fill_tile(j, carry):
        off = pl.multiple_of(j * tk, tk)
        bias_ref[pl.ds(off, tk), :] = jnp.full((tk, tq), NEG, BF16)
        return carry

    lax.fori_loop(n_tiles, seq // tk, fill_tile, 0)


def _select(qi, ki, wit):
    s = qi.shape[0]
    assert s % TK_SEL == 0 and s % TQ_SEL == 0 and 2 * TOPK <= s <= 32767
    return pl.pallas_call(
        functools.partial(_select_kernel, seq=s),
        grid=(s // TQ_SEL,),
        in_specs=[pl.BlockSpec((TQ_SEL, N_HEADS * LANE), lambda i: (i, 0)),
                  pl.BlockSpec((s, LANE), lambda i: (0, 0)),
                  pl.BlockSpec((BF16_ROWS, TQ_SEL), lambda i: (0, i))],
        out_specs=pl.BlockSpec((s, TQ_SEL), lambda i: (0, i)),
        out_shape=jax.ShapeDtypeStruct((s, s), BF16),
        scratch_shapes=[pltpu.VMEM((s, TQ_SEL), jnp.int16), pltpu.VMEM((s, TQ_SEL), jnp.int16)]
        + [pltpu.VMEM((TK_SEL, TQ_SEL), jnp.int16)] * 4 + [pltpu.VMEM((s // TK_SEL, TQ_SEL), F32)],
        compiler_params=_params(),
        name="dsa_select",
    )(qi, ki, wit)


def _flash_kernel(qi_of, kj_of, *refs, alibi, use_bias):
    if use_bias:
        q_ref, k_ref, vt_ref, bias_ref, o_ref, m_ref, acc_ref = refs
    else:
        q_ref, k_ref, vt_ref, o_ref, m_ref, acc_ref = refs
    tq, tk = TQ_ATT, TK_ATT
    p = pl.program_id(0)
    qi = qi_of[p]
    kj = kj_of[p]
    last = ((qi + 1) * tq - 1) // tk

    @pl.when(kj == 0)
    def _():
        m_ref[...] = jnp.full_like(m_ref, NEG)
        acc_ref[...] = jnp.zeros_like(acc_ref)

    def step(diagonal):
        base = bias_ref[...].astype(F32) if use_bias else None
        if diagonal:
            tpos = qi * tq + lax.broadcasted_iota(jnp.int32, (tk, tq), 1)
            spos = kj * tk + lax.broadcasted_iota(jnp.int32, (tk, tq), 0)
            if not use_bias:
                base = jnp.where(spos < (tpos // CHUNK + 1) * CHUNK, 0.0, NEG)
            if alibi:
                ahead = jnp.maximum(spos - tpos, 0).astype(F32)

        def logits(h):
            s = _dot_nt(k_ref[:, h * LANE:(h + 1) * LANE], q_ref[:, h * LANE:(h + 1) * LANE])
            if diagonal and alibi:
                return s + (base - (2.0 * ALIBI_SLOPES[h] * LOG2E) * ahead)
            return s if base is None else s + base

        pending = [logits(h) for h in range(QK_AHEAD)]
        for h in range(N_HEADS):
            rows = slice(h * V_ROWS, (h + 1) * V_ROWS)
            if h + QK_AHEAD < N_HEADS:
                pending.append(logits(h + QK_AHEAD))
            s = pending.pop(0)
            m_prev = m_ref[h:h + 1, :]
            m_new = jnp.maximum(m_prev, jnp.max(s, axis=0, keepdims=True))
            alpha = jnp.exp2(m_prev - m_new)
            pexp = jnp.exp2(s - m_new)
            acc_ref[rows, :] = alpha * acc_ref[rows, :] + _dot(vt_ref[rows, :], pexp.astype(BF16))
            m_ref[h:h + 1, :] = m_new

    @pl.when(kj != last)
    def _():
        step(False)

    @pl.when(kj == last)
    def _():
        step(True)
        outs = [acc_ref[h * V_ROWS:h * V_ROWS + D_V, :] / acc_ref[h * V_ROWS + D_V:h * V_ROWS + D_V + 1, :]
                for h in range(N_HEADS)]
        o_ref[...] = jnp.concatenate(outs, axis=0).T.astype(BF16)


def _flash(q, k, vt, bias, *, alibi):
    s = q.shape[0]
    nq = s // TQ_ATT
    pairs = [(i, j) for i in range(nq) for j in range(((i + 1) * TQ_ATT - 1) // TK_ATT + 1)]
    qi_of = jnp.asarray(np.array([a for a, _ in pairs], np.int32))
    kj_of = jnp.asarray(np.array([b for _, b in pairs], np.int32))
    use_bias = bias is not None
    in_specs = [pl.BlockSpec((TQ_ATT, N_HEADS * LANE), lambda p, qi, kj: (qi[p], 0)),
                pl.BlockSpec((TK_ATT, k.shape[1]), lambda p, qi, kj: (kj[p], 0)),
                pl.BlockSpec((N_HEADS * V_ROWS, TK_ATT), lambda p, qi, kj: (0, kj[p]))]
    args = [q, k, vt]
    if use_bias:
        in_specs.append(pl.BlockSpec((TK_ATT, TQ_ATT), lambda p, qi, kj: (kj[p], qi[p])))
        args.append(bias)
    kern = functools.partial(_flash_kernel, alibi=alibi, use_bias=use_bias)
    return pl.pallas_call(
        kern,
        grid_spec=pltpu.PrefetchScalarGridSpec(
            num_scalar_prefetch=2,
            grid=(len(pairs),),
            in_specs=in_specs,
            out_specs=pl.BlockSpec((TQ_ATT, N_HEADS * D_V), lambda p, qi, kj: (qi[p], 0)),
            scratch_shapes=[pltpu.VMEM((N_HEADS, TQ_ATT), F32),
                            pltpu.VMEM((N_HEADS * V_ROWS, TQ_ATT), F32)]),
        out_shape=jax.ShapeDtypeStruct((s, N_HEADS * D_V), BF16),
        compiler_params=_params(),
        name="flash_dsa" if use_bias else "flash_mla",
    )(qi_of, kj_of, *args)


def _mix_ffn_kernel(ya_ref, yb_ref, x_ref, mod_ref, wo_ref, gm_ref, bm_ref,
                    wup_ref, cw_ref, wdn_ref, gf_ref, bf_ref, o_ref, carry_ref, act_ref):
    d = D_MODEL

    @pl.when(pl.program_id(0) == 0)
    def _():
        carry_ref[...] = jnp.zeros_like(carry_ref)

    na = ya_ref.shape[1]
    y = _dot(ya_ref[...], wo_ref[0:na, :]) + _dot(yb_ref[...], wo_ref[na:, :])
    x = _layer_norm(DN_ALPHA * x_ref[...] + (1.0 + mod_ref[:, 2 * d:3 * d]) * y, gm_ref[...], bm_ref[...])
    h = (x * (1.0 + mod_ref[:, 4 * d:5 * d]) + mod_ref[:, 3 * d:4 * d]).astype(BF16)
    for c in range(D_FF // FFN_CHUNK):
        va = c * FFN_CHUNK
        ga = D_FF + va
        uv = _dot(h, wup_ref[:, va:va + FFN_CHUNK])
        ug = _dot(h, wup_ref[:, ga:ga + FFN_CHUNK])
        val = _causal_conv3(uv, carry_ref[:, va:va + FFN_CHUNK], cw_ref[:, va:va + FFN_CHUNK])
        gate = _causal_conv3(ug, carry_ref[:, ga:ga + FFN_CHUNK], cw_ref[:, ga:ga + FFN_CHUNK])
        carry_ref[:, va:va + FFN_CHUNK] = uv[TM - SUBLANE:TM]
        carry_ref[:, ga:ga + FFN_CHUNK] = ug[TM - SUBLANE:TM]
        act_ref[:, va:va + FFN_CHUNK] = (gate * jax.nn.sigmoid(gate) * val).astype(BF16)
    y = _dot(act_ref[...], wdn_ref[...])
    z = DN_ALPHA * x + (1.0 + mod_ref[:, 5 * d:6 * d]) * y
    o_ref[...] = _layer_norm(z, gf_ref[...], bf_ref[...])


def _mix_ffn(ya, yb, x, mod, w_out, g_mix, b_mix, w_up, conv_w, w_down, g_ffn, b_ffn):
    s, d = x.shape
    assert D_FF % FFN_CHUNK == 0
    row = lambda n: pl.BlockSpec((TM, n), lambda i: (i, 0))
    full = lambda a: pl.BlockSpec(a.shape, lambda i: (0,) * a.ndim)
    once = lambda a: pl.BlockSpec(a.shape, lambda i: (0,) * a.ndim, pipeline_mode=pl.Buffered(1))
    vec = lambda a: a.reshape(1, d)
    args = [ya, yb, x, mod, w_out, vec(g_mix), vec(b_mix), w_up, conv_w, w_down, vec(g_ffn), vec(b_ffn)]
    specs = [row(ya.shape[1]), row(yb.shape[1]), row(d), full(mod), once(w_out), full(args[5]), full(args[6]),
             once(w_up), full(conv_w), once(w_down), full(args[10]), full(args[11])]
    return pl.pallas_call(
        _mix_ffn_kernel,
        grid=(s // TM,),
        in_specs=specs,
        out_specs=row(d),
        out_shape=jax.ShapeDtypeStruct((s, d), F32),
        scratch_shapes=[pltpu.VMEM((SUBLANE, 2 * D_FF), F32), pltpu.VMEM((TM, D_FF), BF16)],
        compiler_params=_params(),
        name="mix_ffn",
    )(*args)


_O_U = (0, 512)
_O_QLAT = (512, 896)
_O_KVLAT = (896, 1152)
_O_KR = (1152, 1280)
_O_KR_ROT = (1280, 1408)
_O_COLS = 1408


def _rotate_half_cols(w):
    half = w.shape[-1] // 2
    return jnp.concatenate([-w[..., half:], w[..., :half]], axis=-1)


def _odd_weights(w_in, w_uq, w_ukv):
    d = w_in.shape[0]
    o = C_WIDTH + Q_LORA + KV_LORA
    kr = w_in[:, o:o + D_ROPE]
    zl = jnp.zeros((d, D_NOPE), w_in.dtype)
    zr = jnp.zeros((d, LANE - D_NOPE - D_ROPE), w_in.dtype)
    w1 = jnp.concatenate([w_in[:, 0:o], zl, kr, zr, zl, _rotate_half_cols(kr), zr], axis=1)
    assert w1.shape[1] == _O_COLS
    dq = D_NOPE + D_ROPE
    uq = w_uq.reshape(Q_LORA, N_HEADS, dq)
    padq = jnp.zeros((Q_LORA, N_HEADS, LANE - dq), w_uq.dtype)
    wq1 = jnp.concatenate([uq, padq], axis=2).reshape(Q_LORA, N_HEADS * LANE)
    wq2 = jnp.concatenate([jnp.zeros((Q_LORA, N_HEADS, D_NOPE), w_uq.dtype),
                           _rotate_half_cols(uq[:, :, D_NOPE:]), padq], axis=2
                          ).reshape(Q_LORA, N_HEADS * LANE)
    ukv = w_ukv.reshape(KV_LORA, N_HEADS, D_NOPE + D_V)
    wk = jnp.concatenate([ukv[:, :, :D_NOPE], jnp.zeros((KV_LORA, N_HEADS, LANE - D_NOPE), w_ukv.dtype)],
                         axis=2).reshape(KV_LORA, N_HEADS * LANE)
    wvt = ukv[:, :, D_NOPE:].reshape(KV_LORA, N_HEADS * D_V).T
    return w1.astype(BF16), wq1.astype(BF16), wq2.astype(BF16), wk.astype(BF16), wvt.astype(BF16)


def _rope_lane_tables(seq):
    inv = ROPE_BASE ** (-jnp.arange(0, D_ROPE, 2, dtype=F32) / D_ROPE)
    ang = jnp.arange(seq, dtype=F32)[:, None] * inv[None, :]
    cos, sin = jnp.cos(ang), jnp.sin(ang)
    pad = jnp.zeros((seq, LANE - D_NOPE - D_ROPE), F32)
    cos_t = jnp.concatenate([jnp.ones((seq, D_NOPE), F32), cos, cos, pad], axis=1)
    sin_t = jnp.concatenate([jnp.zeros((seq, D_NOPE), F32), sin, sin, pad], axis=1)
    return cos_t, sin_t


def _odd_in_kernel(x_ref, mod_ref, w1_ref, wq1_ref, wq2_ref, wk_ref, wvt_ref, pw_ref, ps_ref,
                   qg_ref, kvg_ref, cos_ref, sin_ref,
                   yc_ref, q_ref, k_ref, vt_ref, carry_ref):
    d = D_MODEL
    i = pl.program_id(0)

    @pl.when(i == 0)
    def _():
        carry_ref[...] = jnp.zeros_like(carry_ref)

    h = (x_ref[...] * (1.0 + mod_ref[:, d:2 * d]) + mod_ref[:, 0:d]).astype(BF16)

    def proj(cols):
        return _dot(h, w1_ref[:, cols[0]:cols[1]])

    cos = cos_ref[...]
    sin = sin_ref[...]

    u = proj(_O_U)
    ext = jnp.concatenate([carry_ref[...], u], axis=0)
    pos = i * TM + lax.broadcasted_iota(jnp.int32, (TM, 1), 0)
    for g, win in enumerate(POOL_WINDOWS):
        sl = slice(g * C_GROUP_DIM, (g + 1) * C_GROUP_DIM)
        acc = ext[:, sl]
        shift = 1
        while shift < win:
            acc = acc + pltpu.roll(acc, shift, 0)
            shift *= 2
        cnt = jnp.minimum(pos + 1, win).astype(F32)
        pooled = acc[POOL_HALO:] / cnt - u[:, sl]
        mixed = _dot(pooled.astype(BF16), pw_ref[g])
        yc_ref[:, sl] = (mixed * ps_ref[:, sl]).astype(BF16)
    carry_ref[...] = u[TM - POOL_HALO:TM]

    r = _rms_norm(proj(_O_QLAT), qg_ref[...]).astype(BF16)
    qa = _dot(r, wq1_ref[...])
    qb = _dot(r, wq2_ref[...])
    kr = proj(_O_KR) * cos + proj(_O_KR_ROT) * sin
    rk = _rms_norm(proj(_O_KVLAT), kvg_ref[...]).astype(BF16)
    kn = _dot(rk, wk_ref[...])
    for hd in range(N_HEADS):
        sl = slice(hd * LANE, (hd + 1) * LANE)
        q_ref[:, sl] = ((qa[:, sl] * cos + qb[:, sl] * sin) * MLA_Q_SCALE).astype(BF16)
        k_ref[:, sl] = (kn[:, sl] + kr).astype(BF16)
    _store_values_t(vt_ref, _dot_nt(wvt_ref[...], rk))


def _odd_in(x, mod, w1, wq1, wq2, wk, wvt, pool_w, pool_scale, q_norm_g, kv_norm_g, cos_t, sin_t):
    s, d = x.shape
    row = lambda n: pl.BlockSpec((TM, n), lambda i: (i, 0))
    col = lambda n: pl.BlockSpec((n, TM), lambda i: (0, i))
    full = lambda a: pl.BlockSpec(a.shape, lambda i: (0,) * a.ndim)
    consts = [w1, wq1, wq2, wk, wvt, pool_w.astype(BF16), pool_scale.reshape(1, C_WIDTH),
              q_norm_g.reshape(1, Q_LORA), kv_norm_g.reshape(1, KV_LORA)]
    rows_out = lambda n, t: (row(n), jax.ShapeDtypeStruct((s, n), t))
    cols_out = lambda n, t: (col(n), jax.ShapeDtypeStruct((n, s), t))
    outs = [rows_out(C_WIDTH, BF16), rows_out(N_HEADS * LANE, BF16), rows_out(N_HEADS * LANE, BF16),
            cols_out(N_HEADS * V_ROWS, BF16)]
    return pl.pallas_call(
        _odd_in_kernel,
        grid=(s // TM,),
        in_specs=[row(d), full(mod)] + [full(a) for a in consts] + [row(LANE), row(LANE)],
        out_specs=[o[0] for o in outs],
        out_shape=[o[1] for o in outs],
        scratch_shapes=[pltpu.VMEM((POOL_HALO, C_WIDTH), F32)],
        compiler_params=_params(),
        name="odd_in",
    )(x, mod, *consts, cos_t, sin_t)


def kernel(x, c, ada_w, ada_b, ln_mix_g, ln_mix_b, ln_ffn_g, ln_ffn_b, ev_w_in, ev_conv_w, ev_w_out,
           od_w_in, pool_w, pool_scale, q_norm_g, w_uq, kv_norm_g, w_ukv, od_w_out,
           ffn_w_up, ffn_conv_w, ffn_w_down):
    bsz, seq, d = x.shape
    assert bsz == 1 and d == D_MODEL and seq % TM == 0 and seq % TK_ATT == 0 and seq % TQ_ATT == 0
    assert SHORT_CONV == 3 and FFN_CONV == 3
    xs = x.reshape(seq, d)
    mods = _adaln(c, ada_w, ada_b)
    for l in range(DEPTH):
        mod = mods[l]
        if l % 2 == 0:
            e = l // 2
            w, wt = _even_weights(ev_w_in[e])
            q, k, vt, qi, ki, wit, yb = _even_in(xs, mod, w, wt, ev_conv_w[e], _alibi_q_features())
            bias = _select(qi, ki, wit)
            mixed = (_flash(q, k, vt, bias, alibi=True), yb)
            w_out = ev_w_out[e]
        else:
            o = l // 2
            cos_t, sin_t = _rope_lane_tables(seq)
            ws = _odd_weights(od_w_in[o], w_uq[o], w_ukv[o])
            yc, q, k, vt = _odd_in(xs, mod, *ws, pool_w[o], pool_scale[o], q_norm_g[o], kv_norm_g[o],
                                   cos_t, sin_t)
            mixed = (yc, _flash(q, k, vt, None, alibi=False))
            w_out = od_w_out[o]
        xs = _mix_ffn(*mixed, xs, mod, w_out.astype(BF16), ln_mix_g[l], ln_mix_b[l],
                      ffn_w_up[l].astype(BF16), ffn_conv_w[l], ffn_w_down[l].astype(BF16),
                      ln_ffn_g[l], ln_ffn_b[l])
    return xs.reshape(bsz, seq, d)
```

```python
import functools

import numpy as np
import jax
import jax.numpy as jnp
from jax import lax
from jax.experimental import pallas as pl
from jax.experimental.pallas import tpu as pltpu

D_MODEL = 1024
DEPTH = 2
CHUNK = 64
N_HEADS = 8
A_HEAD_DIM = 64
A_WIDTH = N_HEADS * A_HEAD_DIM
IDX_DIM = 64
TOPK = 256
B_WIDTH = 512
SHORT_CONV = 3
C_WIDTH = 512
POOL_WINDOWS = (2, 4, 8, 16)
C_GROUP_DIM = C_WIDTH // len(POOL_WINDOWS)
D_NOPE = 64
D_ROPE = 32
D_V = 64
Q_LORA = 384
KV_LORA = 256
ROPE_BASE = 10000.0
D_FF = 2816
FFN_CONV = 3
LN_EPS = 1e-5
RMS_EPS = 1e-6
DN_ALPHA = (2 * DEPTH) ** 0.25
NEG = -1e30
IDX_W_SCALE = N_HEADS ** -0.5 * IDX_DIM ** -0.5
LOG2E = 1.4426950408889634
DSA_Q_SCALE = A_HEAD_DIM ** -0.5 * LOG2E
MLA_Q_SCALE = (D_NOPE + D_ROPE) ** -0.5 * LOG2E
ALIBI_SLOPES = tuple(2.0 ** (-8.0 * (i + 1) / N_HEADS) for i in range(N_HEADS))

LANE = 128
SUBLANE = 8
BF16_ROWS = 16
V_ROWS = D_V + BF16_ROWS
VMEM_LIMIT_BYTES = 56 * 1024 * 1024

TM = 512
FFN_CHUNK = 256
TQ_SEL = 256
TQ_ATT = 512
TK_SEL = 512
SCORE_GROUP = 4
TK_ATT = 1024
QK_AHEAD = 4
POOL_HALO = 16

INT_MIN = -2 ** 31
F32 = jnp.float32
BF16 = jnp.bfloat16


def _params():
    return pltpu.CompilerParams(dimension_semantics=("arbitrary",),
                                vmem_limit_bytes=VMEM_LIMIT_BYTES)


def _dot(a, b):
    return jnp.dot(a, b, preferred_element_type=F32)


def _dot_nt(a, b):
    return lax.dot_general(a, b, (((1,), (1,)), ((), ())), preferred_element_type=F32)


def _layer_norm(z, g, b):
    mu = jnp.mean(z, axis=-1, keepdims=True)
    zc = z - mu
    var = jnp.mean(zc * zc, axis=-1, keepdims=True)
    return zc * lax.rsqrt(var + LN_EPS) * g + b


def _rms_norm(z, g):
    return z * lax.rsqrt(jnp.mean(z * z, axis=-1, keepdims=True) + RMS_EPS) * g


def _causal_conv3(u, prev, w):
    w0, w1, w2 = w[0:1], w[1:2], w[2:3]
    y = pltpu.roll(u, 2, 0) * w0 + pltpu.roll(u, 1, 0) * w1 + u * w2
    head = u[0:SUBLANE]
    r = lax.broadcasted_iota(jnp.int32, (SUBLANE, 1), 0)
    h1 = jnp.where(r == 0, prev[7:8], pltpu.roll(head, 1, 0))
    h2 = jnp.where(r == 0, prev[6:7], jnp.where(r == 1, prev[7:8], pltpu.roll(head, 2, 0)))
    yh = h2 * w0 + h1 * w1 + head * w2
    return jnp.concatenate([yh, y[SUBLANE:]], axis=0)


def _fold_rows(c, rows_out):
    rows = c.shape[0]
    while rows > rows_out:
        rows //= 2
        c = c[0:rows] + c[rows:2 * rows]
    return c


def _store_values_t(vt_ref, vt):
    t = vt.shape[1]
    ones_row = jnp.where(lax.broadcasted_iota(jnp.int32, (BF16_ROWS, t), 0) == 0, 1.0, 0.0)
    for h in range(N_HEADS):
        vt_ref[h * V_ROWS:h * V_ROWS + D_V, :] = vt[h * D_V:(h + 1) * D_V, :].astype(BF16)
        vt_ref[h * V_ROWS + D_V:(h + 1) * V_ROWS, :] = ones_row.astype(BF16)


def _adaln_kernel(c_ref, w_ref, b_ref, o_ref):
    c = c_ref[...]
    cond = c * jax.nn.sigmoid(c)
    o_ref[0] = jnp.sum(cond * w_ref[0], axis=0, keepdims=True) + b_ref[0]


def _adaln(c, ada_w, ada_b):
    depth, d, n = ada_w.shape
    tn = 1536
    return pl.pallas_call(
        _adaln_kernel,
        grid=(depth, n // tn),
        in_specs=[pl.BlockSpec((d, 1), lambda l, j: (0, 0)),
                  pl.BlockSpec((1, d, tn), lambda l, j: (l, 0, j)),
                  pl.BlockSpec((1, 1, tn), lambda l, j: (l, 0, j))],
        out_specs=pl.BlockSpec((1, 1, tn), lambda l, j: (l, 0, j)),
        out_shape=jax.ShapeDtypeStruct((depth, 1, n), F32),
        compiler_params=pltpu.CompilerParams(dimension_semantics=("arbitrary", "arbitrary"),
                                             vmem_limit_bytes=VMEM_LIMIT_BYTES),
        name="adaln",
    )(c.reshape(d, 1), ada_w, ada_b.reshape(depth, 1, n))


_E_Q = (0, 1024)
_E_K = (1024, 2048)
_E_QI = (2048, 3072)
_E_KI = (3072, 3200)
_E_BG = (3200, 3712)
_E_CG = (3712, 4224)
_E_XB = (4224, 4736)
_E_COLS = 4736
_ET_V = (0, 512)
_ET_WI = (512, 528)
_ET_ROWS = 528
_FEAT_LANE = A_HEAD_DIM
_N_LOG2E_TERMS = 3
_POS_RADIX = 128


def _pad_heads(w):
    d = w.shape[0]
    w3 = w.reshape(d, N_HEADS, A_HEAD_DIM)
    return jnp.concatenate([w3, jnp.zeros_like(w3)], axis=2).reshape(d, N_HEADS * LANE)


def _alibi_q_features():
    terms, rest = [], np.float64(LOG2E)
    for _ in range(_N_LOG2E_TERMS):
        t = np.float64(np.asarray(rest, np.float32).astype(jnp.bfloat16).astype(np.float32))
        terms.append(t)
        rest = rest - t
    row = np.zeros((N_HEADS, LANE), np.float32)
    for h in range(N_HEADS):
        for n, t in enumerate(terms):
            row[h, _FEAT_LANE + n] = ALIBI_SLOPES[h] * _POS_RADIX * t
            row[h, _FEAT_LANE + _N_LOG2E_TERMS + n] = ALIBI_SLOPES[h] * t
    return jnp.asarray(row.reshape(1, N_HEADS * LANE))


def _even_weights(w_in):
    d = w_in.shape[0]
    a = A_WIDTH
    q = _pad_heads(w_in[:, 0:a])
    k = _pad_heads(w_in[:, a:2 * a])
    v = w_in[:, 2 * a:3 * a]
    qi = _pad_heads(w_in[:, 3 * a:4 * a])
    o = 4 * a
    ki = w_in[:, o:o + IDX_DIM]
    wi = w_in[:, o + IDX_DIM:o + IDX_DIM + N_HEADS]
    o = o + IDX_DIM + N_HEADS
    rest = w_in[:, o:o + 3 * B_WIDTH]
    w = jnp.concatenate([q, k, qi, ki, ki, rest], axis=1)
    assert w.shape[1] == _E_COLS
    wt = jnp.concatenate([v.T, wi.T, jnp.zeros((BF16_ROWS - N_HEADS, d), w_in.dtype)], axis=0)
    assert wt.shape[0] == _ET_ROWS
    return w.astype(BF16), wt.astype(BF16)


def _even_in_kernel(x_ref, mod_ref, w_ref, wt_ref, cw_ref, qf_ref,
                    q_ref, k_ref, vt_ref, qi_ref, ki_ref, wit_ref, yb_ref, carry_ref):
    d = D_MODEL
    i = pl.program_id(0)

    @pl.when(i == 0)
    def _():
        carry_ref[...] = jnp.zeros_like(carry_ref)

    h = (x_ref[...] * (1.0 + mod_ref[:, d:2 * d]) + mod_ref[:, 0:d]).astype(BF16)

    def proj(cols):
        return _dot(h, w_ref[:, cols[0]:cols[1]])

    def proj_t(rows):
        return _dot_nt(wt_ref[rows[0]:rows[1], :], h)

    q_ref[...] = (proj(_E_Q) * DSA_Q_SCALE + qf_ref[...]).astype(BF16)
    pos = i * TM + lax.broadcasted_iota(jnp.int32, (TM, LANE), 0)
    lane = lax.broadcasted_iota(jnp.int32, (TM, LANE), 1) - _FEAT_LANE
    pos_hi = (pos // _POS_RADIX).astype(F32)
    pos_lo = (pos % _POS_RADIX).astype(F32)
    kfeat = jnp.where(lane < 0, 0.0,
                      jnp.where(lane < _N_LOG2E_TERMS, pos_hi,
                                jnp.where(lane < 2 * _N_LOG2E_TERMS, pos_lo, 0.0)))
    kproj = proj(_E_K)
    for hd in range(N_HEADS):
        sl = slice(hd * LANE, (hd + 1) * LANE)
        k_ref[:, sl] = (kproj[:, sl] + kfeat).astype(BF16)
    _store_values_t(vt_ref, proj_t(_ET_V))
    qi_ref[...] = proj(_E_QI).astype(BF16)
    ki_ref[...] = proj(_E_KI).astype(BF16)
    wit_ref[...] = proj_t(_ET_WI) * IDX_W_SCALE
    g = proj(_E_CG) * proj(_E_XB)
    y = _causal_conv3(g, carry_ref[...], cw_ref[...])
    yb_ref[...] = (proj(_E_BG) * y).astype(BF16)
    carry_ref[...] = g[TM - SUBLANE:TM]


def _even_in(x, mod, w, wt, conv_w, qfeat):
    s, d = x.shape
    row = lambda n: pl.BlockSpec((TM, n), lambda i: (i, 0))
    col = lambda n: pl.BlockSpec((n, TM), lambda i: (0, i))
    full = lambda a: pl.BlockSpec(a.shape, lambda i: (0,) * a.ndim)
    rows_out = lambda n, t: (row(n), jax.ShapeDtypeStruct((s, n), t))
    cols_out = lambda n, t: (col(n), jax.ShapeDtypeStruct((n, s), t))
    outs = [rows_out(N_HEADS * LANE, BF16), rows_out(N_HEADS * LANE, BF16), cols_out(N_HEADS * V_ROWS, BF16),
            rows_out(N_HEADS * LANE, BF16), rows_out(LANE, BF16), cols_out(BF16_ROWS, F32),
            rows_out(B_WIDTH, BF16)]
    return pl.pallas_call(
        _even_in_kernel,
        grid=(s // TM,),
        in_specs=[row(d), full(mod), full(w), full(wt), full(conv_w), full(qfeat)],
        out_specs=[o[0] for o in outs],
        out_shape=[o[1] for o in outs],
        scratch_shapes=[pltpu.VMEM((SUBLANE, B_WIDTH), F32)],
        compiler_params=_params(),
        name="even_in",
    )(x, mod, w, wt, conv_w, qfeat)


def _select_kernel(qi_ref, ki_ref, wt_ref, bias_ref, hi_ref, lo_ref, m1_ref, m2_ref, mc_ref, my_ref, tc_ref,
                   *, seq):
    tq, tk = TQ_SEL, TK_SEL
    i16 = jnp.int16
    i = pl.program_id(0)
    start = i * tq
    n_tiles = (start + tq + tk - 1) // tk
    tpos = start + lax.broadcasted_iota(jnp.int32, (1, tq), 1)
    limit = (tpos // CHUNK + 1) * CHUNK
    int_min = jnp.int32(INT_MIN)
    dmin, dmax = -2 ** 15, 2 ** 15 - 1
    zero16, one16 = jnp.zeros((), i16), jnp.ones((), i16)
    sel0, seln = jnp.zeros((), BF16), jnp.full((), NEG, BF16)

    def key_pos(off):
        return off + lax.broadcasted_iota(jnp.int32, (tk, tq), 0)

    def score_tile(j, carry, *, masked, tiles=1):
        for t in range(tiles):
            score_rows(pl.multiple_of((j * tiles + t) * tk, tk), masked)
        return carry

    def score_rows(off, masked):
        kt = ki_ref[pl.ds(off, tk), :]
        score = jnp.zeros((tk, tq), F32)
        for h in range(N_HEADS):
            rel = _dot_nt(kt, qi_ref[:, h * LANE:(h + 1) * LANE])
            score = score + wt_ref[h:h + 1, :] * jnp.maximum(rel, 0.0)
        bits = pltpu.bitcast(score, jnp.int32)
        key = bits ^ ((bits >> 31) & jnp.int32(0x7FFFFFFF))
        key = jnp.where(bits == int_min, 0, key)
        if masked:
            key = jnp.where(key_pos(off) < limit, key, int_min)
        hi_ref[pl.ds(off, tk), :] = (key >> 16).astype(i16)
        lo_ref[pl.ds(off, tk), :] = key.astype(i16) ^ jnp.asarray(dmin, i16)

    n_before = start // tk
    n_groups = n_before // SCORE_GROUP
    lax.fori_loop(0, n_groups, functools.partial(score_tile, masked=False, tiles=SCORE_GROUP), 0)
    lax.fori_loop(SCORE_GROUP * n_groups, n_before, functools.partial(score_tile, masked=False), 0)
    lax.fori_loop(n_before, n_tiles, functools.partial(score_tile, masked=True), 0)

    def rows_at(ref, off, rows):
        return ref[pl.ds(off, rows), :]

    def for_key_tiles(fn, carry):
        return lax.fori_loop(0, n_tiles, lambda j, c: fn(pl.multiple_of(j * tk, tk), tk, c), carry)

    def per_query(c):
        return jnp.sum(c.astype(jnp.int32).astype(F32), axis=0, keepdims=True)

    def count(flag):
        def body(off, rows, acc):
            return acc + _fold_rows(flag(off, rows), BF16_ROWS)
        return per_query(for_key_tiles(body, jnp.zeros((BF16_ROWS, tq), i16)))

    def radix_search(count_ge, need):
        def bit_step(b, thr):
            cand = thr + jnp.left_shift(jnp.int32(1), 15 - b)
            return jnp.where(count_ge(cand.astype(i16)) >= need, cand, thr)
        return lax.fori_loop(0, 16, bit_step, jnp.full((1, tq), dmin, jnp.int32))

    topk = float(TOPK)
    t_hi = radix_search(lambda c16: count(
        lambda off, rows: jnp.where(rows_at(hi_ref, off, rows) >= c16, one16, zero16)), topk)
    t_hi16 = t_hi.astype(i16)
    n_above = count(lambda off, rows: jnp.where(rows_at(hi_ref, off, rows) > t_hi16, one16, zero16))
    need_lo = topk - n_above

    m1_ref[...] = jnp.full((tk, tq), dmin, i16)
    m2_ref[...] = jnp.full((tk, tq), dmin, i16)
    mc_ref[...] = jnp.zeros((tk, tq), i16)

    def fold_members(off, rows, carry):
        member = rows_at(hi_ref, off, rows) == t_hi16
        x = jnp.where(member, rows_at(lo_ref, off, rows), jnp.asarray(dmin, i16))
        lo_ref[pl.ds(off, rows), :] = x
        a, b = m1_ref[...], m2_ref[...]
        above = x > a
        m1_ref[...] = jnp.where(above, x, a)
        second = jnp.where(above, a, x)
        m2_ref[...] = jnp.where(second > b, second, b)
        mc_ref[...] = mc_ref[...] + jnp.where(member, one16, zero16)
        return carry

    for_key_tiles(fold_members, 0)

    def count_slots(flag):
        return per_query(_fold_rows(flag(m1_ref[...]) + flag(m2_ref[...]), BF16_ROWS))

    def level2_slots():
        t = radix_search(lambda c16: count_slots(lambda m: jnp.where(m >= c16, one16, zero16)), need_lo)
        t16 = t.astype(i16)
        n_members = per_query(_fold_rows(mc_ref[...], BF16_ROWS))
        gt = count_slots(lambda m: jnp.where(m > t16, one16, zero16))
        ge = jnp.where(t == dmin, n_members, count_slots(lambda m: jnp.where(m >= t16, one16, zero16)))
        return t, gt, ge

    def level2_full():
        t = radix_search(lambda c16: count(
            lambda off, rows: jnp.where(rows_at(lo_ref, off, rows) >= c16, one16, zero16)), need_lo)
        t16 = t.astype(i16)
        gt = count(lambda off, rows: jnp.where(rows_at(lo_ref, off, rows) > t16, one16, zero16))
        ge = count(lambda off, rows: jnp.where(
            rows_at(hi_ref, off, rows) == t_hi16,
            jnp.where(rows_at(lo_ref, off, rows) >= t16, one16, zero16), zero16))
        return t, gt, ge

    crowded = per_query(_fold_rows(jnp.where(mc_ref[...] > jnp.asarray(2, i16), one16, zero16), BF16_ROWS))
    t_lo, n_gt_lo, n_ge_lo = lax.cond(jnp.max(crowded) > 0.0, level2_full, level2_slots)
    t_lo16 = t_lo.astype(i16)
    n_gt = n_above + n_gt_lo
    n_ge = n_above + n_ge_lo
    room = topk - n_gt
    sentinel = t_hi == dmin
    has_ties = jnp.where(sentinel, 0.0, jnp.where(n_ge > topk, 1.0, 0.0))
    any_ties = jnp.max(has_ties) > 0.0

    def pos16(off, rows):
        return (off + lax.broadcasted_iota(jnp.int32, (rows, tq), 0)).astype(i16)

    def tie_cutoff():
        def tied_rows(j):
            off = pl.multiple_of(j * tk, tk)
            return jnp.where(rows_at(hi_ref, off, tk) == t_hi16,
                             jnp.where(rows_at(lo_ref, off, tk) == t_lo16, one16, zero16), zero16)

        def count_tile(j, carry):
            tc_ref[pl.ds(j, 1), :] = per_query(_fold_rows(tied_rows(j), BF16_ROWS))
            return carry
        lax.fori_loop(0, n_tiles, count_tile, 0)

        def find_tile(j, carry):
            seen, tile_of, left = carry
            here = tc_ref[pl.ds(j, 1), :]
            crossing = jnp.logical_and(seen < room, seen + here >= room)
            return (seen + here, jnp.where(crossing, j, tile_of), jnp.where(crossing, room - seen, left))
        _, tile_of, left = lax.fori_loop(
            0, n_tiles, find_tile, (jnp.zeros((1, tq), F32), jnp.zeros((1, tq), jnp.int32), room))

        my_ref[...] = jnp.zeros((tk, tq), i16)
        tile_of16 = tile_of.astype(i16)

        def gather_tile(j, carry):
            mine = jnp.where(tile_of16 == j.astype(i16), jnp.asarray(-1, i16), zero16)
            my_ref[...] = my_ref[...] | (tied_rows(j) & mine)
            return carry
        lax.fori_loop(0, n_tiles, gather_tile, 0)

        row16 = lax.broadcasted_iota(jnp.int32, (tk, tq), 0).astype(i16)
        row_bits = tk.bit_length()

        def row_step(b, cut):
            cand = cut + jnp.left_shift(jnp.int32(1), row_bits - 1 - b)
            below = jnp.where(row16 < cand.astype(i16), my_ref[...], zero16)
            return jnp.where(per_query(_fold_rows(below, BF16_ROWS)) <= left, cand, cut)
        row_cut = lax.fori_loop(0, row_bits, row_step, jnp.zeros((1, tq), jnp.int32))
        return tile_of * tk + jnp.minimum(row_cut, tk)

    def write_plain():
        lo_min16 = jnp.where(sentinel, dmax, t_lo).astype(i16)

        def write_rows(off, rows, carry):
            hi = rows_at(hi_ref, off, rows)
            inner = jnp.where(rows_at(lo_ref, off, rows) >= lo_min16, sel0, seln)
            bias_ref[pl.ds(off, rows), :] = jnp.where(hi > t_hi16, sel0,
                                                      jnp.where(hi == t_hi16, inner, seln))
            return carry
        for_key_tiles(write_rows, 0)

    def write_with_ties():
        cut16 = jnp.where(sentinel, 0, tie_cutoff()).astype(i16)

        def write_rows(off, rows, carry):
            hi = rows_at(hi_ref, off, rows)
            lo = rows_at(lo_ref, off, rows)
            tie = jnp.where(pos16(off, rows) < cut16, sel0, seln)
            inner = jnp.where(lo > t_lo16, sel0, jnp.where(lo == t_lo16, tie, seln))
            bias_ref[pl.ds(off, rows), :] = jnp.where(hi > t_hi16, sel0,
                                                      jnp.where(hi == t_hi16, inner, seln))
            return carry
        for_key_tiles(write_rows, 0)

    lax.cond(any_ties, write_with_ties, write_plain)

    def fill_tile(j, carry):
        off = pl.multiple_of(j * tk, tk)
        bias_ref[pl.ds(off, tk), :] = jnp.full((tk, tq), NEG, BF16)
        return carry

    lax.fori_loop(n_tiles, seq // tk, fill_tile, 0)


def _select(qi, ki, wit):
    s = qi.shape[0]
    assert s % TK_SEL == 0 and s % TQ_SEL == 0 and 2 * TOPK <= s <= 32767
    return pl.pallas_call(
        functools.partial(_select_kernel, seq=s),
        grid=(s // TQ_SEL,),
        in_specs=[pl.BlockSpec((TQ_SEL, N_HEADS * LANE), lambda i: (i, 0)),
                  pl.BlockSpec((s, LANE), lambda i: (0, 0)),
                  pl.BlockSpec((BF16_ROWS, TQ_SEL), lambda i: (0, i))],
        out_specs=pl.BlockSpec((s, TQ_SEL), lambda i: (0, i)),
        out_shape=jax.ShapeDtypeStruct((s, s), BF16),
        scratch_shapes=[pltpu.VMEM((s, TQ_SEL), jnp.int16), pltpu.VMEM((s, TQ_SEL), jnp.int16)]
        + [pltpu.VMEM((TK_SEL, TQ_SEL), jnp.int16)] * 4 + [pltpu.VMEM((s // TK_SEL, TQ_SEL), F32)],
        compiler_params=_params(),
        name="dsa_select",
    )(qi, ki, wit)


def _flash_kernel(qi_of, kj_of, *refs, alibi, use_bias):
    if use_bias:
        q_ref, k_ref, vt_ref, bias_ref, o_ref, m_ref, acc_ref = refs
    else:
        q_ref, k_ref, vt_ref, o_ref, m_ref, acc_ref = refs
    tq, tk = TQ_ATT, TK_ATT
    p = pl.program_id(0)
    qi = qi_of[p]
    kj = kj_of[p]
    last = ((qi + 1) * tq - 1) // tk

    @pl.when(kj == 0)
    def _():
        m_ref[...] = jnp.full_like(m_ref, NEG)
        acc_ref[...] = jnp.zeros_like(acc_ref)

    def step(diagonal, live=tk):
        base = bias_ref[0:live, :].astype(F32) if use_bias else None
        if diagonal:
            tpos = qi * tq + lax.broadcasted_iota(jnp.int32, (live, tq), 1)
            spos = kj * tk + lax.broadcasted_iota(jnp.int32, (live, tq), 0)
            if not use_bias:
                base = jnp.where(spos < (tpos // CHUNK + 1) * CHUNK, 0.0, NEG)
            if alibi:
                ahead = jnp.maximum(spos - tpos, 0).astype(F32)

        def logits(h):
            s = _dot_nt(k_ref[0:live, h * LANE:(h + 1) * LANE], q_ref[:, h * LANE:(h + 1) * LANE])
            if diagonal and alibi:
                return s + (base - (2.0 * ALIBI_SLOPES[h] * LOG2E) * ahead)
            return s if base is None else s + base

        pending = [logits(h) for h in range(QK_AHEAD)]
        for h in range(N_HEADS):
            rows = slice(h * V_ROWS, (h + 1) * V_ROWS)
            if h + QK_AHEAD < N_HEADS:
                pending.append(logits(h + QK_AHEAD))
            s = pending.pop(0)
            m_prev = m_ref[h:h + 1, :]
            m_new = jnp.maximum(m_prev, jnp.max(s, axis=0, keepdims=True))
            alpha = jnp.exp2(m_prev - m_new)
            pexp = jnp.exp2(s - m_new)
            acc_ref[rows, :] = alpha * acc_ref[rows, :] + _dot(vt_ref[rows, 0:live], pexp.astype(BF16))
            m_ref[h:h + 1, :] = m_new

    @pl.when(kj != last)
    def _():
        step(False)

    blocks_per_tile = max(tk // tq, 1)
    for v in range(blocks_per_tile):
        @pl.when(jnp.logical_and(kj == last, qi % blocks_per_tile == v))
        def _(v=v):
            step(True, live=min((v + 1) * tq, tk))

    @pl.when(kj == last)
    def _():
        outs = [acc_ref[h * V_ROWS:h * V_ROWS + D_V, :] / acc_ref[h * V_ROWS + D_V:h * V_ROWS + D_V + 1, :]
                for h in range(N_HEADS)]
        o_ref[...] = jnp.concatenate(outs, axis=0).T.astype(BF16)


def _flash(q, k, vt, bias, *, alibi):
    s = q.shape[0]
    nq = s // TQ_ATT
    pairs = [(i, j) for i in range(nq) for j in range(((i + 1) * TQ_ATT - 1) // TK_ATT + 1)]
    qi_of = jnp.asarray(np.array([a for a, _ in pairs], np.int32))
    kj_of = jnp.asarray(np.array([b for _, b in pairs], np.int32))
    use_bias = bias is not None
    in_specs = [pl.BlockSpec((TQ_ATT, N_HEADS * LANE), lambda p, qi, kj: (qi[p], 0)),
                pl.BlockSpec((TK_ATT, k.shape[1]), lambda p, qi, kj: (kj[p], 0)),
                pl.BlockSpec((N_HEADS * V_ROWS, TK_ATT), lambda p, qi, kj: (0, kj[p]))]
    args = [q, k, vt]
    if use_bias:
        in_specs.append(pl.BlockSpec((TK_ATT, TQ_ATT), lambda p, qi, kj: (kj[p], qi[p])))
        args.append(bias)
    kern = functools.partial(_flash_kernel, alibi=alibi, use_bias=use_bias)
    return pl.pallas_call(
        kern,
        grid_spec=pltpu.PrefetchScalarGridSpec(
            num_scalar_prefetch=2,
            grid=(len(pairs),),
            in_specs=in_specs,
            out_specs=pl.BlockSpec((TQ_ATT, N_HEADS * D_V), lambda p, qi, kj: (qi[p], 0)),
            scratch_shapes=[pltpu.VMEM((N_HEADS, TQ_ATT), F32),
                            pltpu.VMEM((N_HEADS * V_ROWS, TQ_ATT), F32)]),
        out_shape=jax.ShapeDtypeStruct((s, N_HEADS * D_V), BF16),
        compiler_params=_params(),
        name="flash_dsa" if use_bias else "flash_mla",
    )(qi_of, kj_of, *args)


def _mix_ffn_kernel(ya_ref, yb_ref, x_ref, mod_ref, wo_ref, gm_ref, bm_ref,
                    wup_ref, cw_ref, wdn_ref, gf_ref, bf_ref, o_ref, carry_ref, act_ref):
    d = D_MODEL

    @pl.when(pl.program_id(0) == 0)
    def _():
        carry_ref[...] = jnp.zeros_like(carry_ref)

    na = ya_ref.shape[1]
    y = _dot(ya_ref[...], wo_ref[0:na, :]) + _dot(yb_ref[...], wo_ref[na:, :])
    x = _layer_norm(DN_ALPHA * x_ref[...] + (1.0 + mod_ref[:, 2 * d:3 * d]) * y, gm_ref[...], bm_ref[...])
    h = (x * (1.0 + mod_ref[:, 4 * d:5 * d]) + mod_ref[:, 3 * d:4 * d]).astype(BF16)
    for c in range(D_FF // FFN_CHUNK):
        va = c * FFN_CHUNK
        ga = D_FF + va
        uv = _dot(h, wup_ref[:, va:va + FFN_CHUNK])
        ug = _dot(h, wup_ref[:, ga:ga + FFN_CHUNK])
        val = _causal_conv3(uv, carry_ref[:, va:va + FFN_CHUNK], cw_ref[:, va:va + FFN_CHUNK])
        gate = _causal_conv3(ug, carry_ref[:, ga:ga + FFN_CHUNK], cw_ref[:, ga:ga + FFN_CHUNK])
        carry_ref[:, va:va + FFN_CHUNK] = uv[TM - SUBLANE:TM]
        carry_ref[:, ga:ga + FFN_CHUNK] = ug[TM - SUBLANE:TM]
        act_ref[:, va:va + FFN_CHUNK] = (gate * jax.nn.sigmoid(gate) * val).astype(BF16)
    y = _dot(act_ref[...], wdn_ref[...])
    z = DN_ALPHA * x + (1.0 + mod_ref[:, 5 * d:6 * d]) * y
    o_ref[...] = _layer_norm(z, gf_ref[...], bf_ref[...])


def _mix_ffn(ya, yb, x, mod, w_out, g_mix, b_mix, w_up, conv_w, w_down, g_ffn, b_ffn):
    s, d = x.shape
    assert D_FF % FFN_CHUNK == 0
    row = lambda n: pl.BlockSpec((TM, n), lambda i: (i, 0))
    full = lambda a: pl.BlockSpec(a.shape, lambda i: (0,) * a.ndim)
    once = lambda a: pl.BlockSpec(a.shape, lambda i: (0,) * a.ndim, pipeline_mode=pl.Buffered(1))
    vec = lambda a: a.reshape(1, d)
    args = [ya, yb, x, mod, w_out, vec(g_mix), vec(b_mix), w_up, conv_w, w_down, vec(g_ffn), vec(b_ffn)]
    specs = [row(ya.shape[1]), row(yb.shape[1]), row(d), full(mod), once(w_out), full(args[5]), full(args[6]),
             once(w_up), full(conv_w), once(w_down), full(args[10]), full(args[11])]
    return pl.pallas_call(
        _mix_ffn_kernel,
        grid=(s // TM,),
        in_specs=specs,
        out_specs=row(d),
        out_shape=jax.ShapeDtypeStruct((s, d), F32),
        scratch_shapes=[pltpu.VMEM((SUBLANE, 2 * D_FF), F32), pltpu.VMEM((TM, D_FF), BF16)],
        compiler_params=_params(),
        name="mix_ffn",
    )(*args)


_O_U = (0, 512)
_O_QLAT = (512, 896)
_O_KVLAT = (896, 1152)
_O_KR = (1152, 1280)
_O_KR_ROT = (1280, 1408)
_O_COLS = 1408


def _rotate_half_cols(w):
    half = w.shape[-1] // 2
    return jnp.concatenate([-w[..., half:], w[..., :half]], axis=-1)


def _odd_weights(w_in, w_uq, w_ukv):
    d = w_in.shape[0]
    o = C_WIDTH + Q_LORA + KV_LORA
    kr = w_in[:, o:o + D_ROPE]
    zl = jnp.zeros((d, D_NOPE), w_in.dtype)
    zr = jnp.zeros((d, LANE - D_NOPE - D_ROPE), w_in.dtype)
    w1 = jnp.concatenate([w_in[:, 0:o], zl, kr, zr, zl, _rotate_half_cols(kr), zr], axis=1)
    assert w1.shape[1] == _O_COLS
    dq = D_NOPE + D_ROPE
    uq = w_uq.reshape(Q_LORA, N_HEADS, dq)
    padq = jnp.zeros((Q_LORA, N_HEADS, LANE - dq), w_uq.dtype)
    wq1 = jnp.concatenate([uq, padq], axis=2).reshape(Q_LORA, N_HEADS * LANE)
    wq2 = jnp.concatenate([jnp.zeros((Q_LORA, N_HEADS, D_NOPE), w_uq.dtype),
                           _rotate_half_cols(uq[:, :, D_NOPE:]), padq], axis=2
                          ).reshape(Q_LORA, N_HEADS * LANE)
    ukv = w_ukv.reshape(KV_LORA, N_HEADS, D_NOPE + D_V)
    wk = jnp.concatenate([ukv[:, :, :D_NOPE], jnp.zeros((KV_LORA, N_HEADS, LANE - D_NOPE), w_ukv.dtype)],
                         axis=2).reshape(KV_LORA, N_HEADS * LANE)
    wvt = ukv[:, :, D_NOPE:].reshape(KV_LORA, N_HEADS * D_V).T
    return w1.astype(BF16), wq1.astype(BF16), wq2.astype(BF16), wk.astype(BF16), wvt.astype(BF16)


def _rope_lane_tables(seq):
    inv = ROPE_BASE ** (-jnp.arange(0, D_ROPE, 2, dtype=F32) / D_ROPE)
    ang = jnp.arange(seq, dtype=F32)[:, None] * inv[None, :]
    cos, sin = jnp.cos(ang), jnp.sin(ang)
    pad = jnp.zeros((seq, LANE - D_NOPE - D_ROPE), F32)
    cos_t = jnp.concatenate([jnp.ones((seq, D_NOPE), F32), cos, cos, pad], axis=1)
    sin_t = jnp.concatenate([jnp.zeros((seq, D_NOPE), F32), sin, sin, pad], axis=1)
    return cos_t, sin_t


def _odd_in_kernel(x_ref, mod_ref, w1_ref, wq1_ref, wq2_ref, wk_ref, wvt_ref, pw_ref, ps_ref,
                   qg_ref, kvg_ref, cos_ref, sin_ref,
                   yc_ref, q_ref, k_ref, vt_ref, carry_ref):
    d = D_MODEL
    i = pl.program_id(0)

    @pl.when(i == 0)
    def _():
        carry_ref[...] = jnp.zeros_like(carry_ref)

    h = (x_ref[...] * (1.0 + mod_ref[:, d:2 * d]) + mod_ref[:, 0:d]).astype(BF16)

    def proj(cols):
        return _dot(h, w1_ref[:, cols[0]:cols[1]])

    cos = cos_ref[...]
    sin = sin_ref[...]

    u = proj(_O_U)
    ext = jnp.concatenate([carry_ref[...], u], axis=0)
    pos = i * TM + lax.broadcasted_iota(jnp.int32, (TM, 1), 0)
    for g, win in enumerate(POOL_WINDOWS):
        sl = slice(g * C_GROUP_DIM, (g + 1) * C_GROUP_DIM)
        acc = ext[:, sl]
        shift = 1
        while shift < win:
            acc = acc + pltpu.roll(acc, shift, 0)
            shift *= 2
        cnt = jnp.minimum(pos + 1, win).astype(F32)
        pooled = acc[POOL_HALO:] / cnt - u[:, sl]
        mixed = _dot(pooled.astype(BF16), pw_ref[g])
        yc_ref[:, sl] = (mixed * ps_ref[:, sl]).astype(BF16)
    carry_ref[...] = u[TM - POOL_HALO:TM]

    r = _rms_norm(proj(_O_QLAT), qg_ref[...]).astype(BF16)
    qa = _dot(r, wq1_ref[...])
    qb = _dot(r, wq2_ref[...])
    kr = proj(_O_KR) * cos + proj(_O_KR_ROT) * sin
    rk = _rms_norm(proj(_O_KVLAT), kvg_ref[...]).astype(BF16)
    kn = _dot(rk, wk_ref[...])
    for hd in range(N_HEADS):
        sl = slice(hd * LANE, (hd + 1) * LANE)
        q_ref[:, sl] = ((qa[:, sl] * cos + qb[:, sl] * sin) * MLA_Q_SCALE).astype(BF16)
        k_ref[:, sl] = (kn[:, sl] + kr).astype(BF16)
    _store_values_t(vt_ref, _dot_nt(wvt_ref[...], rk))


def _odd_in(x, mod, w1, wq1, wq2, wk, wvt, pool_w, pool_scale, q_norm_g, kv_norm_g, cos_t, sin_t):
    s, d = x.shape
    row = lambda n: pl.BlockSpec((TM, n), lambda i: (i, 0))
    col = lambda n: pl.BlockSpec((n, TM), lambda i: (0, i))
    full = lambda a: pl.BlockSpec(a.shape, lambda i: (0,) * a.ndim)
    consts = [w1, wq1, wq2, wk, wvt, pool_w.astype(BF16), pool_scale.reshape(1, C_WIDTH),
              q_norm_g.reshape(1, Q_LORA), kv_norm_g.reshape(1, KV_LORA)]
    rows_out = lambda n, t: (row(n), jax.ShapeDtypeStruct((s, n), t))
    cols_out = lambda n, t: (col(n), jax.ShapeDtypeStruct((n, s), t))
    outs = [rows_out(C_WIDTH, BF16), rows_out(N_HEADS * LANE, BF16), rows_out(N_HEADS * LANE, BF16),
            cols_out(N_HEADS * V_ROWS, BF16)]
    return pl.pallas_call(
        _odd_in_kernel,
        grid=(s // TM,),
        in_specs=[row(d), full(mod)] + [full(a) for a in consts] + [row(LANE), row(LANE)],
        out_specs=[o[0] for o in outs],
        out_shape=[o[1] for o in outs],
        scratch_shapes=[pltpu.VMEM((POOL_HALO, C_WIDTH), F32)],
        compiler_params=_params(),
        name="odd_in",
    )(x, mod, *consts, cos_t, sin_t)


def kernel(x, c, ada_w, ada_b, ln_mix_g, ln_mix_b, ln_ffn_g, ln_ffn_b, ev_w_in, ev_conv_w, ev_w_out,
           od_w_in, pool_w, pool_scale, q_norm_g, w_uq, kv_norm_g, w_ukv, od_w_out,
           ffn_w_up, ffn_conv_w, ffn_w_down):
    bsz, seq, d = x.shape
    assert bsz == 1 and d == D_MODEL and seq % TM == 0 and seq % TK_ATT == 0 and seq % TQ_ATT == 0
    assert SHORT_CONV == 3 and FFN_CONV == 3
    xs = x.reshape(seq, d)
    mods = _adaln(c, ada_w, ada_b)
    for l in range(DEPTH):
        mod = mods[l]
        if l % 2 == 0:
            e = l // 2
            w, wt = _even_weights(ev_w_in[e])
            q, k, vt, qi, ki, wit, yb = _even_in(xs, mod, w, wt, ev_conv_w[e], _alibi_q_features())
            bias = _select(qi, ki, wit)
            mixed = (_flash(q, k, vt, bias, alibi=True), yb)
            w_out = ev_w_out[e]
        else:
            o = l // 2
            cos_t, sin_t = _rope_lane_tables(seq)
            ws = _odd_weights(od_w_in[o], w_uq[o], w_ukv[o])
            yc, q, k, vt = _odd_in(xs, mod, *ws, pool_w[o], pool_scale[o], q_norm_g[o], kv_norm_g[o],
                                   cos_t, sin_t)
            mixed = (yc, _flash(q, k, vt, None, alibi=False))
            w_out = od_w_out[o]
        xs = _mix_ffn(*mixed, xs, mod, w_out.astype(BF16), ln_mix_g[l], ln_mix_b[l],
                      ffn_w_up[l].astype(BF16), ffn_conv_w[l], ffn_w_down[l].astype(BF16),
                      ln_ffn_g[l], ln_ffn_b[l])
    return xs.reshape(bsz, seq, d)
```

```python
import functools

import numpy as np
import jax
import jax.numpy as jnp
from jax import lax
from jax.experimental import pallas as pl
from jax.experimental.pallas import tpu as pltpu

D_MODEL = 1024
DEPTH = 2
CHUNK = 64
N_HEADS = 8
A_HEAD_DIM = 64
A_WIDTH = N_HEADS * A_HEAD_DIM
IDX_DIM = 64
TOPK = 256
B_WIDTH = 512
SHORT_CONV = 3
C_WIDTH = 512
POOL_WINDOWS = (2, 4, 8, 16)
C_GROUP_DIM = C_WIDTH // len(POOL_WINDOWS)
D_NOPE = 64
D_ROPE = 32
D_V = 64
Q_LORA = 384
KV_LORA = 256
ROPE_BASE = 10000.0
D_FF = 2816
FFN_CONV = 3
LN_EPS = 1e-5
RMS_EPS = 1e-6
DN_ALPHA = (2 * DEPTH) ** 0.25
NEG = -1e30
IDX_W_SCALE = N_HEADS ** -0.5 * IDX_DIM ** -0.5
LOG2E = 1.4426950408889634
DSA_Q_SCALE = A_HEAD_DIM ** -0.5 * LOG2E
MLA_Q_SCALE = (D_NOPE + D_ROPE) ** -0.5 * LOG2E
ALIBI_SLOPES = tuple(2.0 ** (-8.0 * (i + 1) / N_HEADS) for i in range(N_HEADS))

LANE = 128
SUBLANE = 8
BF16_ROWS = 16
V_ROWS = D_V + BF16_ROWS
VMEM_LIMIT_BYTES = 56 * 1024 * 1024

TM = 512
FFN_CHUNK = 256
TQ_SEL = 256
TQ_ATT = 512
TK_SEL = 512
SCORE_GROUP = 4
TK_ATT = 2048
QK_AHEAD = 4
POOL_HALO = 16

INT_MIN = -2 ** 31
F32 = jnp.float32
BF16 = jnp.bfloat16


def _params():
    return pltpu.CompilerParams(dimension_semantics=("arbitrary",),
                                vmem_limit_bytes=VMEM_LIMIT_BYTES)


def _dot(a, b):
    return jnp.dot(a, b, preferred_element_type=F32)


def _dot_nt(a, b):
    return lax.dot_general(a, b, (((1,), (1,)), ((), ())), preferred_element_type=F32)


def _layer_norm(z, g, b):
    mu = jnp.mean(z, axis=-1, keepdims=True)
    zc = z - mu
    var = jnp.mean(zc * zc, axis=-1, keepdims=True)
    return zc * lax.rsqrt(var + LN_EPS) * g + b


def _rms_norm(z, g):
    return z * lax.rsqrt(jnp.mean(z * z, axis=-1, keepdims=True) + RMS_EPS) * g


def _causal_conv3(u, prev, w):
    w0, w1, w2 = w[0:1], w[1:2], w[2:3]
    y = pltpu.roll(u, 2, 0) * w0 + pltpu.roll(u, 1, 0) * w1 + u * w2
    head = u[0:SUBLANE]
    r = lax.broadcasted_iota(jnp.int32, (SUBLANE, 1), 0)
    h1 = jnp.where(r == 0, prev[7:8], pltpu.roll(head, 1, 0))
    h2 = jnp.where(r == 0, prev[6:7], jnp.where(r == 1, prev[7:8], pltpu.roll(head, 2, 0)))
    yh = h2 * w0 + h1 * w1 + head * w2
    return jnp.concatenate([yh, y[SUBLANE:]], axis=0)


def _fold_rows(c, rows_out):
    rows = c.shape[0]
    while rows > rows_out:
        rows //= 2
        c = c[0:rows] + c[rows:2 * rows]
    return c


def _store_values_t(vt_ref, vt):
    t = vt.shape[1]
    ones_row = jnp.where(lax.broadcasted_iota(jnp.int32, (BF16_ROWS, t), 0) == 0, 1.0, 0.0)
    for h in range(N_HEADS):
        vt_ref[h * V_ROWS:h * V_ROWS + D_V, :] = vt[h * D_V:(h + 1) * D_V, :].astype(BF16)
        vt_ref[h * V_ROWS + D_V:(h + 1) * V_ROWS, :] = ones_row.astype(BF16)


def _adaln_kernel(c_ref, w_ref, b_ref, o_ref):
    c = c_ref[...]
    cond = c * jax.nn.sigmoid(c)
    o_ref[0] = jnp.sum(cond * w_ref[0], axis=0, keepdims=True) + b_ref[0]


def _adaln(c, ada_w, ada_b):
    depth, d, n = ada_w.shape
    tn = 1536
    return pl.pallas_call(
        _adaln_kernel,
        grid=(depth, n // tn),
        in_specs=[pl.BlockSpec((d, 1), lambda l, j: (0, 0)),
                  pl.BlockSpec((1, d, tn), lambda l, j: (l, 0, j)),
                  pl.BlockSpec((1, 1, tn), lambda l, j: (l, 0, j))],
        out_specs=pl.BlockSpec((1, 1, tn), lambda l, j: (l, 0, j)),
        out_shape=jax.ShapeDtypeStruct((depth, 1, n), F32),
        compiler_params=pltpu.CompilerParams(dimension_semantics=("arbitrary", "arbitrary"),
                                             vmem_limit_bytes=VMEM_LIMIT_BYTES),
        name="adaln",
    )(c.reshape(d, 1), ada_w, ada_b.reshape(depth, 1, n))


_E_Q = (0, 1024)
_E_K = (1024, 2048)
_E_QI = (2048, 3072)
_E_KI = (3072, 3200)
_E_BG = (3200, 3712)
_E_CG = (3712, 4224)
_E_XB = (4224, 4736)
_E_COLS = 4736
_ET_V = (0, 512)
_ET_WI = (512, 528)
_ET_ROWS = 528
_FEAT_LANE = A_HEAD_DIM
_N_LOG2E_TERMS = 3
_POS_RADIX = 128


def _pad_heads(w):
    d = w.shape[0]
    w3 = w.reshape(d, N_HEADS, A_HEAD_DIM)
    return jnp.concatenate([w3, jnp.zeros_like(w3)], axis=2).reshape(d, N_HEADS * LANE)


def _alibi_q_features():
    terms, rest = [], np.float64(LOG2E)
    for _ in range(_N_LOG2E_TERMS):
        t = np.float64(np.asarray(rest, np.float32).astype(jnp.bfloat16).astype(np.float32))
        terms.append(t)
        rest = rest - t
    row = np.zeros((N_HEADS, LANE), np.float32)
    for h in range(N_HEADS):
        for n, t in enumerate(terms):
            row[h, _FEAT_LANE + n] = ALIBI_SLOPES[h] * _POS_RADIX * t
            row[h, _FEAT_LANE + _N_LOG2E_TERMS + n] = ALIBI_SLOPES[h] * t
    return jnp.asarray(row.reshape(1, N_HEADS * LANE))


def _even_weights(w_in):
    d = w_in.shape[0]
    a = A_WIDTH
    q = _pad_heads(w_in[:, 0:a])
    k = _pad_heads(w_in[:, a:2 * a])
    v = w_in[:, 2 * a:3 * a]
    qi = _pad_heads(w_in[:, 3 * a:4 * a])
    o = 4 * a
    ki = w_in[:, o:o + IDX_DIM]
    wi = w_in[:, o + IDX_DIM:o + IDX_DIM + N_HEADS]
    o = o + IDX_DIM + N_HEADS
    rest = w_in[:, o:o + 3 * B_WIDTH]
    w = jnp.concatenate([q, k, qi, ki, ki, rest], axis=1)
    assert w.shape[1] == _E_COLS
    wt = jnp.concatenate([v.T, wi.T, jnp.zeros((BF16_ROWS - N_HEADS, d), w_in.dtype)], axis=0)
    assert wt.shape[0] == _ET_ROWS
    return w.astype(BF16), wt.astype(BF16)


def _even_in_kernel(x_ref, mod_ref, w_ref, wt_ref, cw_ref, qf_ref,
                    q_ref, k_ref, vt_ref, qi_ref, ki_ref, wit_ref, yb_ref, carry_ref):
    d = D_MODEL
    i = pl.program_id(0)

    @pl.when(i == 0)
    def _():
        carry_ref[...] = jnp.zeros_like(carry_ref)

    h = (x_ref[...] * (1.0 + mod_ref[:, d:2 * d]) + mod_ref[:, 0:d]).astype(BF16)

    def proj(cols):
        return _dot(h, w_ref[:, cols[0]:cols[1]])

    def proj_t(rows):
        return _dot_nt(wt_ref[rows[0]:rows[1], :], h)

    q_ref[...] = (proj(_E_Q) * DSA_Q_SCALE + qf_ref[...]).astype(BF16)
    pos = i * TM + lax.broadcasted_iota(jnp.int32, (TM, LANE), 0)
    lane = lax.broadcasted_iota(jnp.int32, (TM, LANE), 1) - _FEAT_LANE
    pos_hi = (pos // _POS_RADIX).astype(F32)
    pos_lo = (pos % _POS_RADIX).astype(F32)
    kfeat = jnp.where(lane < 0, 0.0,
                      jnp.where(lane < _N_LOG2E_TERMS, pos_hi,
                                jnp.where(lane < 2 * _N_LOG2E_TERMS, pos_lo, 0.0)))
    kproj = proj(_E_K)
    for hd in range(N_HEADS):
        sl = slice(hd * LANE, (hd + 1) * LANE)
        k_ref[:, sl] = (kproj[:, sl] + kfeat).astype(BF16)
    _store_values_t(vt_ref, proj_t(_ET_V))
    qi_ref[...] = proj(_E_QI).astype(BF16)
    ki_ref[...] = proj(_E_KI).astype(BF16)
    wit_ref[...] = proj_t(_ET_WI) * IDX_W_SCALE
    g = proj(_E_CG) * proj(_E_XB)
    y = _causal_conv3(g, carry_ref[...], cw_ref[...])
    yb_ref[...] = (proj(_E_BG) * y).astype(BF16)
    carry_ref[...] = g[TM - SUBLANE:TM]


def _even_in(x, mod, w, wt, conv_w, qfeat):
    s, d = x.shape
    row = lambda n: pl.BlockSpec((TM, n), lambda i: (i, 0))
    col = lambda n: pl.BlockSpec((n, TM), lambda i: (0, i))
    full = lambda a: pl.BlockSpec(a.shape, lambda i: (0,) * a.ndim)
    rows_out = lambda n, t: (row(n), jax.ShapeDtypeStruct((s, n), t))
    cols_out = lambda n, t: (col(n), jax.ShapeDtypeStruct((n, s), t))
    outs = [rows_out(N_HEADS * LANE, BF16), rows_out(N_HEADS * LANE, BF16), cols_out(N_HEADS * V_ROWS, BF16),
            rows_out(N_HEADS * LANE, BF16), rows_out(LANE, BF16), cols_out(BF16_ROWS, F32),
            rows_out(B_WIDTH, BF16)]
    return pl.pallas_call(
        _even_in_kernel,
        grid=(s // TM,),
        in_specs=[row(d), full(mod), full(w), full(wt), full(conv_w), full(qfeat)],
        out_specs=[o[0] for o in outs],
        out_shape=[o[1] for o in outs],
        scratch_shapes=[pltpu.VMEM((SUBLANE, B_WIDTH), F32)],
        compiler_params=_params(),
        name="even_in",
    )(x, mod, w, wt, conv_w, qfeat)


def _select_kernel(qi_ref, ki_ref, wt_ref, bias_ref, hi_ref, lo_ref, m1_ref, m2_ref, mc_ref, my_ref, tc_ref,
                   *, seq):
    tq, tk = TQ_SEL, TK_SEL
    i16 = jnp.int16
    i = pl.program_id(0)
    start = i * tq
    n_tiles = (start + tq + tk - 1) // tk
    tpos = start + lax.broadcasted_iota(jnp.int32, (1, tq), 1)
    limit = (tpos // CHUNK + 1) * CHUNK
    int_min = jnp.int32(INT_MIN)
    dmin, dmax = -2 ** 15, 2 ** 15 - 1
    zero16, one16 = jnp.zeros((), i16), jnp.ones((), i16)
    sel0, seln = jnp.zeros((), BF16), jnp.full((), NEG, BF16)

    def key_pos(off):
        return off + lax.broadcasted_iota(jnp.int32, (tk, tq), 0)

    def score_tile(j, carry, *, masked, tiles=1):
        for t in range(tiles):
            score_rows(pl.multiple_of((j * tiles + t) * tk, tk), masked)
        return carry

    def score_rows(off, masked):
        kt = ki_ref[pl.ds(off, tk), :]
        score = jnp.zeros((tk, tq), F32)
        for h in range(N_HEADS):
            rel = _dot_nt(kt, qi_ref[:, h * LANE:(h + 1) * LANE])
            score = score + wt_ref[h:h + 1, :] * jnp.maximum(rel, 0.0)
        bits = pltpu.bitcast(score, jnp.int32)
        key = bits ^ ((bits >> 31) & jnp.int32(0x7FFFFFFF))
        key = jnp.where(bits == int_min, 0, key)
        if masked:
            key = jnp.where(key_pos(off) < limit, key, int_min)
        hi_ref[pl.ds(off, tk), :] = (key >> 16).astype(i16)
        lo_ref[pl.ds(off, tk), :] = key.astype(i16) ^ jnp.asarray(dmin, i16)

    n_before = start // tk
    n_groups = n_before // SCORE_GROUP
    lax.fori_loop(0, n_groups, functools.partial(score_tile, masked=False, tiles=SCORE_GROUP), 0)
    lax.fori_loop(SCORE_GROUP * n_groups, n_before, functools.partial(score_tile, masked=False), 0)
    lax.fori_loop(n_before, n_tiles, functools.partial(score_tile, masked=True), 0)

    def rows_at(ref, off, rows):
        return ref[pl.ds(off, rows), :]

    def for_key_tiles(fn, carry):
        return lax.fori_loop(0, n_tiles, lambda j, c: fn(pl.multiple_of(j * tk, tk), tk, c), carry)

    def per_query(c):
        return jnp.sum(c.astype(jnp.int32).astype(F32), axis=0, keepdims=True)

    def count(flag):
        def body(off, rows, acc):
            return acc + _fold_rows(flag(off, rows), BF16_ROWS)
        return per_query(for_key_tiles(body, jnp.zeros((BF16_ROWS, tq), i16)))

    def radix_search(count_ge, need):
        def bit_step(b, thr):
            cand = thr + jnp.left_shift(jnp.int32(1), 15 - b)
            return jnp.where(count_ge(cand.astype(i16)) >= need, cand, thr)
        return lax.fori_loop(0, 16, bit_step, jnp.full((1, tq), dmin, jnp.int32))

    topk = float(TOPK)
    t_hi = radix_search(lambda c16: count(
        lambda off, rows: jnp.where(rows_at(hi_ref, off, rows) >= c16, one16, zero16)), topk)
    t_hi16 = t_hi.astype(i16)
    n_above = count(lambda off, rows: jnp.where(rows_at(hi_ref, off, rows) > t_hi16, one16, zero16))
    need_lo = topk - n_above

    m1_ref[...] = jnp.full((tk, tq), dmin, i16)
    m2_ref[...] = jnp.full((tk, tq), dmin, i16)
    mc_ref[...] = jnp.zeros((tk, tq), i16)

    def fold_members(off, rows, carry):
        member = rows_at(hi_ref, off, rows) == t_hi16
        x = jnp.where(member, rows_at(lo_ref, off, rows), jnp.asarray(dmin, i16))
        lo_ref[pl.ds(off, rows), :] = x
        a, b = m1_ref[...], m2_ref[...]
        above = x > a
        m1_ref[...] = jnp.where(above, x, a)
        second = jnp.where(above, a, x)
        m2_ref[...] = jnp.where(second > b, second, b)
        mc_ref[...] = mc_ref[...] + jnp.where(member, one16, zero16)
        return carry

    for_key_tiles(fold_members, 0)

    def count_slots(flag):
        return per_query(_fold_rows(flag(m1_ref[...]) + flag(m2_ref[...]), BF16_ROWS))

    def level2_slots():
        t = radix_search(lambda c16: count_slots(lambda m: jnp.where(m >= c16, one16, zero16)), need_lo)
        t16 = t.astype(i16)
        n_members = per_query(_fold_rows(mc_ref[...], BF16_ROWS))
        gt = count_slots(lambda m: jnp.where(m > t16, one16, zero16))
        ge = jnp.where(t == dmin, n_members, count_slots(lambda m: jnp.where(m >= t16, one16, zero16)))
        return t, gt, ge

    def level2_full():
        t = radix_search(lambda c16: count(
            lambda off, rows: jnp.where(rows_at(lo_ref, off, rows) >= c16, one16, zero16)), need_lo)
        t16 = t.astype(i16)
        gt = count(lambda off, rows: jnp.where(rows_at(lo_ref, off, rows) > t16, one16, zero16))
        ge = count(lambda off, rows: jnp.where(
            rows_at(hi_ref, off, rows) == t_hi16,
            jnp.where(rows_at(lo_ref, off, rows) >= t16, one16, zero16), zero16))
        return t, gt, ge

    crowded = per_query(_fold_rows(jnp.where(mc_ref[...] > jnp.asarray(2, i16), one16, zero16), BF16_ROWS))
    t_lo, n_gt_lo, n_ge_lo = lax.cond(jnp.max(crowded) > 0.0, level2_full, level2_slots)
    t_lo16 = t_lo.astype(i16)
    n_gt = n_above + n_gt_lo
    n_ge = n_above + n_ge_lo
    room = topk - n_gt
    sentinel = t_hi == dmin
    has_ties = jnp.where(sentinel, 0.0, jnp.where(n_ge > topk, 1.0, 0.0))
    any_ties = jnp.max(has_ties) > 0.0

    def pos16(off, rows):
        return (off + lax.broadcasted_iota(jnp.int32, (rows, tq), 0)).astype(i16)

    def tie_cutoff():
        def tied_rows(j):
            off = pl.multiple_of(j * tk, tk)
            return jnp.where(rows_at(hi_ref, off, tk) == t_hi16,
                             jnp.where(rows_at(lo_ref, off, tk) == t_lo16, one16, zero16), zero16)

        def count_tile(j, carry):
            tc_ref[pl.ds(j, 1), :] = per_query(_fold_rows(tied_rows(j), BF16_ROWS))
            return carry
        lax.fori_loop(0, n_tiles, count_tile, 0)

        def find_tile(j, carry):
            seen, tile_of, left = carry
            here = tc_ref[pl.ds(j, 1), :]
            crossing = jnp.logical_and(seen < room, seen + here >= room)
            return (seen + here, jnp.where(crossing, j, tile_of), jnp.where(crossing, room - seen, left))
        _, tile_of, left = lax.fori_loop(
            0, n_tiles, find_tile, (jnp.zeros((1, tq), F32), jnp.zeros((1, tq), jnp.int32), room))

        my_ref[...] = jnp.zeros((tk, tq), i16)
        tile_of16 = tile_of.astype(i16)

        def gather_tile(j, carry):
            mine = jnp.where(tile_of16 == j.astype(i16), jnp.asarray(-1, i16), zero16)
            my_ref[...] = my_ref[...] | (tied_rows(j) & mine)
            return carry
        lax.fori_loop(0, n_tiles, gather_tile, 0)

        row16 = lax.broadcasted_iota(jnp.int32, (tk, tq), 0).astype(i16)
        row_bits = tk.bit_length()

        def row_step(b, cut):
            cand = cut + jnp.left_shift(jnp.int32(1), row_bits - 1 - b)
            below = jnp.where(row16 < cand.astype(i16), my_ref[...], zero16)
            return jnp.where(per_query(_fold_rows(below, BF16_ROWS)) <= left, cand, cut)
        row_cut = lax.fori_loop(0, row_bits, row_step, jnp.zeros((1, tq), jnp.int32))
        return tile_of * tk + jnp.minimum(row_cut, tk)

    def write_plain():
        lo_min16 = jnp.where(sentinel, dmax, t_lo).astype(i16)

        def write_rows(off, rows, carry):
            hi = rows_at(hi_ref, off, rows)
            inner = jnp.where(rows_at(lo_ref, off, rows) >= lo_min16, sel0, seln)
            bias_ref[pl.ds(off, rows), :] = jnp.where(hi > t_hi16, sel0,
                                                      jnp.where(hi == t_hi16, inner, seln))
            return carry
        for_key_tiles(write_rows, 0)

    def write_with_ties():
        cut16 = jnp.where(sentinel, 0, tie_cutoff()).astype(i16)

        def write_rows(off, rows, carry):
            hi = rows_at(hi_ref, off, rows)
            lo = rows_at(lo_ref, off, rows)
            tie = jnp.where(pos16(off, rows) < cut16, sel0, seln)
            inner = jnp.where(lo > t_lo16, sel0, jnp.where(lo == t_lo16, tie, seln))
            bias_ref[pl.ds(off, rows), :] = jnp.where(hi > t_hi16, sel0,
                                                      jnp.where(hi == t_hi16, inner, seln))
            return carry
        for_key_tiles(write_rows, 0)

    lax.cond(any_ties, write_with_ties, write_plain)

    def fill_tile(j, carry):
        off = pl.multiple_of(j * tk, tk)
        bias_ref[pl.ds(off, tk), :] = jnp.full((tk, tq), NEG, BF16)
        return carry

    lax.fori_loop(n_tiles, seq // tk, fill_tile, 0)


def _select(qi, ki, wit):
    s = qi.shape[0]
    assert s % TK_SEL == 0 and s % TQ_SEL == 0 and 2 * TOPK <= s <= 32767
    return pl.pallas_call(
        functools.partial(_select_kernel, seq=s),
        grid=(s // TQ_SEL,),
        in_specs=[pl.BlockSpec((TQ_SEL, N_HEADS * LANE), lambda i: (i, 0)),
                  pl.BlockSpec((s, LANE), lambda i: (0, 0)),
                  pl.BlockSpec((BF16_ROWS, TQ_SEL), lambda i: (0, i))],
        out_specs=pl.BlockSpec((s, TQ_SEL), lambda i: (0, i)),
        out_shape=jax.ShapeDtypeStruct((s, s), BF16),
        scratch_shapes=[pltpu.VMEM((s, TQ_SEL), jnp.int16), pltpu.VMEM((s, TQ_SEL), jnp.int16)]
        + [pltpu.VMEM((TK_SEL, TQ_SEL), jnp.int16)] * 4 + [pltpu.VMEM((s // TK_SEL, TQ_SEL), F32)],
        compiler_params=_params(),
        name="dsa_select",
    )(qi, ki, wit)


def _flash_kernel(qi_of, kj_of, *refs, alibi, use_bias):
    if use_bias:
        q_ref, k_ref, vt_ref, bias_ref, o_ref, m_ref, acc_ref = refs
    else:
        q_ref, k_ref, vt_ref, o_ref, m_ref, acc_ref = refs
    tq, tk = TQ_ATT, TK_ATT
    p = pl.program_id(0)
    qi = qi_of[p]
    kj = kj_of[p]
    last = ((qi + 1) * tq - 1) // tk

    @pl.when(kj == 0)
    def _():
        m_ref[...] = jnp.full_like(m_ref, NEG)
        acc_ref[...] = jnp.zeros_like(acc_ref)

    def step(diagonal, live=tk):
        base = bias_ref[0:live, :].astype(F32) if use_bias else None
        if diagonal:
            tpos = qi * tq + lax.broadcasted_iota(jnp.int32, (live, tq), 1)
            spos = kj * tk + lax.broadcasted_iota(jnp.int32, (live, tq), 0)
            if not use_bias:
                base = jnp.where(spos < (tpos // CHUNK + 1) * CHUNK, 0.0, NEG)
            if alibi:
                ahead = jnp.maximum(spos - tpos, 0).astype(F32)

        def logits(h):
            s = _dot_nt(k_ref[0:live, h * LANE:(h + 1) * LANE], q_ref[:, h * LANE:(h + 1) * LANE])
            if diagonal and alibi:
                return s + (base - (2.0 * ALIBI_SLOPES[h] * LOG2E) * ahead)
            return s if base is None else s + base

        pending = [logits(h) for h in range(QK_AHEAD)]
        for h in range(N_HEADS):
            rows = slice(h * V_ROWS, (h + 1) * V_ROWS)
            if h + QK_AHEAD < N_HEADS:
                pending.append(logits(h + QK_AHEAD))
            s = pending.pop(0)
            m_prev = m_ref[h:h + 1, :]
            m_new = jnp.maximum(m_prev, jnp.max(s, axis=0, keepdims=True))
            alpha = jnp.exp2(m_prev - m_new)
            pexp = jnp.exp2(s - m_new)
            acc_ref[rows, :] = alpha * acc_ref[rows, :] + _dot(vt_ref[rows, 0:live], pexp.astype(BF16))
            m_ref[h:h + 1, :] = m_new

    @pl.when(kj != last)
    def _():
        step(False)

    blocks_per_tile = max(tk // tq, 1)
    for v in range(blocks_per_tile):
        @pl.when(jnp.logical_and(kj == last, qi % blocks_per_tile == v))
        def _(v=v):
            step(True, live=min((v + 1) * tq, tk))

    @pl.when(kj == last)
    def _():
        outs = [acc_ref[h * V_ROWS:h * V_ROWS + D_V, :] / acc_ref[h * V_ROWS + D_V:h * V_ROWS + D_V + 1, :]
                for h in range(N_HEADS)]
        o_ref[...] = jnp.concatenate(outs, axis=0).T.astype(BF16)


def _flash(q, k, vt, bias, *, alibi):
    s = q.shape[0]
    nq = s // TQ_ATT
    pairs = [(i, j) for i in range(nq) for j in range(((i + 1) * TQ_ATT - 1) // TK_ATT + 1)]
    qi_of = jnp.asarray(np.array([a for a, _ in pairs], np.int32))
    kj_of = jnp.asarray(np.array([b for _, b in pairs], np.int32))
    use_bias = bias is not None
    in_specs = [pl.BlockSpec((TQ_ATT, N_HEADS * LANE), lambda p, qi, kj: (qi[p], 0)),
                pl.BlockSpec((TK_ATT, k.shape[1]), lambda p, qi, kj: (kj[p], 0)),
                pl.BlockSpec((N_HEADS * V_ROWS, TK_ATT), lambda p, qi, kj: (0, kj[p]))]
    args = [q, k, vt]
    if use_bias:
        in_specs.append(pl.BlockSpec((TK_ATT, TQ_ATT), lambda p, qi, kj: (kj[p], qi[p])))
        args.append(bias)
    kern = functools.partial(_flash_kernel, alibi=alibi, use_bias=use_bias)
    return pl.pallas_call(
        kern,
        grid_spec=pltpu.PrefetchScalarGridSpec(
            num_scalar_prefetch=2,
            grid=(len(pairs),),
            in_specs=in_specs,
            out_specs=pl.BlockSpec((TQ_ATT, N_HEADS * D_V), lambda p, qi, kj: (qi[p], 0)),
            scratch_shapes=[pltpu.VMEM((N_HEADS, TQ_ATT), F32),
                            pltpu.VMEM((N_HEADS * V_ROWS, TQ_ATT), F32)]),
        out_shape=jax.ShapeDtypeStruct((s, N_HEADS * D_V), BF16),
        compiler_params=_params(),
        name="flash_dsa" if use_bias else "flash_mla",
    )(qi_of, kj_of, *args)


def _mix_ffn_kernel(ya_ref, yb_ref, x_ref, mod_ref, wo_ref, gm_ref, bm_ref,
                    wup_ref, cw_ref, wdn_ref, gf_ref, bf_ref, o_ref, carry_ref, act_ref):
    d = D_MODEL

    @pl.when(pl.program_id(0) == 0)
    def _():
        carry_ref[...] = jnp.zeros_like(carry_ref)

    na = ya_ref.shape[1]
    y = _dot(ya_ref[...], wo_ref[0:na, :]) + _dot(yb_ref[...], wo_ref[na:, :])
    x = _layer_norm(DN_ALPHA * x_ref[...] + (1.0 + mod_ref[:, 2 * d:3 * d]) * y, gm_ref[...], bm_ref[...])
    h = (x * (1.0 + mod_ref[:, 4 * d:5 * d]) + mod_ref[:, 3 * d:4 * d]).astype(BF16)
    for c in range(D_FF // FFN_CHUNK):
        va = c * FFN_CHUNK
        ga = D_FF + va
        uv = _dot(h, wup_ref[:, va:va + FFN_CHUNK])
        ug = _dot(h, wup_ref[:, ga:ga + FFN_CHUNK])
        val = _causal_conv3(uv, carry_ref[:, va:va + FFN_CHUNK], cw_ref[:, va:va + FFN_CHUNK])
        gate = _causal_conv3(ug, carry_ref[:, ga:ga + FFN_CHUNK], cw_ref[:, ga:ga + FFN_CHUNK])
        carry_ref[:, va:va + FFN_CHUNK] = uv[TM - SUBLANE:TM]
        carry_ref[:, ga:ga + FFN_CHUNK] = ug[TM - SUBLANE:TM]
        act_ref[:, va:va + FFN_CHUNK] = (gate * jax.nn.sigmoid(gate) * val).astype(BF16)
    y = _dot(act_ref[...], wdn_ref[...])
    z = DN_ALPHA * x + (1.0 + mod_ref[:, 5 * d:6 * d]) * y
    o_ref[...] = _layer_norm(z, gf_ref[...], bf_ref[...])


def _mix_ffn(ya, yb, x, mod, w_out, g_mix, b_mix, w_up, conv_w, w_down, g_ffn, b_ffn):
    s, d = x.shape
    assert D_FF % FFN_CHUNK == 0
    row = lambda n: pl.BlockSpec((TM, n), lambda i: (i, 0))
    full = lambda a: pl.BlockSpec(a.shape, lambda i: (0,) * a.ndim)
    once = lambda a: pl.BlockSpec(a.shape, lambda i: (0,) * a.ndim, pipeline_mode=pl.Buffered(1))
    vec = lambda a: a.reshape(1, d)
    args = [ya, yb, x, mod, w_out, vec(g_mix), vec(b_mix), w_up, conv_w, w_down, vec(g_ffn), vec(b_ffn)]
    specs = [row(ya.shape[1]), row(yb.shape[1]), row(d), full(mod), once(w_out), full(args[5]), full(args[6]),
             once(w_up), full(conv_w), once(w_down), full(args[10]), full(args[11])]
    return pl.pallas_call(
        _mix_ffn_kernel,
        grid=(s // TM,),
        in_specs=specs,
        out_specs=row(d),
        out_shape=jax.ShapeDtypeStruct((s, d), F32),
        scratch_shapes=[pltpu.VMEM((SUBLANE, 2 * D_FF), F32), pltpu.VMEM((TM, D_FF), BF16)],
        compiler_params=_params(),
        name="mix_ffn",
    )(*args)


_O_U = (0, 512)
_O_QLAT = (512, 896)
_O_KVLAT = (896, 1152)
_O_KR = (1152, 1280)
_O_KR_ROT = (1280, 1408)
_O_COLS = 1408


def _rotate_half_cols(w):
    half = w.shape[-1] // 2
    return jnp.concatenate([-w[..., half:], w[..., :half]], axis=-1)


def _odd_weights(w_in, w_uq, w_ukv):
    d = w_in.shape[0]
    o = C_WIDTH + Q_LORA + KV_LORA
    kr = w_in[:, o:o + D_ROPE]
    zl = jnp.zeros((d, D_NOPE), w_in.dtype)
    zr = jnp.zeros((d, LANE - D_NOPE - D_ROPE), w_in.dtype)
    w1 = jnp.concatenate([w_in[:, 0:o], zl, kr, zr, zl, _rotate_half_cols(kr), zr], axis=1)
    assert w1.shape[1] == _O_COLS
    dq = D_NOPE + D_ROPE
    uq = w_uq.reshape(Q_LORA, N_HEADS, dq)
    padq = jnp.zeros((Q_LORA, N_HEADS, LANE - dq), w_uq.dtype)
    wq1 = jnp.concatenate([uq, padq], axis=2).reshape(Q_LORA, N_HEADS * LANE)
    wq2 = jnp.concatenate([jnp.zeros((Q_LORA, N_HEADS, D_NOPE), w_uq.dtype),
                           _rotate_half_cols(uq[:, :, D_NOPE:]), padq], axis=2
                          ).reshape(Q_LORA, N_HEADS * LANE)
    ukv = w_ukv.reshape(KV_LORA, N_HEADS, D_NOPE + D_V)
    wk = jnp.concatenate([ukv[:, :, :D_NOPE], jnp.zeros((KV_LORA, N_HEADS, LANE - D_NOPE), w_ukv.dtype)],
                         axis=2).reshape(KV_LORA, N_HEADS * LANE)
    wvt = ukv[:, :, D_NOPE:].reshape(KV_LORA, N_HEADS * D_V).T
    return w1.astype(BF16), wq1.astype(BF16), wq2.astype(BF16), wk.astype(BF16), wvt.astype(BF16)


def _rope_lane_tables(seq):
    inv = ROPE_BASE ** (-jnp.arange(0, D_ROPE, 2, dtype=F32) / D_ROPE)
    ang = jnp.arange(seq, dtype=F32)[:, None] * inv[None, :]
    cos, sin = jnp.cos(ang), jnp.sin(ang)
    pad = jnp.zeros((seq, LANE - D_NOPE - D_ROPE), F32)
    cos_t = jnp.concatenate([jnp.ones((seq, D_NOPE), F32), cos, cos, pad], axis=1)
    sin_t = jnp.concatenate([jnp.zeros((seq, D_NOPE), F32), sin, sin, pad], axis=1)
    return cos_t, sin_t


def _odd_in_kernel(x_ref, mod_ref, w1_ref, wq1_ref, wq2_ref, wk_ref, wvt_ref, pw_ref, ps_ref,
                   qg_ref, kvg_ref, cos_ref, sin_ref,
                   yc_ref, q_ref, k_ref, vt_ref, carry_ref):
    d = D_MODEL
    i = pl.program_id(0)

    @pl.when(i == 0)
    def _():
        carry_ref[...] = jnp.zeros_like(carry_ref)

    h = (x_ref[...] * (1.0 + mod_ref[:, d:2 * d]) + mod_ref[:, 0:d]).astype(BF16)

    def proj(cols):
        return _dot(h, w1_ref[:, cols[0]:cols[1]])

    cos = cos_ref[...]
    sin = sin_ref[...]

    u = proj(_O_U)
    ext = jnp.concatenate([carry_ref[...], u], axis=0)
    pos = i * TM + lax.broadcasted_iota(jnp.int32, (TM, 1), 0)
    for g, win in enumerate(POOL_WINDOWS):
        sl = slice(g * C_GROUP_DIM, (g + 1) * C_GROUP_DIM)
        acc = ext[:, sl]
        shift = 1
        while shift < win:
            acc = acc + pltpu.roll(acc, shift, 0)
            shift *= 2
        cnt = jnp.minimum(pos + 1, win).astype(F32)
        pooled = acc[POOL_HALO:] / cnt - u[:, sl]
        mixed = _dot(pooled.astype(BF16), pw_ref[g])
        yc_ref[:, sl] = (mixed * ps_ref[:, sl]).astype(BF16)
    carry_ref[...] = u[TM - POOL_HALO:TM]

    r = _rms_norm(proj(_O_QLAT), qg_ref[...]).astype(BF16)
    qa = _dot(r, wq1_ref[...])
    qb = _dot(r, wq2_ref[...])
    kr = proj(_O_KR) * cos + proj(_O_KR_ROT) * sin
    rk = _rms_norm(proj(_O_KVLAT), kvg_ref[...]).astype(BF16)
    kn = _dot(rk, wk_ref[...])
    for hd in range(N_HEADS):
        sl = slice(hd * LANE, (hd + 1) * LANE)
        q_ref[:, sl] = ((qa[:, sl] * cos + qb[:, sl] * sin) * MLA_Q_SCALE).astype(BF16)
        k_ref[:, sl] = (kn[:, sl] + kr).astype(BF16)
    _store_values_t(vt_ref, _dot_nt(wvt_ref[...], rk))


def _odd_in(x, mod, w1, wq1, wq2, wk, wvt, pool_w, pool_scale, q_norm_g, kv_norm_g, cos_t, sin_t):
    s, d = x.shape
    row = lambda n: pl.BlockSpec((TM, n), lambda i: (i, 0))
    col = lambda n: pl.BlockSpec((n, TM), lambda i: (0, i))
    full = lambda a: pl.BlockSpec(a.shape, lambda i: (0,) * a.ndim)
    consts = [w1, wq1, wq2, wk, wvt, pool_w.astype(BF16), pool_scale.reshape(1, C_WIDTH),
              q_norm_g.reshape(1, Q_LORA), kv_norm_g.reshape(1, KV_LORA)]
    rows_out = lambda n, t: (row(n), jax.ShapeDtypeStruct((s, n), t))
    cols_out = lambda n, t: (col(n), jax.ShapeDtypeStruct((n, s), t))
    outs = [rows_out(C_WIDTH, BF16), rows_out(N_HEADS * LANE, BF16), rows_out(N_HEADS * LANE, BF16),
            cols_out(N_HEADS * V_ROWS, BF16)]
    return pl.pallas_call(
        _odd_in_kernel,
        grid=(s // TM,),
        in_specs=[row(d), full(mod)] + [full(a) for a in consts] + [row(LANE), row(LANE)],
        out_specs=[o[0] for o in outs],
        out_shape=[o[1] for o in outs],
        scratch_shapes=[pltpu.VMEM((POOL_HALO, C_WIDTH), F32)],
        compiler_params=_params(),
        name="odd_in",
    )(x, mod, *consts, cos_t, sin_t)


def kernel(x, c, ada_w, ada_b, ln_mix_g, ln_mix_b, ln_ffn_g, ln_ffn_b, ev_w_in, ev_conv_w, ev_w_out,
           od_w_in, pool_w, pool_scale, q_norm_g, w_uq, kv_norm_g, w_ukv, od_w_out,
           ffn_w_up, ffn_conv_w, ffn_w_down):
    bsz, seq, d = x.shape
    assert bsz == 1 and d == D_MODEL and seq % TM == 0 and seq % TK_ATT == 0 and seq % TQ_ATT == 0
    assert SHORT_CONV == 3 and FFN_CONV == 3
    xs = x.reshape(seq, d)
    mods = _adaln(c, ada_w, ada_b)
    for l in range(DEPTH):
        mod = mods[l]
        if l % 2 == 0:
            e = l // 2
            w, wt = _even_weights(ev_w_in[e])
            q, k, vt, qi, ki, wit, yb = _even_in(xs, mod, w, wt, ev_conv_w[e], _alibi_q_features())
            bias = _select(qi, ki, wit)
            mixed = (_flash(q, k, vt, bias, alibi=True), yb)
            w_out = ev_w_out[e]
        else:
            o = l // 2
            cos_t, sin_t = _rope_lane_tables(seq)
            ws = _odd_weights(od_w_in[o], w_uq[o], w_ukv[o])
            yc, q, k, vt = _odd_in(xs, mod, *ws, pool_w[o], pool_scale[o], q_norm_g[o], kv_norm_g[o],
                                   cos_t, sin_t)
            mixed = (yc, _flash(q, k, vt, None, alibi=False))
            w_out = od_w_out[o]
        xs = _mix_ffn(*mixed, xs, mod, w_out.astype(BF16), ln_mix_g[l], ln_mix_b[l],
                      ffn_w_up[l].astype(BF16), ffn_conv_w[l], ffn_w_down[l].astype(BF16),
                      ln_ffn_g[l], ln_ffn_b[l])
    return xs.reshape(bsz, seq, d)
```

```python
import functools

import numpy as np
import jax
import jax.numpy as jnp
from jax import lax
from jax.experimental import pallas as pl
from jax.experimental.pallas import tpu as pltpu

D_MODEL = 1024
DEPTH = 2
CHUNK = 64
N_HEADS = 8
A_HEAD_DIM = 64
A_WIDTH = N_HEADS * A_HEAD_DIM
IDX_DIM = 64
TOPK = 256
B_WIDTH = 512
SHORT_CONV = 3
C_WIDTH = 512
POOL_WINDOWS = (2, 4, 8, 16)
C_GROUP_DIM = C_WIDTH // len(POOL_WINDOWS)
D_NOPE = 64
D_ROPE = 32
D_V = 64
Q_LORA = 384
KV_LORA = 256
ROPE_BASE = 10000.0
D_FF = 2816
FFN_CONV = 3
LN_EPS = 1e-5
RMS_EPS = 1e-6
DN_ALPHA = (2 * DEPTH) ** 0.25
NEG = -1e30
IDX_W_SCALE = N_HEADS ** -0.5 * IDX_DIM ** -0.5
LOG2E = 1.4426950408889634
DSA_Q_SCALE = A_HEAD_DIM ** -0.5 * LOG2E
MLA_Q_SCALE = (D_NOPE + D_ROPE) ** -0.5 * LOG2E
ALIBI_SLOPES = tuple(2.0 ** (-8.0 * (i + 1) / N_HEADS) for i in range(N_HEADS))

LANE = 128
SUBLANE = 8
BF16_ROWS = 16
V_ROWS = D_V + BF16_ROWS
VMEM_LIMIT_BYTES = 56 * 1024 * 1024

TM = 512
FFN_CHUNK = 256
TQ_SEL = 256
TQ_ATT = 512
TK_SEL = 512
SCORE_GROUP = 4
TK_ATT = 1024
QK_AHEAD = 4
POOL_HALO = 16

INT_MIN = -2 ** 31
F32 = jnp.float32
BF16 = jnp.bfloat16


def _params():
    return pltpu.CompilerParams(dimension_semantics=("arbitrary",),
                                vmem_limit_bytes=VMEM_LIMIT_BYTES)


def _dot(a, b):
    return jnp.dot(a, b, preferred_element_type=F32)


def _dot_nt(a, b):
    return lax.dot_general(a, b, (((1,), (1,)), ((), ())), preferred_element_type=F32)


def _layer_norm(z, g, b):
    mu = jnp.mean(z, axis=-1, keepdims=True)
    zc = z - mu
    var = jnp.mean(zc * zc, axis=-1, keepdims=True)
    return zc * lax.rsqrt(var + LN_EPS) * g + b


def _rms_norm(z, g):
    return z * lax.rsqrt(jnp.mean(z * z, axis=-1, keepdims=True) + RMS_EPS) * g


def _causal_conv3(u, prev, w):
    w0, w1, w2 = w[0:1], w[1:2], w[2:3]
    y = pltpu.roll(u, 2, 0) * w0 + pltpu.roll(u, 1, 0) * w1 + u * w2
    head = u[0:SUBLANE]
    r = lax.broadcasted_iota(jnp.int32, (SUBLANE, 1), 0)
    h1 = jnp.where(r == 0, prev[7:8], pltpu.roll(head, 1, 0))
    h2 = jnp.where(r == 0, prev[6:7], jnp.where(r == 1, prev[7:8], pltpu.roll(head, 2, 0)))
    yh = h2 * w0 + h1 * w1 + head * w2
    return jnp.concatenate([yh, y[SUBLANE:]], axis=0)


def _fold_rows(c, rows_out):
    rows = c.shape[0]
    while rows > rows_out:
        rows //= 2
        c = c[0:rows] + c[rows:2 * rows]
    return c


def _store_values_t(vt_ref, vt):
    t = vt.shape[1]
    ones_row = jnp.where(lax.broadcasted_iota(jnp.int32, (BF16_ROWS, t), 0) == 0, 1.0, 0.0)
    for h in range(N_HEADS):
        vt_ref[h * V_ROWS:h * V_ROWS + D_V, :] = vt[h * D_V:(h + 1) * D_V, :].astype(BF16)
        vt_ref[h * V_ROWS + D_V:(h + 1) * V_ROWS, :] = ones_row.astype(BF16)


def _adaln_kernel(c_ref, w_ref, b_ref, o_ref):
    c = c_ref[...]
    cond = c * jax.nn.sigmoid(c)
    o_ref[0] = jnp.sum(cond * w_ref[0], axis=0, keepdims=True) + b_ref[0]


def _adaln(c, ada_w, ada_b):
    depth, d, n = ada_w.shape
    tn = 1536
    return pl.pallas_call(
        _adaln_kernel,
        grid=(depth, n // tn),
        in_specs=[pl.BlockSpec((d, 1), lambda l, j: (0, 0)),
                  pl.BlockSpec((1, d, tn), lambda l, j: (l, 0, j)),
                  pl.BlockSpec((1, 1, tn), lambda l, j: (l, 0, j))],
        out_specs=pl.BlockSpec((1, 1, tn), lambda l, j: (l, 0, j)),
        out_shape=jax.ShapeDtypeStruct((depth, 1, n), F32),
        compiler_params=pltpu.CompilerParams(dimension_semantics=("arbitrary", "arbitrary"),
                                             vmem_limit_bytes=VMEM_LIMIT_BYTES),
        name="adaln",
    )(c.reshape(d, 1), ada_w, ada_b.reshape(depth, 1, n))


_E_Q = (0, 1024)
_E_K = (1024, 2048)
_E_QI = (2048, 3072)
_E_KI = (3072, 3200)
_E_BG = (3200, 3712)
_E_CG = (3712, 4224)
_E_XB = (4224, 4736)
_E_COLS = 4736
_ET_V = (0, 512)
_ET_WI = (512, 528)
_ET_ROWS = 528
_FEAT_LANE = A_HEAD_DIM
_N_LOG2E_TERMS = 3
_POS_RADIX = 128


def _pad_heads(w):
    d = w.shape[0]
    w3 = w.reshape(d, N_HEADS, A_HEAD_DIM)
    return jnp.concatenate([w3, jnp.zeros_like(w3)], axis=2).reshape(d, N_HEADS * LANE)


def _alibi_q_features():
    terms, rest = [], np.float64(LOG2E)
    for _ in range(_N_LOG2E_TERMS):
        t = np.float64(np.asarray(rest, np.float32).astype(jnp.bfloat16).astype(np.float32))
        terms.append(t)
        rest = rest - t
    row = np.zeros((N_HEADS, LANE), np.float32)
    for h in range(N_HEADS):
        for n, t in enumerate(terms):
            row[h, _FEAT_LANE + n] = ALIBI_SLOPES[h] * _POS_RADIX * t
            row[h, _FEAT_LANE + _N_LOG2E_TERMS + n] = ALIBI_SLOPES[h] * t
    return jnp.asarray(row.reshape(1, N_HEADS * LANE))


def _even_weights(w_in):
    d = w_in.shape[0]
    a = A_WIDTH
    q = _pad_heads(w_in[:, 0:a])
    k = _pad_heads(w_in[:, a:2 * a])
    v = w_in[:, 2 * a:3 * a]
    qi = _pad_heads(w_in[:, 3 * a:4 * a])
    o = 4 * a
    ki = w_in[:, o:o + IDX_DIM]
    wi = w_in[:, o + IDX_DIM:o + IDX_DIM + N_HEADS]
    o = o + IDX_DIM + N_HEADS
    rest = w_in[:, o:o + 3 * B_WIDTH]
    w = jnp.concatenate([q, k, qi, ki, ki, rest], axis=1)
    assert w.shape[1] == _E_COLS
    wt = jnp.concatenate([v.T, wi.T, jnp.zeros((BF16_ROWS - N_HEADS, d), w_in.dtype)], axis=0)
    assert wt.shape[0] == _ET_ROWS
    return w.astype(BF16), wt.astype(BF16)


def _even_in_kernel(x_ref, mod_ref, w_ref, wt_ref, cw_ref, qf_ref,
                    q_ref, k_ref, vt_ref, qi_ref, ki_ref, wit_ref, yb_ref, carry_ref):
    d = D_MODEL
    i = pl.program_id(0)

    @pl.when(i == 0)
    def _():
        carry_ref[...] = jnp.zeros_like(carry_ref)

    h = (x_ref[...] * (1.0 + mod_ref[:, d:2 * d]) + mod_ref[:, 0:d]).astype(BF16)

    def proj(cols):
        return _dot(h, w_ref[:, cols[0]:cols[1]])

    def proj_t(rows):
        return _dot_nt(wt_ref[rows[0]:rows[1], :], h)

    q_ref[...] = (proj(_E_Q) * DSA_Q_SCALE + qf_ref[...]).astype(BF16)
    pos = i * TM + lax.broadcasted_iota(jnp.int32, (TM, LANE), 0)
    lane = lax.broadcasted_iota(jnp.int32, (TM, LANE), 1) - _FEAT_LANE
    pos_hi = (pos // _POS_RADIX).astype(F32)
    pos_lo = (pos % _POS_RADIX).astype(F32)
    kfeat = jnp.where(lane < 0, 0.0,
                      jnp.where(lane < _N_LOG2E_TERMS, pos_hi,
                                jnp.where(lane < 2 * _N_LOG2E_TERMS, pos_lo, 0.0)))
    kproj = proj(_E_K)
    for hd in range(N_HEADS):
        sl = slice(hd * LANE, (hd + 1) * LANE)
        k_ref[:, sl] = (kproj[:, sl] + kfeat).astype(BF16)
    _store_values_t(vt_ref, proj_t(_ET_V))
    qi_ref[...] = proj(_E_QI).astype(BF16)
    ki_ref[...] = proj(_E_KI).astype(BF16)
    wit_ref[...] = proj_t(_ET_WI) * IDX_W_SCALE
    g = proj(_E_CG) * proj(_E_XB)
    y = _causal_conv3(g, carry_ref[...], cw_ref[...])
    yb_ref[...] = (proj(_E_BG) * y).astype(BF16)
    carry_ref[...] = g[TM - SUBLANE:TM]


def _even_in(x, mod, w, wt, conv_w, qfeat):
    s, d = x.shape
    row = lambda n: pl.BlockSpec((TM, n), lambda i: (i, 0))
    col = lambda n: pl.BlockSpec((n, TM), lambda i: (0, i))
    full = lambda a: pl.BlockSpec(a.shape, lambda i: (0,) * a.ndim)
    rows_out = lambda n, t: (row(n), jax.ShapeDtypeStruct((s, n), t))
    cols_out = lambda n, t: (col(n), jax.ShapeDtypeStruct((n, s), t))
    outs = [rows_out(N_HEADS * LANE, BF16), rows_out(N_HEADS * LANE, BF16), cols_out(N_HEADS * V_ROWS, BF16),
            rows_out(N_HEADS * LANE, BF16), rows_out(LANE, BF16), cols_out(BF16_ROWS, F32),
            rows_out(B_WIDTH, BF16)]
    return pl.pallas_call(
        _even_in_kernel,
        grid=(s // TM,),
        in_specs=[row(d), full(mod), full(w), full(wt), full(conv_w), full(qfeat)],
        out_specs=[o[0] for o in outs],
        out_shape=[o[1] for o in outs],
        scratch_shapes=[pltpu.VMEM((SUBLANE, B_WIDTH), F32)],
        compiler_params=_params(),
        name="even_in",
    )(x, mod, w, wt, conv_w, qfeat)


def _select_kernel(qi_ref, ki_ref, wt_ref, bias_ref, hi_ref, lo_ref, m1_ref, m2_ref, mc_ref, my_ref, tc_ref,
                   *, seq):
    tq, tk = TQ_SEL, TK_SEL
    i16 = jnp.int16
    i = pl.program_id(0)
    start = i * tq
    n_tiles = (start + tq + tk - 1) // tk
    tpos = start + lax.broadcasted_iota(jnp.int32, (1, tq), 1)
    limit = (tpos // CHUNK + 1) * CHUNK
    int_min = jnp.int32(INT_MIN)
    dmin, dmax = -2 ** 15, 2 ** 15 - 1
    zero16, one16 = jnp.zeros((), i16), jnp.ones((), i16)
    sel0, seln = jnp.zeros((), BF16), jnp.full((), NEG, BF16)

    def key_pos(off):
        return off + lax.broadcasted_iota(jnp.int32, (tk, tq), 0)

    def score_tile(j, carry, *, masked, tiles=1):
        for t in range(tiles):
            score_rows(pl.multiple_of((j * tiles + t) * tk, tk), masked)
        return carry

    def score_rows(off, masked):
        kt = ki_ref[pl.ds(off, tk), :]
        score = jnp.zeros((tk, tq), F32)
        for h in range(N_HEADS):
            rel = _dot_nt(kt, qi_ref[:, h * LANE:(h + 1) * LANE])
            score = score + wt_ref[h:h + 1, :] * jnp.maximum(rel, 0.0)
        bits = pltpu.bitcast(score, jnp.int32)
        key = bits ^ ((bits >> 31) & jnp.int32(0x7FFFFFFF))
        key = jnp.where(bits == int_min, 0, key)
        if masked:
            key = jnp.where(key_pos(off) < limit, key, int_min)
        hi_ref[pl.ds(off, tk), :] = (key >> 16).astype(i16)
        lo_ref[pl.ds(off, tk), :] = key.astype(i16) ^ jnp.asarray(dmin, i16)

    n_before = start // tk
    n_groups = n_before // SCORE_GROUP
    lax.fori_loop(0, n_groups, functools.partial(score_tile, masked=False, tiles=SCORE_GROUP), 0)
    lax.fori_loop(SCORE_GROUP * n_groups, n_before, functools.partial(score_tile, masked=False), 0)
    lax.fori_loop(n_before, n_tiles, functools.partial(score_tile, masked=True), 0)

    def rows_at(ref, off, rows):
        return ref[pl.ds(off, rows), :]

    def for_key_tiles(fn, carry):
        return lax.fori_loop(0, n_tiles, lambda j, c: fn(pl.multiple_of(j * tk, tk), tk, c), carry)

    def per_query(c):
        return jnp.sum(c.astype(jnp.int32).astype(F32), axis=0, keepdims=True)

    def count(flag):
        def body(off, rows, acc):
            return acc + _fold_rows(flag(off, rows), BF16_ROWS)
        return per_query(for_key_tiles(body, jnp.zeros((BF16_ROWS, tq), i16)))

    def radix_search(count_ge, need):
        def bit_step(b, thr):
            cand = thr + jnp.left_shift(jnp.int32(1), 15 - b)
            return jnp.where(count_ge(cand.astype(i16)) >= need, cand, thr)
        return lax.fori_loop(0, 16, bit_step, jnp.full((1, tq), dmin, jnp.int32))

    topk = float(TOPK)
    t_hi = radix_search(lambda c16: count(
        lambda off, rows: jnp.where(rows_at(hi_ref, off, rows) >= c16, one16, zero16)), topk)
    t_hi16 = t_hi.astype(i16)
    n_above = count(lambda off, rows: jnp.where(rows_at(hi_ref, off, rows) > t_hi16, one16, zero16))
    need_lo = topk - n_above

    m1_ref[...] = jnp.full((tk, tq), dmin, i16)
    m2_ref[...] = jnp.full((tk, tq), dmin, i16)
    mc_ref[...] = jnp.zeros((tk, tq), i16)

    def fold_members(off, rows, carry):
        member = rows_at(hi_ref, off, rows) == t_hi16
        x = jnp.where(member, rows_at(lo_ref, off, rows), jnp.asarray(dmin, i16))
        lo_ref[pl.ds(off, rows), :] = x
        a, b = m1_ref[...], m2_ref[...]
        above = x > a
        m1_ref[...] = jnp.where(above, x, a)
        second = jnp.where(above, a, x)
        m2_ref[...] = jnp.where(second > b, second, b)
        mc_ref[...] = mc_ref[...] + jnp.where(member, one16, zero16)
        return carry

    for_key_tiles(fold_members, 0)

    def count_slots(flag):
        return per_query(_fold_rows(flag(m1_ref[...]) + flag(m2_ref[...]), BF16_ROWS))

    def level2_slots():
        t = radix_search(lambda c16: count_slots(lambda m: jnp.where(m >= c16, one16, zero16)), need_lo)
        t16 = t.astype(i16)
        n_members = per_query(_fold_rows(mc_ref[...], BF16_ROWS))
        gt = count_slots(lambda m: jnp.where(m > t16, one16, zero16))
        ge = jnp.where(t == dmin, n_members, count_slots(lambda m: jnp.where(m >= t16, one16, zero16)))
        return t, gt, ge

    def level2_full():
        t = radix_search(lambda c16: count(
            lambda off, rows: jnp.where(rows_at(lo_ref, off, rows) >= c16, one16, zero16)), need_lo)
        t16 = t.astype(i16)
        gt = count(lambda off, rows: jnp.where(rows_at(lo_ref, off, rows) > t16, one16, zero16))
        ge = count(lambda off, rows: jnp.where(
            rows_at(hi_ref, off, rows) == t_hi16,
            jnp.where(rows_at(lo_ref, off, rows) >= t16, one16, zero16), zero16))
        return t, gt, ge

    crowded = per_query(_fold_rows(jnp.where(mc_ref[...] > jnp.asarray(2, i16), one16, zero16), BF16_ROWS))
    t_lo, n_gt_lo, n_ge_lo = lax.cond(jnp.max(crowded) > 0.0, level2_full, level2_slots)
    t_lo16 = t_lo.astype(i16)
    n_gt = n_above + n_gt_lo
    n_ge = n_above + n_ge_lo
    room = topk - n_gt
    sentinel = t_hi == dmin
    has_ties = jnp.where(sentinel, 0.0, jnp.where(n_ge > topk, 1.0, 0.0))
    any_ties = jnp.max(has_ties) > 0.0

    def pos16(off, rows):
        return (off + lax.broadcasted_iota(jnp.int32, (rows, tq), 0)).astype(i16)

    def tie_cutoff():
        def tied_rows(j):
            off = pl.multiple_of(j * tk, tk)
            return jnp.where(rows_at(hi_ref, off, tk) == t_hi16,
                             jnp.where(rows_at(lo_ref, off, tk) == t_lo16, one16, zero16), zero16)

        def count_tile(j, carry):
            tc_ref[pl.ds(j, 1), :] = per_query(_fold_rows(tied_rows(j), BF16_ROWS))
            return carry
        lax.fori_loop(0, n_tiles, count_tile, 0)

        def find_tile(j, carry):
            seen, tile_of, left = carry
            here = tc_ref[pl.ds(j, 1), :]
            crossing = jnp.logical_and(seen < room, seen + here >= room)
            return (seen + here, jnp.where(crossing, j, tile_of), jnp.where(crossing, room - seen, left))
        _, tile_of, left = lax.fori_loop(
            0, n_tiles, find_tile, (jnp.zeros((1, tq), F32), jnp.zeros((1, tq), jnp.int32), room))

        my_ref[...] = jnp.zeros((tk, tq), i16)
        tile_of16 = tile_of.astype(i16)

        def gather_tile(j, carry):
            mine = jnp.where(tile_of16 == j.astype(i16), jnp.asarray(-1, i16), zero16)
            my_ref[...] = my_ref[...] | (tied_rows(j) & mine)
            return carry
        lax.fori_loop(0, n_tiles, gather_tile, 0)

        row16 = lax.broadcasted_iota(jnp.int32, (tk, tq), 0).astype(i16)
        row_bits = tk.bit_length()

        def row_step(b, cut):
            cand = cut + jnp.left_shift(jnp.int32(1), row_bits - 1 - b)
            below = jnp.where(row16 < cand.astype(i16), my_ref[...], zero16)
            return jnp.where(per_query(_fold_rows(below, BF16_ROWS)) <= left, cand, cut)
        row_cut = lax.fori_loop(0, row_bits, row_step, jnp.zeros((1, tq), jnp.int32))
        return tile_of * tk + jnp.minimum(row_cut, tk)

    def write_plain():
        lo_min16 = jnp.where(sentinel, dmax, t_lo).astype(i16)

        def write_rows(off, rows, carry):
            hi = rows_at(hi_ref, off, rows)
            inner = jnp.where(rows_at(lo_ref, off, rows) >= lo_min16, sel0, seln)
            bias_ref[pl.ds(off, rows), :] = jnp.where(hi > t_hi16, sel0,
                                                      jnp.where(hi == t_hi16, inner, seln))
            return carry
        for_key_tiles(write_rows, 0)

    def write_with_ties():
        cut16 = jnp.where(sentinel, 0, tie_cutoff()).astype(i16)

        def write_rows(off, rows, carry):
            hi = rows_at(hi_ref, off, rows)
            lo = rows_at(lo_ref, off, rows)
            tie = jnp.where(pos16(off, rows) < cut16, sel0, seln)
            inner = jnp.where(lo > t_lo16, sel0, jnp.where(lo == t_lo16, tie, seln))
            bias_ref[pl.ds(off, rows), :] = jnp.where(hi > t_hi16, sel0,
                                                      jnp.where(hi == t_hi16, inner, seln))
            return carry
        for_key_tiles(write_rows, 0)

    lax.cond(any_ties, write_with_ties, write_plain)

    def fill_tile(j, carry):
        off = pl.multiple_of(j * tk, tk)
        bias_ref[pl.ds(off, tk), :] = jnp.full((tk, tq), NEG, BF16)
        return carry

    lax.fori_loop(n_tiles, seq // tk, fill_tile, 0)


def _select(qi, ki, wit):
    s = qi.shape[0]
    assert s % TK_SEL == 0 and s % TQ_SEL == 0 and 2 * TOPK <= s <= 32767
    return pl.pallas_call(
        functools.partial(_select_kernel, seq=s),
        grid=(s // TQ_SEL,),
        in_specs=[pl.BlockSpec((TQ_SEL, N_HEADS * LANE), lambda i: (i, 0)),
                  pl.BlockSpec((s, LANE), lambda i: (0, 0)),
                  pl.BlockSpec((BF16_ROWS, TQ_SEL), lambda i: (0, i))],
        out_specs=pl.BlockSpec((s, TQ_SEL), lambda i: (0, i)),
        out_shape=jax.ShapeDtypeStruct((s, s), BF16),
        scratch_shapes=[pltpu.VMEM((s, TQ_SEL), jnp.int16), pltpu.VMEM((s, TQ_SEL), jnp.int16)]
        + [pltpu.VMEM((TK_SEL, TQ_SEL), jnp.int16)] * 4 + [pltpu.VMEM((s // TK_SEL, TQ_SEL), F32)],
        compiler_params=_params(),
        name="dsa_select",
    )(qi, ki, wit)


def _flash_kernel(qi_of, kj_of, *refs, alibi, use_bias):
    if use_bias:
        q_ref, k_ref, vt_ref, bias_ref, o_ref, m_ref, acc_ref = refs
    else:
        q_ref, k_ref, vt_ref, o_ref, m_ref, acc_ref = refs
    tq, tk = TQ_ATT, TK_ATT
    p = pl.program_id(0)
    qi = qi_of[p]
    kj = kj_of[p]
    last = ((qi + 1) * tq - 1) // tk

    @pl.when(kj == 0)
    def _():
        m_ref[...] = jnp.full_like(m_ref, NEG)
        acc_ref[...] = jnp.zeros_like(acc_ref)

    def step(diagonal, live=tk):
        base = bias_ref[0:live, :].astype(F32) if use_bias else None
        if diagonal:
            tpos = qi * tq + lax.broadcasted_iota(jnp.int32, (live, tq), 1)
            spos = kj * tk + lax.broadcasted_iota(jnp.int32, (live, tq), 0)
            if not use_bias:
                base = jnp.where(spos < (tpos // CHUNK + 1) * CHUNK, 0.0, NEG)
            if alibi:
                ahead = jnp.maximum(spos - tpos, 0).astype(F32)

        def logits(h):
            s = _dot_nt(k_ref[0:live, h * LANE:(h + 1) * LANE], q_ref[:, h * LANE:(h + 1) * LANE])
            if diagonal and alibi:
                return s + (base - (2.0 * ALIBI_SLOPES[h] * LOG2E) * ahead)
            return s if base is None else s + base

        pending = [logits(h) for h in range(QK_AHEAD)]
        for h in range(N_HEADS):
            rows = slice(h * V_ROWS, (h + 1) * V_ROWS)
            if h + QK_AHEAD < N_HEADS:
                pending.append(logits(h + QK_AHEAD))
            s = pending.pop(0)
            m_prev = m_ref[h:h + 1, :]
            m_new = jnp.maximum(m_prev, jnp.max(s, axis=0, keepdims=True))
            alpha = jnp.exp2(m_prev - m_new)
            pexp = jnp.exp2(s - m_new)
            acc_ref[rows, :] = alpha * acc_ref[rows, :] + _dot(vt_ref[rows, 0:live], pexp.astype(BF16))
            m_ref[h:h + 1, :] = m_new

    @pl.when(kj != last)
    def _():
        step(False)

    blocks_per_tile = max(tk // tq, 1)
    for v in range(blocks_per_tile):
        @pl.when(jnp.logical_and(kj == last, qi % blocks_per_tile == v))
        def _(v=v):
            step(True, live=min((v + 1) * tq, tk))

    @pl.when(kj == last)
    def _():
        outs = [acc_ref[h * V_ROWS:h * V_ROWS + D_V, :] / acc_ref[h * V_ROWS + D_V:h * V_ROWS + D_V + 1, :]
                for h in range(N_HEADS)]
        o_ref[...] = jnp.concatenate(outs, axis=0).T.astype(BF16)


def _flash(q, k, vt, bias, *, alibi):
    s = q.shape[0]
    nq = s // TQ_ATT
    pairs = [(i, j) for i in range(nq) for j in range(((i + 1) * TQ_ATT - 1) // TK_ATT + 1)]
    qi_of = jnp.asarray(np.array([a for a, _ in pairs], np.int32))
    kj_of = jnp.asarray(np.array([b for _, b in pairs], np.int32))
    use_bias = bias is not None
    in_specs = [pl.BlockSpec((TQ_ATT, N_HEADS * LANE), lambda p, qi, kj: (qi[p], 0)),
                pl.BlockSpec((TK_ATT, k.shape[1]), lambda p, qi, kj: (kj[p], 0)),
                pl.BlockSpec((N_HEADS * V_ROWS, TK_ATT), lambda p, qi, kj: (0, kj[p]))]
    args = [q, k, vt]
    if use_bias:
        in_specs.append(pl.BlockSpec((TK_ATT, TQ_ATT), lambda p, qi, kj: (kj[p], qi[p])))
        args.append(bias)
    kern = functools.partial(_flash_kernel, alibi=alibi, use_bias=use_bias)
    return pl.pallas_call(
        kern,
        grid_spec=pltpu.PrefetchScalarGridSpec(
            num_scalar_prefetch=2,
            grid=(len(pairs),),
            in_specs=in_specs,
            out_specs=pl.BlockSpec((TQ_ATT, N_HEADS * D_V), lambda p, qi, kj: (qi[p], 0)),
            scratch_shapes=[pltpu.VMEM((N_HEADS, TQ_ATT), F32),
                            pltpu.VMEM((N_HEADS * V_ROWS, TQ_ATT), F32)]),
        out_shape=jax.ShapeDtypeStruct((s, N_HEADS * D_V), BF16),
        compiler_params=_params(),
        name="flash_dsa" if use_bias else "flash_mla",
    )(qi_of, kj_of, *args)


def _mix_ffn_kernel(ya_ref, yb_ref, x_ref, mod_ref, wo_ref, gm_ref, bm_ref,
                    wup_ref, cw_ref, wdn_ref, gf_ref, bf_ref, o_ref, carry_ref, act_ref):
    d = D_MODEL

    @pl.when(pl.program_id(0) == 0)
    def _():
        carry_ref[...] = jnp.zeros_like(carry_ref)

    na = ya_ref.shape[1]
    y = _dot(ya_ref[...], wo_ref[0:na, :]) + _dot(yb_ref[...], wo_ref[na:, :])
    x = _layer_norm(DN_ALPHA * x_ref[...] + (1.0 + mod_ref[:, 2 * d:3 * d]) * y, gm_ref[...], bm_ref[...])
    h = (x * (1.0 + mod_ref[:, 4 * d:5 * d]) + mod_ref[:, 3 * d:4 * d]).astype(BF16)
    for c in range(D_FF // FFN_CHUNK):
        va = c * FFN_CHUNK
        ga = D_FF + va
        uv = _dot(h, wup_ref[:, va:va + FFN_CHUNK])
        ug = _dot(h, wup_ref[:, ga:ga + FFN_CHUNK])
        val = _causal_conv3(uv, carry_ref[:, va:va + FFN_CHUNK], cw_ref[:, va:va + FFN_CHUNK])
        gate = _causal_conv3(ug, carry_ref[:, ga:ga + FFN_CHUNK], cw_ref[:, ga:ga + FFN_CHUNK])
        carry_ref[:, va:va + FFN_CHUNK] = uv[TM - SUBLANE:TM]
        carry_ref[:, ga:ga + FFN_CHUNK] = ug[TM - SUBLANE:TM]
        act_ref[:, va:va + FFN_CHUNK] = (gate * jax.nn.sigmoid(gate) * val).astype(BF16)
    y = _dot(act_ref[...], wdn_ref[...])
    z = DN_ALPHA * x + (1.0 + mod_ref[:, 5 * d:6 * d]) * y
    o_ref[...] = _layer_norm(z, gf_ref[...], bf_ref[...])


def _mix_ffn(ya, yb, x, mod, w_out, g_mix, b_mix, layer, w_up_all, conv_w_all, w_down_all, g_ffn, b_ffn):
    s, d = x.shape
    assert D_FF % FFN_CHUNK == 0
    row = lambda n: pl.BlockSpec((TM, n), lambda i: (i, 0))
    full = lambda a: pl.BlockSpec(a.shape, lambda i: (0,) * a.ndim)
    once = lambda a: pl.BlockSpec(a.shape, lambda i: (0,) * a.ndim, pipeline_mode=pl.Buffered(1))
    of_layer = lambda a, **kw: pl.BlockSpec((None,) + a.shape[1:], lambda i: (layer, 0, 0), **kw)
    vec = lambda a: a.reshape(1, d)
    args = [ya, yb, x, mod, w_out, vec(g_mix), vec(b_mix), w_up_all, conv_w_all, w_down_all,
            vec(g_ffn), vec(b_ffn)]
    specs = [row(ya.shape[1]), row(yb.shape[1]), row(d), full(mod), once(w_out), full(args[5]), full(args[6]),
             of_layer(w_up_all, pipeline_mode=pl.Buffered(1)), of_layer(conv_w_all),
             of_layer(w_down_all, pipeline_mode=pl.Buffered(1)), full(args[10]), full(args[11])]
    return pl.pallas_call(
        _mix_ffn_kernel,
        grid=(s // TM,),
        in_specs=specs,
        out_specs=row(d),
        out_shape=jax.ShapeDtypeStruct((s, d), F32),
        scratch_shapes=[pltpu.VMEM((SUBLANE, 2 * D_FF), F32), pltpu.VMEM((TM, D_FF), BF16)],
        compiler_params=_params(),
        name="mix_ffn",
    )(*args)


_O_U = (0, 512)
_O_QLAT = (512, 896)
_O_KVLAT = (896, 1152)
_O_KR = (1152, 1280)
_O_KR_ROT = (1280, 1408)
_O_COLS = 1408


def _rotate_half_cols(w):
    half = w.shape[-1] // 2
    return jnp.concatenate([-w[..., half:], w[..., :half]], axis=-1)


def _odd_weights(w_in, w_uq, w_ukv):
    d = w_in.shape[0]
    o = C_WIDTH + Q_LORA + KV_LORA
    kr = w_in[:, o:o + D_ROPE]
    zl = jnp.zeros((d, D_NOPE), w_in.dtype)
    zr = jnp.zeros((d, LANE - D_NOPE - D_ROPE), w_in.dtype)
    w1 = jnp.concatenate([w_in[:, 0:o], zl, kr, zr, zl, _rotate_half_cols(kr), zr], axis=1)
    assert w1.shape[1] == _O_COLS
    dq = D_NOPE + D_ROPE
    uq = w_uq.reshape(Q_LORA, N_HEADS, dq)
    padq = jnp.zeros((Q_LORA, N_HEADS, LANE - dq), w_uq.dtype)
    wq1 = jnp.concatenate([uq, padq], axis=2).reshape(Q_LORA, N_HEADS * LANE)
    wq2 = jnp.concatenate([jnp.zeros((Q_LORA, N_HEADS, D_NOPE), w_uq.dtype),
                           _rotate_half_cols(uq[:, :, D_NOPE:]), padq], axis=2
                          ).reshape(Q_LORA, N_HEADS * LANE)
    ukv = w_ukv.reshape(KV_LORA, N_HEADS, D_NOPE + D_V)
    wk = jnp.concatenate([ukv[:, :, :D_NOPE], jnp.zeros((KV_LORA, N_HEADS, LANE - D_NOPE), w_ukv.dtype)],
                         axis=2).reshape(KV_LORA, N_HEADS * LANE)
    wvt = ukv[:, :, D_NOPE:].reshape(KV_LORA, N_HEADS * D_V).T
    return w1.astype(BF16), wq1.astype(BF16), wq2.astype(BF16), wk.astype(BF16), wvt.astype(BF16)


def _rope_lane_tables(seq):
    inv = ROPE_BASE ** (-jnp.arange(0, D_ROPE, 2, dtype=F32) / D_ROPE)
    ang = jnp.arange(seq, dtype=F32)[:, None] * inv[None, :]
    cos, sin = lax.optimization_barrier((jnp.cos(ang), jnp.sin(ang)))
    pad = jnp.zeros((seq, LANE - D_NOPE - D_ROPE), F32)
    cos_t = jnp.concatenate([jnp.ones((seq, D_NOPE), F32), cos, cos, pad], axis=1)
    sin_t = jnp.concatenate([jnp.zeros((seq, D_NOPE), F32), sin, sin, pad], axis=1)
    return cos_t, sin_t


def _odd_in_kernel(x_ref, mod_ref, w1_ref, wq1_ref, wq2_ref, wk_ref, wvt_ref, pw_ref, ps_ref,
                   qg_ref, kvg_ref, cos_ref, sin_ref,
                   yc_ref, q_ref, k_ref, vt_ref, carry_ref):
    d = D_MODEL
    i = pl.program_id(0)

    @pl.when(i == 0)
    def _():
        carry_ref[...] = jnp.zeros_like(carry_ref)

    h = (x_ref[...] * (1.0 + mod_ref[:, d:2 * d]) + mod_ref[:, 0:d]).astype(BF16)

    def proj(cols):
        return _dot(h, w1_ref[:, cols[0]:cols[1]])

    cos = cos_ref[...]
    sin = sin_ref[...]

    u = proj(_O_U)
    ext = jnp.concatenate([carry_ref[...], u], axis=0)
    pos = i * TM + lax.broadcasted_iota(jnp.int32, (TM, 1), 0)
    for g, win in enumerate(POOL_WINDOWS):
        sl = slice(g * C_GROUP_DIM, (g + 1) * C_GROUP_DIM)
        acc = ext[:, sl]
        shift = 1
        while shift < win:
            acc = acc + pltpu.roll(acc, shift, 0)
            shift *= 2
        cnt = jnp.minimum(pos + 1, win).astype(F32)
        pooled = acc[POOL_HALO:] / cnt - u[:, sl]
        mixed = _dot(pooled.astype(BF16), pw_ref[g])
        yc_ref[:, sl] = (mixed * ps_ref[:, sl]).astype(BF16)
    carry_ref[...] = u[TM - POOL_HALO:TM]

    r = _rms_norm(proj(_O_QLAT), qg_ref[...]).astype(BF16)
    qa = _dot(r, wq1_ref[...])
    qb = _dot(r, wq2_ref[...])
    kr = proj(_O_KR) * cos + proj(_O_KR_ROT) * sin
    rk = _rms_norm(proj(_O_KVLAT), kvg_ref[...]).astype(BF16)
    kn = _dot(rk, wk_ref[...])
    for hd in range(N_HEADS):
        sl = slice(hd * LANE, (hd + 1) * LANE)
        q_ref[:, sl] = ((qa[:, sl] * cos + qb[:, sl] * sin) * MLA_Q_SCALE).astype(BF16)
        k_ref[:, sl] = (kn[:, sl] + kr).astype(BF16)
    _store_values_t(vt_ref, _dot_nt(wvt_ref[...], rk))


def _odd_in(x, mod, w1, wq1, wq2, wk, wvt, pool_w, pool_scale, q_norm_g, kv_norm_g, cos_t, sin_t):
    s, d = x.shape
    row = lambda n: pl.BlockSpec((TM, n), lambda i: (i, 0))
    col = lambda n: pl.BlockSpec((n, TM), lambda i: (0, i))
    full = lambda a: pl.BlockSpec(a.shape, lambda i: (0,) * a.ndim)
    consts = [w1, wq1, wq2, wk, wvt, pool_w.astype(BF16), pool_scale.reshape(1, C_WIDTH),
              q_norm_g.reshape(1, Q_LORA), kv_norm_g.reshape(1, KV_LORA)]
    rows_out = lambda n, t: (row(n), jax.ShapeDtypeStruct((s, n), t))
    cols_out = lambda n, t: (col(n), jax.ShapeDtypeStruct((n, s), t))
    outs = [rows_out(C_WIDTH, BF16), rows_out(N_HEADS * LANE, BF16), rows_out(N_HEADS * LANE, BF16),
            cols_out(N_HEADS * V_ROWS, BF16)]
    return pl.pallas_call(
        _odd_in_kernel,
        grid=(s // TM,),
        in_specs=[row(d), full(mod)] + [full(a) for a in consts] + [row(LANE), row(LANE)],
        out_specs=[o[0] for o in outs],
        out_shape=[o[1] for o in outs],
        scratch_shapes=[pltpu.VMEM((POOL_HALO, C_WIDTH), F32)],
        compiler_params=_params(),
        name="odd_in",
    )(x, mod, *consts, cos_t, sin_t)


def kernel(x, c, ada_w, ada_b, ln_mix_g, ln_mix_b, ln_ffn_g, ln_ffn_b, ev_w_in, ev_conv_w, ev_w_out,
           od_w_in, pool_w, pool_scale, q_norm_g, w_uq, kv_norm_g, w_ukv, od_w_out,
           ffn_w_up, ffn_conv_w, ffn_w_down):
    bsz, seq, d = x.shape
    assert bsz == 1 and d == D_MODEL and seq % TM == 0 and seq % TK_ATT == 0 and seq % TQ_ATT == 0
    assert SHORT_CONV == 3 and FFN_CONV == 3
    xs = x.reshape(seq, d)
    mods = _adaln(c, ada_w, ada_b)
    w_up_bf16, w_down_bf16 = ffn_w_up.astype(BF16), ffn_w_down.astype(BF16)
    for l in range(DEPTH):
        mod = mods[l]
        if l % 2 == 0:
            e = l // 2
            w, wt = _even_weights(ev_w_in[e])
            q, k, vt, qi, ki, wit, yb = _even_in(xs, mod, w, wt, ev_conv_w[e], _alibi_q_features())
            bias = _select(qi, ki, wit)
            mixed = (_flash(q, k, vt, bias, alibi=True), yb)
            w_out = ev_w_out[e]
        else:
            o = l // 2
            cos_t, sin_t = _rope_lane_tables(seq)
            ws = _odd_weights(od_w_in[o], w_uq[o], w_ukv[o])
            yc, q, k, vt = _odd_in(xs, mod, *ws, pool_w[o], pool_scale[o], q_norm_g[o], kv_norm_g[o],
                                   cos_t, sin_t)
            mixed = (yc, _flash(q, k, vt, None, alibi=False))
            w_out = od_w_out[o]
        xs = _mix_ffn(*mixed, xs, mod, w_out.astype(BF16), ln_mix_g[l], ln_mix_b[l],
                      l, w_up_bf16, ffn_conv_w, w_down_bf16, ln_ffn_g[l], ln_ffn_b[l])
    return xs.reshape(bsz, seq, d)
```

```python
import functools

import numpy as np
import jax
import jax.numpy as jnp
from jax import lax
from jax.experimental import pallas as pl
from jax.experimental.pallas import tpu as pltpu

D_MODEL = 1024
DEPTH = 2
CHUNK = 64
N_HEADS = 8
A_HEAD_DIM = 64
A_WIDTH = N_HEADS * A_HEAD_DIM
IDX_DIM = 64
TOPK = 256
B_WIDTH = 512
SHORT_CONV = 3
C_WIDTH = 512
POOL_WINDOWS = (2, 4, 8, 16)
C_GROUP_DIM = C_WIDTH // len(POOL_WINDOWS)
D_NOPE = 64
D_ROPE = 32
D_V = 64
Q_LORA = 384
KV_LORA = 256
ROPE_BASE = 10000.0
D_FF = 2816
FFN_CONV = 3
LN_EPS = 1e-5
RMS_EPS = 1e-6
DN_ALPHA = (2 * DEPTH) ** 0.25
NEG = -1e30
IDX_W_SCALE = N_HEADS ** -0.5 * IDX_DIM ** -0.5
LOG2E = 1.4426950408889634
DSA_Q_SCALE = A_HEAD_DIM ** -0.5 * LOG2E
MLA_Q_SCALE = (D_NOPE + D_ROPE) ** -0.5 * LOG2E
ALIBI_SLOPES = tuple(2.0 ** (-8.0 * (i + 1) / N_HEADS) for i in range(N_HEADS))

LANE = 128
SUBLANE = 8
BF16_ROWS = 16
V_ROWS = D_V + BF16_ROWS
VMEM_LIMIT_BYTES = 56 * 1024 * 1024

TM = 512
FFN_CHUNK = 256
TQ_SEL = 256
TQ_ATT = 512
TK_SEL = 512
SCORE_GROUP = 4
TK_ATT = 1024
QK_AHEAD = 4
QK_AHEAD_MASKED = 6
POOL_HALO = 16

INT_MIN = -2 ** 31
F32 = jnp.float32
BF16 = jnp.bfloat16


def _params():
    return pltpu.CompilerParams(dimension_semantics=("arbitrary",),
                                vmem_limit_bytes=VMEM_LIMIT_BYTES)


def _dot(a, b):
    return jnp.dot(a, b, preferred_element_type=F32)


def _dot_nt(a, b):
    return lax.dot_general(a, b, (((1,), (1,)), ((), ())), preferred_element_type=F32)


def _layer_norm(z, g, b):
    mu = jnp.mean(z, axis=-1, keepdims=True)
    zc = z - mu
    var = jnp.mean(zc * zc, axis=-1, keepdims=True)
    return zc * lax.rsqrt(var + LN_EPS) * g + b


def _rms_norm(z, g):
    return z * lax.rsqrt(jnp.mean(z * z, axis=-1, keepdims=True) + RMS_EPS) * g


def _causal_conv3(u, prev, w):
    w0, w1, w2 = w[0:1], w[1:2], w[2:3]
    y = pltpu.roll(u, 2, 0) * w0 + pltpu.roll(u, 1, 0) * w1 + u * w2
    head = u[0:SUBLANE]
    r = lax.broadcasted_iota(jnp.int32, (SUBLANE, 1), 0)
    h1 = jnp.where(r == 0, prev[7:8], pltpu.roll(head, 1, 0))
    h2 = jnp.where(r == 0, prev[6:7], jnp.where(r == 1, prev[7:8], pltpu.roll(head, 2, 0)))
    yh = h2 * w0 + h1 * w1 + head * w2
    return jnp.concatenate([yh, y[SUBLANE:]], axis=0)


def _fold_rows(c, rows_out):
    rows = c.shape[0]
    while rows > rows_out:
        rows //= 2
        c = c[0:rows] + c[rows:2 * rows]
    return c


def _store_values_t(vt_ref, vt):
    t = vt.shape[1]
    ones_row = jnp.where(lax.broadcasted_iota(jnp.int32, (BF16_ROWS, t), 0) == 0, 1.0, 0.0)
    for h in range(N_HEADS):
        vt_ref[h * V_ROWS:h * V_ROWS + D_V, :] = vt[h * D_V:(h + 1) * D_V, :].astype(BF16)
        vt_ref[h * V_ROWS + D_V:(h + 1) * V_ROWS, :] = ones_row.astype(BF16)


def _adaln_kernel(c_ref, w_ref, b_ref, o_ref):
    c = c_ref[...]
    cond = c * jax.nn.sigmoid(c)
    o_ref[0] = jnp.sum(cond * w_ref[0], axis=0, keepdims=True) + b_ref[0]


def _adaln(c, ada_w, ada_b):
    depth, d, n = ada_w.shape
    tn = 1536
    return pl.pallas_call(
        _adaln_kernel,
        grid=(depth, n // tn),
        in_specs=[pl.BlockSpec((d, 1), lambda l, j: (0, 0)),
                  pl.BlockSpec((1, d, tn), lambda l, j: (l, 0, j)),
                  pl.BlockSpec((1, 1, tn), lambda l, j: (l, 0, j))],
        out_specs=pl.BlockSpec((1, 1, tn), lambda l, j: (l, 0, j)),
        out_shape=jax.ShapeDtypeStruct((depth, 1, n), F32),
        compiler_params=pltpu.CompilerParams(dimension_semantics=("arbitrary", "arbitrary"),
                                             vmem_limit_bytes=VMEM_LIMIT_BYTES),
        name="adaln",
    )(c.reshape(d, 1), ada_w, ada_b.reshape(depth, 1, n))


_E_Q = (0, 1024)
_E_K = (1024, 2048)
_E_QI = (2048, 3072)
_E_KI = (3072, 3200)
_E_BG = (3200, 3712)
_E_CG = (3712, 4224)
_E_XB = (4224, 4736)
_E_COLS = 4736
_ET_V = (0, 512)
_ET_WI = (512, 528)
_ET_ROWS = 528
_FEAT_LANE = A_HEAD_DIM
_N_LOG2E_TERMS = 3
_POS_RADIX = 128


def _pad_heads(w):
    d = w.shape[0]
    w3 = w.reshape(d, N_HEADS, A_HEAD_DIM)
    return jnp.concatenate([w3, jnp.zeros_like(w3)], axis=2).reshape(d, N_HEADS * LANE)


def _alibi_q_features():
    terms, rest = [], np.float64(LOG2E)
    for _ in range(_N_LOG2E_TERMS):
        t = np.float64(np.asarray(rest, np.float32).astype(jnp.bfloat16).astype(np.float32))
        terms.append(t)
        rest = rest - t
    row = np.zeros((N_HEADS, LANE), np.float32)
    for h in range(N_HEADS):
        for n, t in enumerate(terms):
            row[h, _FEAT_LANE + n] = ALIBI_SLOPES[h] * _POS_RADIX * t
            row[h, _FEAT_LANE + _N_LOG2E_TERMS + n] = ALIBI_SLOPES[h] * t
    return jnp.asarray(row.reshape(1, N_HEADS * LANE))


def _even_weights(w_in):
    d = w_in.shape[0]
    a = A_WIDTH
    q = _pad_heads(w_in[:, 0:a])
    k = _pad_heads(w_in[:, a:2 * a])
    v = w_in[:, 2 * a:3 * a]
    qi = _pad_heads(w_in[:, 3 * a:4 * a])
    o = 4 * a
    ki = w_in[:, o:o + IDX_DIM]
    wi = w_in[:, o + IDX_DIM:o + IDX_DIM + N_HEADS]
    o = o + IDX_DIM + N_HEADS
    rest = w_in[:, o:o + 3 * B_WIDTH]
    w = jnp.concatenate([q, k, qi, ki, ki, rest], axis=1)
    assert w.shape[1] == _E_COLS
    wt = jnp.concatenate([v.T, wi.T, jnp.zeros((BF16_ROWS - N_HEADS, d), w_in.dtype)], axis=0)
    assert wt.shape[0] == _ET_ROWS
    return w.astype(BF16), wt.astype(BF16)


def _even_in_kernel(x_ref, mod_ref, w_ref, wt_ref, cw_ref, qf_ref,
                    q_ref, k_ref, vt_ref, qi_ref, ki_ref, wit_ref, yb_ref, carry_ref):
    d = D_MODEL
    i = pl.program_id(0)

    @pl.when(i == 0)
    def _():
        carry_ref[...] = jnp.zeros_like(carry_ref)

    h = (x_ref[...] * (1.0 + mod_ref[:, d:2 * d]) + mod_ref[:, 0:d]).astype(BF16)

    def proj(cols):
        return _dot(h, w_ref[:, cols[0]:cols[1]])

    def proj_t(rows):
        return _dot_nt(wt_ref[rows[0]:rows[1], :], h)

    q_ref[...] = (proj(_E_Q) * DSA_Q_SCALE + qf_ref[...]).astype(BF16)
    pos = i * TM + lax.broadcasted_iota(jnp.int32, (TM, LANE), 0)
    lane = lax.broadcasted_iota(jnp.int32, (TM, LANE), 1) - _FEAT_LANE
    pos_hi = (pos // _POS_RADIX).astype(F32)
    pos_lo = (pos % _POS_RADIX).astype(F32)
    kfeat = jnp.where(lane < 0, 0.0,
                      jnp.where(lane < _N_LOG2E_TERMS, pos_hi,
                                jnp.where(lane < 2 * _N_LOG2E_TERMS, pos_lo, 0.0)))
    kproj = proj(_E_K)
    for hd in range(N_HEADS):
        sl = slice(hd * LANE, (hd + 1) * LANE)
        k_ref[:, sl] = (kproj[:, sl] + kfeat).astype(BF16)
    _store_values_t(vt_ref, proj_t(_ET_V))
    qi_ref[...] = proj(_E_QI).astype(BF16)
    ki_ref[...] = proj(_E_KI).astype(BF16)
    wit_ref[...] = proj_t(_ET_WI) * IDX_W_SCALE
    g = proj(_E_CG) * proj(_E_XB)
    y = _causal_conv3(g, carry_ref[...], cw_ref[...])
    yb_ref[...] = (proj(_E_BG) * y).astype(BF16)
    carry_ref[...] = g[TM - SUBLANE:TM]


def _even_in(x, mod, w, wt, conv_w, qfeat):
    s, d = x.shape
    row = lambda n: pl.BlockSpec((TM, n), lambda i: (i, 0))
    col = lambda n: pl.BlockSpec((n, TM), lambda i: (0, i))
    full = lambda a: pl.BlockSpec(a.shape, lambda i: (0,) * a.ndim)
    rows_out = lambda n, t: (row(n), jax.ShapeDtypeStruct((s, n), t))
    cols_out = lambda n, t: (col(n), jax.ShapeDtypeStruct((n, s), t))
    outs = [rows_out(N_HEADS * LANE, BF16), rows_out(N_HEADS * LANE, BF16), cols_out(N_HEADS * V_ROWS, BF16),
            rows_out(N_HEADS * LANE, BF16), rows_out(LANE, BF16), cols_out(BF16_ROWS, F32),
            rows_out(B_WIDTH, BF16)]
    return pl.pallas_call(
        _even_in_kernel,
        grid=(s // TM,),
        in_specs=[row(d), full(mod), full(w), full(wt), full(conv_w), full(qfeat)],
        out_specs=[o[0] for o in outs],
        out_shape=[o[1] for o in outs],
        scratch_shapes=[pltpu.VMEM((SUBLANE, B_WIDTH), F32)],
        compiler_params=_params(),
        name="even_in",
    )(x, mod, w, wt, conv_w, qfeat)


def _select_kernel(qi_ref, ki_ref, wt_ref, bias_ref, hi_ref, lo_ref, m1_ref, m2_ref, mc_ref, my_ref, tc_ref,
                   *, seq):
    tq, tk = TQ_SEL, TK_SEL
    i16 = jnp.int16
    i = pl.program_id(0)
    start = i * tq
    n_tiles = (start + tq + tk - 1) // tk
    tpos = start + lax.broadcasted_iota(jnp.int32, (1, tq), 1)
    limit = (tpos // CHUNK + 1) * CHUNK
    int_min = jnp.int32(INT_MIN)
    dmin, dmax = -2 ** 15, 2 ** 15 - 1
    zero16, one16 = jnp.zeros((), i16), jnp.ones((), i16)
    sel0, seln = jnp.zeros((), BF16), jnp.full((), NEG, BF16)

    def key_pos(off):
        return off + lax.broadcasted_iota(jnp.int32, (tk, tq), 0)

    def score_tile(j, carry, *, masked, tiles=1):
        for t in range(tiles):
            score_rows(pl.multiple_of((j * tiles + t) * tk, tk), masked)
        return carry

    def score_rows(off, masked):
        kt = ki_ref[pl.ds(off, tk), :]
        score = jnp.zeros((tk, tq), F32)
        for h in range(N_HEADS):
            rel = _dot_nt(kt, qi_ref[:, h * LANE:(h + 1) * LANE])
            score = score + wt_ref[h:h + 1, :] * jnp.maximum(rel, 0.0)
        bits = pltpu.bitcast(score, jnp.int32)
        key = bits ^ ((bits >> 31) & jnp.int32(0x7FFFFFFF))
        key = jnp.where(bits == int_min, 0, key)
        if masked:
            key = jnp.where(key_pos(off) < limit, key, int_min)
        hi_ref[pl.ds(off, tk), :] = (key >> 16).astype(i16)
        lo_ref[pl.ds(off, tk), :] = key.astype(i16) ^ jnp.asarray(dmin, i16)

    n_before = start // tk
    n_groups = n_before // SCORE_GROUP
    lax.fori_loop(0, n_groups, functools.partial(score_tile, masked=False, tiles=SCORE_GROUP), 0)
    lax.fori_loop(SCORE_GROUP * n_groups, n_before, functools.partial(score_tile, masked=False), 0)
    lax.fori_loop(n_before, n_tiles, functools.partial(score_tile, masked=True), 0)

    def rows_at(ref, off, rows):
        return ref[pl.ds(off, rows), :]

    def for_key_tiles(fn, carry):
        return lax.fori_loop(0, n_tiles, lambda j, c: fn(pl.multiple_of(j * tk, tk), tk, c), carry)

    def per_query(c):
        return jnp.sum(c.astype(jnp.int32).astype(F32), axis=0, keepdims=True)

    def count(flag):
        def body(off, rows, acc):
            return acc + _fold_rows(flag(off, rows), BF16_ROWS)
        return per_query(for_key_tiles(body, jnp.zeros((BF16_ROWS, tq), i16)))

    def radix_search(count_ge, need):
        def bit_step(b, thr):
            cand = thr + jnp.left_shift(jnp.int32(1), 15 - b)
            return jnp.where(count_ge(cand.astype(i16)) >= need, cand, thr)
        return lax.fori_loop(0, 16, bit_step, jnp.full((1, tq), dmin, jnp.int32))

    topk = float(TOPK)
    t_hi = radix_search(lambda c16: count(
        lambda off, rows: jnp.where(rows_at(hi_ref, off, rows) >= c16, one16, zero16)), topk)
    t_hi16 = t_hi.astype(i16)

    m1_ref[...] = jnp.full((tk, tq), dmin, i16)
    m2_ref[...] = jnp.full((tk, tq), dmin, i16)
    mc_ref[...] = jnp.zeros((tk, tq), i16)

    def fold_members(off, rows, n_above16):
        hi = rows_at(hi_ref, off, rows)
        n_above16 = n_above16 + _fold_rows(jnp.where(hi > t_hi16, one16, zero16), BF16_ROWS)
        member = hi == t_hi16
        x = jnp.where(member, rows_at(lo_ref, off, rows), jnp.asarray(dmin, i16))
        lo_ref[pl.ds(off, rows), :] = x
        a, b = m1_ref[...], m2_ref[...]
        above = x > a
        m1_ref[...] = jnp.where(above, x, a)
        second = jnp.where(above, a, x)
        m2_ref[...] = jnp.where(second > b, second, b)
        mc_ref[...] = mc_ref[...] + jnp.where(member, one16, zero16)
        return n_above16

    n_above = per_query(for_key_tiles(fold_members, jnp.zeros((BF16_ROWS, tq), i16)))
    need_lo = topk - n_above

    def count_slots(flag):
        return per_query(_fold_rows(flag(m1_ref[...]) + flag(m2_ref[...]), BF16_ROWS))

    def level2_slots():
        t = radix_search(lambda c16: count_slots(lambda m: jnp.where(m >= c16, one16, zero16)), need_lo)
        t16 = t.astype(i16)
        n_members = per_query(_fold_rows(mc_ref[...], BF16_ROWS))
        gt = count_slots(lambda m: jnp.where(m > t16, one16, zero16))
        ge = jnp.where(t == dmin, n_members, count_slots(lambda m: jnp.where(m >= t16, one16, zero16)))
        return t, gt, ge

    def level2_full():
        t = radix_search(lambda c16: count(
            lambda off, rows: jnp.where(rows_at(lo_ref, off, rows) >= c16, one16, zero16)), need_lo)
        t16 = t.astype(i16)
        gt = count(lambda off, rows: jnp.where(rows_at(lo_ref, off, rows) > t16, one16, zero16))
        ge = count(lambda off, rows: jnp.where(
            rows_at(hi_ref, off, rows) == t_hi16,
            jnp.where(rows_at(lo_ref, off, rows) >= t16, one16, zero16), zero16))
        return t, gt, ge

    crowded = per_query(_fold_rows(jnp.where(mc_ref[...] > jnp.asarray(2, i16), one16, zero16), BF16_ROWS))
    t_lo, n_gt_lo, n_ge_lo = lax.cond(jnp.max(crowded) > 0.0, level2_full, level2_slots)
    t_lo16 = t_lo.astype(i16)
    n_gt = n_above + n_gt_lo
    n_ge = n_above + n_ge_lo
    room = topk - n_gt
    sentinel = t_hi == dmin
    has_ties = jnp.where(sentinel, 0.0, jnp.where(n_ge > topk, 1.0, 0.0))
    any_ties = jnp.max(has_ties) > 0.0

    def pos16(off, rows):
        return (off + lax.broadcasted_iota(jnp.int32, (rows, tq), 0)).astype(i16)

    def tie_cutoff():
        def tied_rows(j):
            off = pl.multiple_of(j * tk, tk)
            return jnp.where(rows_at(hi_ref, off, tk) == t_hi16,
                             jnp.where(rows_at(lo_ref, off, tk) == t_lo16, one16, zero16), zero16)

        def count_tile(j, carry):
            tc_ref[pl.ds(j, 1), :] = per_query(_fold_rows(tied_rows(j), BF16_ROWS))
            return carry
        lax.fori_loop(0, n_tiles, count_tile, 0)

        def find_tile(j, carry):
            seen, tile_of, left = carry
            here = tc_ref[pl.ds(j, 1), :]
            crossing = jnp.logical_and(seen < room, seen + here >= room)
            return (seen + here, jnp.where(crossing, j, tile_of), jnp.where(crossing, room - seen, left))
        _, tile_of, left = lax.fori_loop(
            0, n_tiles, find_tile, (jnp.zeros((1, tq), F32), jnp.zeros((1, tq), jnp.int32), room))

        my_ref[...] = jnp.zeros((tk, tq), i16)
        tile_of16 = tile_of.astype(i16)

        def gather_tile(j, carry):
            mine = jnp.where(tile_of16 == j.astype(i16), jnp.asarray(-1, i16), zero16)
            my_ref[...] = my_ref[...] | (tied_rows(j) & mine)
            return carry
        lax.fori_loop(0, n_tiles, gather_tile, 0)

        row16 = lax.broadcasted_iota(jnp.int32, (tk, tq), 0).astype(i16)
        row_bits = tk.bit_length()

        def row_step(b, cut):
            cand = cut + jnp.left_shift(jnp.int32(1), row_bits - 1 - b)
            below = jnp.where(row16 < cand.astype(i16), my_ref[...], zero16)
            return jnp.where(per_query(_fold_rows(below, BF16_ROWS)) <= left, cand, cut)
        row_cut = lax.fori_loop(0, row_bits, row_step, jnp.zeros((1, tq), jnp.int32))
        return tile_of * tk + jnp.minimum(row_cut, tk)

    def write_plain():
        lo_min16 = jnp.where(sentinel, dmax, t_lo).astype(i16)

        def write_rows(off, rows, carry):
            hi = rows_at(hi_ref, off, rows)
            inner = jnp.where(rows_at(lo_ref, off, rows) >= lo_min16, sel0, seln)
            bias_ref[pl.ds(off, rows), :] = jnp.where(hi > t_hi16, sel0,
                                                      jnp.where(hi == t_hi16, inner, seln))
            return carry
        for_key_tiles(write_rows, 0)

    def write_with_ties():
        cut16 = jnp.where(sentinel, 0, tie_cutoff()).astype(i16)

        def write_rows(off, rows, carry):
            hi = rows_at(hi_ref, off, rows)
            lo = rows_at(lo_ref, off, rows)
            tie = jnp.where(pos16(off, rows) < cut16, sel0, seln)
            inner = jnp.where(lo > t_lo16, sel0, jnp.where(lo == t_lo16, tie, seln))
            bias_ref[pl.ds(off, rows), :] = jnp.where(hi > t_hi16, sel0,
                                                      jnp.where(hi == t_hi16, inner, seln))
            return carry
        for_key_tiles(write_rows, 0)

    lax.cond(any_ties, write_with_ties, write_plain)

    def fill_tile(j, carry):
        off = pl.multiple_of(j * tk, tk)
        bias_ref[pl.ds(off, tk), :] = jnp.full((tk, tq), NEG, BF16)
        return carry

    lax.fori_loop(n_tiles, seq // tk, fill_tile, 0)


def _select(qi, ki, wit):
    s = qi.shape[0]
    assert s % TK_SEL == 0 and s % TQ_SEL == 0 and 2 * TOPK <= s <= 32767
    return pl.pallas_call(
        functools.partial(_select_kernel, seq=s),
        grid=(s // TQ_SEL,),
        in_specs=[pl.BlockSpec((TQ_SEL, N_HEADS * LANE), lambda i: (i, 0)),
                  pl.BlockSpec((s, LANE), lambda i: (0, 0)),
                  pl.BlockSpec((BF16_ROWS, TQ_SEL), lambda i: (0, i))],
        out_specs=pl.BlockSpec((s, TQ_SEL), lambda i: (0, i)),
        out_shape=jax.ShapeDtypeStruct((s, s), BF16),
        scratch_shapes=[pltpu.VMEM((s, TQ_SEL), jnp.int16), pltpu.VMEM((s, TQ_SEL), jnp.int16)]
        + [pltpu.VMEM((TK_SEL, TQ_SEL), jnp.int16)] * 4 + [pltpu.VMEM((s // TK_SEL, TQ_SEL), F32)],
        compiler_params=_params(),
        name="dsa_select",
    )(qi, ki, wit)


def _flash_kernel(qi_of, kj_of, *refs, alibi, use_bias, qk_ahead):
    if use_bias:
        q_ref, k_ref, vt_ref, bias_ref, o_ref, m_ref, acc_ref = refs
    else:
        q_ref, k_ref, vt_ref, o_ref, m_ref, acc_ref = refs
    tq, tk = TQ_ATT, TK_ATT
    p = pl.program_id(0)
    qi = qi_of[p]
    kj = kj_of[p]
    last = ((qi + 1) * tq - 1) // tk

    @pl.when(kj == 0)
    def _():
        m_ref[...] = jnp.full_like(m_ref, NEG)
        acc_ref[...] = jnp.zeros_like(acc_ref)

    def step(diagonal, live=tk):
        base = bias_ref[0:live, :].astype(F32) if use_bias else None
        if diagonal:
            tpos = qi * tq + lax.broadcasted_iota(jnp.int32, (live, tq), 1)
            spos = kj * tk + lax.broadcasted_iota(jnp.int32, (live, tq), 0)
            if not use_bias:
                base = jnp.where(spos < (tpos // CHUNK + 1) * CHUNK, 0.0, NEG)
            if alibi:
                ahead = jnp.maximum(spos - tpos, 0).astype(F32)

        def logits(h):
            s = _dot_nt(k_ref[0:live, h * LANE:(h + 1) * LANE], q_ref[:, h * LANE:(h + 1) * LANE])
            if diagonal and alibi:
                return s + (base - (2.0 * ALIBI_SLOPES[h] * LOG2E) * ahead)
            return s if base is None else s + base

        pending = [logits(h) for h in range(qk_ahead)]
        for h in range(N_HEADS):
            rows = slice(h * V_ROWS, (h + 1) * V_ROWS)
            if h + qk_ahead < N_HEADS:
                pending.append(logits(h + qk_ahead))
            s = pending.pop(0)
            m_prev = m_ref[h:h + 1, :]
            m_new = jnp.maximum(m_prev, jnp.max(s, axis=0, keepdims=True))
            alpha = jnp.exp2(m_prev - m_new)
            pexp = jnp.exp2(s - m_new)
            acc_ref[rows, :] = alpha * acc_ref[rows, :] + _dot(vt_ref[rows, 0:live], pexp.astype(BF16))
            m_ref[h:h + 1, :] = m_new

    @pl.when(kj != last)
    def _():
        step(False)

    blocks_per_tile = max(tk // tq, 1)
    for v in range(blocks_per_tile):
        @pl.when(jnp.logical_and(kj == last, qi % blocks_per_tile == v))
        def _(v=v):
            step(True, live=min((v + 1) * tq, tk))

    @pl.when(kj == last)
    def _():
        outs = [acc_ref[h * V_ROWS:h * V_ROWS + D_V, :] / acc_ref[h * V_ROWS + D_V:h * V_ROWS + D_V + 1, :]
                for h in range(N_HEADS)]
        o_ref[...] = jnp.concatenate(outs, axis=0).T.astype(BF16)


def _flash(q, k, vt, bias, *, alibi):
    s = q.shape[0]
    nq = s // TQ_ATT
    pairs = [(i, j) for i in range(nq) for j in range(((i + 1) * TQ_ATT - 1) // TK_ATT + 1)]
    qi_of = jnp.asarray(np.array([a for a, _ in pairs], np.int32))
    kj_of = jnp.asarray(np.array([b for _, b in pairs], np.int32))
    use_bias = bias is not None
    in_specs = [pl.BlockSpec((TQ_ATT, N_HEADS * LANE), lambda p, qi, kj: (qi[p], 0)),
                pl.BlockSpec((TK_ATT, k.shape[1]), lambda p, qi, kj: (kj[p], 0)),
                pl.BlockSpec((N_HEADS * V_ROWS, TK_ATT), lambda p, qi, kj: (0, kj[p]))]
    args = [q, k, vt]
    if use_bias:
        in_specs.append(pl.BlockSpec((TK_ATT, TQ_ATT), lambda p, qi, kj: (kj[p], qi[p])))
        args.append(bias)
    kern = functools.partial(_flash_kernel, alibi=alibi, use_bias=use_bias,
                             qk_ahead=QK_AHEAD_MASKED if use_bias else QK_AHEAD)
    return pl.pallas_call(
        kern,
        grid_spec=pltpu.PrefetchScalarGridSpec(
            num_scalar_prefetch=2,
            grid=(len(pairs),),
            in_specs=in_specs,
            out_specs=pl.BlockSpec((TQ_ATT, N_HEADS * D_V), lambda p, qi, kj: (qi[p], 0)),
            scratch_shapes=[pltpu.VMEM((N_HEADS, TQ_ATT), F32),
                            pltpu.VMEM((N_HEADS * V_ROWS, TQ_ATT), F32)]),
        out_shape=jax.ShapeDtypeStruct((s, N_HEADS * D_V), BF16),
        compiler_params=_params(),
        name="flash_dsa" if use_bias else "flash_mla",
    )(qi_of, kj_of, *args)


def _mix_ffn_kernel(ya_ref, yb_ref, x_ref, mod_ref, wo_ref, gm_ref, bm_ref,
                    wup_ref, cw_ref, wdn_ref, gf_ref, bf_ref, o_ref, carry_ref, act_ref):
    d = D_MODEL

    @pl.when(pl.program_id(0) == 0)
    def _():
        carry_ref[...] = jnp.zeros_like(carry_ref)

    na = ya_ref.shape[1]
    y = _dot(ya_ref[...], wo_ref[0:na, :]) + _dot(yb_ref[...], wo_ref[na:, :])
    x = _layer_norm(DN_ALPHA * x_ref[...] + (1.0 + mod_ref[:, 2 * d:3 * d]) * y, gm_ref[...], bm_ref[...])
    h = (x * (1.0 + mod_ref[:, 4 * d:5 * d]) + mod_ref[:, 3 * d:4 * d]).astype(BF16)
    for c in range(D_FF // FFN_CHUNK):
        va = c * FFN_CHUNK
        ga = D_FF + va
        uv = _dot(h, wup_ref[:, va:va + FFN_CHUNK])
        ug = _dot(h, wup_ref[:, ga:ga + FFN_CHUNK])
        val = _causal_conv3(uv, carry_ref[:, va:va + FFN_CHUNK], cw_ref[:, va:va + FFN_CHUNK])
        gate = _causal_conv3(ug, carry_ref[:, ga:ga + FFN_CHUNK], cw_ref[:, ga:ga + FFN_CHUNK])
        carry_ref[:, va:va + FFN_CHUNK] = uv[TM - SUBLANE:TM]
        carry_ref[:, ga:ga + FFN_CHUNK] = ug[TM - SUBLANE:TM]
        act_ref[:, va:va + FFN_CHUNK] = (gate * jax.nn.sigmoid(gate) * val).astype(BF16)
    y = _dot(act_ref[...], wdn_ref[...])
    z = DN_ALPHA * x + (1.0 + mod_ref[:, 5 * d:6 * d]) * y
    o_ref[...] = _layer_norm(z, gf_ref[...], bf_ref[...])


def _mix_ffn(ya, yb, x, mod, w_out, g_mix, b_mix, layer, w_up_all, conv_w_all, w_down_all, g_ffn, b_ffn):
    s, d = x.shape
    assert D_FF % FFN_CHUNK == 0
    row = lambda n: pl.BlockSpec((TM, n), lambda i: (i, 0))
    full = lambda a: pl.BlockSpec(a.shape, lambda i: (0,) * a.ndim)
    once = lambda a: pl.BlockSpec(a.shape, lambda i: (0,) * a.ndim, pipeline_mode=pl.Buffered(1))
    of_layer = lambda a, **kw: pl.BlockSpec((None,) + a.shape[1:], lambda i: (layer, 0, 0), **kw)
    vec = lambda a: a.reshape(1, d)
    args = [ya, yb, x, mod, w_out, vec(g_mix), vec(b_mix), w_up_all, conv_w_all, w_down_all,
            vec(g_ffn), vec(b_ffn)]
    specs = [row(ya.shape[1]), row(yb.shape[1]), row(d), full(mod), once(w_out), full(args[5]), full(args[6]),
             of_layer(w_up_all, pipeline_mode=pl.Buffered(1)), of_layer(conv_w_all),
             of_layer(w_down_all, pipeline_mode=pl.Buffered(1)), full(args[10]), full(args[11])]
    return pl.pallas_call(
        _mix_ffn_kernel,
        grid=(s // TM,),
        in_specs=specs,
        out_specs=row(d),
        out_shape=jax.ShapeDtypeStruct((s, d), F32),
        scratch_shapes=[pltpu.VMEM((SUBLANE, 2 * D_FF), F32), pltpu.VMEM((TM, D_FF), BF16)],
        compiler_params=_params(),
        name="mix_ffn",
    )(*args)


_O_U = (0, 512)
_O_QLAT = (512, 896)
_O_KVLAT = (896, 1152)
_O_KR = (1152, 1280)
_O_KR_ROT = (1280, 1408)
_O_COLS = 1408


def _rotate_half_cols(w):
    half = w.shape[-1] // 2
    return jnp.concatenate([-w[..., half:], w[..., :half]], axis=-1)


def _odd_weights(w_in, w_uq, w_ukv):
    d = w_in.shape[0]
    o = C_WIDTH + Q_LORA + KV_LORA
    kr = w_in[:, o:o + D_ROPE]
    zl = jnp.zeros((d, D_NOPE), w_in.dtype)
    zr = jnp.zeros((d, LANE - D_NOPE - D_ROPE), w_in.dtype)
    w1 = jnp.concatenate([w_in[:, 0:o], zl, kr, zr, zl, _rotate_half_cols(kr), zr], axis=1)
    assert w1.shape[1] == _O_COLS
    dq = D_NOPE + D_ROPE
    uq = w_uq.reshape(Q_LORA, N_HEADS, dq)
    padq = jnp.zeros((Q_LORA, N_HEADS, LANE - dq), w_uq.dtype)
    wq1 = jnp.concatenate([uq, padq], axis=2).reshape(Q_LORA, N_HEADS * LANE)
    wq2 = jnp.concatenate([jnp.zeros((Q_LORA, N_HEADS, D_NOPE), w_uq.dtype),
                           _rotate_half_cols(uq[:, :, D_NOPE:]), padq], axis=2
                          ).reshape(Q_LORA, N_HEADS * LANE)
    ukv = w_ukv.reshape(KV_LORA, N_HEADS, D_NOPE + D_V)
    wk = jnp.concatenate([ukv[:, :, :D_NOPE], jnp.zeros((KV_LORA, N_HEADS, LANE - D_NOPE), w_ukv.dtype)],
                         axis=2).reshape(KV_LORA, N_HEADS * LANE)
    wvt = ukv[:, :, D_NOPE:].reshape(KV_LORA, N_HEADS * D_V).T
    return w1.astype(BF16), wq1.astype(BF16), wq2.astype(BF16), wk.astype(BF16), wvt.astype(BF16)


def _rope_lane_tables(seq):
    inv = ROPE_BASE ** (-jnp.arange(0, D_ROPE, 2, dtype=F32) / D_ROPE)
    ang = jnp.arange(seq, dtype=F32)[:, None] * inv[None, :]
    cos, sin = lax.optimization_barrier((jnp.cos(ang), jnp.sin(ang)))
    pad = jnp.zeros((seq, LANE - D_NOPE - D_ROPE), F32)
    cos_t = jnp.concatenate([jnp.ones((seq, D_NOPE), F32), cos, cos, pad], axis=1)
    sin_t = jnp.concatenate([jnp.zeros((seq, D_NOPE), F32), sin, sin, pad], axis=1)
    return cos_t, sin_t


def _odd_in_kernel(x_ref, mod_ref, w1_ref, wq1_ref, wq2_ref, wk_ref, wvt_ref, pw_ref, ps_ref,
                   qg_ref, kvg_ref, cos_ref, sin_ref,
                   yc_ref, q_ref, k_ref, vt_ref, carry_ref):
    d = D_MODEL
    i = pl.program_id(0)

    @pl.when(i == 0)
    def _():
        carry_ref[...] = jnp.zeros_like(carry_ref)

    h = (x_ref[...] * (1.0 + mod_ref[:, d:2 * d]) + mod_ref[:, 0:d]).astype(BF16)

    def proj(cols):
        return _dot(h, w1_ref[:, cols[0]:cols[1]])

    cos = cos_ref[...]
    sin = sin_ref[...]

    u = proj(_O_U)
    ext = jnp.concatenate([carry_ref[...], u], axis=0)
    pos = i * TM + lax.broadcasted_iota(jnp.int32, (TM, 1), 0)
    for g, win in enumerate(POOL_WINDOWS):
        sl = slice(g * C_GROUP_DIM, (g + 1) * C_GROUP_DIM)
        acc = ext[:, sl]
        shift = 1
        while shift < win:
            acc = acc + pltpu.roll(acc, shift, 0)
            shift *= 2
        cnt = jnp.minimum(pos + 1, win).astype(F32)
        pooled = acc[POOL_HALO:] / cnt - u[:, sl]
        mixed = _dot(pooled.astype(BF16), pw_ref[g])
        yc_ref[:, sl] = (mixed * ps_ref[:, sl]).astype(BF16)
    carry_ref[...] = u[TM - POOL_HALO:TM]

    r = _rms_norm(proj(_O_QLAT), qg_ref[...]).astype(BF16)
    qa = _dot(r, wq1_ref[...])
    qb = _dot(r, wq2_ref[...])
    kr = proj(_O_KR) * cos + proj(_O_KR_ROT) * sin
    rk = _rms_norm(proj(_O_KVLAT), kvg_ref[...]).astype(BF16)
    kn = _dot(rk, wk_ref[...])
    for hd in range(N_HEADS):
        sl = slice(hd * LANE, (hd + 1) * LANE)
        q_ref[:, sl] = ((qa[:, sl] * cos + qb[:, sl] * sin) * MLA_Q_SCALE).astype(BF16)
        k_ref[:, sl] = (kn[:, sl] + kr).astype(BF16)
    _store_values_t(vt_ref, _dot_nt(wvt_ref[...], rk))


def _odd_in(x, mod, w1, wq1, wq2, wk, wvt, pool_w, pool_scale, q_norm_g, kv_norm_g, cos_t, sin_t):
    s, d = x.shape
    row = lambda n: pl.BlockSpec((TM, n), lambda i: (i, 0))
    col = lambda n: pl.BlockSpec((n, TM), lambda i: (0, i))
    full = lambda a: pl.BlockSpec(a.shape, lambda i: (0,) * a.ndim)
    consts = [w1, wq1, wq2, wk, wvt, pool_w.astype(BF16), pool_scale.reshape(1, C_WIDTH),
              q_norm_g.reshape(1, Q_LORA), kv_norm_g.reshape(1, KV_LORA)]
    rows_out = lambda n, t: (row(n), jax.ShapeDtypeStruct((s, n), t))
    cols_out = lambda n, t: (col(n), jax.ShapeDtypeStruct((n, s), t))
    outs = [rows_out(C_WIDTH, BF16), rows_out(N_HEADS * LANE, BF16), rows_out(N_HEADS * LANE, BF16),
            cols_out(N_HEADS * V_ROWS, BF16)]
    return pl.pallas_call(
        _odd_in_kernel,
        grid=(s // TM,),
        in_specs=[row(d), full(mod)] + [full(a) for a in consts] + [row(LANE), row(LANE)],
        out_specs=[o[0] for o in outs],
        out_shape=[o[1] for o in outs],
        scratch_shapes=[pltpu.VMEM((POOL_HALO, C_WIDTH), F32)],
        compiler_params=_params(),
        name="odd_in",
    )(x, mod, *consts, cos_t, sin_t)


def kernel(x, c, ada_w, ada_b, ln_mix_g, ln_mix_b, ln_ffn_g, ln_ffn_b, ev_w_in, ev_conv_w, ev_w_out,
           od_w_in, pool_w, pool_scale, q_norm_g, w_uq, kv_norm_g, w_ukv, od_w_out,
           ffn_w_up, ffn_conv_w, ffn_w_down):
    bsz, seq, d = x.shape
    assert bsz == 1 and d == D_MODEL and seq % TM == 0 and seq % TK_ATT == 0 and seq % TQ_ATT == 0
    assert SHORT_CONV == 3 and FFN_CONV == 3
    xs = x.reshape(seq, d)
    mods = _adaln(c, ada_w, ada_b)
    w_up_bf16, w_down_bf16 = ffn_w_up.astype(BF16), ffn_w_down.astype(BF16)
    for l in range(DEPTH):
        mod = mods[l]
        if l % 2 == 0:
            e = l // 2
            w, wt = _even_weights(ev_w_in[e])
            q, k, vt, qi, ki, wit, yb = _even_in(xs, mod, w, wt, ev_conv_w[e], _alibi_q_features())
            bias = _select(qi, ki, wit)
            mixed = (_flash(q, k, vt, bias, alibi=True), yb)
            w_out = ev_w_out[e]
        else:
            o = l // 2
            cos_t, sin_t = _rope_lane_tables(seq)
            ws = _odd_weights(od_w_in[o], w_uq[o], w_ukv[o])
            yc, q, k, vt = _odd_in(xs, mod, *ws, pool_w[o], pool_scale[o], q_norm_g[o], kv_norm_g[o],
                                   cos_t, sin_t)
            mixed = (yc, _flash(q, k, vt, None, alibi=False))
            w_out = od_w_out[o]
        xs = _mix_ffn(*mixed, xs, mod, w_out.astype(BF16), ln_mix_g[l], ln_mix_b[l],
                      l, w_up_bf16, ffn_conv_w, w_down_bf16, ln_ffn_g[l], ln_ffn_b[l])
    return xs.reshape(bsz, seq, d)
```

```python
import functools

import numpy as np
import jax
import jax.numpy as jnp
from jax import lax
from jax.experimental import pallas as pl
from jax.experimental.pallas import tpu as pltpu

D_MODEL = 1024
DEPTH = 2
CHUNK = 64
N_HEADS = 8
A_HEAD_DIM = 64
A_WIDTH = N_HEADS * A_HEAD_DIM
IDX_DIM = 64
TOPK = 256
B_WIDTH = 512
SHORT_CONV = 3
C_WIDTH = 512
POOL_WINDOWS = (2, 4, 8, 16)
C_GROUP_DIM = C_WIDTH // len(POOL_WINDOWS)
D_NOPE = 64
D_ROPE = 32
D_V = 64
Q_LORA = 384
KV_LORA = 256
ROPE_BASE = 10000.0
D_FF = 2816
FFN_CONV = 3
LN_EPS = 1e-5
RMS_EPS = 1e-6
DN_ALPHA = (2 * DEPTH) ** 0.25
NEG = -1e30
IDX_W_SCALE = N_HEADS ** -0.5 * IDX_DIM ** -0.5
LOG2E = 1.4426950408889634
DSA_Q_SCALE = A_HEAD_DIM ** -0.5 * LOG2E
MLA_Q_SCALE = (D_NOPE + D_ROPE) ** -0.5 * LOG2E
ALIBI_SLOPES = tuple(2.0 ** (-8.0 * (i + 1) / N_HEADS) for i in range(N_HEADS))

LANE = 128
SUBLANE = 8
BF16_ROWS = 16
V_ROWS = D_V + BF16_ROWS
VMEM_LIMIT_BYTES = 56 * 1024 * 1024

TM = 512
FFN_CHUNK = 256
TQ_SEL = 256
TQ_ATT = 512
TK_SEL = 512
SCORE_GROUP = 4
TK_ATT = 1024
QK_AHEAD = 4
QK_AHEAD_MASKED = 6
POOL_HALO = 16

INT_MIN = -2 ** 31
F32 = jnp.float32
BF16 = jnp.bfloat16


def _params():
    return pltpu.CompilerParams(dimension_semantics=("arbitrary",),
                                vmem_limit_bytes=VMEM_LIMIT_BYTES)


def _dot(a, b):
    return jnp.dot(a, b, preferred_element_type=F32)


def _dot_nt(a, b):
    return lax.dot_general(a, b, (((1,), (1,)), ((), ())), preferred_element_type=F32)


def _layer_norm(z, g, b):
    mu = jnp.mean(z, axis=-1, keepdims=True)
    zc = z - mu
    var = jnp.mean(zc * zc, axis=-1, keepdims=True)
    return zc * lax.rsqrt(var + LN_EPS) * g + b


def _rms_norm(z, g):
    return z * lax.rsqrt(jnp.mean(z * z, axis=-1, keepdims=True) + RMS_EPS) * g


def _causal_conv3(u, prev, w):
    w0, w1, w2 = w[0:1], w[1:2], w[2:3]
    y = pltpu.roll(u, 2, 0) * w0 + pltpu.roll(u, 1, 0) * w1 + u * w2
    head = u[0:SUBLANE]
    r = lax.broadcasted_iota(jnp.int32, (SUBLANE, 1), 0)
    h1 = jnp.where(r == 0, prev[7:8], pltpu.roll(head, 1, 0))
    h2 = jnp.where(r == 0, prev[6:7], jnp.where(r == 1, prev[7:8], pltpu.roll(head, 2, 0)))
    yh = h2 * w0 + h1 * w1 + head * w2
    return jnp.concatenate([yh, y[SUBLANE:]], axis=0)


def _fold_rows(c, rows_out):
    rows = c.shape[0]
    while rows > rows_out:
        rows //= 2
        c = c[0:rows] + c[rows:2 * rows]
    return c


def _store_values_t(vt_ref, vt):
    t = vt.shape[1]
    ones_row = jnp.where(lax.broadcasted_iota(jnp.int32, (BF16_ROWS, t), 0) == 0, 1.0, 0.0)
    for h in range(N_HEADS):
        vt_ref[h * V_ROWS:h * V_ROWS + D_V, :] = vt[h * D_V:(h + 1) * D_V, :].astype(BF16)
        vt_ref[h * V_ROWS + D_V:(h + 1) * V_ROWS, :] = ones_row.astype(BF16)


def _adaln_kernel(c_ref, w_ref, b_ref, o_ref):
    c = c_ref[...]
    cond = c * jax.nn.sigmoid(c)
    o_ref[0] = jnp.sum(cond * w_ref[0], axis=0, keepdims=True) + b_ref[0]


def _adaln(c, ada_w, ada_b):
    depth, d, n = ada_w.shape
    tn = 1536
    return pl.pallas_call(
        _adaln_kernel,
        grid=(depth, n // tn),
        in_specs=[pl.BlockSpec((d, 1), lambda l, j: (0, 0)),
                  pl.BlockSpec((1, d, tn), lambda l, j: (l, 0, j)),
                  pl.BlockSpec((1, 1, tn), lambda l, j: (l, 0, j))],
        out_specs=pl.BlockSpec((1, 1, tn), lambda l, j: (l, 0, j)),
        out_shape=jax.ShapeDtypeStruct((depth, 1, n), F32),
        compiler_params=pltpu.CompilerParams(dimension_semantics=("arbitrary", "arbitrary"),
                                             vmem_limit_bytes=VMEM_LIMIT_BYTES),
        name="adaln",
    )(c.reshape(d, 1), ada_w, ada_b.reshape(depth, 1, n))


_E_Q = (0, 1024)
_E_K = (1024, 2048)
_E_QI = (2048, 3072)
_E_KI = (3072, 3200)
_E_BG = (3200, 3712)
_E_CG = (3712, 4224)
_E_XB = (4224, 4736)
_E_COLS = 4736
_ET_V = (0, 512)
_ET_WI = (512, 528)
_ET_ROWS = 528
_FEAT_LANE = A_HEAD_DIM
_N_LOG2E_TERMS = 3
_POS_RADIX = 128


def _pad_heads(w):
    d = w.shape[0]
    w3 = w.reshape(d, N_HEADS, A_HEAD_DIM)
    return jnp.concatenate([w3, jnp.zeros_like(w3)], axis=2).reshape(d, N_HEADS * LANE)


def _alibi_q_features():
    terms, rest = [], np.float64(LOG2E)
    for _ in range(_N_LOG2E_TERMS):
        t = np.float64(np.asarray(rest, np.float32).astype(jnp.bfloat16).astype(np.float32))
        terms.append(t)
        rest = rest - t
    row = np.zeros((N_HEADS, LANE), np.float32)
    for h in range(N_HEADS):
        for n, t in enumerate(terms):
            row[h, _FEAT_LANE + n] = ALIBI_SLOPES[h] * _POS_RADIX * t
            row[h, _FEAT_LANE + _N_LOG2E_TERMS + n] = ALIBI_SLOPES[h] * t
    return jnp.asarray(row.reshape(1, N_HEADS * LANE))


def _even_weights(w_in):
    d = w_in.shape[0]
    a = A_WIDTH
    q = _pad_heads(w_in[:, 0:a])
    k = _pad_heads(w_in[:, a:2 * a])
    v = w_in[:, 2 * a:3 * a]
    qi = _pad_heads(w_in[:, 3 * a:4 * a])
    o = 4 * a
    ki = w_in[:, o:o + IDX_DIM]
    wi = w_in[:, o + IDX_DIM:o + IDX_DIM + N_HEADS]
    o = o + IDX_DIM + N_HEADS
    rest = w_in[:, o:o + 3 * B_WIDTH]
    w = jnp.concatenate([q, k, qi, ki, ki, rest], axis=1)
    assert w.shape[1] == _E_COLS
    wt = jnp.concatenate([v.T, wi.T, jnp.zeros((BF16_ROWS - N_HEADS, d), w_in.dtype)], axis=0)
    assert wt.shape[0] == _ET_ROWS
    return w.astype(BF16), wt.astype(BF16)


def _even_in_kernel(x_ref, mod_ref, w_ref, wt_ref, cw_ref, qf_ref,
                    q_ref, k_ref, vt_ref, qi_ref, ki_ref, wit_ref, yb_ref, carry_ref):
    d = D_MODEL
    i = pl.program_id(0)

    @pl.when(i == 0)
    def _():
        carry_ref[...] = jnp.zeros_like(carry_ref)

    h = (x_ref[...] * (1.0 + mod_ref[:, d:2 * d]) + mod_ref[:, 0:d]).astype(BF16)

    def proj(cols):
        return _dot(h, w_ref[:, cols[0]:cols[1]])

    def proj_t(rows):
        return _dot_nt(wt_ref[rows[0]:rows[1], :], h)

    q_ref[...] = (proj(_E_Q) * DSA_Q_SCALE + qf_ref[...]).astype(BF16)
    pos = i * TM + lax.broadcasted_iota(jnp.int32, (TM, LANE), 0)
    lane = lax.broadcasted_iota(jnp.int32, (TM, LANE), 1) - _FEAT_LANE
    pos_hi = (pos // _POS_RADIX).astype(F32)
    pos_lo = (pos % _POS_RADIX).astype(F32)
    kfeat = jnp.where(lane < 0, 0.0,
                      jnp.where(lane < _N_LOG2E_TERMS, pos_hi,
                                jnp.where(lane < 2 * _N_LOG2E_TERMS, pos_lo, 0.0)))
    kproj = proj(_E_K)
    for hd in range(N_HEADS):
        sl = slice(hd * LANE, (hd + 1) * LANE)
        k_ref[:, sl] = (kproj[:, sl] + kfeat).astype(BF16)
    _store_values_t(vt_ref, proj_t(_ET_V))
    qi_ref[...] = proj(_E_QI).astype(BF16)
    ki_ref[...] = proj(_E_KI).astype(BF16)
    wit_ref[...] = proj_t(_ET_WI) * IDX_W_SCALE
    g = proj(_E_CG) * proj(_E_XB)
    y = _causal_conv3(g, carry_ref[...], cw_ref[...])
    yb_ref[...] = (proj(_E_BG) * y).astype(BF16)
    carry_ref[...] = g[TM - SUBLANE:TM]


def _even_in(x, mod, w, wt, conv_w, qfeat):
    s, d = x.shape
    row = lambda n: pl.BlockSpec((TM, n), lambda i: (i, 0))
    col = lambda n: pl.BlockSpec((n, TM), lambda i: (0, i))
    full = lambda a: pl.BlockSpec(a.shape, lambda i: (0,) * a.ndim)
    rows_out = lambda n, t: (row(n), jax.ShapeDtypeStruct((s, n), t))
    cols_out = lambda n, t: (col(n), jax.ShapeDtypeStruct((n, s), t))
    outs = [rows_out(N_HEADS * LANE, BF16), rows_out(N_HEADS * LANE, BF16), cols_out(N_HEADS * V_ROWS, BF16),
            rows_out(N_HEADS * LANE, BF16), rows_out(LANE, BF16), cols_out(BF16_ROWS, F32),
            rows_out(B_WIDTH, BF16)]
    return pl.pallas_call(
        _even_in_kernel,
        grid=(s // TM,),
        in_specs=[row(d), full(mod), full(w), full(wt), full(conv_w), full(qfeat)],
        out_specs=[o[0] for o in outs],
        out_shape=[o[1] for o in outs],
        scratch_shapes=[pltpu.VMEM((SUBLANE, B_WIDTH), F32)],
        compiler_params=_params(),
        name="even_in",
    )(x, mod, w, wt, conv_w, qfeat)


def _select_kernel(qi_ref, ki_ref, wt_ref, bias_ref, hi_ref, lo_ref, m1_ref, m2_ref, mc_ref, my_ref, tc_ref,
                   *, seq):
    tq, tk = TQ_SEL, TK_SEL
    i16 = jnp.int16
    i = pl.program_id(0)
    start = i * tq
    n_tiles = (start + tq + tk - 1) // tk
    tpos = start + lax.broadcasted_iota(jnp.int32, (1, tq), 1)
    limit = (tpos // CHUNK + 1) * CHUNK
    int_min = jnp.int32(INT_MIN)
    dmin, dmax = -2 ** 15, 2 ** 15 - 1
    zero16, one16 = jnp.zeros((), i16), jnp.ones((), i16)
    sel0, seln = jnp.zeros((), BF16), jnp.full((), NEG, BF16)

    def key_pos(off):
        return off + lax.broadcasted_iota(jnp.int32, (tk, tq), 0)

    def score_tile(j, carry, *, masked, tiles=1):
        for t in range(tiles):
            score_rows(pl.multiple_of((j * tiles + t) * tk, tk), masked)
        return carry

    def score_rows(off, masked):
        kt = ki_ref[pl.ds(off, tk), :]
        score = jnp.zeros((tk, tq), F32)
        for h in range(N_HEADS):
            rel = _dot_nt(kt, qi_ref[:, h * LANE:(h + 1) * LANE])
            score = score + wt_ref[h:h + 1, :] * jnp.maximum(rel, 0.0)
        bits = pltpu.bitcast(score, jnp.int32)
        key = bits ^ ((bits >> 31) & jnp.int32(0x7FFFFFFF))
        key = jnp.where(bits == int_min, 0, key)
        if masked:
            key = jnp.where(key_pos(off) < limit, key, int_min)
        hi_ref[pl.ds(off, tk), :] = (key >> 16).astype(i16)
        lo_ref[pl.ds(off, tk), :] = key.astype(i16) ^ jnp.asarray(dmin, i16)

    n_before = start // tk
    n_groups = n_before // SCORE_GROUP
    lax.fori_loop(0, n_groups, functools.partial(score_tile, masked=False, tiles=SCORE_GROUP), 0)
    lax.fori_loop(SCORE_GROUP * n_groups, n_before, functools.partial(score_tile, masked=False), 0)
    lax.fori_loop(n_before, n_tiles, functools.partial(score_tile, masked=True), 0)

    def rows_at(ref, off, rows):
        return ref[pl.ds(off, rows), :]

    def for_key_tiles(fn, carry):
        return lax.fori_loop(0, n_tiles, lambda j, c: fn(pl.multiple_of(j * tk, tk), tk, c), carry)

    def per_query(c):
        return jnp.sum(c.astype(jnp.int32).astype(F32), axis=0, keepdims=True)

    def count(flag):
        def body(off, rows, acc):
            return acc + _fold_rows(flag(off, rows), BF16_ROWS)
        return per_query(for_key_tiles(body, jnp.zeros((BF16_ROWS, tq), i16)))

    def radix_search(count_ge, need):
        def bit_step(b, thr):
            cand = thr + jnp.left_shift(jnp.int32(1), 15 - b)
            return jnp.where(count_ge(cand.astype(i16)) >= need, cand, thr)
        return lax.fori_loop(0, 16, bit_step, jnp.full((1, tq), dmin, jnp.int32))

    topk = float(TOPK)
    t_hi = radix_search(lambda c16: count(
        lambda off, rows: jnp.where(rows_at(hi_ref, off, rows) >= c16, one16, zero16)), topk)
    t_hi16 = t_hi.astype(i16)

    m1_ref[...] = jnp.full((tk, tq), dmin, i16)
    m2_ref[...] = jnp.full((tk, tq), dmin, i16)
    mc_ref[...] = jnp.zeros((tk, tq), i16)

    def fold_members(off, rows, n_above16):
        hi = rows_at(hi_ref, off, rows)
        n_above16 = n_above16 + _fold_rows(jnp.where(hi > t_hi16, one16, zero16), BF16_ROWS)
        member = hi == t_hi16
        x = jnp.where(member, rows_at(lo_ref, off, rows), jnp.asarray(dmin, i16))
        lo_ref[pl.ds(off, rows), :] = x
        a, b = m1_ref[...], m2_ref[...]
        above = x > a
        m1_ref[...] = jnp.where(above, x, a)
        second = jnp.where(above, a, x)
        m2_ref[...] = jnp.where(second > b, second, b)
        mc_ref[...] = mc_ref[...] + jnp.where(member, one16, zero16)
        return n_above16

    n_above = per_query(for_key_tiles(fold_members, jnp.zeros((BF16_ROWS, tq), i16)))
    need_lo = topk - n_above

    def count_slots(flag):
        return per_query(_fold_rows(flag(m1_ref[...]) + flag(m2_ref[...]), BF16_ROWS))

    def level2_slots():
        t = radix_search(lambda c16: count_slots(lambda m: jnp.where(m >= c16, one16, zero16)), need_lo)
        t16 = t.astype(i16)
        n_members = per_query(_fold_rows(mc_ref[...], BF16_ROWS))
        gt = count_slots(lambda m: jnp.where(m > t16, one16, zero16))
        ge = jnp.where(t == dmin, n_members, count_slots(lambda m: jnp.where(m >= t16, one16, zero16)))
        return t, gt, ge

    def level2_full():
        t = radix_search(lambda c16: count(
            lambda off, rows: jnp.where(rows_at(lo_ref, off, rows) >= c16, one16, zero16)), need_lo)
        t16 = t.astype(i16)
        gt = count(lambda off, rows: jnp.where(rows_at(lo_ref, off, rows) > t16, one16, zero16))
        ge = count(lambda off, rows: jnp.where(
            rows_at(hi_ref, off, rows) == t_hi16,
            jnp.where(rows_at(lo_ref, off, rows) >= t16, one16, zero16), zero16))
        return t, gt, ge

    crowded = per_query(_fold_rows(jnp.where(mc_ref[...] > jnp.asarray(2, i16), one16, zero16), BF16_ROWS))
    t_lo, n_gt_lo, n_ge_lo = lax.cond(jnp.max(crowded) > 0.0, level2_full, level2_slots)
    t_lo16 = t_lo.astype(i16)
    n_gt = n_above + n_gt_lo
    n_ge = n_above + n_ge_lo
    room = topk - n_gt
    sentinel = t_hi == dmin
    has_ties = jnp.where(sentinel, 0.0, jnp.where(n_ge > topk, 1.0, 0.0))
    any_ties = jnp.max(has_ties) > 0.0

    def tie_cutoff():
        def tied_rows(j):
            off = pl.multiple_of(j * tk, tk)
            return jnp.where(rows_at(hi_ref, off, tk) == t_hi16,
                             jnp.where(rows_at(lo_ref, off, tk) == t_lo16, one16, zero16), zero16)

        def count_tile(j, carry):
            tc_ref[pl.ds(j, 1), :] = per_query(_fold_rows(tied_rows(j), BF16_ROWS))
            return carry
        lax.fori_loop(0, n_tiles, count_tile, 0)

        def find_tile(j, carry):
            seen, tile_of, left = carry
            here = tc_ref[pl.ds(j, 1), :]
            crossing = jnp.logical_and(seen < room, seen + here >= room)
            return (seen + here, jnp.where(crossing, j, tile_of), jnp.where(crossing, room - seen, left))
        _, tile_of, left = lax.fori_loop(
            0, n_tiles, find_tile, (jnp.zeros((1, tq), F32), jnp.zeros((1, tq), jnp.int32), room))

        my_ref[...] = jnp.zeros((tk, tq), i16)
        tile_of16 = tile_of.astype(i16)

        def gather_tile(j, carry):
            mine = jnp.where(tile_of16 == j.astype(i16), jnp.asarray(-1, i16), zero16)
            my_ref[...] = my_ref[...] | (tied_rows(j) & mine)
            return carry
        lax.fori_loop(0, n_tiles, gather_tile, 0)

        row16 = lax.broadcasted_iota(jnp.int32, (tk, tq), 0).astype(i16)
        row_bits = tk.bit_length()

        def row_step(b, cut):
            cand = cut + jnp.left_shift(jnp.int32(1), row_bits - 1 - b)
            below = jnp.where(row16 < cand.astype(i16), my_ref[...], zero16)
            return jnp.where(per_query(_fold_rows(below, BF16_ROWS)) <= left, cand, cut)
        row_cut = lax.fori_loop(0, row_bits, row_step, jnp.zeros((1, tq), jnp.int32))
        return tile_of, jnp.minimum(row_cut, tk), row16

    def write_plain():
        lo_min16 = jnp.where(sentinel, dmax, t_lo).astype(i16)

        def write_rows(off, rows, carry):
            hi = rows_at(hi_ref, off, rows)
            inner = jnp.where(rows_at(lo_ref, off, rows) >= lo_min16, sel0, seln)
            bias_ref[pl.ds(off, rows), :] = jnp.where(hi > t_hi16, sel0,
                                                      jnp.where(hi == t_hi16, inner, seln))
            return carry
        for_key_tiles(write_rows, 0)

    def write_with_ties():
        tile_of, row_cut, row16 = tie_cutoff()
        tile_of = jnp.where(sentinel, 0, tile_of)
        row_cut = jnp.where(sentinel, 0, row_cut)

        def write_rows(off, rows, carry):
            hi = rows_at(hi_ref, off, rows)
            lo = rows_at(lo_ref, off, rows)
            j = off // tk
            rows_in = jnp.where(tile_of > j, tk, jnp.where(tile_of == j, row_cut, 0)).astype(i16)
            tie = jnp.where(row16 < rows_in, sel0, seln)
            inner = jnp.where(lo > t_lo16, sel0, jnp.where(lo == t_lo16, tie, seln))
            bias_ref[pl.ds(off, rows), :] = jnp.where(hi > t_hi16, sel0,
                                                      jnp.where(hi == t_hi16, inner, seln))
            return carry
        for_key_tiles(write_rows, 0)

    lax.cond(any_ties, write_with_ties, write_plain)

    def fill_tile(j, carry):
        off = pl.multiple_of(j * tk, tk)
        bias_ref[pl.ds(off, tk), :] = jnp.full((tk, tq), NEG, BF16)
        return carry

    lax.fori_loop(n_tiles, seq // tk, fill_tile, 0)


def _select(qi, ki, wit):
    s = qi.shape[0]
    assert s % TK_SEL == 0 and s % TQ_SEL == 0 and 2 * TOPK <= s <= 32767
    return pl.pallas_call(
        functools.partial(_select_kernel, seq=s),
        grid=(s // TQ_SEL,),
        in_specs=[pl.BlockSpec((TQ_SEL, N_HEADS * LANE), lambda i: (i, 0)),
                  pl.BlockSpec((s, LANE), lambda i: (0, 0)),
                  pl.BlockSpec((BF16_ROWS, TQ_SEL), lambda i: (0, i))],
        out_specs=pl.BlockSpec((s, TQ_SEL), lambda i: (0, i)),
        out_shape=jax.ShapeDtypeStruct((s, s), BF16),
        scratch_shapes=[pltpu.VMEM((s, TQ_SEL), jnp.int16), pltpu.VMEM((s, TQ_SEL), jnp.int16)]
        + [pltpu.VMEM((TK_SEL, TQ_SEL), jnp.int16)] * 4 + [pltpu.VMEM((s // TK_SEL, TQ_SEL), F32)],
        compiler_params=_params(),
        name="dsa_select",
    )(qi, ki, wit)


def _flash_kernel(qi_of, kj_of, *refs, alibi, use_bias, qk_ahead):
    if use_bias:
        q_ref, k_ref, vt_ref, bias_ref, o_ref, m_ref, acc_ref = refs
    else:
        q_ref, k_ref, vt_ref, o_ref, m_ref, acc_ref = refs
    tq, tk = TQ_ATT, TK_ATT
    p = pl.program_id(0)
    qi = qi_of[p]
    kj = kj_of[p]
    last = ((qi + 1) * tq - 1) // tk

    @pl.when(kj == 0)
    def _():
        m_ref[...] = jnp.full_like(m_ref, NEG)
        acc_ref[...] = jnp.zeros_like(acc_ref)

    def step(diagonal, live=tk):
        base = bias_ref[0:live, :].astype(F32) if use_bias else None
        if diagonal:
            tpos = qi * tq + lax.broadcasted_iota(jnp.int32, (live, tq), 1)
            spos = kj * tk + lax.broadcasted_iota(jnp.int32, (live, tq), 0)
            if not use_bias:
                base = jnp.where(spos < (tpos // CHUNK + 1) * CHUNK, 0.0, NEG)
            if alibi:
                ahead = jnp.maximum(spos - tpos, 0).astype(F32)

        def logits(h):
            s = _dot_nt(k_ref[0:live, h * LANE:(h + 1) * LANE], q_ref[:, h * LANE:(h + 1) * LANE])
            if diagonal and alibi:
                return s + (base - (2.0 * ALIBI_SLOPES[h] * LOG2E) * ahead)
            return s if base is None else s + base

        pending = [logits(h) for h in range(qk_ahead)]
        for h in range(N_HEADS):
            rows = slice(h * V_ROWS, (h + 1) * V_ROWS)
            if h + qk_ahead < N_HEADS:
                pending.append(logits(h + qk_ahead))
            s = pending.pop(0)
            m_prev = m_ref[h:h + 1, :]
            m_new = jnp.maximum(m_prev, jnp.max(s, axis=0, keepdims=True))
            alpha = jnp.exp2(m_prev - m_new)
            pexp = jnp.exp2(s - m_new)
            acc_ref[rows, :] = alpha * acc_ref[rows, :] + _dot(vt_ref[rows, 0:live], pexp.astype(BF16))
            m_ref[h:h + 1, :] = m_new

    @pl.when(kj != last)
    def _():
        step(False)

    blocks_per_tile = max(tk // tq, 1)
    for v in range(blocks_per_tile):
        @pl.when(jnp.logical_and(kj == last, qi % blocks_per_tile == v))
        def _(v=v):
            step(True, live=min((v + 1) * tq, tk))

    @pl.when(kj == last)
    def _():
        outs = [acc_ref[h * V_ROWS:h * V_ROWS + D_V, :] / acc_ref[h * V_ROWS + D_V:h * V_ROWS + D_V + 1, :]
                for h in range(N_HEADS)]
        o_ref[...] = jnp.concatenate(outs, axis=0).T.astype(BF16)


def _flash(q, k, vt, bias, *, alibi):
    s = q.shape[0]
    nq = s // TQ_ATT
    pairs = [(i, j) for i in range(nq) for j in range(((i + 1) * TQ_ATT - 1) // TK_ATT + 1)]
    qi_of = jnp.asarray(np.array([a for a, _ in pairs], np.int32))
    kj_of = jnp.asarray(np.array([b for _, b in pairs], np.int32))
    use_bias = bias is not None
    in_specs = [pl.BlockSpec((TQ_ATT, N_HEADS * LANE), lambda p, qi, kj: (qi[p], 0)),
                pl.BlockSpec((TK_ATT, k.shape[1]), lambda p, qi, kj: (kj[p], 0)),
                pl.BlockSpec((N_HEADS * V_ROWS, TK_ATT), lambda p, qi, kj: (0, kj[p]))]
    args = [q, k, vt]
    if use_bias:
        in_specs.append(pl.BlockSpec((TK_ATT, TQ_ATT), lambda p, qi, kj: (kj[p], qi[p])))
        args.append(bias)
    kern = functools.partial(_flash_kernel, alibi=alibi, use_bias=use_bias,
                             qk_ahead=QK_AHEAD_MASKED if use_bias else QK_AHEAD)
    return pl.pallas_call(
        kern,
        grid_spec=pltpu.PrefetchScalarGridSpec(
            num_scalar_prefetch=2,
            grid=(len(pairs),),
            in_specs=in_specs,
            out_specs=pl.BlockSpec((TQ_ATT, N_HEADS * D_V), lambda p, qi, kj: (qi[p], 0)),
            scratch_shapes=[pltpu.VMEM((N_HEADS, TQ_ATT), F32),
                            pltpu.VMEM((N_HEADS * V_ROWS, TQ_ATT), F32)]),
        out_shape=jax.ShapeDtypeStruct((s, N_HEADS * D_V), BF16),
        compiler_params=_params(),
        name="flash_dsa" if use_bias else "flash_mla",
    )(qi_of, kj_of, *args)


def _mix_ffn_kernel(ya_ref, yb_ref, x_ref, mod_ref, wo_ref, gm_ref, bm_ref,
                    wup_ref, cw_ref, wdn_ref, gf_ref, bf_ref, o_ref, carry_ref, act_ref):
    d = D_MODEL

    @pl.when(pl.program_id(0) == 0)
    def _():
        carry_ref[...] = jnp.zeros_like(carry_ref)

    na = ya_ref.shape[1]
    y = _dot(ya_ref[...], wo_ref[0:na, :]) + _dot(yb_ref[...], wo_ref[na:, :])
    x = _layer_norm(DN_ALPHA * x_ref[...] + (1.0 + mod_ref[:, 2 * d:3 * d]) * y, gm_ref[...], bm_ref[...])
    h = (x * (1.0 + mod_ref[:, 4 * d:5 * d]) + mod_ref[:, 3 * d:4 * d]).astype(BF16)
    for c in range(D_FF // FFN_CHUNK):
        va = c * FFN_CHUNK
        ga = D_FF + va
        uv = _dot(h, wup_ref[:, va:va + FFN_CHUNK])
        ug = _dot(h, wup_ref[:, ga:ga + FFN_CHUNK])
        val = _causal_conv3(uv, carry_ref[:, va:va + FFN_CHUNK], cw_ref[:, va:va + FFN_CHUNK])
        gate = _causal_conv3(ug, carry_ref[:, ga:ga + FFN_CHUNK], cw_ref[:, ga:ga + FFN_CHUNK])
        carry_ref[:, va:va + FFN_CHUNK] = uv[TM - SUBLANE:TM]
        carry_ref[:, ga:ga + FFN_CHUNK] = ug[TM - SUBLANE:TM]
        act_ref[:, va:va + FFN_CHUNK] = (gate * jax.nn.sigmoid(gate) * val).astype(BF16)
    y = _dot(act_ref[...], wdn_ref[...])
    z = DN_ALPHA * x + (1.0 + mod_ref[:, 5 * d:6 * d]) * y
    o_ref[...] = _layer_norm(z, gf_ref[...], bf_ref[...])


def _mix_ffn(ya, yb, x, mod, w_out, g_mix, b_mix, layer, w_up_all, conv_w_all, w_down_all, g_ffn, b_ffn):
    s, d = x.shape
    assert D_FF % FFN_CHUNK == 0
    row = lambda n: pl.BlockSpec((TM, n), lambda i: (i, 0))
    full = lambda a: pl.BlockSpec(a.shape, lambda i: (0,) * a.ndim)
    once = lambda a: pl.BlockSpec(a.shape, lambda i: (0,) * a.ndim, pipeline_mode=pl.Buffered(1))
    of_layer = lambda a, **kw: pl.BlockSpec((None,) + a.shape[1:], lambda i: (layer, 0, 0), **kw)
    vec = lambda a: a.reshape(1, d)
    args = [ya, yb, x, mod, w_out, vec(g_mix), vec(b_mix), w_up_all, conv_w_all, w_down_all,
            vec(g_ffn), vec(b_ffn)]
    specs = [row(ya.shape[1]), row(yb.shape[1]), row(d), full(mod), once(w_out), full(args[5]), full(args[6]),
             of_layer(w_up_all, pipeline_mode=pl.Buffered(1)), of_layer(conv_w_all),
             of_layer(w_down_all, pipeline_mode=pl.Buffered(1)), full(args[10]), full(args[11])]
    return pl.pallas_call(
        _mix_ffn_kernel,
        grid=(s // TM,),
        in_specs=specs,
        out_specs=row(d),
        out_shape=jax.ShapeDtypeStruct((s, d), F32),
        scratch_shapes=[pltpu.VMEM((SUBLANE, 2 * D_FF), F32), pltpu.VMEM((TM, D_FF), BF16)],
        compiler_params=_params(),
        name="mix_ffn",
    )(*args)


_O_U = (0, 512)
_O_QLAT = (512, 896)
_O_KVLAT = (896, 1152)
_O_KR = (1152, 1280)
_O_KR_ROT = (1280, 1408)
_O_COLS = 1408


def _rotate_half_cols(w):
    half = w.shape[-1] // 2
    return jnp.concatenate([-w[..., half:], w[..., :half]], axis=-1)


def _odd_weights(w_in, w_uq, w_ukv):
    d = w_in.shape[0]
    o = C_WIDTH + Q_LORA + KV_LORA
    kr = w_in[:, o:o + D_ROPE]
    zl = jnp.zeros((d, D_NOPE), w_in.dtype)
    zr = jnp.zeros((d, LANE - D_NOPE - D_ROPE), w_in.dtype)
    w1 = jnp.concatenate([w_in[:, 0:o], zl, kr, zr, zl, _rotate_half_cols(kr), zr], axis=1)
    assert w1.shape[1] == _O_COLS
    dq = D_NOPE + D_ROPE
    uq = w_uq.reshape(Q_LORA, N_HEADS, dq)
    padq = jnp.zeros((Q_LORA, N_HEADS, LANE - dq), w_uq.dtype)
    wq1 = jnp.concatenate([uq, padq], axis=2).reshape(Q_LORA, N_HEADS * LANE)
    wq2 = jnp.concatenate([jnp.zeros((Q_LORA, N_HEADS, D_NOPE), w_uq.dtype),
                           _rotate_half_cols(uq[:, :, D_NOPE:]), padq], axis=2
                          ).reshape(Q_LORA, N_HEADS * LANE)
    ukv = w_ukv.reshape(KV_LORA, N_HEADS, D_NOPE + D_V)
    wk = jnp.concatenate([ukv[:, :, :D_NOPE], jnp.zeros((KV_LORA, N_HEADS, LANE - D_NOPE), w_ukv.dtype)],
                         axis=2).reshape(KV_LORA, N_HEADS * LANE)
    wvt = ukv[:, :, D_NOPE:].reshape(KV_LORA, N_HEADS * D_V).T
    return w1.astype(BF16), wq1.astype(BF16), wq2.astype(BF16), wk.astype(BF16), wvt.astype(BF16)


def _rope_lane_tables(seq):
    inv = ROPE_BASE ** (-jnp.arange(0, D_ROPE, 2, dtype=F32) / D_ROPE)
    ang = jnp.arange(seq, dtype=F32)[:, None] * inv[None, :]
    cos, sin = lax.optimization_barrier((jnp.cos(ang), jnp.sin(ang)))
    pad = jnp.zeros((seq, LANE - D_NOPE - D_ROPE), F32)
    cos_t = jnp.concatenate([jnp.ones((seq, D_NOPE), F32), cos, cos, pad], axis=1)
    sin_t = jnp.concatenate([jnp.zeros((seq, D_NOPE), F32), sin, sin, pad], axis=1)
    return cos_t, sin_t


def _odd_in_kernel(x_ref, mod_ref, w1_ref, wq1_ref, wq2_ref, wk_ref, wvt_ref, pw_ref, ps_ref,
                   qg_ref, kvg_ref, cos_ref, sin_ref,
                   yc_ref, q_ref, k_ref, vt_ref, carry_ref):
    d = D_MODEL
    i = pl.program_id(0)

    @pl.when(i == 0)
    def _():
        carry_ref[...] = jnp.zeros_like(carry_ref)

    h = (x_ref[...] * (1.0 + mod_ref[:, d:2 * d]) + mod_ref[:, 0:d]).astype(BF16)

    def proj(cols):
        return _dot(h, w1_ref[:, cols[0]:cols[1]])

    cos = cos_ref[...]
    sin = sin_ref[...]

    u = proj(_O_U)
    ext = jnp.concatenate([carry_ref[...], u], axis=0)
    pos = i * TM + lax.broadcasted_iota(jnp.int32, (TM, 1), 0)
    for g, win in enumerate(POOL_WINDOWS):
        sl = slice(g * C_GROUP_DIM, (g + 1) * C_GROUP_DIM)
        acc = ext[:, sl]
        shift = 1
        while shift < win:
            acc = acc + pltpu.roll(acc, shift, 0)
            shift *= 2
        cnt = jnp.minimum(pos + 1, win).astype(F32)
        pooled = acc[POOL_HALO:] / cnt - u[:, sl]
        mixed = _dot(pooled.astype(BF16), pw_ref[g])
        yc_ref[:, sl] = (mixed * ps_ref[:, sl]).astype(BF16)
    carry_ref[...] = u[TM - POOL_HALO:TM]

    r = _rms_norm(proj(_O_QLAT), qg_ref[...]).astype(BF16)
    qa = _dot(r, wq1_ref[...])
    qb = _dot(r, wq2_ref[...])
    kr = proj(_O_KR) * cos + proj(_O_KR_ROT) * sin
    rk = _rms_norm(proj(_O_KVLAT), kvg_ref[...]).astype(BF16)
    kn = _dot(rk, wk_ref[...])
    for hd in range(N_HEADS):
        sl = slice(hd * LANE, (hd + 1) * LANE)
        q_ref[:, sl] = ((qa[:, sl] * cos + qb[:, sl] * sin) * MLA_Q_SCALE).astype(BF16)
        k_ref[:, sl] = (kn[:, sl] + kr).astype(BF16)
    _store_values_t(vt_ref, _dot_nt(wvt_ref[...], rk))


def _odd_in(x, mod, w1, wq1, wq2, wk, wvt, pool_w, pool_scale, q_norm_g, kv_norm_g, cos_t, sin_t):
    s, d = x.shape
    row = lambda n: pl.BlockSpec((TM, n), lambda i: (i, 0))
    col = lambda n: pl.BlockSpec((n, TM), lambda i: (0, i))
    full = lambda a: pl.BlockSpec(a.shape, lambda i: (0,) * a.ndim)
    consts = [w1, wq1, wq2, wk, wvt, pool_w.astype(BF16), pool_scale.reshape(1, C_WIDTH),
              q_norm_g.reshape(1, Q_LORA), kv_norm_g.reshape(1, KV_LORA)]
    rows_out = lambda n, t: (row(n), jax.ShapeDtypeStruct((s, n), t))
    cols_out = lambda n, t: (col(n), jax.ShapeDtypeStruct((n, s), t))
    outs = [rows_out(C_WIDTH, BF16), rows_out(N_HEADS * LANE, BF16), rows_out(N_HEADS * LANE, BF16),
            cols_out(N_HEADS * V_ROWS, BF16)]
    return pl.pallas_call(
        _odd_in_kernel,
        grid=(s // TM,),
        in_specs=[row(d), full(mod)] + [full(a) for a in consts] + [row(LANE), row(LANE)],
        out_specs=[o[0] for o in outs],
        out_shape=[o[1] for o in outs],
        scratch_shapes=[pltpu.VMEM((POOL_HALO, C_WIDTH), F32)],
        compiler_params=_params(),
        name="odd_in",
    )(x, mod, *consts, cos_t, sin_t)


def kernel(x, c, ada_w, ada_b, ln_mix_g, ln_mix_b, ln_ffn_g, ln_ffn_b, ev_w_in, ev_conv_w, ev_w_out,
           od_w_in, pool_w, pool_scale, q_norm_g, w_uq, kv_norm_g, w_ukv, od_w_out,
           ffn_w_up, ffn_conv_w, ffn_w_down):
    bsz, seq, d = x.shape
    assert bsz == 1 and d == D_MODEL and seq % TM == 0 and seq % TK_ATT == 0 and seq % TQ_ATT == 0
    assert SHORT_CONV == 3 and FFN_CONV == 3
    xs = x.reshape(seq, d)
    mods = _adaln(c, ada_w, ada_b)
    w_up_bf16, w_down_bf16 = ffn_w_up.astype(BF16), ffn_w_down.astype(BF16)
    for l in range(DEPTH):
        mod = mods[l]
        if l % 2 == 0:
            e = l // 2
            w, wt = _even_weights(ev_w_in[e])
            q, k, vt, qi, ki, wit, yb = _even_in(xs, mod, w, wt, ev_conv_w[e], _alibi_q_features())
            bias = _select(qi, ki, wit)
            mixed = (_flash(q, k, vt, bias, alibi=True), yb)
            w_out = ev_w_out[e]
        else:
            o = l // 2
            cos_t, sin_t = _rope_lane_tables(seq)
            ws = _odd_weights(od_w_in[o], w_uq[o], w_ukv[o])
            yc, q, k, vt = _odd_in(xs, mod, *ws, pool_w[o], pool_scale[o], q_norm_g[o], kv_norm_g[o],
                                   cos_t, sin_t)
            mixed = (yc, _flash(q, k, vt, None, alibi=False))
            w_out = od_w_out[o]
        xs = _mix_ffn(*mixed, xs, mod, w_out.astype(BF16), ln_mix_g[l], ln_mix_b[l],
                      l, w_up_bf16, ffn_conv_w, w_down_bf16, ln_ffn_g[l], ln_ffn_b[l])
    return xs.reshape(bsz, seq, d)
```

```python
import functools

import numpy as np
import jax
import jax.numpy as jnp
from jax import lax
from jax.experimental import pallas as pl
from jax.experimental.pallas import tpu as pltpu

D_MODEL = 1024
DEPTH = 2
CHUNK = 64
N_HEADS = 8
A_HEAD_DIM = 64
A_WIDTH = N_HEADS * A_HEAD_DIM
IDX_DIM = 64
TOPK = 256
B_WIDTH = 512
SHORT_CONV = 3
C_WIDTH = 512
POOL_WINDOWS = (2, 4, 8, 16)
C_GROUP_DIM = C_WIDTH // len(POOL_WINDOWS)
D_NOPE = 64
D_ROPE = 32
D_V = 64
Q_LORA = 384
KV_LORA = 256
ROPE_BASE = 10000.0
D_FF = 2816
FFN_CONV = 3
LN_EPS = 1e-5
RMS_EPS = 1e-6
DN_ALPHA = (2 * DEPTH) ** 0.25
NEG = -1e30
IDX_W_SCALE = N_HEADS ** -0.5 * IDX_DIM ** -0.5
LOG2E = 1.4426950408889634
DSA_Q_SCALE = A_HEAD_DIM ** -0.5 * LOG2E
MLA_Q_SCALE = (D_NOPE + D_ROPE) ** -0.5 * LOG2E
ALIBI_SLOPES = tuple(2.0 ** (-8.0 * (i + 1) / N_HEADS) for i in range(N_HEADS))

LANE = 128
SUBLANE = 8
BF16_ROWS = 16
V_ROWS = D_V + BF16_ROWS
VMEM_LIMIT_BYTES = 56 * 1024 * 1024

TM = 512
FFN_CHUNK = 256
TQ_SEL = 256
TQ_ATT = 512
TK_SEL = 512
SCORE_GROUP = 4
TK_ATT = 1024
KV_TILES = 2
QK_AHEAD = 4
QK_AHEAD_MASKED = 6
POOL_HALO = 16

INT_MIN = -2 ** 31
F32 = jnp.float32
BF16 = jnp.bfloat16


def _params():
    return pltpu.CompilerParams(dimension_semantics=("arbitrary",),
                                vmem_limit_bytes=VMEM_LIMIT_BYTES)


def _dot(a, b):
    return jnp.dot(a, b, preferred_element_type=F32)


def _dot_nt(a, b):
    return lax.dot_general(a, b, (((1,), (1,)), ((), ())), preferred_element_type=F32)


def _layer_norm(z, g, b):
    mu = jnp.mean(z, axis=-1, keepdims=True)
    zc = z - mu
    var = jnp.mean(zc * zc, axis=-1, keepdims=True)
    return zc * lax.rsqrt(var + LN_EPS) * g + b


def _rms_norm(z, g):
    return z * lax.rsqrt(jnp.mean(z * z, axis=-1, keepdims=True) + RMS_EPS) * g


def _causal_conv3(u, prev, w):
    w0, w1, w2 = w[0:1], w[1:2], w[2:3]
    y = pltpu.roll(u, 2, 0) * w0 + pltpu.roll(u, 1, 0) * w1 + u * w2
    head = u[0:SUBLANE]
    r = lax.broadcasted_iota(jnp.int32, (SUBLANE, 1), 0)
    h1 = jnp.where(r == 0, prev[7:8], pltpu.roll(head, 1, 0))
    h2 = jnp.where(r == 0, prev[6:7], jnp.where(r == 1, prev[7:8], pltpu.roll(head, 2, 0)))
    yh = h2 * w0 + h1 * w1 + head * w2
    return jnp.concatenate([yh, y[SUBLANE:]], axis=0)


def _fold_rows(c, rows_out):
    rows = c.shape[0]
    while rows > rows_out:
        rows //= 2
        c = c[0:rows] + c[rows:2 * rows]
    return c


def _values_t_out(seq):
    per_tile = TK_ATT // TM
    spec = pl.BlockSpec((None, N_HEADS * V_ROWS, TM), lambda i: (i // per_tile, 0, i % per_tile))
    return spec, jax.ShapeDtypeStruct((seq // TK_ATT, N_HEADS * V_ROWS, TK_ATT), BF16)


def _store_values_t(vt_ref, vt):
    t = vt.shape[1]
    ones_row = jnp.where(lax.broadcasted_iota(jnp.int32, (BF16_ROWS, t), 0) == 0, 1.0, 0.0)
    for h in range(N_HEADS):
        vt_ref[h * V_ROWS:h * V_ROWS + D_V, :] = vt[h * D_V:(h + 1) * D_V, :].astype(BF16)
        vt_ref[h * V_ROWS + D_V:(h + 1) * V_ROWS, :] = ones_row.astype(BF16)


def _adaln_kernel(c_ref, w_ref, b_ref, o_ref):
    c = c_ref[...]
    cond = c * jax.nn.sigmoid(c)
    o_ref[0] = jnp.sum(cond * w_ref[0], axis=0, keepdims=True) + b_ref[0]


def _adaln(c, ada_w, ada_b):
    depth, d, n = ada_w.shape
    tn = 1536
    return pl.pallas_call(
        _adaln_kernel,
        grid=(depth, n // tn),
        in_specs=[pl.BlockSpec((d, 1), lambda l, j: (0, 0)),
                  pl.BlockSpec((1, d, tn), lambda l, j: (l, 0, j)),
                  pl.BlockSpec((1, 1, tn), lambda l, j: (l, 0, j))],
        out_specs=pl.BlockSpec((1, 1, tn), lambda l, j: (l, 0, j)),
        out_shape=jax.ShapeDtypeStruct((depth, 1, n), F32),
        compiler_params=pltpu.CompilerParams(dimension_semantics=("arbitrary", "arbitrary"),
                                             vmem_limit_bytes=VMEM_LIMIT_BYTES),
        name="adaln",
    )(c.reshape(d, 1), ada_w, ada_b.reshape(depth, 1, n))


_E_Q = (0, 1024)
_E_K = (1024, 2048)
_E_QI = (2048, 3072)
_E_KI = (3072, 3200)
_E_BG = (3200, 3712)
_E_CG = (3712, 4224)
_E_XB = (4224, 4736)
_E_COLS = 4736
_ET_V = (0, 512)
_ET_WI = (512, 528)
_ET_ROWS = 528
_FEAT_LANE = A_HEAD_DIM
_N_LOG2E_TERMS = 3
_POS_RADIX = 128


def _pad_heads(w):
    d = w.shape[0]
    w3 = w.reshape(d, N_HEADS, A_HEAD_DIM)
    return jnp.concatenate([w3, jnp.zeros_like(w3)], axis=2).reshape(d, N_HEADS * LANE)


def _alibi_q_features():
    terms, rest = [], np.float64(LOG2E)
    for _ in range(_N_LOG2E_TERMS):
        t = np.float64(np.asarray(rest, np.float32).astype(jnp.bfloat16).astype(np.float32))
        terms.append(t)
        rest = rest - t
    row = np.zeros((N_HEADS, LANE), np.float32)
    for h in range(N_HEADS):
        for n, t in enumerate(terms):
            row[h, _FEAT_LANE + n] = ALIBI_SLOPES[h] * _POS_RADIX * t
            row[h, _FEAT_LANE + _N_LOG2E_TERMS + n] = ALIBI_SLOPES[h] * t
    return jnp.asarray(row.reshape(1, N_HEADS * LANE))


def _even_weights(w_in):
    d = w_in.shape[0]
    a = A_WIDTH
    q = _pad_heads(w_in[:, 0:a])
    k = _pad_heads(w_in[:, a:2 * a])
    v = w_in[:, 2 * a:3 * a]
    qi = _pad_heads(w_in[:, 3 * a:4 * a])
    o = 4 * a
    ki = w_in[:, o:o + IDX_DIM]
    wi = w_in[:, o + IDX_DIM:o + IDX_DIM + N_HEADS]
    o = o + IDX_DIM + N_HEADS
    rest = w_in[:, o:o + 3 * B_WIDTH]
    w = jnp.concatenate([q, k, qi, ki, ki, rest], axis=1)
    assert w.shape[1] == _E_COLS
    wt = jnp.concatenate([v.T, wi.T, jnp.zeros((BF16_ROWS - N_HEADS, d), w_in.dtype)], axis=0)
    assert wt.shape[0] == _ET_ROWS
    return w.astype(BF16), wt.astype(BF16)


def _even_in_kernel(x_ref, mod_ref, w_ref, wt_ref, cw_ref, qf_ref,
                    q_ref, k_ref, vt_ref, qi_ref, ki_ref, wit_ref, yb_ref, carry_ref):
    d = D_MODEL
    i = pl.program_id(0)

    @pl.when(i == 0)
    def _():
        carry_ref[...] = jnp.zeros_like(carry_ref)

    h = (x_ref[...] * (1.0 + mod_ref[:, d:2 * d]) + mod_ref[:, 0:d]).astype(BF16)

    def proj(cols):
        return _dot(h, w_ref[:, cols[0]:cols[1]])

    def proj_t(rows):
        return _dot_nt(wt_ref[rows[0]:rows[1], :], h)

    q_ref[...] = (proj(_E_Q) * DSA_Q_SCALE + qf_ref[...]).astype(BF16)
    pos = i * TM + lax.broadcasted_iota(jnp.int32, (TM, LANE), 0)
    lane = lax.broadcasted_iota(jnp.int32, (TM, LANE), 1) - _FEAT_LANE
    pos_hi = (pos // _POS_RADIX).astype(F32)
    pos_lo = (pos % _POS_RADIX).astype(F32)
    kfeat = jnp.where(lane < 0, 0.0,
                      jnp.where(lane < _N_LOG2E_TERMS, pos_hi,
                                jnp.where(lane < 2 * _N_LOG2E_TERMS, pos_lo, 0.0)))
    kproj = proj(_E_K)
    for hd in range(N_HEADS):
        sl = slice(hd * LANE, (hd + 1) * LANE)
        k_ref[:, sl] = (kproj[:, sl] + kfeat).astype(BF16)
    _store_values_t(vt_ref, proj_t(_ET_V))
    qi_ref[...] = proj(_E_QI).astype(BF16)
    ki_ref[...] = proj(_E_KI).astype(BF16)
    wit_ref[...] = proj_t(_ET_WI) * IDX_W_SCALE
    g = proj(_E_CG) * proj(_E_XB)
    y = _causal_conv3(g, carry_ref[...], cw_ref[...])
    yb_ref[...] = (proj(_E_BG) * y).astype(BF16)
    carry_ref[...] = g[TM - SUBLANE:TM]


def _even_in(x, mod, w, wt, conv_w, qfeat):
    s, d = x.shape
    row = lambda n: pl.BlockSpec((TM, n), lambda i: (i, 0))
    col = lambda n: pl.BlockSpec((n, TM), lambda i: (0, i))
    full = lambda a: pl.BlockSpec(a.shape, lambda i: (0,) * a.ndim)
    rows_out = lambda n, t: (row(n), jax.ShapeDtypeStruct((s, n), t))
    cols_out = lambda n, t: (col(n), jax.ShapeDtypeStruct((n, s), t))
    outs = [rows_out(N_HEADS * LANE, BF16), rows_out(N_HEADS * LANE, BF16), _values_t_out(s),
            rows_out(N_HEADS * LANE, BF16), rows_out(LANE, BF16), cols_out(BF16_ROWS, F32),
            rows_out(B_WIDTH, BF16)]
    return pl.pallas_call(
        _even_in_kernel,
        grid=(s // TM,),
        in_specs=[row(d), full(mod), full(w), full(wt), full(conv_w), full(qfeat)],
        out_specs=[o[0] for o in outs],
        out_shape=[o[1] for o in outs],
        scratch_shapes=[pltpu.VMEM((SUBLANE, B_WIDTH), F32)],
        compiler_params=_params(),
        name="even_in",
    )(x, mod, w, wt, conv_w, qfeat)


def _select_kernel(qi_ref, ki_ref, wt_ref, bias_ref, hi_ref, lo_ref, m1_ref, m2_ref, mc_ref, my_ref, tc_ref,
                   *, seq):
    tq, tk = TQ_SEL, TK_SEL
    i16 = jnp.int16
    i = pl.program_id(0)
    start = i * tq
    n_tiles = (start + tq + tk - 1) // tk
    tpos = start + lax.broadcasted_iota(jnp.int32, (1, tq), 1)
    limit = (tpos // CHUNK + 1) * CHUNK
    int_min = jnp.int32(INT_MIN)
    dmin, dmax = -2 ** 15, 2 ** 15 - 1
    zero16, one16 = jnp.zeros((), i16), jnp.ones((), i16)
    sel0, seln = jnp.zeros((), BF16), jnp.full((), NEG, BF16)

    def key_pos(off):
        return off + lax.broadcasted_iota(jnp.int32, (tk, tq), 0)

    def score_tile(j, carry, *, masked, tiles=1):
        for t in range(tiles):
            score_rows(pl.multiple_of((j * tiles + t) * tk, tk), masked)
        return carry

    def score_rows(off, masked):
        kt = ki_ref[pl.ds(off, tk), :]
        score = jnp.zeros((tk, tq), F32)
        for h in range(N_HEADS):
            rel = _dot_nt(kt, qi_ref[:, h * LANE:(h + 1) * LANE])
            score = score + wt_ref[h:h + 1, :] * jnp.maximum(rel, 0.0)
        bits = pltpu.bitcast(score, jnp.int32)
        key = bits ^ ((bits >> 31) & jnp.int32(0x7FFFFFFF))
        key = jnp.where(bits == int_min, 0, key)
        if masked:
            key = jnp.where(key_pos(off) < limit, key, int_min)
        hi_ref[pl.ds(off, tk), :] = (key >> 16).astype(i16)
        lo_ref[pl.ds(off, tk), :] = key.astype(i16) ^ jnp.asarray(dmin, i16)

    n_before = start // tk
    n_groups = n_before // SCORE_GROUP
    lax.fori_loop(0, n_groups, functools.partial(score_tile, masked=False, tiles=SCORE_GROUP), 0)
    lax.fori_loop(SCORE_GROUP * n_groups, n_before, functools.partial(score_tile, masked=False), 0)
    lax.fori_loop(n_before, n_tiles, functools.partial(score_tile, masked=True), 0)

    def rows_at(ref, off, rows):
        return ref[pl.ds(off, rows), :]

    def for_key_tiles(fn, carry):
        return lax.fori_loop(0, n_tiles, lambda j, c: fn(pl.multiple_of(j * tk, tk), tk, c), carry)

    def per_query(c):
        return jnp.sum(c.astype(jnp.int32).astype(F32), axis=0, keepdims=True)

    def count(flag):
        def body(off, rows, acc):
            return acc + _fold_rows(flag(off, rows), BF16_ROWS)
        return per_query(for_key_tiles(body, jnp.zeros((BF16_ROWS, tq), i16)))

    def radix_search(count_ge, need):
        def bit_step(b, thr):
            cand = thr + jnp.left_shift(jnp.int32(1), 15 - b)
            return jnp.where(count_ge(cand.astype(i16)) >= need, cand, thr)
        return lax.fori_loop(0, 16, bit_step, jnp.full((1, tq), dmin, jnp.int32))

    topk = float(TOPK)
    t_hi = radix_search(lambda c16: count(
        lambda off, rows: jnp.where(rows_at(hi_ref, off, rows) >= c16, one16, zero16)), topk)
    t_hi16 = t_hi.astype(i16)

    m1_ref[...] = jnp.full((tk, tq), dmin, i16)
    m2_ref[...] = jnp.full((tk, tq), dmin, i16)
    mc_ref[...] = jnp.zeros((tk, tq), i16)

    def fold_members(off, rows, n_above16):
        hi = rows_at(hi_ref, off, rows)
        n_above16 = n_above16 + _fold_rows(jnp.where(hi > t_hi16, one16, zero16), BF16_ROWS)
        member = hi == t_hi16
        x = jnp.where(member, rows_at(lo_ref, off, rows), jnp.asarray(dmin, i16))
        lo_ref[pl.ds(off, rows), :] = x
        a, b = m1_ref[...], m2_ref[...]
        above = x > a
        m1_ref[...] = jnp.where(above, x, a)
        second = jnp.where(above, a, x)
        m2_ref[...] = jnp.where(second > b, second, b)
        mc_ref[...] = mc_ref[...] + jnp.where(member, one16, zero16)
        return n_above16

    n_above = per_query(for_key_tiles(fold_members, jnp.zeros((BF16_ROWS, tq), i16)))
    need_lo = topk - n_above

    def count_slots(flag):
        return per_query(_fold_rows(flag(m1_ref[...]) + flag(m2_ref[...]), BF16_ROWS))

    def level2_slots():
        t = radix_search(lambda c16: count_slots(lambda m: jnp.where(m >= c16, one16, zero16)), need_lo)
        t16 = t.astype(i16)
        n_members = per_query(_fold_rows(mc_ref[...], BF16_ROWS))
        gt = count_slots(lambda m: jnp.where(m > t16, one16, zero16))
        ge = jnp.where(t == dmin, n_members, count_slots(lambda m: jnp.where(m >= t16, one16, zero16)))
        return t, gt, ge

    def level2_full():
        t = radix_search(lambda c16: count(
            lambda off, rows: jnp.where(rows_at(lo_ref, off, rows) >= c16, one16, zero16)), need_lo)
        t16 = t.astype(i16)
        gt = count(lambda off, rows: jnp.where(rows_at(lo_ref, off, rows) > t16, one16, zero16))
        ge = count(lambda off, rows: jnp.where(
            rows_at(hi_ref, off, rows) == t_hi16,
            jnp.where(rows_at(lo_ref, off, rows) >= t16, one16, zero16), zero16))
        return t, gt, ge

    crowded = per_query(_fold_rows(jnp.where(mc_ref[...] > jnp.asarray(2, i16), one16, zero16), BF16_ROWS))
    t_lo, n_gt_lo, n_ge_lo = lax.cond(jnp.max(crowded) > 0.0, level2_full, level2_slots)
    t_lo16 = t_lo.astype(i16)
    n_gt = n_above + n_gt_lo
    n_ge = n_above + n_ge_lo
    room = topk - n_gt
    sentinel = t_hi == dmin
    has_ties = jnp.where(sentinel, 0.0, jnp.where(n_ge > topk, 1.0, 0.0))
    any_ties = jnp.max(has_ties) > 0.0

    def pos16(off, rows):
        return (off + lax.broadcasted_iota(jnp.int32, (rows, tq), 0)).astype(i16)

    def tie_cutoff():
        def tied_rows(j):
            off = pl.multiple_of(j * tk, tk)
            return jnp.where(rows_at(hi_ref, off, tk) == t_hi16,
                             jnp.where(rows_at(lo_ref, off, tk) == t_lo16, one16, zero16), zero16)

        def count_tile(j, carry):
            tc_ref[pl.ds(j, 1), :] = per_query(_fold_rows(tied_rows(j), BF16_ROWS))
            return carry
        lax.fori_loop(0, n_tiles, count_tile, 0)

        def find_tile(j, carry):
            seen, tile_of, left = carry
            here = tc_ref[pl.ds(j, 1), :]
            crossing = jnp.logical_and(seen < room, seen + here >= room)
            return (seen + here, jnp.where(crossing, j, tile_of), jnp.where(crossing, room - seen, left))
        _, tile_of, left = lax.fori_loop(
            0, n_tiles, find_tile, (jnp.zeros((1, tq), F32), jnp.zeros((1, tq), jnp.int32), room))

        my_ref[...] = jnp.zeros((tk, tq), i16)
        tile_of16 = tile_of.astype(i16)

        def gather_tile(j, carry):
            mine = jnp.where(tile_of16 == j.astype(i16), jnp.asarray(-1, i16), zero16)
            my_ref[...] = my_ref[...] | (tied_rows(j) & mine)
            return carry
        lax.fori_loop(0, n_tiles, gather_tile, 0)

        row16 = lax.broadcasted_iota(jnp.int32, (tk, tq), 0).astype(i16)
        row_bits = tk.bit_length()

        def row_step(b, cut):
            cand = cut + jnp.left_shift(jnp.int32(1), row_bits - 1 - b)
            below = jnp.where(row16 < cand.astype(i16), my_ref[...], zero16)
            return jnp.where(per_query(_fold_rows(below, BF16_ROWS)) <= left, cand, cut)
        row_cut = lax.fori_loop(0, row_bits, row_step, jnp.zeros((1, tq), jnp.int32))
        return tile_of * tk + jnp.minimum(row_cut, tk)

    def write_plain():
        lo_min16 = jnp.where(sentinel, dmax, t_lo).astype(i16)

        def write_rows(off, rows, carry):
            hi = rows_at(hi_ref, off, rows)
            inner = jnp.where(rows_at(lo_ref, off, rows) >= lo_min16, sel0, seln)
            bias_ref[pl.ds(off, rows), :] = jnp.where(hi > t_hi16, sel0,
                                                      jnp.where(hi == t_hi16, inner, seln))
            return carry
        for_key_tiles(write_rows, 0)

    def write_with_ties():
        cut16 = jnp.where(sentinel, 0, tie_cutoff()).astype(i16)

        def write_rows(off, rows, carry):
            hi = rows_at(hi_ref, off, rows)
            lo = rows_at(lo_ref, off, rows)
            tie = jnp.where(pos16(off, rows) < cut16, sel0, seln)
            inner = jnp.where(lo > t_lo16, sel0, jnp.where(lo == t_lo16, tie, seln))
            bias_ref[pl.ds(off, rows), :] = jnp.where(hi > t_hi16, sel0,
                                                      jnp.where(hi == t_hi16, inner, seln))
            return carry
        for_key_tiles(write_rows, 0)

    lax.cond(any_ties, write_with_ties, write_plain)

    def fill_tile(j, carry):
        off = pl.multiple_of(j * tk, tk)
        bias_ref[pl.ds(off, tk), :] = jnp.full((tk, tq), NEG, BF16)
        return carry

    lax.fori_loop(n_tiles, seq // tk, fill_tile, 0)


def _select(qi, ki, wit):
    s = qi.shape[0]
    assert s % TK_SEL == 0 and s % TQ_SEL == 0 and 2 * TOPK <= s <= 32767
    return pl.pallas_call(
        functools.partial(_select_kernel, seq=s),
        grid=(s // TQ_SEL,),
        in_specs=[pl.BlockSpec((TQ_SEL, N_HEADS * LANE), lambda i: (i, 0)),
                  pl.BlockSpec((s, LANE), lambda i: (0, 0)),
                  pl.BlockSpec((BF16_ROWS, TQ_SEL), lambda i: (0, i))],
        out_specs=pl.BlockSpec((s, TQ_SEL), lambda i: (0, i)),
        out_shape=jax.ShapeDtypeStruct((s, s), BF16),
        scratch_shapes=[pltpu.VMEM((s, TQ_SEL), jnp.int16), pltpu.VMEM((s, TQ_SEL), jnp.int16)]
        + [pltpu.VMEM((TK_SEL, TQ_SEL), jnp.int16)] * 4 + [pltpu.VMEM((s // TK_SEL, TQ_SEL), F32)],
        compiler_params=_params(),
        name="dsa_select",
    )(qi, ki, wit)


def _flash_kernel(qi_of, kj_of, *refs, alibi, use_bias, qk_ahead):
    if use_bias:
        q_ref, k_ref, vt_ref, bias_ref, o_ref, m_ref, acc_ref = refs
    else:
        q_ref, k_ref, vt_ref, o_ref, m_ref, acc_ref = refs
    tq, tk = TQ_ATT, TK_ATT
    p = pl.program_id(0)
    qi = qi_of[p]
    kb = kj_of[p]
    last = ((qi + 1) * tq - 1) // tk
    last_block = last // KV_TILES
    n_sub = jnp.where(kb == last_block, last - kb * KV_TILES + 1, KV_TILES)

    @pl.when(kb == 0)
    def _():
        m_ref[...] = jnp.full_like(m_ref, NEG)
        acc_ref[...] = jnp.zeros_like(acc_ref)

    def step(t, kj, diagonal, live=tk):
        row0 = pl.multiple_of(t * tk, tk)
        base = bias_ref[pl.ds(row0, live), :].astype(F32) if use_bias else None
        if diagonal:
            tpos = qi * tq + lax.broadcasted_iota(jnp.int32, (live, tq), 1)
            spos = kj * tk + lax.broadcasted_iota(jnp.int32, (live, tq), 0)
            if not use_bias:
                base = jnp.where(spos < (tpos // CHUNK + 1) * CHUNK, 0.0, NEG)
            if alibi:
                ahead = jnp.maximum(spos - tpos, 0).astype(F32)

        def logits(h):
            s = _dot_nt(k_ref[pl.ds(row0, live), h * LANE:(h + 1) * LANE], q_ref[:, h * LANE:(h + 1) * LANE])
            if diagonal and alibi:
                return s + (base - (2.0 * ALIBI_SLOPES[h] * LOG2E) * ahead)
            return s if base is None else s + base

        pending = [logits(h) for h in range(qk_ahead)]
        for h in range(N_HEADS):
            rows = slice(h * V_ROWS, (h + 1) * V_ROWS)
            if h + qk_ahead < N_HEADS:
                pending.append(logits(h + qk_ahead))
            s = pending.pop(0)
            m_prev = m_ref[h:h + 1, :]
            m_new = jnp.maximum(m_prev, jnp.max(s, axis=0, keepdims=True))
            alpha = jnp.exp2(m_prev - m_new)
            pexp = jnp.exp2(s - m_new)
            acc_ref[rows, :] = alpha * acc_ref[rows, :] + _dot(vt_ref[t, rows, 0:live], pexp.astype(BF16))
            m_ref[h:h + 1, :] = m_new

    blocks_per_tile = max(tk // tq, 1)

    def one_tile(t, carry):
        kj = kb * KV_TILES + t

        @pl.when(kj != last)
        def _():
            step(t, kj, False)

        for v in range(blocks_per_tile):
            @pl.when(jnp.logical_and(kj == last, qi % blocks_per_tile == v))
            def _(v=v):
                step(t, kj, True, live=min((v + 1) * tq, tk))
        return carry

    lax.fori_loop(0, n_sub, one_tile, 0)

    @pl.when(kb == last_block)
    def _():
        outs = [acc_ref[h * V_ROWS:h * V_ROWS + D_V, :] / acc_ref[h * V_ROWS + D_V:h * V_ROWS + D_V + 1, :]
                for h in range(N_HEADS)]
        o_ref[...] = jnp.concatenate(outs, axis=0).T.astype(BF16)


def _flash(q, k, vt, bias, *, alibi):
    s = q.shape[0]
    nq = s // TQ_ATT
    tkb = KV_TILES * TK_ATT
    assert s % tkb == 0
    pairs = [(i, j) for i in range(nq) for j in range(((i + 1) * TQ_ATT - 1) // tkb + 1)]
    qi_of = jnp.asarray(np.array([a for a, _ in pairs], np.int32))
    kj_of = jnp.asarray(np.array([b for _, b in pairs], np.int32))
    use_bias = bias is not None
    in_specs = [pl.BlockSpec((TQ_ATT, N_HEADS * LANE), lambda p, qi, kj: (qi[p], 0)),
                pl.BlockSpec((tkb, k.shape[1]), lambda p, qi, kj: (kj[p], 0)),
                pl.BlockSpec((KV_TILES, N_HEADS * V_ROWS, TK_ATT), lambda p, qi, kj: (kj[p], 0, 0))]
    args = [q, k, vt]
    if use_bias:
        in_specs.append(pl.BlockSpec((tkb, TQ_ATT), lambda p, qi, kj: (kj[p], qi[p])))
        args.append(bias)
    kern = functools.partial(_flash_kernel, alibi=alibi, use_bias=use_bias,
                             qk_ahead=QK_AHEAD_MASKED if use_bias else QK_AHEAD)
    return pl.pallas_call(
        kern,
        grid_spec=pltpu.PrefetchScalarGridSpec(
            num_scalar_prefetch=2,
            grid=(len(pairs),),
            in_specs=in_specs,
            out_specs=pl.BlockSpec((TQ_ATT, N_HEADS * D_V), lambda p, qi, kj: (qi[p], 0)),
            scratch_shapes=[pltpu.VMEM((N_HEADS, TQ_ATT), F32),
                            pltpu.VMEM((N_HEADS * V_ROWS, TQ_ATT), F32)]),
        out_shape=jax.ShapeDtypeStruct((s, N_HEADS * D_V), BF16),
        compiler_params=_params(),
        name="flash_dsa" if use_bias else "flash_mla",
    )(qi_of, kj_of, *args)


def _mix_ffn_kernel(ya_ref, yb_ref, x_ref, mod_ref, wo_ref, gm_ref, bm_ref,
                    wup_ref, cw_ref, wdn_ref, gf_ref, bf_ref, o_ref, carry_ref, act_ref):
    d = D_MODEL

    @pl.when(pl.program_id(0) == 0)
    def _():
        carry_ref[...] = jnp.zeros_like(carry_ref)

    na = ya_ref.shape[1]
    y = _dot(ya_ref[...], wo_ref[0:na, :]) + _dot(yb_ref[...], wo_ref[na:, :])
    x = _layer_norm(DN_ALPHA * x_ref[...] + (1.0 + mod_ref[:, 2 * d:3 * d]) * y, gm_ref[...], bm_ref[...])
    h = (x * (1.0 + mod_ref[:, 4 * d:5 * d]) + mod_ref[:, 3 * d:4 * d]).astype(BF16)
    for c in range(D_FF // FFN_CHUNK):
        va = c * FFN_CHUNK
        ga = D_FF + va
        uv = _dot(h, wup_ref[:, va:va + FFN_CHUNK])
        ug = _dot(h, wup_ref[:, ga:ga + FFN_CHUNK])
        val = _causal_conv3(uv, carry_ref[:, va:va + FFN_CHUNK], cw_ref[:, va:va + FFN_CHUNK])
        gate = _causal_conv3(ug, carry_ref[:, ga:ga + FFN_CHUNK], cw_ref[:, ga:ga + FFN_CHUNK])
        carry_ref[:, va:va + FFN_CHUNK] = uv[TM - SUBLANE:TM]
        carry_ref[:, ga:ga + FFN_CHUNK] = ug[TM - SUBLANE:TM]
        act_ref[:, va:va + FFN_CHUNK] = (gate * jax.nn.sigmoid(gate) * val).astype(BF16)
    y = _dot(act_ref[...], wdn_ref[...])
    z = DN_ALPHA * x + (1.0 + mod_ref[:, 5 * d:6 * d]) * y
    o_ref[...] = _layer_norm(z, gf_ref[...], bf_ref[...])


def _mix_ffn(ya, yb, x, mod, w_out, g_mix, b_mix, layer, w_up_all, conv_w_all, w_down_all, g_ffn, b_ffn):
    s, d = x.shape
    assert D_FF % FFN_CHUNK == 0
    row = lambda n: pl.BlockSpec((TM, n), lambda i: (i, 0))
    full = lambda a: pl.BlockSpec(a.shape, lambda i: (0,) * a.ndim)
    once = lambda a: pl.BlockSpec(a.shape, lambda i: (0,) * a.ndim, pipeline_mode=pl.Buffered(1))
    of_layer = lambda a, **kw: pl.BlockSpec((None,) + a.shape[1:], lambda i: (layer, 0, 0), **kw)
    vec = lambda a: a.reshape(1, d)
    args = [ya, yb, x, mod, w_out, vec(g_mix), vec(b_mix), w_up_all, conv_w_all, w_down_all,
            vec(g_ffn), vec(b_ffn)]
    specs = [row(ya.shape[1]), row(yb.shape[1]), row(d), full(mod), once(w_out), full(args[5]), full(args[6]),
             of_layer(w_up_all, pipeline_mode=pl.Buffered(1)), of_layer(conv_w_all),
             of_layer(w_down_all, pipeline_mode=pl.Buffered(1)), full(args[10]), full(args[11])]
    return pl.pallas_call(
        _mix_ffn_kernel,
        grid=(s // TM,),
        in_specs=specs,
        out_specs=row(d),
        out_shape=jax.ShapeDtypeStruct((s, d), F32),
        scratch_shapes=[pltpu.VMEM((SUBLANE, 2 * D_FF), F32), pltpu.VMEM((TM, D_FF), BF16)],
        compiler_params=_params(),
        name="mix_ffn",
    )(*args)


_O_U = (0, 512)
_O_QLAT = (512, 896)
_O_KVLAT = (896, 1152)
_O_KR = (1152, 1280)
_O_KR_ROT = (1280, 1408)
_O_COLS = 1408


def _rotate_half_cols(w):
    half = w.shape[-1] // 2
    return jnp.concatenate([-w[..., half:], w[..., :half]], axis=-1)


def _odd_weights(w_in, w_uq, w_ukv):
    d = w_in.shape[0]
    o = C_WIDTH + Q_LORA + KV_LORA
    kr = w_in[:, o:o + D_ROPE]
    zl = jnp.zeros((d, D_NOPE), w_in.dtype)
    zr = jnp.zeros((d, LANE - D_NOPE - D_ROPE), w_in.dtype)
    w1 = jnp.concatenate([w_in[:, 0:o], zl, kr, zr, zl, _rotate_half_cols(kr), zr], axis=1)
    assert w1.shape[1] == _O_COLS
    dq = D_NOPE + D_ROPE
    uq = w_uq.reshape(Q_LORA, N_HEADS, dq)
    padq = jnp.zeros((Q_LORA, N_HEADS, LANE - dq), w_uq.dtype)
    wq1 = jnp.concatenate([uq, padq], axis=2).reshape(Q_LORA, N_HEADS * LANE)
    wq2 = jnp.concatenate([jnp.zeros((Q_LORA, N_HEADS, D_NOPE), w_uq.dtype),
                           _rotate_half_cols(uq[:, :, D_NOPE:]), padq], axis=2
                          ).reshape(Q_LORA, N_HEADS * LANE)
    ukv = w_ukv.reshape(KV_LORA, N_HEADS, D_NOPE + D_V)
    wk = jnp.concatenate([ukv[:, :, :D_NOPE], jnp.zeros((KV_LORA, N_HEADS, LANE - D_NOPE), w_ukv.dtype)],
                         axis=2).reshape(KV_LORA, N_HEADS * LANE)
    wvt = ukv[:, :, D_NOPE:].reshape(KV_LORA, N_HEADS * D_V).T
    return w1.astype(BF16), wq1.astype(BF16), wq2.astype(BF16), wk.astype(BF16), wvt.astype(BF16)


def _rope_lane_tables(seq):
    inv = ROPE_BASE ** (-jnp.arange(0, D_ROPE, 2, dtype=F32) / D_ROPE)
    ang = jnp.arange(seq, dtype=F32)[:, None] * inv[None, :]
    cos, sin = lax.optimization_barrier((jnp.cos(ang), jnp.sin(ang)))
    pad = jnp.zeros((seq, LANE - D_NOPE - D_ROPE), F32)
    cos_t = jnp.concatenate([jnp.ones((seq, D_NOPE), F32), cos, cos, pad], axis=1)
    sin_t = jnp.concatenate([jnp.zeros((seq, D_NOPE), F32), sin, sin, pad], axis=1)
    return cos_t, sin_t


def _odd_in_kernel(x_ref, mod_ref, w1_ref, wq1_ref, wq2_ref, wk_ref, wvt_ref, pw_ref, ps_ref,
                   qg_ref, kvg_ref, cos_ref, sin_ref,
                   yc_ref, q_ref, k_ref, vt_ref, carry_ref):
    d = D_MODEL
    i = pl.program_id(0)

    @pl.when(i == 0)
    def _():
        carry_ref[...] = jnp.zeros_like(carry_ref)

    h = (x_ref[...] * (1.0 + mod_ref[:, d:2 * d]) + mod_ref[:, 0:d]).astype(BF16)

    def proj(cols):
        return _dot(h, w1_ref[:, cols[0]:cols[1]])

    cos = cos_ref[...]
    sin = sin_ref[...]

    u = proj(_O_U)
    ext = jnp.concatenate([carry_ref[...], u], axis=0)
    pos = i * TM + lax.broadcasted_iota(jnp.int32, (TM, 1), 0)
    for g, win in enumerate(POOL_WINDOWS):
        sl = slice(g * C_GROUP_DIM, (g + 1) * C_GROUP_DIM)
        acc = ext[:, sl]
        shift = 1
        while shift < win:
            acc = acc + pltpu.roll(acc, shift, 0)
            shift *= 2
        cnt = jnp.minimum(pos + 1, win).astype(F32)
        pooled = acc[POOL_HALO:] / cnt - u[:, sl]
        mixed = _dot(pooled.astype(BF16), pw_ref[g])
        yc_ref[:, sl] = (mixed * ps_ref[:, sl]).astype(BF16)
    carry_ref[...] = u[TM - POOL_HALO:TM]

    r = _rms_norm(proj(_O_QLAT), qg_ref[...]).astype(BF16)
    qa = _dot(r, wq1_ref[...])
    qb = _dot(r, wq2_ref[...])
    kr = proj(_O_KR) * cos + proj(_O_KR_ROT) * sin
    rk = _rms_norm(proj(_O_KVLAT), kvg_ref[...]).astype(BF16)
    kn = _dot(rk, wk_ref[...])
    for hd in range(N_HEADS):
        sl = slice(hd * LANE, (hd + 1) * LANE)
        q_ref[:, sl] = ((qa[:, sl] * cos + qb[:, sl] * sin) * MLA_Q_SCALE).astype(BF16)
        k_ref[:, sl] = (kn[:, sl] + kr).astype(BF16)
    _store_values_t(vt_ref, _dot_nt(wvt_ref[...], rk))


def _odd_in(x, mod, w1, wq1, wq2, wk, wvt, pool_w, pool_scale, q_norm_g, kv_norm_g, cos_t, sin_t):
    s, d = x.shape
    row = lambda n: pl.BlockSpec((TM, n), lambda i: (i, 0))
    col = lambda n: pl.BlockSpec((n, TM), lambda i: (0, i))
    full = lambda a: pl.BlockSpec(a.shape, lambda i: (0,) * a.ndim)
    consts = [w1, wq1, wq2, wk, wvt, pool_w.astype(BF16), pool_scale.reshape(1, C_WIDTH),
              q_norm_g.reshape(1, Q_LORA), kv_norm_g.reshape(1, KV_LORA)]
    rows_out = lambda n, t: (row(n), jax.ShapeDtypeStruct((s, n), t))
    cols_out = lambda n, t: (col(n), jax.ShapeDtypeStruct((n, s), t))
    outs = [rows_out(C_WIDTH, BF16), rows_out(N_HEADS * LANE, BF16), rows_out(N_HEADS * LANE, BF16),
            _values_t_out(s)]
    return pl.pallas_call(
        _odd_in_kernel,
        grid=(s // TM,),
        in_specs=[row(d), full(mod)] + [full(a) for a in consts] + [row(LANE), row(LANE)],
        out_specs=[o[0] for o in outs],
        out_shape=[o[1] for o in outs],
        scratch_shapes=[pltpu.VMEM((POOL_HALO, C_WIDTH), F32)],
        compiler_params=_params(),
        name="odd_in",
    )(x, mod, *consts, cos_t, sin_t)


def kernel(x, c, ada_w, ada_b, ln_mix_g, ln_mix_b, ln_ffn_g, ln_ffn_b, ev_w_in, ev_conv_w, ev_w_out,
           od_w_in, pool_w, pool_scale, q_norm_g, w_uq, kv_norm_g, w_ukv, od_w_out,
           ffn_w_up, ffn_conv_w, ffn_w_down):
    bsz, seq, d = x.shape
    assert bsz == 1 and d == D_MODEL and seq % TM == 0 and seq % TK_ATT == 0 and seq % TQ_ATT == 0
    assert SHORT_CONV == 3 and FFN_CONV == 3
    xs = x.reshape(seq, d)
    mods = _adaln(c, ada_w, ada_b)
    w_up_bf16, w_down_bf16 = ffn_w_up.astype(BF16), ffn_w_down.astype(BF16)
    for l in range(DEPTH):
        mod = mods[l]
        if l % 2 == 0:
            e = l // 2
            w, wt = _even_weights(ev_w_in[e])
            q, k, vt, qi, ki, wit, yb = _even_in(xs, mod, w, wt, ev_conv_w[e], _alibi_q_features())
            bias = _select(qi, ki, wit)
            mixed = (_flash(q, k, vt, bias, alibi=True), yb)
            w_out = ev_w_out[e]
        else:
            o = l // 2
            cos_t, sin_t = _rope_lane_tables(seq)
            ws = _odd_weights(od_w_in[o], w_uq[o], w_ukv[o])
            yc, q, k, vt = _odd_in(xs, mod, *ws, pool_w[o], pool_scale[o], q_norm_g[o], kv_norm_g[o],
                                   cos_t, sin_t)
            mixed = (yc, _flash(q, k, vt, None, alibi=False))
            w_out = od_w_out[o]
        xs = _mix_ffn(*mixed, xs, mod, w_out.astype(BF16), ln_mix_g[l], ln_mix_b[l],
                      l, w_up_bf16, ffn_conv_w, w_down_bf16, ln_ffn_g[l], ln_ffn_b[l])
    return xs.reshape(bsz, seq, d)
```

```python
import functools

import numpy as np
import jax
import jax.numpy as jnp
from jax import lax
from jax.experimental import pallas as pl
from jax.experimental.pallas import tpu as pltpu

D_MODEL = 1024
DEPTH = 2
CHUNK = 64
N_HEADS = 8
A_HEAD_DIM = 64
A_WIDTH = N_HEADS * A_HEAD_DIM
IDX_DIM = 64
TOPK = 256
B_WIDTH = 512
SHORT_CONV = 3
C_WIDTH = 512
POOL_WINDOWS = (2, 4, 8, 16)
C_GROUP_DIM = C_WIDTH // len(POOL_WINDOWS)
D_NOPE = 64
D_ROPE = 32
D_V = 64
Q_LORA = 384
KV_LORA = 256
ROPE_BASE = 10000.0
D_FF = 2816
FFN_CONV = 3
LN_EPS = 1e-5
RMS_EPS = 1e-6
DN_ALPHA = (2 * DEPTH) ** 0.25
NEG = -1e30
IDX_W_SCALE = N_HEADS ** -0.5 * IDX_DIM ** -0.5
LOG2E = 1.4426950408889634
DSA_Q_SCALE = A_HEAD_DIM ** -0.5 * LOG2E
MLA_Q_SCALE = (D_NOPE + D_ROPE) ** -0.5 * LOG2E
ALIBI_SLOPES = tuple(2.0 ** (-8.0 * (i + 1) / N_HEADS) for i in range(N_HEADS))

LANE = 128
SUBLANE = 8
BF16_ROWS = 16
V_ROWS = D_V + BF16_ROWS
VMEM_LIMIT_BYTES = 56 * 1024 * 1024

TM = 512
FFN_CHUNK = 256
TQ_SEL = 256
TQ_ATT = 512
TK_SEL = 512
SCORE_GROUP = 4
TK_ATT = 1024
QK_AHEAD = 4
QK_AHEAD_MASKED = 6
POOL_HALO = 16

INT_MIN = -2 ** 31
F32 = jnp.float32
BF16 = jnp.bfloat16


def _params():
    return pltpu.CompilerParams(dimension_semantics=("arbitrary",),
                                vmem_limit_bytes=VMEM_LIMIT_BYTES)


def _resident(a):
    return pl.BlockSpec(a.shape, lambda i: (0,) * a.ndim, pipeline_mode=pl.Buffered(1))


def _dot(a, b):
    return jnp.dot(a, b, preferred_element_type=F32)


def _dot_nt(a, b):
    return lax.dot_general(a, b, (((1,), (1,)), ((), ())), preferred_element_type=F32)


def _layer_norm(z, g, b):
    mu = jnp.mean(z, axis=-1, keepdims=True)
    zc = z - mu
    var = jnp.mean(zc * zc, axis=-1, keepdims=True)
    return zc * lax.rsqrt(var + LN_EPS) * g + b


def _rms_norm(z, g):
    return z * lax.rsqrt(jnp.mean(z * z, axis=-1, keepdims=True) + RMS_EPS) * g


def _causal_conv3(u, prev, w):
    w0, w1, w2 = w[0:1], w[1:2], w[2:3]
    y = pltpu.roll(u, 2, 0) * w0 + pltpu.roll(u, 1, 0) * w1 + u * w2
    head = u[0:SUBLANE]
    r = lax.broadcasted_iota(jnp.int32, (SUBLANE, 1), 0)
    h1 = jnp.where(r == 0, prev[7:8], pltpu.roll(head, 1, 0))
    h2 = jnp.where(r == 0, prev[6:7], jnp.where(r == 1, prev[7:8], pltpu.roll(head, 2, 0)))
    yh = h2 * w0 + h1 * w1 + head * w2
    return jnp.concatenate([yh, y[SUBLANE:]], axis=0)


def _fold_rows(c, rows_out):
    rows = c.shape[0]
    while rows > rows_out:
        rows //= 2
        c = c[0:rows] + c[rows:2 * rows]
    return c


def _store_values_t(vt_ref, vt):
    t = vt.shape[1]
    ones_row = jnp.where(lax.broadcasted_iota(jnp.int32, (BF16_ROWS, t), 0) == 0, 1.0, 0.0)
    for h in range(N_HEADS):
        vt_ref[h * V_ROWS:h * V_ROWS + D_V, :] = vt[h * D_V:(h + 1) * D_V, :].astype(BF16)
        vt_ref[h * V_ROWS + D_V:(h + 1) * V_ROWS, :] = ones_row.astype(BF16)


def _adaln_kernel(c_ref, w_ref, b_ref, o_ref):
    c = c_ref[...]
    cond = c * jax.nn.sigmoid(c)
    o_ref[0] = jnp.sum(cond * w_ref[0], axis=0, keepdims=True) + b_ref[0]


def _adaln(c, ada_w, ada_b):
    depth, d, n = ada_w.shape
    tn = 1536
    return pl.pallas_call(
        _adaln_kernel,
        grid=(depth, n // tn),
        in_specs=[pl.BlockSpec((d, 1), lambda l, j: (0, 0)),
                  pl.BlockSpec((1, d, tn), lambda l, j: (l, 0, j)),
                  pl.BlockSpec((1, 1, tn), lambda l, j: (l, 0, j))],
        out_specs=pl.BlockSpec((1, 1, tn), lambda l, j: (l, 0, j)),
        out_shape=jax.ShapeDtypeStruct((depth, 1, n), F32),
        compiler_params=pltpu.CompilerParams(dimension_semantics=("arbitrary", "arbitrary"),
                                             vmem_limit_bytes=VMEM_LIMIT_BYTES),
        name="adaln",
    )(c.reshape(d, 1), ada_w, ada_b.reshape(depth, 1, n))


_E_Q = (0, 1024)
_E_K = (1024, 2048)
_E_QI = (2048, 3072)
_E_KI = (3072, 3200)
_E_BG = (3200, 3712)
_E_CG = (3712, 4224)
_E_XB = (4224, 4736)
_E_COLS = 4736
_ET_V = (0, 512)
_ET_WI = (512, 528)
_ET_ROWS = 528
_FEAT_LANE = A_HEAD_DIM
_N_LOG2E_TERMS = 3
_POS_RADIX = 128


def _pad_heads(w):
    d = w.shape[0]
    w3 = w.reshape(d, N_HEADS, A_HEAD_DIM)
    return jnp.concatenate([w3, jnp.zeros_like(w3)], axis=2).reshape(d, N_HEADS * LANE)


def _alibi_q_features():
    terms, rest = [], np.float64(LOG2E)
    for _ in range(_N_LOG2E_TERMS):
        t = np.float64(np.asarray(rest, np.float32).astype(jnp.bfloat16).astype(np.float32))
        terms.append(t)
        rest = rest - t
    row = np.zeros((N_HEADS, LANE), np.float32)
    for h in range(N_HEADS):
        for n, t in enumerate(terms):
            row[h, _FEAT_LANE + n] = ALIBI_SLOPES[h] * _POS_RADIX * t
            row[h, _FEAT_LANE + _N_LOG2E_TERMS + n] = ALIBI_SLOPES[h] * t
    return jnp.asarray(row.reshape(1, N_HEADS * LANE))


def _even_weights(w_in):
    d = w_in.shape[0]
    a = A_WIDTH
    q = _pad_heads(w_in[:, 0:a])
    k = _pad_heads(w_in[:, a:2 * a])
    v = w_in[:, 2 * a:3 * a]
    qi = _pad_heads(w_in[:, 3 * a:4 * a])
    o = 4 * a
    ki = w_in[:, o:o + IDX_DIM]
    wi = w_in[:, o + IDX_DIM:o + IDX_DIM + N_HEADS]
    o = o + IDX_DIM + N_HEADS
    rest = w_in[:, o:o + 3 * B_WIDTH]
    w = jnp.concatenate([q, k, qi, ki, ki, rest], axis=1)
    assert w.shape[1] == _E_COLS
    wt = jnp.concatenate([v.T, wi.T, jnp.zeros((BF16_ROWS - N_HEADS, d), w_in.dtype)], axis=0)
    assert wt.shape[0] == _ET_ROWS
    return w.astype(BF16), wt.astype(BF16)


def _even_in_kernel(x_ref, mod_ref, w_ref, wt_ref, cw_ref, qf_ref,
                    q_ref, k_ref, vt_ref, qi_ref, ki_ref, wit_ref, yb_ref, carry_ref):
    d = D_MODEL
    i = pl.program_id(0)

    @pl.when(i == 0)
    def _():
        carry_ref[...] = jnp.zeros_like(carry_ref)

    h = (x_ref[...] * (1.0 + mod_ref[:, d:2 * d]) + mod_ref[:, 0:d]).astype(BF16)

    def proj(cols):
        return _dot(h, w_ref[:, cols[0]:cols[1]])

    def proj_t(rows):
        return _dot_nt(wt_ref[rows[0]:rows[1], :], h)

    q_ref[...] = (proj(_E_Q) * DSA_Q_SCALE + qf_ref[...]).astype(BF16)
    pos = i * TM + lax.broadcasted_iota(jnp.int32, (TM, LANE), 0)
    lane = lax.broadcasted_iota(jnp.int32, (TM, LANE), 1) - _FEAT_LANE
    pos_hi = (pos // _POS_RADIX).astype(F32)
    pos_lo = (pos % _POS_RADIX).astype(F32)
    kfeat = jnp.where(lane < 0, 0.0,
                      jnp.where(lane < _N_LOG2E_TERMS, pos_hi,
                                jnp.where(lane < 2 * _N_LOG2E_TERMS, pos_lo, 0.0)))
    kproj = proj(_E_K)
    for hd in range(N_HEADS):
        sl = slice(hd * LANE, (hd + 1) * LANE)
        k_ref[:, sl] = (kproj[:, sl] + kfeat).astype(BF16)
    _store_values_t(vt_ref, proj_t(_ET_V))
    qi_ref[...] = proj(_E_QI).astype(BF16)
    ki_ref[...] = proj(_E_KI).astype(BF16)
    wit_ref[...] = proj_t(_ET_WI) * IDX_W_SCALE
    g = proj(_E_CG) * proj(_E_XB)
    y = _causal_conv3(g, carry_ref[...], cw_ref[...])
    yb_ref[...] = (proj(_E_BG) * y).astype(BF16)
    carry_ref[...] = g[TM - SUBLANE:TM]


def _even_in(x, mod, w, wt, conv_w, qfeat):
    s, d = x.shape
    row = lambda n: pl.BlockSpec((TM, n), lambda i: (i, 0))
    col = lambda n: pl.BlockSpec((n, TM), lambda i: (0, i))
    full = lambda a: pl.BlockSpec(a.shape, lambda i: (0,) * a.ndim)
    rows_out = lambda n, t: (row(n), jax.ShapeDtypeStruct((s, n), t))
    cols_out = lambda n, t: (col(n), jax.ShapeDtypeStruct((n, s), t))
    outs = [rows_out(N_HEADS * LANE, BF16), rows_out(N_HEADS * LANE, BF16), cols_out(N_HEADS * V_ROWS, BF16),
            rows_out(N_HEADS * LANE, BF16), rows_out(LANE, BF16), cols_out(BF16_ROWS, F32),
            rows_out(B_WIDTH, BF16)]
    return pl.pallas_call(
        _even_in_kernel,
        grid=(s // TM,),
        in_specs=[row(d), full(mod), _resident(w), _resident(wt), full(conv_w), full(qfeat)],
        out_specs=[o[0] for o in outs],
        out_shape=[o[1] for o in outs],
        scratch_shapes=[pltpu.VMEM((SUBLANE, B_WIDTH), F32)],
        compiler_params=_params(),
        name="even_in",
    )(x, mod, w, wt, conv_w, qfeat)


def _select_kernel(qi_ref, ki_ref, wt_ref, bias_ref, hi_ref, lo_ref, m1_ref, m2_ref, mc_ref, my_ref, tc_ref,
                   *, seq):
    tq, tk = TQ_SEL, TK_SEL
    i16 = jnp.int16
    i = pl.program_id(0)
    start = i * tq
    n_tiles = (start + tq + tk - 1) // tk
    tpos = start + lax.broadcasted_iota(jnp.int32, (1, tq), 1)
    limit = (tpos // CHUNK + 1) * CHUNK
    int_min = jnp.int32(INT_MIN)
    dmin, dmax = -2 ** 15, 2 ** 15 - 1
    zero16, one16 = jnp.zeros((), i16), jnp.ones((), i16)
    sel0, seln = jnp.zeros((), BF16), jnp.full((), NEG, BF16)

    def key_pos(off):
        return off + lax.broadcasted_iota(jnp.int32, (tk, tq), 0)

    def score_tile(j, carry, *, masked, tiles=1):
        for t in range(tiles):
            score_rows(pl.multiple_of((j * tiles + t) * tk, tk), masked)
        return carry

    def score_rows(off, masked):
        kt = ki_ref[pl.ds(off, tk), :]
        score = jnp.zeros((tk, tq), F32)
        for h in range(N_HEADS):
            rel = _dot_nt(kt, qi_ref[:, h * LANE:(h + 1) * LANE])
            score = score + wt_ref[h:h + 1, :] * jnp.maximum(rel, 0.0)
        bits = pltpu.bitcast(score, jnp.int32)
        key = bits ^ ((bits >> 31) & jnp.int32(0x7FFFFFFF))
        key = jnp.where(bits == int_min, 0, key)
        if masked:
            key = jnp.where(key_pos(off) < limit, key, int_min)
        hi_ref[pl.ds(off, tk), :] = (key >> 16).astype(i16)
        lo_ref[pl.ds(off, tk), :] = key.astype(i16) ^ jnp.asarray(dmin, i16)

    n_before = start // tk
    n_groups = n_before // SCORE_GROUP
    lax.fori_loop(0, n_groups, functools.partial(score_tile, masked=False, tiles=SCORE_GROUP), 0)
    lax.fori_loop(SCORE_GROUP * n_groups, n_before, functools.partial(score_tile, masked=False), 0)
    lax.fori_loop(n_before, n_tiles, functools.partial(score_tile, masked=True), 0)

    def rows_at(ref, off, rows):
        return ref[pl.ds(off, rows), :]

    def for_key_tiles(fn, carry):
        return lax.fori_loop(0, n_tiles, lambda j, c: fn(pl.multiple_of(j * tk, tk), tk, c), carry)

    def per_query(c):
        return jnp.sum(c.astype(jnp.int32).astype(F32), axis=0, keepdims=True)

    def count(flag):
        def body(off, rows, acc):
            return acc + _fold_rows(flag(off, rows), BF16_ROWS)
        return per_query(for_key_tiles(body, jnp.zeros((BF16_ROWS, tq), i16)))

    def radix_search(count_ge, need):
        def bit_step(b, thr):
            cand = thr + jnp.left_shift(jnp.int32(1), 15 - b)
            return jnp.where(count_ge(cand.astype(i16)) >= need, cand, thr)
        return lax.fori_loop(0, 16, bit_step, jnp.full((1, tq), dmin, jnp.int32))

    topk = float(TOPK)
    t_hi = radix_search(lambda c16: count(
        lambda off, rows: jnp.where(rows_at(hi_ref, off, rows) >= c16, one16, zero16)), topk)
    t_hi16 = t_hi.astype(i16)

    m1_ref[...] = jnp.full((tk, tq), dmin, i16)
    m2_ref[...] = jnp.full((tk, tq), dmin, i16)
    mc_ref[...] = jnp.zeros((tk, tq), i16)

    def fold_members(off, rows, n_above16):
        hi = rows_at(hi_ref, off, rows)
        n_above16 = n_above16 + _fold_rows(jnp.where(hi > t_hi16, one16, zero16), BF16_ROWS)
        member = hi == t_hi16
        x = jnp.where(member, rows_at(lo_ref, off, rows), jnp.asarray(dmin, i16))
        lo_ref[pl.ds(off, rows), :] = x
        a, b = m1_ref[...], m2_ref[...]
        above = x > a
        m1_ref[...] = jnp.where(above, x, a)
        second = jnp.where(above, a, x)
        m2_ref[...] = jnp.where(second > b, second, b)
        mc_ref[...] = mc_ref[...] + jnp.where(member, one16, zero16)
        return n_above16

    n_above = per_query(for_key_tiles(fold_members, jnp.zeros((BF16_ROWS, tq), i16)))
    need_lo = topk - n_above

    def count_slots(flag):
        return per_query(_fold_rows(flag(m1_ref[...]) + flag(m2_ref[...]), BF16_ROWS))

    def level2_slots():
        t = radix_search(lambda c16: count_slots(lambda m: jnp.where(m >= c16, one16, zero16)), need_lo)
        t16 = t.astype(i16)
        n_members = per_query(_fold_rows(mc_ref[...], BF16_ROWS))
        gt = count_slots(lambda m: jnp.where(m > t16, one16, zero16))
        ge = jnp.where(t == dmin, n_members, count_slots(lambda m: jnp.where(m >= t16, one16, zero16)))
        return t, gt, ge

    def level2_full():
        t = radix_search(lambda c16: count(
            lambda off, rows: jnp.where(rows_at(lo_ref, off, rows) >= c16, one16, zero16)), need_lo)
        t16 = t.astype(i16)
        gt = count(lambda off, rows: jnp.where(rows_at(lo_ref, off, rows) > t16, one16, zero16))
        ge = count(lambda off, rows: jnp.where(
            rows_at(hi_ref, off, rows) == t_hi16,
            jnp.where(rows_at(lo_ref, off, rows) >= t16, one16, zero16), zero16))
        return t, gt, ge

    crowded = per_query(_fold_rows(jnp.where(mc_ref[...] > jnp.asarray(2, i16), one16, zero16), BF16_ROWS))
    t_lo, n_gt_lo, n_ge_lo = lax.cond(jnp.max(crowded) > 0.0, level2_full, level2_slots)
    t_lo16 = t_lo.astype(i16)
    n_gt = n_above + n_gt_lo
    n_ge = n_above + n_ge_lo
    room = topk - n_gt
    sentinel = t_hi == dmin
    has_ties = jnp.where(sentinel, 0.0, jnp.where(n_ge > topk, 1.0, 0.0))
    any_ties = jnp.max(has_ties) > 0.0

    def pos16(off, rows):
        return (off + lax.broadcasted_iota(jnp.int32, (rows, tq), 0)).astype(i16)

    def tie_cutoff():
        def tied_rows(j):
            off = pl.multiple_of(j * tk, tk)
            return jnp.where(rows_at(hi_ref, off, tk) == t_hi16,
                             jnp.where(rows_at(lo_ref, off, tk) == t_lo16, one16, zero16), zero16)

        def count_tile(j, carry):
            tc_ref[pl.ds(j, 1), :] = per_query(_fold_rows(tied_rows(j), BF16_ROWS))
            return carry
        lax.fori_loop(0, n_tiles, count_tile, 0)

        def find_tile(j, carry):
            seen, tile_of, left = carry
            here = tc_ref[pl.ds(j, 1), :]
            crossing = jnp.logical_and(seen < room, seen + here >= room)
            return (seen + here, jnp.where(crossing, j, tile_of), jnp.where(crossing, room - seen, left))
        _, tile_of, left = lax.fori_loop(
            0, n_tiles, find_tile, (jnp.zeros((1, tq), F32), jnp.zeros((1, tq), jnp.int32), room))

        my_ref[...] = jnp.zeros((tk, tq), i16)
        tile_of16 = tile_of.astype(i16)

        def gather_tile(j, carry):
            mine = jnp.where(tile_of16 == j.astype(i16), jnp.asarray(-1, i16), zero16)
            my_ref[...] = my_ref[...] | (tied_rows(j) & mine)
            return carry
        lax.fori_loop(0, n_tiles, gather_tile, 0)

        row16 = lax.broadcasted_iota(jnp.int32, (tk, tq), 0).astype(i16)
        row_bits = tk.bit_length()

        def row_step(b, cut):
            cand = cut + jnp.left_shift(jnp.int32(1), row_bits - 1 - b)
            below = jnp.where(row16 < cand.astype(i16), my_ref[...], zero16)
            return jnp.where(per_query(_fold_rows(below, BF16_ROWS)) <= left, cand, cut)
        row_cut = lax.fori_loop(0, row_bits, row_step, jnp.zeros((1, tq), jnp.int32))
        return tile_of * tk + jnp.minimum(row_cut, tk)

    def write_plain():
        lo_min16 = jnp.where(sentinel, dmax, t_lo).astype(i16)

        def write_rows(off, rows, carry):
            hi = rows_at(hi_ref, off, rows)
            inner = jnp.where(rows_at(lo_ref, off, rows) >= lo_min16, sel0, seln)
            bias_ref[pl.ds(off, rows), :] = jnp.where(hi > t_hi16, sel0,
                                                      jnp.where(hi == t_hi16, inner, seln))
            return carry
        for_key_tiles(write_rows, 0)

    def write_with_ties():
        cut16 = jnp.where(sentinel, 0, tie_cutoff()).astype(i16)

        def write_rows(off, rows, carry):
            hi = rows_at(hi_ref, off, rows)
            lo = rows_at(lo_ref, off, rows)
            tie = jnp.where(pos16(off, rows) < cut16, sel0, seln)
            inner = jnp.where(lo > t_lo16, sel0, jnp.where(lo == t_lo16, tie, seln))
            bias_ref[pl.ds(off, rows), :] = jnp.where(hi > t_hi16, sel0,
                                                      jnp.where(hi == t_hi16, inner, seln))
            return carry
        for_key_tiles(write_rows, 0)

    lax.cond(any_ties, write_with_ties, write_plain)

    def fill_tile(j, carry):
        off = pl.multiple_of(j * tk, tk)
        bias_ref[pl.ds(off, tk), :] = jnp.full((tk, tq), NEG, BF16)
        return carry

    lax.fori_loop(n_tiles, seq // tk, fill_tile, 0)


def _select(qi, ki, wit):
    s = qi.shape[0]
    assert s % TK_SEL == 0 and s % TQ_SEL == 0 and 2 * TOPK <= s <= 32767
    return pl.pallas_call(
        functools.partial(_select_kernel, seq=s),
        grid=(s // TQ_SEL,),
        in_specs=[pl.BlockSpec((TQ_SEL, N_HEADS * LANE), lambda i: (i, 0)),
                  pl.BlockSpec((s, LANE), lambda i: (0, 0)),
                  pl.BlockSpec((BF16_ROWS, TQ_SEL), lambda i: (0, i))],
        out_specs=pl.BlockSpec((s, TQ_SEL), lambda i: (0, i)),
        out_shape=jax.ShapeDtypeStruct((s, s), BF16),
        scratch_shapes=[pltpu.VMEM((s, TQ_SEL), jnp.int16), pltpu.VMEM((s, TQ_SEL), jnp.int16)]
        + [pltpu.VMEM((TK_SEL, TQ_SEL), jnp.int16)] * 4 + [pltpu.VMEM((s // TK_SEL, TQ_SEL), F32)],
        compiler_params=_params(),
        name="dsa_select",
    )(qi, ki, wit)


def _flash_kernel(qi_of, kj_of, *refs, alibi, use_bias, qk_ahead):
    if use_bias:
        q_ref, k_ref, vt_ref, bias_ref, o_ref, m_ref, acc_ref = refs
    else:
        q_ref, k_ref, vt_ref, o_ref, m_ref, acc_ref = refs
    tq, tk = TQ_ATT, TK_ATT
    p = pl.program_id(0)
    qi = qi_of[p]
    kj = kj_of[p]
    last = ((qi + 1) * tq - 1) // tk

    @pl.when(kj == 0)
    def _():
        m_ref[...] = jnp.full_like(m_ref, NEG)
        acc_ref[...] = jnp.zeros_like(acc_ref)

    def step(diagonal, live=tk):
        base = bias_ref[0:live, :].astype(F32) if use_bias else None
        if diagonal:
            tpos = qi * tq + lax.broadcasted_iota(jnp.int32, (live, tq), 1)
            spos = kj * tk + lax.broadcasted_iota(jnp.int32, (live, tq), 0)
            if not use_bias:
                base = jnp.where(spos < (tpos // CHUNK + 1) * CHUNK, 0.0, NEG)
            if alibi:
                ahead = jnp.maximum(spos - tpos, 0).astype(F32)

        def logits(h):
            s = _dot_nt(k_ref[0:live, h * LANE:(h + 1) * LANE], q_ref[:, h * LANE:(h + 1) * LANE])
            if diagonal and alibi:
                return s + (base - (2.0 * ALIBI_SLOPES[h] * LOG2E) * ahead)
            return s if base is None else s + base

        pending = [logits(h) for h in range(qk_ahead)]
        for h in range(N_HEADS):
            rows = slice(h * V_ROWS, (h + 1) * V_ROWS)
            if h + qk_ahead < N_HEADS:
                pending.append(logits(h + qk_ahead))
            s = pending.pop(0)
            m_prev = m_ref[h:h + 1, :]
            m_new = jnp.maximum(m_prev, jnp.max(s, axis=0, keepdims=True))
            alpha = jnp.exp2(m_prev - m_new)
            pexp = jnp.exp2(s - m_new)
            acc_ref[rows, :] = alpha * acc_ref[rows, :] + _dot(vt_ref[rows, 0:live], pexp.astype(BF16))
            m_ref[h:h + 1, :] = m_new

    @pl.when(kj != last)
    def _():
        step(False)

    blocks_per_tile = max(tk // tq, 1)
    for v in range(blocks_per_tile):
        @pl.when(jnp.logical_and(kj == last, qi % blocks_per_tile == v))
        def _(v=v):
            step(True, live=min((v + 1) * tq, tk))

    @pl.when(kj == last)
    def _():
        outs = [acc_ref[h * V_ROWS:h * V_ROWS + D_V, :] / acc_ref[h * V_ROWS + D_V:h * V_ROWS + D_V + 1, :]
                for h in range(N_HEADS)]
        o_ref[...] = jnp.concatenate(outs, axis=0).T.astype(BF16)


def _flash(q, k, vt, bias, *, alibi):
    s = q.shape[0]
    nq = s // TQ_ATT
    pairs = [(i, j) for i in range(nq) for j in range(((i + 1) * TQ_ATT - 1) // TK_ATT + 1)]
    qi_of = jnp.asarray(np.array([a for a, _ in pairs], np.int32))
    kj_of = jnp.asarray(np.array([b for _, b in pairs], np.int32))
    use_bias = bias is not None
    in_specs = [pl.BlockSpec((TQ_ATT, N_HEADS * LANE), lambda p, qi, kj: (qi[p], 0)),
                pl.BlockSpec((TK_ATT, k.shape[1]), lambda p, qi, kj: (kj[p], 0)),
                pl.BlockSpec((N_HEADS * V_ROWS, TK_ATT), lambda p, qi, kj: (0, kj[p]))]
    args = [q, k, vt]
    if use_bias:
        in_specs.append(pl.BlockSpec((TK_ATT, TQ_ATT), lambda p, qi, kj: (kj[p], qi[p])))
        args.append(bias)
    kern = functools.partial(_flash_kernel, alibi=alibi, use_bias=use_bias,
                             qk_ahead=QK_AHEAD_MASKED if use_bias else QK_AHEAD)
    return pl.pallas_call(
        kern,
        grid_spec=pltpu.PrefetchScalarGridSpec(
            num_scalar_prefetch=2,
            grid=(len(pairs),),
            in_specs=in_specs,
            out_specs=pl.BlockSpec((TQ_ATT, N_HEADS * D_V), lambda p, qi, kj: (qi[p], 0)),
            scratch_shapes=[pltpu.VMEM((N_HEADS, TQ_ATT), F32),
                            pltpu.VMEM((N_HEADS * V_ROWS, TQ_ATT), F32)]),
        out_shape=jax.ShapeDtypeStruct((s, N_HEADS * D_V), BF16),
        compiler_params=_params(),
        name="flash_dsa" if use_bias else "flash_mla",
    )(qi_of, kj_of, *args)


def _mix_ffn_kernel(ya_ref, yb_ref, x_ref, mod_ref, wo_ref, gm_ref, bm_ref,
                    wup_ref, cw_ref, wdn_ref, gf_ref, bf_ref, o_ref, carry_ref, act_ref):
    d = D_MODEL

    @pl.when(pl.program_id(0) == 0)
    def _():
        carry_ref[...] = jnp.zeros_like(carry_ref)

    na = ya_ref.shape[1]
    y = _dot(ya_ref[...], wo_ref[0:na, :]) + _dot(yb_ref[...], wo_ref[na:, :])
    x = _layer_norm(DN_ALPHA * x_ref[...] + (1.0 + mod_ref[:, 2 * d:3 * d]) * y, gm_ref[...], bm_ref[...])
    h = (x * (1.0 + mod_ref[:, 4 * d:5 * d]) + mod_ref[:, 3 * d:4 * d]).astype(BF16)
    for c in range(D_FF // FFN_CHUNK):
        va = c * FFN_CHUNK
        ga = D_FF + va
        uv = _dot(h, wup_ref[:, va:va + FFN_CHUNK])
        ug = _dot(h, wup_ref[:, ga:ga + FFN_CHUNK])
        val = _causal_conv3(uv, carry_ref[:, va:va + FFN_CHUNK], cw_ref[:, va:va + FFN_CHUNK])
        gate = _causal_conv3(ug, carry_ref[:, ga:ga + FFN_CHUNK], cw_ref[:, ga:ga + FFN_CHUNK])
        carry_ref[:, va:va + FFN_CHUNK] = uv[TM - SUBLANE:TM]
        carry_ref[:, ga:ga + FFN_CHUNK] = ug[TM - SUBLANE:TM]
        act_ref[:, va:va + FFN_CHUNK] = (gate * jax.nn.sigmoid(gate) * val).astype(BF16)
    y = _dot(act_ref[...], wdn_ref[...])
    z = DN_ALPHA * x + (1.0 + mod_ref[:, 5 * d:6 * d]) * y
    o_ref[...] = _layer_norm(z, gf_ref[...], bf_ref[...])


def _mix_ffn(ya, yb, x, mod, w_out, g_mix, b_mix, layer, w_up_all, conv_w_all, w_down_all, g_ffn, b_ffn):
    s, d = x.shape
    assert D_FF % FFN_CHUNK == 0
    row = lambda n: pl.BlockSpec((TM, n), lambda i: (i, 0))
    full = lambda a: pl.BlockSpec(a.shape, lambda i: (0,) * a.ndim)
    once = lambda a: pl.BlockSpec(a.shape, lambda i: (0,) * a.ndim, pipeline_mode=pl.Buffered(1))
    of_layer = lambda a, **kw: pl.BlockSpec((None,) + a.shape[1:], lambda i: (layer, 0, 0), **kw)
    vec = lambda a: a.reshape(1, d)
    args = [ya, yb, x, mod, w_out, vec(g_mix), vec(b_mix), w_up_all, conv_w_all, w_down_all,
            vec(g_ffn), vec(b_ffn)]
    specs = [row(ya.shape[1]), row(yb.shape[1]), row(d), full(mod), once(w_out), full(args[5]), full(args[6]),
             of_layer(w_up_all, pipeline_mode=pl.Buffered(1)), of_layer(conv_w_all),
             of_layer(w_down_all, pipeline_mode=pl.Buffered(1)), full(args[10]), full(args[11])]
    return pl.pallas_call(
        _mix_ffn_kernel,
        grid=(s // TM,),
        in_specs=specs,
        out_specs=row(d),
        out_shape=jax.ShapeDtypeStruct((s, d), F32),
        scratch_shapes=[pltpu.VMEM((SUBLANE, 2 * D_FF), F32), pltpu.VMEM((TM, D_FF), BF16)],
        compiler_params=_params(),
        name="mix_ffn",
    )(*args)


_O_U = (0, 512)
_O_QLAT = (512, 896)
_O_KVLAT = (896, 1152)
_O_KR = (1152, 1280)
_O_KR_ROT = (1280, 1408)
_O_COLS = 1408


def _rotate_half_cols(w):
    half = w.shape[-1] // 2
    return jnp.concatenate([-w[..., half:], w[..., :half]], axis=-1)


def _odd_weights(w_in, w_uq, w_ukv):
    d = w_in.shape[0]
    o = C_WIDTH + Q_LORA + KV_LORA
    kr = w_in[:, o:o + D_ROPE]
    zl = jnp.zeros((d, D_NOPE), w_in.dtype)
    zr = jnp.zeros((d, LANE - D_NOPE - D_ROPE), w_in.dtype)
    w1 = jnp.concatenate([w_in[:, 0:o], zl, kr, zr, zl, _rotate_half_cols(kr), zr], axis=1)
    assert w1.shape[1] == _O_COLS
    dq = D_NOPE + D_ROPE
    uq = w_uq.reshape(Q_LORA, N_HEADS, dq)
    padq = jnp.zeros((Q_LORA, N_HEADS, LANE - dq), w_uq.dtype)
    wq1 = jnp.concatenate([uq, padq], axis=2).reshape(Q_LORA, N_HEADS * LANE)
    wq2 = jnp.concatenate([jnp.zeros((Q_LORA, N_HEADS, D_NOPE), w_uq.dtype),
                           _rotate_half_cols(uq[:, :, D_NOPE:]), padq], axis=2
                          ).reshape(Q_LORA, N_HEADS * LANE)
    ukv = w_ukv.reshape(KV_LORA, N_HEADS, D_NOPE + D_V)
    wk = jnp.concatenate([ukv[:, :, :D_NOPE], jnp.zeros((KV_LORA, N_HEADS, LANE - D_NOPE), w_ukv.dtype)],
                         axis=2).reshape(KV_LORA, N_HEADS * LANE)
    wvt = ukv[:, :, D_NOPE:].reshape(KV_LORA, N_HEADS * D_V).T
    return w1.astype(BF16), wq1.astype(BF16), wq2.astype(BF16), wk.astype(BF16), wvt.astype(BF16)


def _rope_lane_tables(seq):
    inv = ROPE_BASE ** (-jnp.arange(0, D_ROPE, 2, dtype=F32) / D_ROPE)
    ang = jnp.arange(seq, dtype=F32)[:, None] * inv[None, :]
    cos, sin = lax.optimization_barrier((jnp.cos(ang), jnp.sin(ang)))
    pad = jnp.zeros((seq, LANE - D_NOPE - D_ROPE), F32)
    cos_t = jnp.concatenate([jnp.ones((seq, D_NOPE), F32), cos, cos, pad], axis=1)
    sin_t = jnp.concatenate([jnp.zeros((seq, D_NOPE), F32), sin, sin, pad], axis=1)
    return cos_t, sin_t


def _odd_in_kernel(x_ref, mod_ref, w1_ref, wq1_ref, wq2_ref, wk_ref, wvt_ref, pw_ref, ps_ref,
                   qg_ref, kvg_ref, cos_ref, sin_ref,
                   yc_ref, q_ref, k_ref, vt_ref, carry_ref):
    d = D_MODEL
    i = pl.program_id(0)

    @pl.when(i == 0)
    def _():
        carry_ref[...] = jnp.zeros_like(carry_ref)

    h = (x_ref[...] * (1.0 + mod_ref[:, d:2 * d]) + mod_ref[:, 0:d]).astype(BF16)

    def proj(cols):
        return _dot(h, w1_ref[:, cols[0]:cols[1]])

    cos = cos_ref[...]
    sin = sin_ref[...]

    u = proj(_O_U)
    ext = jnp.concatenate([carry_ref[...], u], axis=0)
    pos = i * TM + lax.broadcasted_iota(jnp.int32, (TM, 1), 0)
    for g, win in enumerate(POOL_WINDOWS):
        sl = slice(g * C_GROUP_DIM, (g + 1) * C_GROUP_DIM)
        acc = ext[:, sl]
        shift = 1
        while shift < win:
            acc = acc + pltpu.roll(acc, shift, 0)
            shift *= 2
        cnt = jnp.minimum(pos + 1, win).astype(F32)
        pooled = acc[POOL_HALO:] / cnt - u[:, sl]
        mixed = _dot(pooled.astype(BF16), pw_ref[g])
        yc_ref[:, sl] = (mixed * ps_ref[:, sl]).astype(BF16)
    carry_ref[...] = u[TM - POOL_HALO:TM]

    r = _rms_norm(proj(_O_QLAT), qg_ref[...]).astype(BF16)
    qa = _dot(r, wq1_ref[...])
    qb = _dot(r, wq2_ref[...])
    kr = proj(_O_KR) * cos + proj(_O_KR_ROT) * sin
    rk = _rms_norm(proj(_O_KVLAT), kvg_ref[...]).astype(BF16)
    kn = _dot(rk, wk_ref[...])
    for hd in range(N_HEADS):
        sl = slice(hd * LANE, (hd + 1) * LANE)
        q_ref[:, sl] = ((qa[:, sl] * cos + qb[:, sl] * sin) * MLA_Q_SCALE).astype(BF16)
        k_ref[:, sl] = (kn[:, sl] + kr).astype(BF16)
    _store_values_t(vt_ref, _dot_nt(wvt_ref[...], rk))


def _odd_in(x, mod, w1, wq1, wq2, wk, wvt, pool_w, pool_scale, q_norm_g, kv_norm_g, cos_t, sin_t):
    s, d = x.shape
    row = lambda n: pl.BlockSpec((TM, n), lambda i: (i, 0))
    col = lambda n: pl.BlockSpec((n, TM), lambda i: (0, i))
    full = lambda a: pl.BlockSpec(a.shape, lambda i: (0,) * a.ndim)
    consts = [w1, wq1, wq2, wk, wvt, pool_w.astype(BF16), pool_scale.reshape(1, C_WIDTH),
              q_norm_g.reshape(1, Q_LORA), kv_norm_g.reshape(1, KV_LORA)]
    rows_out = lambda n, t: (row(n), jax.ShapeDtypeStruct((s, n), t))
    cols_out = lambda n, t: (col(n), jax.ShapeDtypeStruct((n, s), t))
    outs = [rows_out(C_WIDTH, BF16), rows_out(N_HEADS * LANE, BF16), rows_out(N_HEADS * LANE, BF16),
            cols_out(N_HEADS * V_ROWS, BF16)]
    return pl.pallas_call(
        _odd_in_kernel,
        grid=(s // TM,),
        in_specs=[row(d), full(mod)] + [_resident(a) for a in consts] + [row(LANE), row(LANE)],
        out_specs=[o[0] for o in outs],
        out_shape=[o[1] for o in outs],
        scratch_shapes=[pltpu.VMEM((POOL_HALO, C_WIDTH), F32)],
        compiler_params=_params(),
        name="odd_in",
    )(x, mod, *consts, cos_t, sin_t)


def kernel(x, c, ada_w, ada_b, ln_mix_g, ln_mix_b, ln_ffn_g, ln_ffn_b, ev_w_in, ev_conv_w, ev_w_out,
           od_w_in, pool_w, pool_scale, q_norm_g, w_uq, kv_norm_g, w_ukv, od_w_out,
           ffn_w_up, ffn_conv_w, ffn_w_down):
    bsz, seq, d = x.shape
    assert bsz == 1 and d == D_MODEL and seq % TM == 0 and seq % TK_ATT == 0 and seq % TQ_ATT == 0
    assert SHORT_CONV == 3 and FFN_CONV == 3
    xs = x.reshape(seq, d)
    mods = _adaln(c, ada_w, ada_b)
    w_up_bf16, w_down_bf16 = ffn_w_up.astype(BF16), ffn_w_down.astype(BF16)
    for l in range(DEPTH):
        mod = mods[l]
        if l % 2 == 0:
            e = l // 2
            w, wt = _even_weights(ev_w_in[e])
            q, k, vt, qi, ki, wit, yb = _even_in(xs, mod, w, wt, ev_conv_w[e], _alibi_q_features())
            bias = _select(qi, ki, wit)
            mixed = (_flash(q, k, vt, bias, alibi=True), yb)
            w_out = ev_w_out[e]
        else:
            o = l // 2
            cos_t, sin_t = _rope_lane_tables(seq)
            ws = _odd_weights(od_w_in[o], w_uq[o], w_ukv[o])
            yc, q, k, vt = _odd_in(xs, mod, *ws, pool_w[o], pool_scale[o], q_norm_g[o], kv_norm_g[o],
                                   cos_t, sin_t)
            mixed = (yc, _flash(q, k, vt, None, alibi=False))
            w_out = od_w_out[o]
        xs = _mix_ffn(*mixed, xs, mod, w_out.astype(BF16), ln_mix_g[l], ln_mix_b[l],
                      l, w_up_bf16, ffn_conv_w, w_down_bf16, ln_ffn_g[l], ln_ffn_b[l])
    return xs.reshape(bsz, seq, d)
```

```python
import functools

import numpy as np
import jax
import jax.numpy as jnp
from jax import lax
from jax.experimental import pallas as pl
from jax.experimental.pallas import tpu as pltpu

D_MODEL = 1024
DEPTH = 2
CHUNK = 64
N_HEADS = 8
A_HEAD_DIM = 64
A_WIDTH = N_HEADS * A_HEAD_DIM
IDX_DIM = 64
TOPK = 256
B_WIDTH = 512
SHORT_CONV = 3
C_WIDTH = 512
POOL_WINDOWS = (2, 4, 8, 16)
C_GROUP_DIM = C_WIDTH // len(POOL_WINDOWS)
D_NOPE = 64
D_ROPE = 32
D_V = 64
Q_LORA = 384
KV_LORA = 256
ROPE_BASE = 10000.0
D_FF = 2816
FFN_CONV = 3
LN_EPS = 1e-5
RMS_EPS = 1e-6
DN_ALPHA = (2 * DEPTH) ** 0.25
NEG = -1e30
IDX_W_SCALE = N_HEADS ** -0.5 * IDX_DIM ** -0.5
LOG2E = 1.4426950408889634
DSA_Q_SCALE = A_HEAD_DIM ** -0.5 * LOG2E
MLA_Q_SCALE = (D_NOPE + D_ROPE) ** -0.5 * LOG2E
ALIBI_SLOPES = tuple(2.0 ** (-8.0 * (i + 1) / N_HEADS) for i in range(N_HEADS))

LANE = 128
SUBLANE = 8
BF16_ROWS = 16
V_ROWS = D_V + BF16_ROWS
VMEM_LIMIT_BYTES = 56 * 1024 * 1024

TM = 512
FFN_CHUNK = 256
TQ_SEL = 256
TQ_ATT = 512
TK_SEL = 512
SCORE_GROUP = 4
TK_ATT = 1024
QK_AHEAD = 4
QK_AHEAD_MASKED = 8
POOL_HALO = 16

INT_MIN = -2 ** 31
F32 = jnp.float32
BF16 = jnp.bfloat16


def _params():
    return pltpu.CompilerParams(dimension_semantics=("arbitrary",),
                                vmem_limit_bytes=VMEM_LIMIT_BYTES)


def _dot(a, b):
    return jnp.dot(a, b, preferred_element_type=F32)


def _dot_nt(a, b):
    return lax.dot_general(a, b, (((1,), (1,)), ((), ())), preferred_element_type=F32)


def _layer_norm(z, g, b):
    mu = jnp.mean(z, axis=-1, keepdims=True)
    zc = z - mu
    var = jnp.mean(zc * zc, axis=-1, keepdims=True)
    return zc * lax.rsqrt(var + LN_EPS) * g + b


def _rms_norm(z, g):
    return z * lax.rsqrt(jnp.mean(z * z, axis=-1, keepdims=True) + RMS_EPS) * g


def _causal_conv3(u, prev, w):
    w0, w1, w2 = w[0:1], w[1:2], w[2:3]
    y = pltpu.roll(u, 2, 0) * w0 + pltpu.roll(u, 1, 0) * w1 + u * w2
    head = u[0:SUBLANE]
    r = lax.broadcasted_iota(jnp.int32, (SUBLANE, 1), 0)
    h1 = jnp.where(r == 0, prev[7:8], pltpu.roll(head, 1, 0))
    h2 = jnp.where(r == 0, prev[6:7], jnp.where(r == 1, prev[7:8], pltpu.roll(head, 2, 0)))
    yh = h2 * w0 + h1 * w1 + head * w2
    return jnp.concatenate([yh, y[SUBLANE:]], axis=0)


def _fold_rows(c, rows_out):
    rows = c.shape[0]
    while rows > rows_out:
        rows //= 2
        c = c[0:rows] + c[rows:2 * rows]
    return c


def _store_values_t(vt_ref, vt):
    t = vt.shape[1]
    ones_row = jnp.where(lax.broadcasted_iota(jnp.int32, (BF16_ROWS, t), 0) == 0, 1.0, 0.0)
    for h in range(N_HEADS):
        vt_ref[h * V_ROWS:h * V_ROWS + D_V, :] = vt[h * D_V:(h + 1) * D_V, :].astype(BF16)
        vt_ref[h * V_ROWS + D_V:(h + 1) * V_ROWS, :] = ones_row.astype(BF16)


def _adaln_kernel(c_ref, w_ref, b_ref, o_ref):
    c = c_ref[...]
    cond = c * jax.nn.sigmoid(c)
    o_ref[0] = jnp.sum(cond * w_ref[0], axis=0, keepdims=True) + b_ref[0]


def _adaln(c, ada_w, ada_b):
    depth, d, n = ada_w.shape
    tn = 1536
    return pl.pallas_call(
        _adaln_kernel,
        grid=(depth, n // tn),
        in_specs=[pl.BlockSpec((d, 1), lambda l, j: (0, 0)),
                  pl.BlockSpec((1, d, tn), lambda l, j: (l, 0, j)),
                  pl.BlockSpec((1, 1, tn), lambda l, j: (l, 0, j))],
        out_specs=pl.BlockSpec((1, 1, tn), lambda l, j: (l, 0, j)),
        out_shape=jax.ShapeDtypeStruct((depth, 1, n), F32),
        compiler_params=pltpu.CompilerParams(dimension_semantics=("arbitrary", "arbitrary"),
                                             vmem_limit_bytes=VMEM_LIMIT_BYTES),
        name="adaln",
    )(c.reshape(d, 1), ada_w, ada_b.reshape(depth, 1, n))


_E_Q = (0, 1024)
_E_K = (1024, 2048)
_E_QI = (2048, 3072)
_E_KI = (3072, 3200)
_E_BG = (3200, 3712)
_E_CG = (3712, 4224)
_E_XB = (4224, 4736)
_E_COLS = 4736
_ET_V = (0, 512)
_ET_WI = (512, 528)
_ET_ROWS = 528
_FEAT_LANE = A_HEAD_DIM
_N_LOG2E_TERMS = 3
_POS_RADIX = 128


def _pad_heads(w):
    d = w.shape[0]
    w3 = w.reshape(d, N_HEADS, A_HEAD_DIM)
    return jnp.concatenate([w3, jnp.zeros_like(w3)], axis=2).reshape(d, N_HEADS * LANE)


def _alibi_q_features():
    terms, rest = [], np.float64(LOG2E)
    for _ in range(_N_LOG2E_TERMS):
        t = np.float64(np.asarray(rest, np.float32).astype(jnp.bfloat16).astype(np.float32))
        terms.append(t)
        rest = rest - t
    row = np.zeros((N_HEADS, LANE), np.float32)
    for h in range(N_HEADS):
        for n, t in enumerate(terms):
            row[h, _FEAT_LANE + n] = ALIBI_SLOPES[h] * _POS_RADIX * t
            row[h, _FEAT_LANE + _N_LOG2E_TERMS + n] = ALIBI_SLOPES[h] * t
    return jnp.asarray(row.reshape(1, N_HEADS * LANE))


def _even_weights(w_in):
    d = w_in.shape[0]
    a = A_WIDTH
    q = _pad_heads(w_in[:, 0:a])
    k = _pad_heads(w_in[:, a:2 * a])
    v = w_in[:, 2 * a:3 * a]
    qi = _pad_heads(w_in[:, 3 * a:4 * a])
    o = 4 * a
    ki = w_in[:, o:o + IDX_DIM]
    wi = w_in[:, o + IDX_DIM:o + IDX_DIM + N_HEADS]
    o = o + IDX_DIM + N_HEADS
    rest = w_in[:, o:o + 3 * B_WIDTH]
    w = jnp.concatenate([q, k, qi, ki, ki, rest], axis=1)
    assert w.shape[1] == _E_COLS
    wt = jnp.concatenate([v.T, wi.T, jnp.zeros((BF16_ROWS - N_HEADS, d), w_in.dtype)], axis=0)
    assert wt.shape[0] == _ET_ROWS
    return w.astype(BF16), wt.astype(BF16)


def _even_in_kernel(x_ref, mod_ref, w_ref, wt_ref, cw_ref, qf_ref,
                    q_ref, k_ref, vt_ref, qi_ref, ki_ref, wit_ref, yb_ref, carry_ref):
    d = D_MODEL
    i = pl.program_id(0)

    @pl.when(i == 0)
    def _():
        carry_ref[...] = jnp.zeros_like(carry_ref)

    h = (x_ref[...] * (1.0 + mod_ref[:, d:2 * d]) + mod_ref[:, 0:d]).astype(BF16)

    def proj(cols):
        return _dot(h, w_ref[:, cols[0]:cols[1]])

    def proj_t(rows):
        return _dot_nt(wt_ref[rows[0]:rows[1], :], h)

    q_ref[...] = (proj(_E_Q) * DSA_Q_SCALE + qf_ref[...]).astype(BF16)
    pos = i * TM + lax.broadcasted_iota(jnp.int32, (TM, LANE), 0)
    lane = lax.broadcasted_iota(jnp.int32, (TM, LANE), 1) - _FEAT_LANE
    pos_hi = (pos // _POS_RADIX).astype(F32)
    pos_lo = (pos % _POS_RADIX).astype(F32)
    kfeat = jnp.where(lane < 0, 0.0,
                      jnp.where(lane < _N_LOG2E_TERMS, pos_hi,
                                jnp.where(lane < 2 * _N_LOG2E_TERMS, pos_lo, 0.0)))
    kproj = proj(_E_K)
    for hd in range(N_HEADS):
        sl = slice(hd * LANE, (hd + 1) * LANE)
        k_ref[:, sl] = (kproj[:, sl] + kfeat).astype(BF16)
    _store_values_t(vt_ref, proj_t(_ET_V))
    qi_ref[...] = proj(_E_QI).astype(BF16)
    ki_ref[...] = proj(_E_KI).astype(BF16)
    wit_ref[...] = proj_t(_ET_WI) * IDX_W_SCALE
    g = proj(_E_CG) * proj(_E_XB)
    y = _causal_conv3(g, carry_ref[...], cw_ref[...])
    yb_ref[...] = (proj(_E_BG) * y).astype(BF16)
    carry_ref[...] = g[TM - SUBLANE:TM]


def _even_in(x, mod, w, wt, conv_w, qfeat):
    s, d = x.shape
    row = lambda n: pl.BlockSpec((TM, n), lambda i: (i, 0))
    col = lambda n: pl.BlockSpec((n, TM), lambda i: (0, i))
    full = lambda a: pl.BlockSpec(a.shape, lambda i: (0,) * a.ndim)
    rows_out = lambda n, t: (row(n), jax.ShapeDtypeStruct((s, n), t))
    cols_out = lambda n, t: (col(n), jax.ShapeDtypeStruct((n, s), t))
    outs = [rows_out(N_HEADS * LANE, BF16), rows_out(N_HEADS * LANE, BF16), cols_out(N_HEADS * V_ROWS, BF16),
            rows_out(N_HEADS * LANE, BF16), rows_out(LANE, BF16), cols_out(BF16_ROWS, F32),
            rows_out(B_WIDTH, BF16)]
    return pl.pallas_call(
        _even_in_kernel,
        grid=(s // TM,),
        in_specs=[row(d), full(mod), full(w), full(wt), full(conv_w), full(qfeat)],
        out_specs=[o[0] for o in outs],
        out_shape=[o[1] for o in outs],
        scratch_shapes=[pltpu.VMEM((SUBLANE, B_WIDTH), F32)],
        compiler_params=_params(),
        name="even_in",
    )(x, mod, w, wt, conv_w, qfeat)


def _select_kernel(qi_ref, ki_ref, wt_ref, bias_ref, hi_ref, lo_ref, m1_ref, m2_ref, mc_ref, my_ref, tc_ref,
                   *, seq):
    tq, tk = TQ_SEL, TK_SEL
    i16 = jnp.int16
    i = pl.program_id(0)
    start = i * tq
    n_tiles = (start + tq + tk - 1) // tk
    tpos = start + lax.broadcasted_iota(jnp.int32, (1, tq), 1)
    limit = (tpos // CHUNK + 1) * CHUNK
    int_min = jnp.int32(INT_MIN)
    dmin, dmax = -2 ** 15, 2 ** 15 - 1
    zero16, one16 = jnp.zeros((), i16), jnp.ones((), i16)
    sel0, seln = jnp.zeros((), BF16), jnp.full((), NEG, BF16)

    def key_pos(off):
        return off + lax.broadcasted_iota(jnp.int32, (tk, tq), 0)

    def score_tile(j, carry, *, masked, tiles=1):
        for t in range(tiles):
            score_rows(pl.multiple_of((j * tiles + t) * tk, tk), masked)
        return carry

    def score_rows(off, masked):
        kt = ki_ref[pl.ds(off, tk), :]
        score = jnp.zeros((tk, tq), F32)
        for h in range(N_HEADS):
            rel = _dot_nt(kt, qi_ref[:, h * LANE:(h + 1) * LANE])
            score = score + wt_ref[h:h + 1, :] * jnp.maximum(rel, 0.0)
        bits = pltpu.bitcast(score, jnp.int32)
        key = bits ^ ((bits >> 31) & jnp.int32(0x7FFFFFFF))
        key = jnp.where(bits == int_min, 0, key)
        if masked:
            key = jnp.where(key_pos(off) < limit, key, int_min)
        hi_ref[pl.ds(off, tk), :] = (key >> 16).astype(i16)
        lo_ref[pl.ds(off, tk), :] = key.astype(i16) ^ jnp.asarray(dmin, i16)

    n_before = start // tk
    n_groups = n_before // SCORE_GROUP
    lax.fori_loop(0, n_groups, functools.partial(score_tile, masked=False, tiles=SCORE_GROUP), 0)
    lax.fori_loop(SCORE_GROUP * n_groups, n_before, functools.partial(score_tile, masked=False), 0)
    lax.fori_loop(n_before, n_tiles, functools.partial(score_tile, masked=True), 0)

    def rows_at(ref, off, rows):
        return ref[pl.ds(off, rows), :]

    def for_key_tiles(fn, carry):
        return lax.fori_loop(0, n_tiles, lambda j, c: fn(pl.multiple_of(j * tk, tk), tk, c), carry)

    def per_query(c):
        return jnp.sum(c.astype(jnp.int32).astype(F32), axis=0, keepdims=True)

    def count(flag):
        def body(off, rows, acc):
            return acc + _fold_rows(flag(off, rows), BF16_ROWS)
        return per_query(for_key_tiles(body, jnp.zeros((BF16_ROWS, tq), i16)))

    def radix_search(count_ge, need):
        def bit_step(b, thr):
            cand = thr + jnp.left_shift(jnp.int32(1), 15 - b)
            return jnp.where(count_ge(cand.astype(i16)) >= need, cand, thr)
        return lax.fori_loop(0, 16, bit_step, jnp.full((1, tq), dmin, jnp.int32))

    topk = float(TOPK)
    t_hi = radix_search(lambda c16: count(
        lambda off, rows: jnp.where(rows_at(hi_ref, off, rows) >= c16, one16, zero16)), topk)
    t_hi16 = t_hi.astype(i16)

    m1_ref[...] = jnp.full((tk, tq), dmin, i16)
    m2_ref[...] = jnp.full((tk, tq), dmin, i16)
    mc_ref[...] = jnp.zeros((tk, tq), i16)

    def fold_members(off, rows, n_above16):
        hi = rows_at(hi_ref, off, rows)
        n_above16 = n_above16 + _fold_rows(jnp.where(hi > t_hi16, one16, zero16), BF16_ROWS)
        member = hi == t_hi16
        x = jnp.where(member, rows_at(lo_ref, off, rows), jnp.asarray(dmin, i16))
        lo_ref[pl.ds(off, rows), :] = x
        a, b = m1_ref[...], m2_ref[...]
        above = x > a
        m1_ref[...] = jnp.where(above, x, a)
        second = jnp.where(above, a, x)
        m2_ref[...] = jnp.where(second > b, second, b)
        mc_ref[...] = mc_ref[...] + jnp.where(member, one16, zero16)
        return n_above16

    n_above = per_query(for_key_tiles(fold_members, jnp.zeros((BF16_ROWS, tq), i16)))
    need_lo = topk - n_above

    def count_slots(flag):
        return per_query(_fold_rows(flag(m1_ref[...]) + flag(m2_ref[...]), BF16_ROWS))

    def level2_slots():
        t = radix_search(lambda c16: count_slots(lambda m: jnp.where(m >= c16, one16, zero16)), need_lo)
        t16 = t.astype(i16)
        n_members = per_query(_fold_rows(mc_ref[...], BF16_ROWS))
        gt = count_slots(lambda m: jnp.where(m > t16, one16, zero16))
        ge = jnp.where(t == dmin, n_members, count_slots(lambda m: jnp.where(m >= t16, one16, zero16)))
        return t, gt, ge

    def level2_full():
        t = radix_search(lambda c16: count(
            lambda off, rows: jnp.where(rows_at(lo_ref, off, rows) >= c16, one16, zero16)), need_lo)
        t16 = t.astype(i16)
        gt = count(lambda off, rows: jnp.where(rows_at(lo_ref, off, rows) > t16, one16, zero16))
        ge = count(lambda off, rows: jnp.where(
            rows_at(hi_ref, off, rows) == t_hi16,
            jnp.where(rows_at(lo_ref, off, rows) >= t16, one16, zero16), zero16))
        return t, gt, ge

    crowded = per_query(_fold_rows(jnp.where(mc_ref[...] > jnp.asarray(2, i16), one16, zero16), BF16_ROWS))
    t_lo, n_gt_lo, n_ge_lo = lax.cond(jnp.max(crowded) > 0.0, level2_full, level2_slots)
    t_lo16 = t_lo.astype(i16)
    n_gt = n_above + n_gt_lo
    n_ge = n_above + n_ge_lo
    room = topk - n_gt
    sentinel = t_hi == dmin
    has_ties = jnp.where(sentinel, 0.0, jnp.where(n_ge > topk, 1.0, 0.0))
    any_ties = jnp.max(has_ties) > 0.0

    def pos16(off, rows):
        return (off + lax.broadcasted_iota(jnp.int32, (rows, tq), 0)).astype(i16)

    def tie_cutoff():
        def tied_rows(j):
            off = pl.multiple_of(j * tk, tk)
            return jnp.where(rows_at(hi_ref, off, tk) == t_hi16,
                             jnp.where(rows_at(lo_ref, off, tk) == t_lo16, one16, zero16), zero16)

        def count_tile(j, carry):
            tc_ref[pl.ds(j, 1), :] = per_query(_fold_rows(tied_rows(j), BF16_ROWS))
            return carry
        lax.fori_loop(0, n_tiles, count_tile, 0)

        def find_tile(j, carry):
            seen, tile_of, left = carry
            here = tc_ref[pl.ds(j, 1), :]
            crossing = jnp.logical_and(seen < room, seen + here >= room)
            return (seen + here, jnp.where(crossing, j, tile_of), jnp.where(crossing, room - seen, left))
        _, tile_of, left = lax.fori_loop(
            0, n_tiles, find_tile, (jnp.zeros((1, tq), F32), jnp.zeros((1, tq), jnp.int32), room))

        my_ref[...] = jnp.zeros((tk, tq), i16)
        tile_of16 = tile_of.astype(i16)

        def gather_tile(j, carry):
            mine = jnp.where(tile_of16 == j.astype(i16), jnp.asarray(-1, i16), zero16)
            my_ref[...] = my_ref[...] | (tied_rows(j) & mine)
            return carry
        lax.fori_loop(0, n_tiles, gather_tile, 0)

        row16 = lax.broadcasted_iota(jnp.int32, (tk, tq), 0).astype(i16)
        row_bits = tk.bit_length()

        def row_step(b, cut):
            cand = cut + jnp.left_shift(jnp.int32(1), row_bits - 1 - b)
            below = jnp.where(row16 < cand.astype(i16), my_ref[...], zero16)
            return jnp.where(per_query(_fold_rows(below, BF16_ROWS)) <= left, cand, cut)
        row_cut = lax.fori_loop(0, row_bits, row_step, jnp.zeros((1, tq), jnp.int32))
        return tile_of * tk + jnp.minimum(row_cut, tk)

    def write_plain():
        lo_min16 = jnp.where(sentinel, dmax, t_lo).astype(i16)

        def write_rows(off, rows, carry):
            hi = rows_at(hi_ref, off, rows)
            inner = jnp.where(rows_at(lo_ref, off, rows) >= lo_min16, sel0, seln)
            bias_ref[pl.ds(off, rows), :] = jnp.where(hi > t_hi16, sel0,
                                                      jnp.where(hi == t_hi16, inner, seln))
            return carry
        for_key_tiles(write_rows, 0)

    def write_with_ties():
        cut16 = jnp.where(sentinel, 0, tie_cutoff()).astype(i16)

        def write_rows(off, rows, carry):
            hi = rows_at(hi_ref, off, rows)
            lo = rows_at(lo_ref, off, rows)
            tie = jnp.where(pos16(off, rows) < cut16, sel0, seln)
            inner = jnp.where(lo > t_lo16, sel0, jnp.where(lo == t_lo16, tie, seln))
            bias_ref[pl.ds(off, rows), :] = jnp.where(hi > t_hi16, sel0,
                                                      jnp.where(hi == t_hi16, inner, seln))
            return carry
        for_key_tiles(write_rows, 0)

    lax.cond(any_ties, write_with_ties, write_plain)

    def fill_tile(j, carry):
        off = pl.multiple_of(j * tk, tk)
        bias_ref[pl.ds(off, tk), :] = jnp.full((tk, tq), NEG, BF16)
        return carry

    lax.fori_loop(n_tiles, seq // tk, fill_tile, 0)


def _select(qi, ki, wit):
    s = qi.shape[0]
    assert s % TK_SEL == 0 and s % TQ_SEL == 0 and 2 * TOPK <= s <= 32767
    return pl.pallas_call(
        functools.partial(_select_kernel, seq=s),
        grid=(s // TQ_SEL,),
        in_specs=[pl.BlockSpec((TQ_SEL, N_HEADS * LANE), lambda i: (i, 0)),
                  pl.BlockSpec((s, LANE), lambda i: (0, 0)),
                  pl.BlockSpec((BF16_ROWS, TQ_SEL), lambda i: (0, i))],
        out_specs=pl.BlockSpec((s, TQ_SEL), lambda i: (0, i)),
        out_shape=jax.ShapeDtypeStruct((s, s), BF16),
        scratch_shapes=[pltpu.VMEM((s, TQ_SEL), jnp.int16), pltpu.VMEM((s, TQ_SEL), jnp.int16)]
        + [pltpu.VMEM((TK_SEL, TQ_SEL), jnp.int16)] * 4 + [pltpu.VMEM((s // TK_SEL, TQ_SEL), F32)],
        compiler_params=_params(),
        name="dsa_select",
    )(qi, ki, wit)


def _flash_kernel(qi_of, kj_of, *refs, alibi, use_bias, qk_ahead):
    if use_bias:
        q_ref, k_ref, vt_ref, bias_ref, o_ref, m_ref, acc_ref = refs
    else:
        q_ref, k_ref, vt_ref, o_ref, m_ref, acc_ref = refs
    tq, tk = TQ_ATT, TK_ATT
    p = pl.program_id(0)
    qi = qi_of[p]
    kj = kj_of[p]
    last = ((qi + 1) * tq - 1) // tk

    @pl.when(kj == 0)
    def _():
        m_ref[...] = jnp.full_like(m_ref, NEG)
        acc_ref[...] = jnp.zeros_like(acc_ref)

    def step(diagonal, live=tk):
        base = bias_ref[0:live, :].astype(F32) if use_bias else None
        if diagonal:
            tpos = qi * tq + lax.broadcasted_iota(jnp.int32, (live, tq), 1)
            spos = kj * tk + lax.broadcasted_iota(jnp.int32, (live, tq), 0)
            if not use_bias:
                base = jnp.where(spos < (tpos // CHUNK + 1) * CHUNK, 0.0, NEG)
            if alibi:
                ahead = jnp.maximum(spos - tpos, 0).astype(F32)

        def logits(h):
            s = _dot_nt(k_ref[0:live, h * LANE:(h + 1) * LANE], q_ref[:, h * LANE:(h + 1) * LANE])
            if diagonal and alibi:
                return s + (base - (2.0 * ALIBI_SLOPES[h] * LOG2E) * ahead)
            return s if base is None else s + base

        pending = [logits(h) for h in range(qk_ahead)]
        for h in range(N_HEADS):
            rows = slice(h * V_ROWS, (h + 1) * V_ROWS)
            if h + qk_ahead < N_HEADS:
                pending.append(logits(h + qk_ahead))
            s = pending.pop(0)
            m_prev = m_ref[h:h + 1, :]
            m_new = jnp.maximum(m_prev, jnp.max(s, axis=0, keepdims=True))
            alpha = jnp.exp2(m_prev - m_new)
            pexp = jnp.exp2(s - m_new)
            acc_ref[rows, :] = alpha * acc_ref[rows, :] + _dot(vt_ref[rows, 0:live], pexp.astype(BF16))
            m_ref[h:h + 1, :] = m_new

    @pl.when(kj != last)
    def _():
        step(False)

    blocks_per_tile = max(tk // tq, 1)
    for v in range(blocks_per_tile):
        @pl.when(jnp.logical_and(kj == last, qi % blocks_per_tile == v))
        def _(v=v):
            step(True, live=min((v + 1) * tq, tk))

    @pl.when(kj == last)
    def _():
        outs = [acc_ref[h * V_ROWS:h * V_ROWS + D_V, :] / acc_ref[h * V_ROWS + D_V:h * V_ROWS + D_V + 1, :]
                for h in range(N_HEADS)]
        o_ref[...] = jnp.concatenate(outs, axis=0).T.astype(BF16)


def _flash(q, k, vt, bias, *, alibi):
    s = q.shape[0]
    nq = s // TQ_ATT
    pairs = [(i, j) for i in range(nq) for j in range(((i + 1) * TQ_ATT - 1) // TK_ATT + 1)]
    qi_of = jnp.asarray(np.array([a for a, _ in pairs], np.int32))
    kj_of = jnp.asarray(np.array([b for _, b in pairs], np.int32))
    use_bias = bias is not None
    in_specs = [pl.BlockSpec((TQ_ATT, N_HEADS * LANE), lambda p, qi, kj: (qi[p], 0)),
                pl.BlockSpec((TK_ATT, k.shape[1]), lambda p, qi, kj: (kj[p], 0)),
                pl.BlockSpec((N_HEADS * V_ROWS, TK_ATT), lambda p, qi, kj: (0, kj[p]))]
    args = [q, k, vt]
    if use_bias:
        in_specs.append(pl.BlockSpec((TK_ATT, TQ_ATT), lambda p, qi, kj: (kj[p], qi[p])))
        args.append(bias)
    kern = functools.partial(_flash_kernel, alibi=alibi, use_bias=use_bias,
                             qk_ahead=QK_AHEAD_MASKED if use_bias else QK_AHEAD)
    return pl.pallas_call(
        kern,
        grid_spec=pltpu.PrefetchScalarGridSpec(
            num_scalar_prefetch=2,
            grid=(len(pairs),),
            in_specs=in_specs,
            out_specs=pl.BlockSpec((TQ_ATT, N_HEADS * D_V), lambda p, qi, kj: (qi[p], 0)),
            scratch_shapes=[pltpu.VMEM((N_HEADS, TQ_ATT), F32),
                            pltpu.VMEM((N_HEADS * V_ROWS, TQ_ATT), F32)]),
        out_shape=jax.ShapeDtypeStruct((s, N_HEADS * D_V), BF16),
        compiler_params=_params(),
        name="flash_dsa" if use_bias else "flash_mla",
    )(qi_of, kj_of, *args)


def _mix_ffn_kernel(ya_ref, yb_ref, x_ref, mod_ref, wo_ref, gm_ref, bm_ref,
                    wup_ref, cw_ref, wdn_ref, gf_ref, bf_ref, o_ref, carry_ref, act_ref):
    d = D_MODEL

    @pl.when(pl.program_id(0) == 0)
    def _():
        carry_ref[...] = jnp.zeros_like(carry_ref)

    na = ya_ref.shape[1]
    y = _dot(ya_ref[...], wo_ref[0:na, :]) + _dot(yb_ref[...], wo_ref[na:, :])
    x = _layer_norm(DN_ALPHA * x_ref[...] + (1.0 + mod_ref[:, 2 * d:3 * d]) * y, gm_ref[...], bm_ref[...])
    h = (x * (1.0 + mod_ref[:, 4 * d:5 * d]) + mod_ref[:, 3 * d:4 * d]).astype(BF16)
    for c in range(D_FF // FFN_CHUNK):
        va = c * FFN_CHUNK
        ga = D_FF + va
        uv = _dot(h, wup_ref[:, va:va + FFN_CHUNK])
        ug = _dot(h, wup_ref[:, ga:ga + FFN_CHUNK])
        val = _causal_conv3(uv, carry_ref[:, va:va + FFN_CHUNK], cw_ref[:, va:va + FFN_CHUNK])
        gate = _causal_conv3(ug, carry_ref[:, ga:ga + FFN_CHUNK], cw_ref[:, ga:ga + FFN_CHUNK])
        carry_ref[:, va:va + FFN_CHUNK] = uv[TM - SUBLANE:TM]
        carry_ref[:, ga:ga + FFN_CHUNK] = ug[TM - SUBLANE:TM]
        act_ref[:, va:va + FFN_CHUNK] = (gate * jax.nn.sigmoid(gate) * val).astype(BF16)
    y = _dot(act_ref[...], wdn_ref[...])
    z = DN_ALPHA * x + (1.0 + mod_ref[:, 5 * d:6 * d]) * y
    o_ref[...] = _layer_norm(z, gf_ref[...], bf_ref[...])


def _mix_ffn(ya, yb, x, mod, w_out, g_mix, b_mix, layer, w_up_all, conv_w_all, w_down_all, g_ffn, b_ffn):
    s, d = x.shape
    assert D_FF % FFN_CHUNK == 0
    row = lambda n: pl.BlockSpec((TM, n), lambda i: (i, 0))
    full = lambda a: pl.BlockSpec(a.shape, lambda i: (0,) * a.ndim)
    once = lambda a: pl.BlockSpec(a.shape, lambda i: (0,) * a.ndim, pipeline_mode=pl.Buffered(1))
    of_layer = lambda a, **kw: pl.BlockSpec((None,) + a.shape[1:], lambda i: (layer, 0, 0), **kw)
    vec = lambda a: a.reshape(1, d)
    args = [ya, yb, x, mod, w_out, vec(g_mix), vec(b_mix), w_up_all, conv_w_all, w_down_all,
            vec(g_ffn), vec(b_ffn)]
    specs = [row(ya.shape[1]), row(yb.shape[1]), row(d), full(mod), once(w_out), full(args[5]), full(args[6]),
             of_layer(w_up_all, pipeline_mode=pl.Buffered(1)), of_layer(conv_w_all),
             of_layer(w_down_all, pipeline_mode=pl.Buffered(1)), full(args[10]), full(args[11])]
    return pl.pallas_call(
        _mix_ffn_kernel,
        grid=(s // TM,),
        in_specs=specs,
        out_specs=row(d),
        out_shape=jax.ShapeDtypeStruct((s, d), F32),
        scratch_shapes=[pltpu.VMEM((SUBLANE, 2 * D_FF), F32), pltpu.VMEM((TM, D_FF), BF16)],
        compiler_params=_params(),
        name="mix_ffn",
    )(*args)


_O_U = (0, 512)
_O_QLAT = (512, 896)
_O_KVLAT = (896, 1152)
_O_KR = (1152, 1280)
_O_KR_ROT = (1280, 1408)
_O_COLS = 1408


def _rotate_half_cols(w):
    half = w.shape[-1] // 2
    return jnp.concatenate([-w[..., half:], w[..., :half]], axis=-1)


def _odd_weights(w_in, w_uq, w_ukv):
    d = w_in.shape[0]
    o = C_WIDTH + Q_LORA + KV_LORA
    kr = w_in[:, o:o + D_ROPE]
    zl = jnp.zeros((d, D_NOPE), w_in.dtype)
    zr = jnp.zeros((d, LANE - D_NOPE - D_ROPE), w_in.dtype)
    w1 = jnp.concatenate([w_in[:, 0:o], zl, kr, zr, zl, _rotate_half_cols(kr), zr], axis=1)
    assert w1.shape[1] == _O_COLS
    dq = D_NOPE + D_ROPE
    uq = w_uq.reshape(Q_LORA, N_HEADS, dq)
    padq = jnp.zeros((Q_LORA, N_HEADS, LANE - dq), w_uq.dtype)
    wq1 = jnp.concatenate([uq, padq], axis=2).reshape(Q_LORA, N_HEADS * LANE)
    wq2 = jnp.concatenate([jnp.zeros((Q_LORA, N_HEADS, D_NOPE), w_uq.dtype),
                           _rotate_half_cols(uq[:, :, D_NOPE:]), padq], axis=2
                          ).reshape(Q_LORA, N_HEADS * LANE)
    ukv = w_ukv.reshape(KV_LORA, N_HEADS, D_NOPE + D_V)
    wk = jnp.concatenate([ukv[:, :, :D_NOPE], jnp.zeros((KV_LORA, N_HEADS, LANE - D_NOPE), w_ukv.dtype)],
                         axis=2).reshape(KV_LORA, N_HEADS * LANE)
    wvt = ukv[:, :, D_NOPE:].reshape(KV_LORA, N_HEADS * D_V).T
    return w1.astype(BF16), wq1.astype(BF16), wq2.astype(BF16), wk.astype(BF16), wvt.astype(BF16)


def _rope_lane_tables(seq):
    inv = ROPE_BASE ** (-jnp.arange(0, D_ROPE, 2, dtype=F32) / D_ROPE)
    ang = jnp.arange(seq, dtype=F32)[:, None] * inv[None, :]
    cos, sin = lax.optimization_barrier((jnp.cos(ang), jnp.sin(ang)))
    pad = jnp.zeros((seq, LANE - D_NOPE - D_ROPE), F32)
    cos_t = jnp.concatenate([jnp.ones((seq, D_NOPE), F32), cos, cos, pad], axis=1)
    sin_t = jnp.concatenate([jnp.zeros((seq, D_NOPE), F32), sin, sin, pad], axis=1)
    return cos_t, sin_t


def _odd_in_kernel(x_ref, mod_ref, w1_ref, wq1_ref, wq2_ref, wk_ref, wvt_ref, pw_ref, ps_ref,
                   qg_ref, kvg_ref, cos_ref, sin_ref,
                   yc_ref, q_ref, k_ref, vt_ref, carry_ref):
    d = D_MODEL
    i = pl.program_id(0)

    @pl.when(i == 0)
    def _():
        carry_ref[...] = jnp.zeros_like(carry_ref)

    h = (x_ref[...] * (1.0 + mod_ref[:, d:2 * d]) + mod_ref[:, 0:d]).astype(BF16)

    def proj(cols):
        return _dot(h, w1_ref[:, cols[0]:cols[1]])

    cos = cos_ref[...]
    sin = sin_ref[...]

    u = proj(_O_U)
    ext = jnp.concatenate([carry_ref[...], u], axis=0)
    pos = i * TM + lax.broadcasted_iota(jnp.int32, (TM, 1), 0)
    for g, win in enumerate(POOL_WINDOWS):
        sl = slice(g * C_GROUP_DIM, (g + 1) * C_GROUP_DIM)
        acc = ext[:, sl]
        shift = 1
        while shift < win:
            acc = acc + pltpu.roll(acc, shift, 0)
            shift *= 2
        cnt = jnp.minimum(pos + 1, win).astype(F32)
        pooled = acc[POOL_HALO:] / cnt - u[:, sl]
        mixed = _dot(pooled.astype(BF16), pw_ref[g])
        yc_ref[:, sl] = (mixed * ps_ref[:, sl]).astype(BF16)
    carry_ref[...] = u[TM - POOL_HALO:TM]

    r = _rms_norm(proj(_O_QLAT), qg_ref[...]).astype(BF16)
    qa = _dot(r, wq1_ref[...])
    qb = _dot(r, wq2_ref[...])
    kr = proj(_O_KR) * cos + proj(_O_KR_ROT) * sin
    rk = _rms_norm(proj(_O_KVLAT), kvg_ref[...]).astype(BF16)
    kn = _dot(rk, wk_ref[...])
    for hd in range(N_HEADS):
        sl = slice(hd * LANE, (hd + 1) * LANE)
        q_ref[:, sl] = ((qa[:, sl] * cos + qb[:, sl] * sin) * MLA_Q_SCALE).astype(BF16)
        k_ref[:, sl] = (kn[:, sl] + kr).astype(BF16)
    _store_values_t(vt_ref, _dot_nt(wvt_ref[...], rk))


def _odd_in(x, mod, w1, wq1, wq2, wk, wvt, pool_w, pool_scale, q_norm_g, kv_norm_g, cos_t, sin_t):
    s, d = x.shape
    row = lambda n: pl.BlockSpec((TM, n), lambda i: (i, 0))
    col = lambda n: pl.BlockSpec((n, TM), lambda i: (0, i))
    full = lambda a: pl.BlockSpec(a.shape, lambda i: (0,) * a.ndim)
    consts = [w1, wq1, wq2, wk, wvt, pool_w.astype(BF16), pool_scale.reshape(1, C_WIDTH),
              q_norm_g.reshape(1, Q_LORA), kv_norm_g.reshape(1, KV_LORA)]
    rows_out = lambda n, t: (row(n), jax.ShapeDtypeStruct((s, n), t))
    cols_out = lambda n, t: (col(n), jax.ShapeDtypeStruct((n, s), t))
    outs = [rows_out(C_WIDTH, BF16), rows_out(N_HEADS * LANE, BF16), rows_out(N_HEADS * LANE, BF16),
            cols_out(N_HEADS * V_ROWS, BF16)]
    return pl.pallas_call(
        _odd_in_kernel,
        grid=(s // TM,),
        in_specs=[row(d), full(mod)] + [full(a) for a in consts] + [row(LANE), row(LANE)],
        out_specs=[o[0] for o in outs],
        out_shape=[o[1] for o in outs],
        scratch_shapes=[pltpu.VMEM((POOL_HALO, C_WIDTH), F32)],
        compiler_params=_params(),
        name="odd_in",
    )(x, mod, *consts, cos_t, sin_t)


def kernel(x, c, ada_w, ada_b, ln_mix_g, ln_mix_b, ln_ffn_g, ln_ffn_b, ev_w_in, ev_conv_w, ev_w_out,
           od_w_in, pool_w, pool_scale, q_norm_g, w_uq, kv_norm_g, w_ukv, od_w_out,
           ffn_w_up, ffn_conv_w, ffn_w_down):
    bsz, seq, d = x.shape
    assert bsz == 1 and d == D_MODEL and seq % TM == 0 and seq % TK_ATT == 0 and seq % TQ_ATT == 0
    assert SHORT_CONV == 3 and FFN_CONV == 3
    xs = x.reshape(seq, d)
    mods = _adaln(c, ada_w, ada_b)
    w_up_bf16, w_down_bf16 = ffn_w_up.astype(BF16), ffn_w_down.astype(BF16)
    for l in range(DEPTH):
        mod = mods[l]
        if l % 2 == 0:
            e = l // 2
            w, wt = _even_weights(ev_w_in[e])
            q, k, vt, qi, ki, wit, yb = _even_in(xs, mod, w, wt, ev_conv_w[e], _alibi_q_features())
            bias = _select(qi, ki, wit)
            mixed = (_flash(q, k, vt, bias, alibi=True), yb)
            w_out = ev_w_out[e]
        else:
            o = l // 2
            cos_t, sin_t = _rope_lane_tables(seq)
            ws = _odd_weights(od_w_in[o], w_uq[o], w_ukv[o])
            yc, q, k, vt = _odd_in(xs, mod, *ws, pool_w[o], pool_scale[o], q_norm_g[o], kv_norm_g[o],
                                   cos_t, sin_t)
            mixed = (yc, _flash(q, k, vt, None, alibi=False))
            w_out = od_w_out[o]
        xs = _mix_ffn(*mixed, xs, mod, w_out.astype(BF16), ln_mix_g[l], ln_mix_b[l],
                      l, w_up_bf16, ffn_conv_w, w_down_bf16, ln_ffn_g[l], ln_ffn_b[l])
    return xs.reshape(bsz, seq, d)
```
